```python
import math
import jax, jax.numpy as jnp
from jax import lax
import numpy as np

D_MODEL = 1024
BATCH = 16
SEQ = 4096
DEPTH = 1

GRID_W = 64
CTX_LEN = 256
N_HEADS = 8
N_KV_HEADS = 2
Q_PER_KV = N_HEADS // N_KV_HEADS
HEAD_DIM = 64
ATTN_WIDTH = N_HEADS * HEAD_DIM
KV_WIDTH = N_KV_HEADS * HEAD_DIM
HYENA_WIDTH = 512
HYENA_ORDER = 2
MIX_WIDTH = ATTN_WIDTH + HYENA_WIDTH
Q_END = ATTN_WIDTH
K_END = Q_END + KV_WIDTH
V_END = K_END + KV_WIDTH
IN_COLS = V_END + (HYENA_ORDER + 1) * HYENA_WIDTH
Q_BLOCK = 128
ROPE_THETA = 10000.0
SHORT_CONV = 3
FILTER_EMB = 33
FILTER_ORDER = 64
FILTER_DECAY_FAST = 0.3
FILTER_DECAY_SLOW = 1.5
FILTER_TARGET = 1e-2
N_EXPERTS = 256
TOP_K = 8
N_GROUPS = 8
TOPK_GROUPS = 4
EXPERT_FF = 256
SHARED_FF = 256
ROUTE_SCALE = 2.5
DISPATCH_BLOCK = 128
EPS = 1e-6

kernel_name = "hymba_hyena_gqa_moe_dit_layer"


def rms_norm(x, w):
    xf = x.astype(jnp.float32)
    y = xf * lax.rsqrt(jnp.mean(xf * xf, axis=-1, keepdims=True) + EPS)
    return (y * w.astype(jnp.float32)).astype(x.dtype)


def adaln(cond, w, b):
    m = (jax.nn.silu(cond) @ w + b)[..., None, :]
    return jnp.split(m, 6, axis=-1)


def modulate(x, g, shift, scale):
    return rms_norm(x, g) * (1 + scale) + shift


def axial_rope(n_tokens):
    t = jnp.arange(n_tokens, dtype=jnp.int32)
    row = (t // GRID_W).astype(jnp.float32)
    col = (t % GRID_W).astype(jnp.float32)
    n_freq = HEAD_DIM // 4
    inv = ROPE_THETA ** (-jnp.arange(n_freq, dtype=jnp.float32) / n_freq)
    ang = jnp.concatenate([row[:, None] * inv, col[:, None] * inv], axis=-1)
    return jnp.cos(ang), jnp.sin(ang)


def apply_rope(x, cos, sin):
    xf = x.astype(jnp.float32).reshape(*x.shape[:-1], HEAD_DIM // 2, 2)
    x0, x1 = xf[..., 0], xf[..., 1]
    c, s = cos[None, :, None, :], sin[None, :, None, :]
    out = jnp.stack([x0 * c - x1 * s, x0 * s + x1 * c], axis=-1)
    return out.reshape(x.shape).astype(x.dtype)


def attention(q, keys, vals):
    B, L = q.shape[:2]
    n_blk = L // Q_BLOCK
    qb = q.reshape(B, n_blk, Q_BLOCK, N_KV_HEADS, Q_PER_KV, HEAD_DIM).transpose(1, 0, 2, 3, 4, 5)
    scale = HEAD_DIM ** -0.5

    def one_block(qblk):
        s = jnp.einsum('bqgrd,bkgd->bgrqk', qblk, keys, preferred_element_type=jnp.float32) * scale
        p = jax.nn.softmax(s, axis=-1).astype(vals.dtype)
        return jnp.einsum('bgrqk,bkgd->bqgrd', p, vals)

    out = lax.map(one_block, qb)
    return out.transpose(1, 0, 2, 3, 4, 5).reshape(B, L, ATTN_WIDTH)


def hyena_filters(n_tokens, w1, b1, w2, b2, w3, b3, w4, freq):
    L = n_tokens
    f32 = jnp.float32
    w1, b1, w2, b2, w3, b3, w4, freq = [a.astype(f32) for a in (w1, b1, w2, b2, w3, b3, w4, freq)]
    t = jnp.linspace(0.0, 1.0, L, dtype=f32)[:, None]
    bands = (FILTER_EMB - 1) // 2
    w = 2.0 * math.pi * jnp.arange(L, dtype=f32)[:, None] / L
    f = jnp.linspace(1e-4, bands - 1, bands, dtype=f32)[None, :]
    z = jnp.concatenate([t, jnp.cos(f * w), -jnp.sin(f * w)], axis=-1)
    act = lambda a: jnp.sin(freq * a)
    h = act(z @ w1 + b1)
    h = act(h @ w2 + b2)
    h = act(h @ w3 + b3)
    h = h @ w4
    min_decay = math.log(FILTER_TARGET) / FILTER_DECAY_FAST
    max_decay = math.log(FILTER_TARGET) / FILTER_DECAY_SLOW
    deltas = jnp.linspace(min_decay, max_decay, HYENA_WIDTH, dtype=f32)
    decay = jnp.exp(-t * jnp.abs(deltas))
    h_fwd = h[:, :HYENA_WIDTH] * decay
    h_bwd = h[:, HYENA_WIDTH:] * decay
    return jnp.concatenate([h_fwd, jnp.zeros((1, HYENA_WIDTH), f32), h_bwd[1:][::-1]], axis=0)


def bidir_fftconv(u, k2, bias):
    L = u.shape[1]
    uf = jnp.fft.rfft(u.astype(jnp.float32), n=2 * L, axis=1)
    kf = jnp.fft.rfft(k2, n=2 * L, axis=0)
    y = jnp.fft.irfft(uf * kf[None], n=2 * L, axis=1)[:, :L]
    return (y + u.astype(jnp.float32) * bias.astype(jnp.float32)).astype(u.dtype)


def hyena_mix(u3, conv_w, conv_b, hyena_bias, filt):
    L = u3.shape[1]
    half = SHORT_CONV // 2
    up = jnp.pad(u3, ((0, 0), (half, half), (0, 0)))
    uc = sum(conv_w[j] * up[:, j:j + L] for j in range(SHORT_CONV)) + conv_b
    x0, x1, v = jnp.split(uc, 3, axis=-1)
    k2 = hyena_filters(L, *filt)
    z = bidir_fftconv(v * x1, k2, hyena_bias)
    return z * x0


def project(h, w_in):
    B, L, _ = h.shape
    p = h @ w_in
    q = p[..., :Q_END].reshape(B, L, N_HEADS, HEAD_DIM)
    k = p[..., Q_END:K_END].reshape(B, L, N_KV_HEADS, HEAD_DIM)
    v = p[..., K_END:V_END].reshape(B, L, N_KV_HEADS, HEAD_DIM)
    return q, k, v, p[..., V_END:]


def merge(a, y, attn_out_norm_w, hyena_out_norm_w, w_out):
    return jnp.concatenate([rms_norm(a, attn_out_norm_w), rms_norm(y, hyena_out_norm_w)], axis=-1) @ w_out


def swiglu(h, wg, wu, wd):
    return (jax.nn.silu(h @ wg) * (h @ wu)) @ wd


def route(t, router_w, router_bias):
    T = t.shape[0]
    scores = jax.nn.sigmoid((t @ router_w).astype(jnp.float32))
    biased = scores + router_bias.astype(jnp.float32)
    per_group = N_EXPERTS // N_GROUPS
    grp_score = lax.top_k(biased.reshape(T, N_GROUPS, per_group), 2)[0].sum(-1)
    _, top_g = lax.top_k(grp_score, TOPK_GROUPS)
    gmask = jnp.any(top_g[..., None] == jnp.arange(N_GROUPS)[None, None, :], axis=1)
    masked = jnp.where(jnp.repeat(gmask, per_group, axis=1), biased, -jnp.inf)
    _, idx = lax.top_k(masked, TOP_K)
    w = jnp.take_along_axis(scores, idx, axis=1)
    w = w / jnp.sum(w, axis=-1, keepdims=True) * ROUTE_SCALE
    return idx, w


def routed_experts(t, idx, w, wg, wu, wd):
    T, D = t.shape
    A = T * TOP_K
    flat_e = idx.reshape(A).astype(jnp.int32)
    flat_t = jnp.repeat(jnp.arange(T, dtype=jnp.int32), TOP_K)
    flat_w = w.reshape(A)
    order = jnp.argsort(flat_e)
    se = flat_e[order]
    counts = jnp.bincount(flat_e, length=N_EXPERTS).astype(jnp.int32)
    padded = (counts + DISPATCH_BLOCK - 1) // DISPATCH_BLOCK * DISPATCH_BLOCK
    pad_end = jnp.cumsum(padded)
    pad_start = pad_end - padded
    start = jnp.cumsum(counts) - counts
    dest = pad_start[se] + jnp.arange(A, dtype=jnp.int32) - start[se]
    n_rows = (A + N_EXPERTS * (DISPATCH_BLOCK - 1) + DISPATCH_BLOCK - 1) // DISPATCH_BLOCK * DISPATCH_BLOCK
    n_blk = n_rows // DISPATCH_BLOCK
    row_tok = jnp.full((n_rows,), T, jnp.int32).at[dest].set(flat_t[order])
    row_w = jnp.zeros((n_rows,), t.dtype).at[dest].set(flat_w[order].astype(t.dtype))
    blk_e = jnp.minimum(jnp.searchsorted(pad_end, jnp.arange(n_blk, dtype=jnp.int32) * DISPATCH_BLOCK,
                                         side='right'), N_EXPERTS - 1)
    t_pad = jnp.concatenate([t, jnp.zeros((1, D), t.dtype)], axis=0)

    def one_block(args):
        toks, gate, e = args
        return swiglu(t_pad[toks], wg[e], wu[e], wd[e]) * gate[:, None]

    out = lax.map(one_block, (row_tok.reshape(n_blk, DISPATCH_BLOCK),
                              row_w.reshape(n_blk, DISPATCH_BLOCK), blk_e))
    y = jax.ops.segment_sum(out.reshape(n_rows, D), row_tok, num_segments=T + 1)
    return y[:T]


def moe_ffn(h, router_w, router_bias, wg, wu, wd, sh_wg, sh_wu, sh_wd):
    B, L, D = h.shape
    t = h.reshape(B * L, D)
    idx, w = route(t, router_w, router_bias)
    y = routed_experts(t, idx, w, wg, wu, wd) + swiglu(t, sh_wg, sh_wu, sh_wd)
    return y.reshape(B, L, D)


def setup_inputs(seed: int = 0) -> dict:
    key = jax.random.key(seed)
    ks = iter(jax.random.split(key, 48))
    nrm = lambda shape, scale: jax.random.normal(next(ks), shape, jnp.float32) * scale
    gain = lambda shape: 1.0 + nrm(shape, 0.05)
    L = DEPTH
    D = D_MODEL
    return {
        "x": nrm((BATCH, SEQ, D), 1.0),
        "c": nrm((BATCH, D), 1.0),
        "ctx": nrm((BATCH, CTX_LEN, D), 1.0),
        "c_ctx": nrm((D,), 1.0),
        "mod_w": nrm((L, D, 6 * D), 0.5 * D ** -0.5),
        "mod_b": nrm((L, 6 * D), 0.02),
        "norm1_w": gain((L, D)),
        "w_in": nrm((L, D, IN_COLS), D ** -0.5),
        "q_norm_w": gain((L, HEAD_DIM)),
        "k_norm_w": gain((L, HEAD_DIM)),
        "conv_w": nrm((L, SHORT_CONV, 3 * HYENA_WIDTH), SHORT_CONV ** -0.5),
        "conv_b": nrm((L, 3 * HYENA_WIDTH), 0.02),
        "filt_w1": nrm((L, FILTER_EMB, FILTER_ORDER), FILTER_EMB ** -0.5),
        "filt_b1": nrm((L, FILTER_ORDER), 0.02),
        "filt_w2": nrm((L, FILTER_ORDER, FILTER_ORDER), FILTER_ORDER ** -0.5),
        "filt_b2": nrm((L, FILTER_ORDER), 0.02),
        "filt_w3": nrm((L, FILTER_ORDER, FILTER_ORDER), FILTER_ORDER ** -0.5),
        "filt_b3": nrm((L, FILTER_ORDER), 0.02),
        "filt_w4": nrm((L, FILTER_ORDER, 2 * HYENA_WIDTH), FILTER_ORDER ** -0.5),
        "filt_freq": gain((L, FILTER_ORDER)),
        "hyena_bias": nrm((L, HYENA_WIDTH), 0.5),
        "attn_out_norm_w": gain((L, ATTN_WIDTH)),
        "hyena_out_norm_w": gain((L, HYENA_WIDTH)),
        "w_out": nrm((L, MIX_WIDTH, D), MIX_WIDTH ** -0.5),
        "norm2_w": gain((L, D)),
        "router_w": nrm((L, D, N_EXPERTS), D ** -0.5),
        "router_bias": nrm((L, N_EXPERTS), 0.01),
        "exp_w_gate": nrm((L, N_EXPERTS, D, EXPERT_FF), D ** -0.5),
        "exp_w_up": nrm((L, N_EXPERTS, D, EXPERT_FF), D ** -0.5),
        "exp_w_down": nrm((L, N_EXPERTS, EXPERT_FF, D), EXPERT_FF ** -0.5),
        "sh_w_gate": nrm((L, D, SHARED_FF), D ** -0.5),
        "sh_w_up": nrm((L, D, SHARED_FF), D ** -0.5),
        "sh_w_down": nrm((L, SHARED_FF, D), SHARED_FF ** -0.5),
        "final_norm_w": gain((D,)),
    }


def reference(x, c, ctx, c_ctx, mod_w, mod_b, norm1_w, w_in, q_norm_w, k_norm_w,
              conv_w, conv_b, filt_w1, filt_b1, filt_w2, filt_b2, filt_w3, filt_b3, filt_w4, filt_freq,
              hyena_bias, attn_out_norm_w, hyena_out_norm_w, w_out, norm2_w,
              router_w, router_bias, exp_w_gate, exp_w_up, exp_w_down,
              sh_w_gate, sh_w_up, sh_w_down, final_norm_w):
    B, S, _ = x.shape
    C = ctx.shape[1]
    cos, sin = axial_rope(S)
    ctx_h = ctx
    for i in range(DEPTH):
        last = i == DEPTH - 1
        sh1, sc1, g1, sh2, sc2, g2 = adaln(c, mod_w[i], mod_b[i])
        csh1, csc1, cg1, csh2, csc2, cg2 = adaln(c_ctx, mod_w[i], mod_b[i])
        filt = (filt_w1[i], filt_b1[i], filt_w2[i], filt_b2[i], filt_w3[i], filt_b3[i], filt_w4[i], filt_freq[i])
        moe_w = (router_w[i], router_bias[i], exp_w_gate[i], exp_w_up[i], exp_w_down[i],
                 sh_w_gate[i], sh_w_up[i], sh_w_down[i])

        hc = modulate(ctx_h, norm1_w[i], csh1, csc1)
        if last:
            kv = hc @ w_in[i][:, Q_END:V_END]
            kc = rms_norm(kv[..., :KV_WIDTH].reshape(B, C, N_KV_HEADS, HEAD_DIM), k_norm_w[i])
            vc = kv[..., KV_WIDTH:].reshape(B, C, N_KV_HEADS, HEAD_DIM)
        else:
            qc, kc, vc, uc = project(hc, w_in[i])
            qc = rms_norm(qc, q_norm_w[i])
            kc = rms_norm(kc, k_norm_w[i])
            ac = attention(qc, kc, vc)
            yc = hyena_mix(uc, conv_w[i], conv_b[i], hyena_bias[i], filt)
            ctx_h = ctx_h + cg1 * merge(ac, yc, attn_out_norm_w[i], hyena_out_norm_w[i], w_out[i])
            ctx_h = ctx_h + cg2 * moe_ffn(modulate(ctx_h, norm2_w[i], csh2, csc2), *moe_w)

        h = modulate(x, norm1_w[i], sh1, sc1)
        q, k, v, u = project(h, w_in[i])
        q = apply_rope(rms_norm(q, q_norm_w[i]), cos, sin)
        k = apply_rope(rms_norm(k, k_norm_w[i]), cos, sin)
        a = attention(q, jnp.concatenate([kc, k], axis=1), jnp.concatenate([vc, v], axis=1))
        y = hyena_mix(u, conv_w[i], conv_b[i], hyena_bias[i], filt)
        x = x + g1 * merge(a, y, attn_out_norm_w[i], hyena_out_norm_w[i], w_out[i])
        x = x + g2 * moe_ffn(modulate(x, norm2_w[i], sh2, sc2), *moe_w)
    return rms_norm(x, final_norm_w)
```

```python
import functools
import math

import jax
import jax.numpy as jnp
from jax import lax
from jax.experimental import pallas as pl
from jax.experimental.pallas import tpu as pltpu

F32 = jnp.float32
BF16 = jnp.bfloat16
HIGHEST = lax.Precision.HIGHEST

GRID_W = 64
N_HEADS = 8
N_KV_HEADS = 2
HEAD_DIM = 64
ATTN_WIDTH = N_HEADS * HEAD_DIM
KV_WIDTH = N_KV_HEADS * HEAD_DIM
HYENA_WIDTH = 512
Q_END = ATTN_WIDTH
K_END = Q_END + KV_WIDTH
V_END = K_END + KV_WIDTH
ROPE_THETA = 10000.0
FILTER_EMB = 33
FILTER_DECAY_FAST = 0.3
FILTER_DECAY_SLOW = 1.5
FILTER_TARGET = 1e-2
N_EXPERTS = 256
TOP_K = 8
N_GROUPS = 8
TOPK_GROUPS = 4
ROUTE_SCALE = 2.5
EPS = 1e-6

LANES = 128
EXPERT_ROWS = 256
NEG_INF = float("-inf")


def _params(semantics, vmem_mb=48):
    return pltpu.CompilerParams(dimension_semantics=semantics, vmem_limit_bytes=vmem_mb * 1024 * 1024)


def _rms(x, w):
    return x * lax.rsqrt(jnp.mean(x * x, axis=-1, keepdims=True) + EPS) * w


def _sigmoid(x):
    return 1.0 / (1.0 + jnp.exp(-x))


def _adaln_kernel(c_ref, w_ref, b_ref, o_ref):
    c = c_ref[...]
    s = c * _sigmoid(c)
    o_ref[...] = jnp.dot(s, w_ref[...], precision=HIGHEST, preferred_element_type=F32) + b_ref[...]


def _adaln(cond, w, b):
    rows, d = cond.shape
    n = w.shape[1]
    tn = 1536
    return pl.pallas_call(
        _adaln_kernel,
        out_shape=jax.ShapeDtypeStruct((rows, n), F32),
        grid=(n // tn,),
        in_specs=[pl.BlockSpec((rows, d), lambda j: (0, 0)),
                  pl.BlockSpec((d, tn), lambda j: (0, j)),
                  pl.BlockSpec((1, tn), lambda j: (0, j))],
        out_specs=pl.BlockSpec((rows, tn), lambda j: (0, j)),
        compiler_params=_params(("arbitrary",)),
        name="adaln",
    )(cond, w, b)


def _head_rms(t, gmat, w):
    ms = jnp.dot((t * t).astype(BF16), gmat, preferred_element_type=F32)
    return t * lax.rsqrt(ms + EPS) * w


def _modulated(x, norm_w, shift, scale):
    return _rms(x, norm_w) * (1.0 + scale) + shift


def _ctx_kv_kernel(ctx_ref, mod_ref, n1_ref, w_ref, g_ref, kn_ref, kc_ref, vc_ref):
    x = ctx_ref[0]
    h = _modulated(x, n1_ref[...], mod_ref[0:1, :], mod_ref[1:2, :])
    kv = jnp.dot(h.astype(BF16), w_ref[...], preferred_element_type=F32)
    k = _head_rms(kv[:, :KV_WIDTH], g_ref[...], kn_ref[...])
    v = kv[:, KV_WIDTH:]
    kc_ref[0] = k.astype(BF16)
    vc_ref[0] = jnp.concatenate([v, jnp.ones_like(v)], axis=1).astype(BF16)


def _ctx_kv(ctx, cmod, n1w, w_kv, gk, knw):
    b, c, d = ctx.shape
    return pl.pallas_call(
        _ctx_kv_kernel,
        out_shape=(jax.ShapeDtypeStruct((b, c, KV_WIDTH), BF16),
                   jax.ShapeDtypeStruct((b, c, 2 * KV_WIDTH), BF16)),
        grid=(b,),
        in_specs=[pl.BlockSpec((1, c, d), lambda i: (i, 0, 0)),
                  pl.BlockSpec(cmod.shape, lambda i: (0, 0)),
                  pl.BlockSpec((1, d), lambda i: (0, 0)),
                  pl.BlockSpec(w_kv.shape, lambda i: (0, 0)),
                  pl.BlockSpec(gk.shape, lambda i: (0, 0)),
                  pl.BlockSpec((1, KV_WIDTH), lambda i: (0, 0))],
        out_specs=(pl.BlockSpec((1, c, KV_WIDTH), lambda i: (i, 0, 0)),
                   pl.BlockSpec((1, c, 2 * KV_WIDTH), lambda i: (i, 0, 0))),
        compiler_params=_params(("arbitrary",)),
        name="ctx_kv",
    )(ctx, cmod, n1w, w_kv, gk, knw)


def _rope(t, cos, sin, even):
    width = t.shape[1]
    partner = jnp.where(even, pltpu.roll(t, width - 1, axis=1), pltpu.roll(t, 1, axis=1))
    return t * cos + partner * sin


def _inproj_kernel(x_ref, mod_ref, n1_ref, w_ref, gq_ref, qn_ref, kn_ref, cos_ref, sin_ref,
                   q_ref, k_ref, v_ref, u_ref):
    x = x_ref[0]
    h = _modulated(x, n1_ref[...], mod_ref[0, 0:1, :], mod_ref[0, 1:2, :])
    p = jnp.dot(h.astype(BF16), w_ref[...], preferred_element_type=F32)
    gq = gq_ref[...]
    q = _head_rms(p[:, :Q_END], gq, qn_ref[...])
    k = _head_rms(p[:, Q_END:K_END], gq[:KV_WIDTH, :KV_WIDTH], kn_ref[...])
    v = p[:, K_END:V_END]
    cos = cos_ref[...]
    sin = sin_ref[...]
    reps = Q_END // LANES
    cos_q = jnp.concatenate([cos] * reps, axis=1)
    sin_q = jnp.concatenate([sin] * reps, axis=1)
    even_q = (lax.broadcasted_iota(jnp.int32, (1, Q_END), 1) & 1) == 0
    even_k = (lax.broadcasted_iota(jnp.int32, (1, KV_WIDTH), 1) & 1) == 0
    q = _rope(q, cos_q, sin_q, even_q) * (HEAD_DIM ** -0.5)
    k = _rope(k, cos, sin, even_k)
    q_ref[0] = q.astype(BF16)
    k_ref[0] = k.astype(BF16)
    v_ref[0] = jnp.concatenate([v, jnp.ones_like(v)], axis=1).astype(BF16)
    u_ref[0] = p[:, V_END:]


def _inproj(x, mods, n1w, w_in, gq, qnw, knw, cos, sin, tl):
    b, l, d = x.shape
    ncol = w_in.shape[1]
    nu = ncol - V_END
    return pl.pallas_call(
        _inproj_kernel,
        out_shape=(jax.ShapeDtypeStruct((b, l, Q_END), BF16),
                   jax.ShapeDtypeStruct((b, l, KV_WIDTH), BF16),
                   jax.ShapeDtypeStruct((b, l, 2 * KV_WIDTH), BF16),
                   jax.ShapeDtypeStruct((b, l, nu), F32)),
        grid=(l // tl, b),
        in_specs=[pl.BlockSpec((1, tl, d), lambda i, j: (j, i, 0)),
                  pl.BlockSpec((1, 6, d), lambda i, j: (j, 0, 0)),
                  pl.BlockSpec((1, d), lambda i, j: (0, 0)),
                  pl.BlockSpec((d, ncol), lambda i, j: (0, 0)),
                  pl.BlockSpec(gq.shape, lambda i, j: (0, 0)),
                  pl.BlockSpec((1, Q_END), lambda i, j: (0, 0)),
                  pl.BlockSpec((1, KV_WIDTH), lambda i, j: (0, 0)),
                  pl.BlockSpec((tl, LANES), lambda i, j: (i, 0)),
                  pl.BlockSpec((tl, LANES), lambda i, j: (i, 0))],
        out_specs=(pl.BlockSpec((1, tl, Q_END), lambda i, j: (j, i, 0)),
                   pl.BlockSpec((1, tl, KV_WIDTH), lambda i, j: (j, i, 0)),
                   pl.BlockSpec((1, tl, 2 * KV_WIDTH), lambda i, j: (j, i, 0)),
                   pl.BlockSpec((1, tl, nu), lambda i, j: (j, i, 0))),
        compiler_params=_params(("arbitrary", "arbitrary")),
        name="inproj",
    )(x, mods, n1w, w_in, gq, qnw, knw, cos, sin)


def _attn_kernel(q_ref, k_ref, v_ref, kc_ref, vc_ref, wn_ref, o_ref, *, chunk):
    n_chunks = k_ref.shape[1] // chunk
    low = lax.broadcasted_iota(jnp.int32, (1, LANES), 1) < HEAD_DIM
    nt = (((1,), (1,)), ((), ()))

    def step(qh, kk, vv, m, acc):
        s = lax.dot_general(qh, kk, nt, preferred_element_type=F32)
        mc = jnp.max(s, axis=-1, keepdims=True)
        m_new = mc if m is None else jnp.maximum(m, mc)
        p = jnp.exp(s - m_new).astype(BF16)
        pv = jnp.dot(p, vv, preferred_element_type=F32)
        if m is None:
            return m_new, pv
        return m_new, acc * jnp.exp(m - m_new) + pv

    outs = []
    for j in range(Q_END // LANES):
        qv = q_ref[0, :, LANES * j:LANES * (j + 1)]
        halves = []
        for g in range(N_KV_HEADS):
            qh = jnp.where(low if g == 0 else jnp.logical_not(low), qv, jnp.zeros_like(qv))
            m, acc = step(qh, kc_ref[0], vc_ref[0], None, None)

            def body(c, carry, qh=qh):
                off = pl.multiple_of(c * chunk, chunk)
                return step(qh, k_ref[0, pl.ds(off, chunk), :], v_ref[0, pl.ds(off, chunk), :], *carry)

            m, acc = lax.fori_loop(0, n_chunks, body, (m, acc))
            halves.append(acc[:, :LANES] / acc[:, LANES:])
        outs.append(jnp.where(low, halves[0], halves[1]))
    a = jnp.concatenate(outs, axis=1)
    o_ref[0] = _rms(a, wn_ref[...]).astype(BF16)


def _attention(q, k, v, kc, vc, wn, tq, chunk):
    b, l, _ = q.shape
    c = kc.shape[1]
    return pl.pallas_call(
        functools.partial(_attn_kernel, chunk=chunk),
        out_shape=jax.ShapeDtypeStruct((b, l, Q_END), BF16),
        grid=(b, l // tq),
        in_specs=[pl.BlockSpec((1, tq, Q_END), lambda i, j: (i, j, 0)),
                  pl.BlockSpec((1, l, KV_WIDTH), lambda i, j: (i, 0, 0)),
                  pl.BlockSpec((1, l, 2 * KV_WIDTH), lambda i, j: (i, 0, 0)),
                  pl.BlockSpec((1, c, KV_WIDTH), lambda i, j: (i, 0, 0)),
                  pl.BlockSpec((1, c, 2 * KV_WIDTH), lambda i, j: (i, 0, 0)),
                  pl.BlockSpec((1, Q_END), lambda i, j: (0, 0))],
        out_specs=pl.BlockSpec((1, tq, Q_END), lambda i, j: (i, j, 0)),
        compiler_params=_params(("arbitrary", "arbitrary")),
        name="attn",
    )(q, k, v, kc, vc, wn)


def _hyena_pre_kernel(u0_ref, u1_ref, u2_ref, cw_ref, cb_ref, g32_ref, gbf_ref, x0_ref):
    l = u0_ref.shape[1]
    row = lax.broadcasted_iota(jnp.int32, (l, LANES), 0)

    def conv(u_ref, gi):
        u = u_ref[0]
        prev = jnp.where(row == 0, 0.0, pltpu.roll(u, 1, axis=0))
        nxt = jnp.where(row == l - 1, 0.0, pltpu.roll(u, l - 1, axis=0))
        w = cw_ref[gi]
        return w[0:1] * prev + w[1:2] * u + w[2:3] * nxt + cb_ref[gi:gi + 1, :]

    x0 = conv(u0_ref, 0)
    x1 = conv(u1_ref, 1)
    v = conv(u2_ref, 2)
    g = v * x1
    g32_ref[0] = g
    gbf_ref[0] = g.astype(BF16)
    x0_ref[0] = x0


def _hyena_pre(u, cw, cb):
    b, l, _ = u.shape
    nblk = HYENA_WIDTH // LANES
    ublk = lambda gi: pl.BlockSpec((1, l, LANES), lambda i, j: (i, 0, gi * nblk + j))
    oblk = pl.BlockSpec((1, l, LANES), lambda i, j: (i, 0, j))
    return pl.pallas_call(
        _hyena_pre_kernel,
        out_shape=(jax.ShapeDtypeStruct((b, l, HYENA_WIDTH), F32),
                   jax.ShapeDtypeStruct((b, l, HYENA_WIDTH), BF16),
                   jax.ShapeDtypeStruct((b, l, HYENA_WIDTH), F32)),
        grid=(b, nblk),
        in_specs=[ublk(0), ublk(1), ublk(2),
                  pl.BlockSpec((3, 3, LANES), lambda i, j: (0, 0, j)),
                  pl.BlockSpec((3, LANES), lambda i, j: (0, j))],
        out_specs=(oblk, oblk, oblk),
        compiler_params=_params(("arbitrary", "arbitrary")),
        name="hyena_pre",
    )(u, u, u, cw, cb)


def _filter_kernel(z_ref, w1_ref, b1_ref, w2_ref, b2_ref, w3_ref, b3_ref, w4_ref, fr_ref, dl_ref, o_ref):
    tl = z_ref.shape[0]
    z = z_ref[...]
    fr = fr_ref[...]
    dot = lambda a, w: jnp.dot(a, w, precision=HIGHEST, preferred_element_type=F32)
    h = jnp.sin(fr * (dot(z, w1_ref[...]) + b1_ref[...]))
    h = jnp.sin(fr * (dot(h, w2_ref[...]) + b2_ref[...]))
    h = jnp.sin(fr * (dot(h, w3_ref[...]) + b3_ref[...]))
    h = dot(h, w4_ref[...])
    t = z[:, 0:1]
    decay = jnp.exp(-t * jnp.abs(dl_ref[...]))
    hf = h[:, :HYENA_WIDTH] * decay
    hb = h[:, HYENA_WIDTH:] * decay
    row = lax.broadcasted_iota(jnp.int32, (tl, HYENA_WIDTH), 0) + pl.program_id(0) * tl
    hb = jnp.where(row == 0, 0.0, hb)
    o_ref[0] = hf + hb
    o_ref[1] = hf - hb


def _hyena_filter(z, w1, b1, w2, b2, w3, b3, w4, freq, deltas, tl):
    l = z.shape[0]
    full = lambda a: pl.BlockSpec(a.shape, lambda i: (0,) * a.ndim)
    return pl.pallas_call(
        _filter_kernel,
        out_shape=jax.ShapeDtypeStruct((2, l, HYENA_WIDTH), F32),
        grid=(l // tl,),
        in_specs=[pl.BlockSpec((tl, z.shape[1]), lambda i: (i, 0)),
                  full(w1), full(b1), full(w2), full(b2), full(w3), full(b3), full(w4), full(freq), full(deltas)],
        out_specs=pl.BlockSpec((2, tl, HYENA_WIDTH), lambda i: (0, i, 0)),
        compiler_params=_params(("arbitrary",)),
        name="hyena_filter",
    )(z, w1, b1, w2, b2, w3, b3, w4, freq, deltas)


def _dft_kernel(f_ref, x_ref, o_ref):
    o_ref[0] = jnp.dot(f_ref[...], x_ref[0].astype(BF16), preferred_element_type=F32)


def _dft(fmat, x, tf):
    nb, l, w = x.shape
    n = fmat.shape[0]
    return pl.pallas_call(
        _dft_kernel,
        out_shape=jax.ShapeDtypeStruct((nb, n, w), F32),
        grid=(n // tf, nb),
        in_specs=[pl.BlockSpec((tf, l), lambda i, j: (i, 0)),
                  pl.BlockSpec((1, l, w), lambda i, j: (j, 0, 0))],
        out_specs=pl.BlockSpec((1, tf, w), lambda i, j: (j, i, 0)),
        compiler_params=_params(("arbitrary", "arbitrary")),
        name="dft_filter",
    )(fmat, x)


def _dft_mul_kernel(fr_ref, fi_ref, x_ref, a_ref, b_ref, d_ref, zr_ref, zi_ref):
    x = x_ref[0]
    xr = jnp.dot(fr_ref[...], x, preferred_element_type=F32)
    xi = jnp.dot(fi_ref[...], x, preferred_element_type=F32)
    bb = b_ref[...]
    zr_ref[0] = (xr * a_ref[...] - xi * bb).astype(BF16)
    zi_ref[0] = (xr * bb + xi * d_ref[...]).astype(BF16)


def _dft_mul(fmat, g, sa, sb, sd, tf):
    b, l, w = g.shape
    nf = l // tf
    spec = pl.BlockSpec((tf, w), lambda i, j: (i, 0))
    return pl.pallas_call(
        _dft_mul_kernel,
        out_shape=(jax.ShapeDtypeStruct((b, l, w), BF16), jax.ShapeDtypeStruct((b, l, w), BF16)),
        grid=(nf, b),
        in_specs=[pl.BlockSpec((tf, l), lambda i, j: (i, 0)),
                  pl.BlockSpec((tf, l), lambda i, j: (i + nf, 0)),
                  pl.BlockSpec((1, l, w), lambda i, j: (j, 0, 0)),
                  spec, spec, spec],
        out_specs=(pl.BlockSpec((1, tf, w), lambda i, j: (j, i, 0)),
                   pl.BlockSpec((1, tf, w), lambda i, j: (j, i, 0))),
        compiler_params=_params(("arbitrary", "arbitrary")),
        name="dft_mul",
    )(fmat, fmat, g, sa, sb, sd)


def _idft_kernel(fr_ref, fi_ref, zr_ref, zi_ref, g_ref, x0_ref, hb_ref, wn_ref, o_ref):
    conv = (jnp.dot(fr_ref[...], zr_ref[0], preferred_element_type=F32)
            + jnp.dot(fi_ref[...], zi_ref[0], preferred_element_type=F32))
    y = (conv + g_ref[0] * hb_ref[...]) * x0_ref[0]
    o_ref[0] = _rms(y, wn_ref[...]).astype(BF16)


def _idft(finv_r, finv_i, zr, zi, g32, x0, hbias, wn, tt):
    b, l, w = zr.shape
    tile = pl.BlockSpec((1, tt, w), lambda i, j: (j, i, 0))
    return pl.pallas_call(
        _idft_kernel,
        out_shape=jax.ShapeDtypeStruct((b, l, w), BF16),
        grid=(l // tt, b),
        in_specs=[pl.BlockSpec((tt, l), lambda i, j: (i, 0)),
                  pl.BlockSpec((tt, l), lambda i, j: (i, 0)),
                  pl.BlockSpec((1, l, w), lambda i, j: (j, 0, 0)),
                  pl.BlockSpec((1, l, w), lambda i, j: (j, 0, 0)),
                  tile, tile,
                  pl.BlockSpec((1, w), lambda i, j: (0, 0)),
                  pl.BlockSpec((1, w), lambda i, j: (0, 0))],
        out_specs=tile,
        compiler_params=_params(("arbitrary", "arbitrary")),
        name="idft",
    )(finv_r, finv_i, zr, zi, g32, x0, hbias, wn)


def _merge_kernel(a_ref, y_ref, x_ref, mod_ref, wa_ref, wy_ref, n2_ref, sg_ref, su_ref, sd_ref,
                  base_ref, hi_ref, lo_ref):
    m = (jnp.dot(a_ref[0], wa_ref[...], preferred_element_type=F32)
         + jnp.dot(y_ref[0], wy_ref[...], preferred_element_type=F32))
    x1 = x_ref[0] + mod_ref[0, 2:3, :] * m
    h2 = _modulated(x1, n2_ref[...], mod_ref[0, 3:4, :], mod_ref[0, 4:5, :])
    hi = h2.astype(BF16)
    hi_ref[0] = hi
    lo_ref[0] = (h2 - hi.astype(F32)).astype(BF16)
    gate = jnp.dot(hi, sg_ref[...], preferred_element_type=F32)
    up = jnp.dot(hi, su_ref[...], preferred_element_type=F32)
    act = (gate * _sigmoid(gate) * up).astype(BF16)
    shared = jnp.dot(act, sd_ref[...], preferred_element_type=F32)
    base_ref[0] = x1 + mod_ref[0, 5:6, :] * shared


def _merge(an, yn, x, mods, wa, wy, n2w, sg, su, sd, tl):
    b, l, d = x.shape
    full = lambda a: pl.BlockSpec(a.shape, lambda i, j: (0,) * a.ndim)
    half = pl.BlockSpec((1, tl, an.shape[2]), lambda i, j: (i, j, 0))
    wide = pl.BlockSpec((1, tl, d), lambda i, j: (i, j, 0))
    return pl.pallas_call(
        _merge_kernel,
        out_shape=(jax.ShapeDtypeStruct((b, l, d), F32),
                   jax.ShapeDtypeStruct((b, l, d), BF16),
                   jax.ShapeDtypeStruct((b, l, d), BF16)),
        grid=(b, l // tl),
        in_specs=[half, half, wide,
                  pl.BlockSpec((1, 6, d), lambda i, j: (i, 0, 0)),
                  full(wa), full(wy), full(n2w), full(sg), full(su), full(sd)],
        out_specs=(wide, wide, wide),
        compiler_params=_params(("arbitrary", "arbitrary")),
        name="merge",
    )(an, yn, x, mods, wa, wy, n2w, sg, su, sd)


def _router_kernel(hi_ref, lo_ref, whi_ref, wlo_ref, bias_ref, tri_ref,
                   idx_ref, wgt_ref, rank_ref, cnt_ref, run_ref):
    tt = hi_ref.shape[0]
    per_group = N_EXPERTS // N_GROUPS

    @pl.when(pl.program_id(0) == 0)
    def _():
        run_ref[...] = jnp.zeros_like(run_ref)

    nt = (((1,), (1,)), ((), ()))
    hi = hi_ref[...]
    whi = whi_ref[...]
    logits = (lax.dot_general(whi, hi, nt, preferred_element_type=F32)
              + lax.dot_general(whi, lo_ref[...], nt, preferred_element_type=F32)
              + lax.dot_general(wlo_ref[...], hi, nt, preferred_element_type=F32))
    scores = _sigmoid(logits)
    biased = scores + bias_ref[...]

    ridx = lax.broadcasted_iota(jnp.int32, (per_group, tt), 0)
    groups = [biased[g * per_group:(g + 1) * per_group, :] for g in range(N_GROUPS)]
    gs = []
    for blk in groups:
        m1 = jnp.max(blk, axis=0, keepdims=True)
        i1 = jnp.min(jnp.where(blk == m1, ridx, per_group), axis=0, keepdims=True)
        m2 = jnp.max(jnp.where(ridx == i1, NEG_INF, blk), axis=0, keepdims=True)
        gs.append(m1 + m2)

    kept = []
    for g in range(N_GROUPS):
        ahead = jnp.zeros((1, tt), F32)
        for o in range(N_GROUPS):
            if o != g:
                wins = (gs[o] >= gs[g]) if o < g else (gs[o] > gs[g])
                ahead = ahead + jnp.where(wins, 1.0, 0.0)
        kept.append(jnp.where(ahead < TOPK_GROUPS, groups[g], NEG_INF))
    cur = jnp.concatenate(kept, axis=0)

    eidx = lax.broadcasted_iota(jnp.int32, cur.shape, 0)
    onehot = jnp.zeros(cur.shape, F32)
    picks = []
    wsel = []
    for _ in range(TOP_K):
        mx = jnp.max(cur, axis=0, keepdims=True)
        first = jnp.min(jnp.where(cur == mx, eidx, N_EXPERTS), axis=0, keepdims=True)
        sel = eidx == first
        picks.append(first)
        wsel.append(jnp.sum(jnp.where(sel, scores, 0.0), axis=0, keepdims=True))
        onehot = jnp.where(sel, 1.0, onehot)
        cur = jnp.where(sel, NEG_INF, cur)
    w = jnp.concatenate(wsel, axis=0)
    w = w / jnp.sum(w, axis=0, keepdims=True) * ROUTE_SCALE
    idx = jnp.concatenate(picks, axis=0)

    oh = onehot.astype(BF16)
    before = jnp.dot(oh, tri_ref[0], preferred_element_type=F32)
    total = jnp.dot(oh, tri_ref[1], preferred_element_type=F32)
    pos = run_ref[...] + before
    ranks = [jnp.sum(jnp.where(eidx == p, pos, 0.0), axis=0, keepdims=True) for p in picks]
    run_ref[...] = run_ref[...] + total

    idx_ref[...] = idx
    wgt_ref[...] = w
    rank_ref[...] = jnp.concatenate(ranks, axis=0).astype(jnp.int32)
    cnt_ref[...] = run_ref[...]


def _router(hi, lo, whi, wlo, bias, tri, tt):
    t, d = hi.shape
    tok = pl.BlockSpec((tt, d), lambda i: (i, 0))
    full = lambda a: pl.BlockSpec(a.shape, lambda i: (0,) * a.ndim)
    out = pl.BlockSpec((TOP_K, tt), lambda i: (0, i))
    return pl.pallas_call(
        _router_kernel,
        out_shape=(jax.ShapeDtypeStruct((TOP_K, t), jnp.int32),
                   jax.ShapeDtypeStruct((TOP_K, t), F32),
                   jax.ShapeDtypeStruct((TOP_K, t), jnp.int32),
                   jax.ShapeDtypeStruct((N_EXPERTS, tt), F32)),
        grid=(t // tt,),
        in_specs=[tok, tok, full(whi), full(wlo), full(bias), full(tri)],
        out_specs=(out, out, out, pl.BlockSpec((N_EXPERTS, tt), lambda i: (0, 0))),
        scratch_shapes=[pltpu.VMEM((N_EXPERTS, tt), F32)],
        compiler_params=_params(("arbitrary",)),
        name="router",
    )(hi, lo, whi, wlo, bias, tri)


def _dispatch_kernel(dest_ref, h_ref, xs_in_ref, xs_ref, sem):
    del xs_in_ref
    td = dest_ref.shape[1]
    base = pl.program_id(0) * td

    def row_copy(t, k):
        return pltpu.make_async_copy(h_ref.at[pl.ds(base + t, 1), :],
                                     xs_ref.at[pl.ds(dest_ref[k, t], 1), :], sem)

    def issue(t, carry):
        for k in range(TOP_K):
            row_copy(t, k).start()
        return carry

    lax.fori_loop(0, td, issue, 0)

    def drain(t, carry):
        for k in range(TOP_K):
            row_copy(t, k).wait()
        return carry

    lax.fori_loop(0, td, drain, 0)


def _dispatch(dest, h, xs0, td):
    t, d = h.shape
    return pl.pallas_call(
        _dispatch_kernel,
        out_shape=jax.ShapeDtypeStruct(xs0.shape, xs0.dtype),
        grid=(t // td,),
        in_specs=[pl.BlockSpec((TOP_K, td), lambda i: (0, i), memory_space=pltpu.SMEM),
                  pl.BlockSpec(memory_space=pl.ANY),
                  pl.BlockSpec(memory_space=pl.ANY)],
        out_specs=pl.BlockSpec(memory_space=pl.ANY),
        scratch_shapes=[pltpu.SemaphoreType.DMA],
        input_output_aliases={2: 0},
        compiler_params=_params(("arbitrary",)),
        name="dispatch",
    )(dest, h, xs0)


def _experts_kernel(be_ref, nu_ref, x_ref, wg_ref, wu_ref, wd_ref, y_ref):
    del be_ref

    @pl.when(pl.program_id(0) < nu_ref[0])
    def _():
        x = x_ref[...].astype(BF16)
        gate = jnp.dot(x, wg_ref[0], preferred_element_type=F32)
        up = jnp.dot(x, wu_ref[0], preferred_element_type=F32)
        act = (gate * _sigmoid(gate) * up).astype(BF16)
        y_ref[...] = jnp.dot(act, wd_ref[0], preferred_element_type=F32)


def _experts(blk_e, n_used, xs, wg, wu, wd):
    n_rows, d = xs.shape
    ff = wg.shape[2]
    nblk = n_rows // EXPERT_ROWS
    row = lambda i, be, nu: (jnp.minimum(i, nu[0] - 1), 0)
    wsel = lambda i, be, nu: (be[jnp.minimum(i, nu[0] - 1)], 0, 0)
    return pl.pallas_call(
        _experts_kernel,
        out_shape=jax.ShapeDtypeStruct((n_rows, d), F32),
        grid_spec=pltpu.PrefetchScalarGridSpec(
            num_scalar_prefetch=2,
            grid=(nblk,),
            in_specs=[pl.BlockSpec((EXPERT_ROWS, d), row),
                      pl.BlockSpec((1, d, ff), wsel),
                      pl.BlockSpec((1, d, ff), wsel),
                      pl.BlockSpec((1, ff, d), wsel)],
            out_specs=pl.BlockSpec((EXPERT_ROWS, d), row)),
        compiler_params=_params(("arbitrary",)),
        name="experts",
    )(blk_e, n_used, xs, wg, wu, wd)


def _combine_kernel(dest_ref, ys_ref, w_ref, base_ref, mod_ref, fw_ref, o_ref, buf, sem):
    tc = dest_ref.shape[1]

    def row_copy(t, k):
        return pltpu.make_async_copy(ys_ref.at[pl.ds(dest_ref[k, t], 1), :],
                                     buf.at[k, pl.ds(t, 1), :], sem)

    def issue(t, carry):
        for k in range(TOP_K):
            row_copy(t, k).start()
        return carry

    lax.fori_loop(0, tc, issue, 0)

    def drain(t, carry):
        for k in range(TOP_K):
            row_copy(t, k).wait()
        return carry

    lax.fori_loop(0, tc, drain, 0)

    w = w_ref[...]
    routed = w[:, 0:1] * buf[0]
    for k in range(1, TOP_K):
        routed = routed + w[:, k:k + 1] * buf[k]
    x = base_ref[0] + mod_ref[0, 5:6, :] * routed
    o_ref[0] = _rms(x, fw_ref[...])


def _combine(dest, ys, wt, base, mods, fw, tc):
    b, l, d = base.shape
    per_b = l // tc
    return pl.pallas_call(
        _combine_kernel,
        out_shape=jax.ShapeDtypeStruct((b, l, d), F32),
        grid=(b * per_b,),
        in_specs=[pl.BlockSpec((TOP_K, tc), lambda i: (0, i), memory_space=pltpu.SMEM),
                  pl.BlockSpec(memory_space=pl.ANY),
                  pl.BlockSpec((tc, TOP_K), lambda i: (i, 0)),
                  pl.BlockSpec((1, tc, d), lambda i: (i // per_b, i % per_b, 0)),
                  pl.BlockSpec((1, 6, d), lambda i: (i // per_b, 0, 0)),
                  pl.BlockSpec((1, d), lambda i: (0, 0))],
        out_specs=pl.BlockSpec((1, tc, d), lambda i: (i // per_b, i % per_b, 0)),
        scratch_shapes=[pltpu.VMEM((TOP_K, tc, d), F32), pltpu.SemaphoreType.DMA],
        compiler_params=_params(("arbitrary",)),
        name="combine",
    )(dest, ys, wt, base, mods, fw)


def _rope_tables(l):
    t = jnp.arange(l, dtype=jnp.int32)
    row = (t // GRID_W).astype(F32)
    col = (t % GRID_W).astype(F32)
    n_freq = HEAD_DIM // 4
    inv = ROPE_THETA ** (-jnp.arange(n_freq, dtype=F32) / n_freq)
    ang = jnp.concatenate([row[:, None] * inv, col[:, None] * inv], axis=-1)
    cos = jnp.repeat(jnp.cos(ang), 2, axis=1)
    sin = jnp.repeat(jnp.sin(ang), 2, axis=1)
    sign = jnp.tile(jnp.array([-1.0, 1.0], F32), HEAD_DIM // 2)
    reps = LANES // HEAD_DIM
    return jnp.tile(cos, (1, reps)), jnp.tile(sin * sign, (1, reps))


def _filter_features(l):
    t = jnp.linspace(0.0, 1.0, l, dtype=F32)[:, None]
    bands = (FILTER_EMB - 1) // 2
    w = 2.0 * math.pi * jnp.arange(l, dtype=F32)[:, None] / l
    f = jnp.linspace(1e-4, bands - 1, bands, dtype=F32)[None, :]
    z = jnp.concatenate([t, jnp.cos(f * w), -jnp.sin(f * w)], axis=-1)
    min_decay = math.log(FILTER_TARGET) / FILTER_DECAY_FAST
    max_decay = math.log(FILTER_TARGET) / FILTER_DECAY_SLOW
    deltas = jnp.linspace(min_decay, max_decay, HYENA_WIDTH, dtype=F32)[None, :]
    return jnp.pad(z, ((0, 0), (0, LANES - FILTER_EMB))), deltas


def _dft_matrices(l):
    n = 2 * l
    idx = jnp.arange(l, dtype=jnp.int32)
    prod = (idx[:, None] * idx[None, :]) % n
    ang = prod.astype(F32) * (2.0 * math.pi / n)
    c = jnp.cos(ang)
    s = jnp.sin(ang)
    alt = jnp.where(idx % 2 == 0, 1.0, -1.0).astype(F32)
    first = (idx == 0)[:, None]
    fwd = jnp.concatenate([c, jnp.where(first, alt[None, :], -s)], axis=0).astype(BF16)
    firstc = (idx == 0)[None, :]
    inv_r = (jnp.where(firstc, 1.0, 2.0) * c / n).astype(BF16)
    inv_i = (jnp.where(firstc, alt[:, None], -2.0 * s) / n).astype(BF16)
    return fwd, inv_r, inv_i


def _head_perm():
    order = []
    for j in range(N_HEADS // 2):
        order += list(range(j * HEAD_DIM, (j + 1) * HEAD_DIM))
        order += list(range((j + N_HEADS // 2) * HEAD_DIM, (j + 1 + N_HEADS // 2) * HEAD_DIM))
    return jnp.array(order, jnp.int32)


def _pad2(a, rows, cols):
    return jnp.pad(a, ((0, rows - a.shape[0]), (0, cols - a.shape[1])))


def kernel(x, c, ctx, c_ctx, mod_w, mod_b, norm1_w, w_in, q_norm_w, k_norm_w, conv_w, conv_b, filt_w1, filt_b1, filt_w2, filt_b2, filt_w3, filt_b3, filt_w4, filt_freq, hyena_bias, attn_out_norm_w, hyena_out_norm_w, w_out, norm2_w, router_w, router_bias, exp_w_gate, exp_w_up, exp_w_down, sh_w_gate, sh_w_up, sh_w_down, final_norm_w):
    b, l, d = x.shape
    t = b * l
    assert mod_w.shape[0] == 1, "single-layer stack"
    tl = min(512, l)

    cond = jnp.concatenate([c, c_ctx[None, :], jnp.zeros((-(b + 1) % 8, d), F32)], axis=0)
    mod = _adaln(cond, mod_w[0], mod_b[0][None, :])
    mods = mod[:b].reshape(b, 6, d)
    cmod = mod[b].reshape(6, d)

    perm = _head_perm()
    w_in0 = w_in[0]
    w_in_k = jnp.concatenate([w_in0[:, :Q_END][:, perm], w_in0[:, Q_END:]], axis=1).astype(BF16)
    w_kv = w_in0[:, Q_END:V_END].astype(BF16)
    gq = jnp.kron(jnp.eye(N_HEADS, dtype=F32), jnp.full((HEAD_DIM, HEAD_DIM), 1.0 / HEAD_DIM, F32)).astype(BF16)
    qnw = jnp.tile(q_norm_w[0], N_HEADS)[None, :]
    knw = jnp.tile(k_norm_w[0], N_KV_HEADS)[None, :]
    n1w = norm1_w[0][None, :]
    cos, sin = _rope_tables(l)

    kc, vc = _ctx_kv(ctx, cmod, n1w, w_kv, gq[:KV_WIDTH, :KV_WIDTH], knw)
    q, k, v, u = _inproj(x, mods, n1w, w_in_k, gq, qnw, knw, cos, sin, tl)
    an = _attention(q, k, v, kc, vc, attn_out_norm_w[0][perm][None, :], min(256, l), min(512, l))

    cw = conv_w[0].reshape(3, 3, HYENA_WIDTH).transpose(1, 0, 2)
    cb = conv_b[0].reshape(3, HYENA_WIDTH)
    g32, gbf, x0 = _hyena_pre(u, cw, cb)
    z, deltas = _filter_features(l)
    fo = filt_w2.shape[1]
    hsd = _hyena_filter(
        z, _pad2(filt_w1[0], LANES, LANES), _pad2(filt_b1[0][None, :], 1, LANES),
        _pad2(filt_w2[0], LANES, LANES), _pad2(filt_b2[0][None, :], 1, LANES),
        _pad2(filt_w3[0], LANES, LANES), _pad2(filt_b3[0][None, :], 1, LANES),
        _pad2(filt_w4[0], LANES, 2 * HYENA_WIDTH), _pad2(filt_freq[0][None, :], 1, LANES), deltas, tl)
    del fo
    fwd, inv_r, inv_i = _dft_matrices(l)
    spec = _dft(fwd, hsd, tl)
    row0 = (jnp.arange(l) == 0)[:, None]
    sa = spec[0, :l]
    sd = jnp.where(row0, spec[0, l:l + 1], sa)
    sb = jnp.where(row0, 0.0, spec[1, l:])
    zr, zi = _dft_mul(fwd, gbf, sa, sb, sd, tl)
    yn = _idft(inv_r, inv_i, zr, zi, g32, x0, hyena_bias[0][None, :], hyena_out_norm_w[0][None, :], tl)

    w_out0 = w_out[0]
    base, h2hi, h2lo = _merge(
        an, yn, x, mods, w_out0[:ATTN_WIDTH][perm].astype(BF16), w_out0[ATTN_WIDTH:].astype(BF16),
        norm2_w[0][None, :], sh_w_gate[0].astype(BF16), sh_w_up[0].astype(BF16), sh_w_down[0].astype(BF16), tl)

    tt = 256
    rwt = router_w[0].T
    rw_hi = rwt.astype(BF16)
    rw_lo = (rwt - rw_hi.astype(F32)).astype(BF16)
    bias = jnp.broadcast_to(router_bias[0][:, None], (N_EXPERTS, tt))
    ti = jnp.arange(tt)
    tri = jnp.stack([(ti[:, None] < ti[None, :]), jnp.ones((tt, tt), bool)]).astype(BF16)
    idx, wgt, rank, cnt = _router(h2hi.reshape(t, d), h2lo.reshape(t, d), rw_hi, rw_lo, bias, tri, tt)

    counts = cnt[:, 0].astype(jnp.int32)
    padded = (counts + EXPERT_ROWS - 1) // EXPERT_ROWS * EXPERT_ROWS
    pad_end = jnp.cumsum(padded)
    pad_start = pad_end - padded
    dest = pad_start[idx] + rank
    n_rows = (t * TOP_K + N_EXPERTS * (EXPERT_ROWS - 1) + EXPERT_ROWS - 1) // EXPERT_ROWS * EXPERT_ROWS
    nblk = n_rows // EXPERT_ROWS
    blk_e = jnp.minimum(jnp.searchsorted(pad_end, jnp.arange(nblk, dtype=jnp.int32) * EXPERT_ROWS, side="right"),
                        N_EXPERTS - 1).astype(jnp.int32)
    n_used = (pad_end[-1:] // EXPERT_ROWS).astype(jnp.int32)

    xs = _dispatch(dest, h2hi.reshape(t, d).astype(F32), jnp.zeros((n_rows, d), F32), min(512, t))
    ys = _experts(blk_e, n_used, xs, exp_w_gate[0].astype(BF16), exp_w_up[0].astype(BF16),
                  exp_w_down[0].astype(BF16))
    return _combine(dest, ys, wgt.T, base, mods, final_norm_w[None, :], min(128, l))
```

```python
import functools
import math

import jax
import jax.numpy as jnp
from jax import lax
from jax.experimental import pallas as pl
from jax.experimental.pallas import tpu as pltpu

F32 = jnp.float32
BF16 = jnp.bfloat16
HIGHEST = lax.Precision.HIGHEST

GRID_W = 64
N_HEADS = 8
N_KV_HEADS = 2
HEAD_DIM = 64
ATTN_WIDTH = N_HEADS * HEAD_DIM
KV_WIDTH = N_KV_HEADS * HEAD_DIM
HYENA_WIDTH = 512
Q_END = ATTN_WIDTH
K_END = Q_END + KV_WIDTH
V_END = K_END + KV_WIDTH
ROPE_THETA = 10000.0
FILTER_EMB = 33
FILTER_DECAY_FAST = 0.3
FILTER_DECAY_SLOW = 1.5
FILTER_TARGET = 1e-2
N_EXPERTS = 256
TOP_K = 8
N_GROUPS = 8
TOPK_GROUPS = 4
ROUTE_SCALE = 2.5
EPS = 1e-6

LANES = 128
SUBLANES = 8
EXPERT_ROWS = 256
NEG_INF = float("-inf")


def _params(semantics, vmem_mb=48):
    return pltpu.CompilerParams(dimension_semantics=semantics, vmem_limit_bytes=vmem_mb * 1024 * 1024)


def _rms(x, w):
    return x * lax.rsqrt(jnp.mean(x * x, axis=-1, keepdims=True) + EPS) * w


def _sigmoid(x):
    return 1.0 / (1.0 + jnp.exp(-x))


def _adaln_kernel(c_ref, w_ref, b_ref, o_ref):
    c = c_ref[...]
    s = c * _sigmoid(c)
    o_ref[...] = jnp.dot(s, w_ref[...], precision=HIGHEST, preferred_element_type=F32) + b_ref[...]


def _adaln(cond, w, b):
    rows, d = cond.shape
    n = w.shape[1]
    tn = 1536
    return pl.pallas_call(
        _adaln_kernel,
        out_shape=jax.ShapeDtypeStruct((rows, n), F32),
        grid=(n // tn,),
        in_specs=[pl.BlockSpec((rows, d), lambda j: (0, 0)),
                  pl.BlockSpec((d, tn), lambda j: (0, j)),
                  pl.BlockSpec((1, tn), lambda j: (0, j))],
        out_specs=pl.BlockSpec((rows, tn), lambda j: (0, j)),
        compiler_params=_params(("arbitrary",)),
        name="adaln",
    )(cond, w, b)


def _head_rms(t, gmat, w):
    ms = jnp.dot((t * t).astype(BF16), gmat, preferred_element_type=F32)
    return t * lax.rsqrt(ms + EPS) * w


def _modulated(x, norm_w, shift, scale):
    return _rms(x, norm_w) * (1.0 + scale) + shift


def _ctx_kv_kernel(ctx_ref, mod_ref, n1_ref, w_ref, g_ref, kn_ref, kc_ref, vc_ref):
    x = ctx_ref[0]
    h = _modulated(x, n1_ref[...], mod_ref[0:1, :], mod_ref[1:2, :])
    kv = jnp.dot(h.astype(BF16), w_ref[...], preferred_element_type=F32)
    k = _head_rms(kv[:, :KV_WIDTH], g_ref[...], kn_ref[...])
    v = kv[:, KV_WIDTH:]
    kc_ref[0] = k.astype(BF16)
    vc_ref[0] = jnp.concatenate([v, jnp.ones_like(v)], axis=1).astype(BF16)


def _ctx_kv(ctx, cmod, n1w, w_kv, gk, knw):
    b, c, d = ctx.shape
    return pl.pallas_call(
        _ctx_kv_kernel,
        out_shape=(jax.ShapeDtypeStruct((b, c, KV_WIDTH), BF16),
                   jax.ShapeDtypeStruct((b, c, 2 * KV_WIDTH), BF16)),
        grid=(b,),
        in_specs=[pl.BlockSpec((1, c, d), lambda i: (i, 0, 0)),
                  pl.BlockSpec(cmod.shape, lambda i: (0, 0)),
                  pl.BlockSpec((1, d), lambda i: (0, 0)),
                  pl.BlockSpec(w_kv.shape, lambda i: (0, 0)),
                  pl.BlockSpec(gk.shape, lambda i: (0, 0)),
                  pl.BlockSpec((1, KV_WIDTH), lambda i: (0, 0))],
        out_specs=(pl.BlockSpec((1, c, KV_WIDTH), lambda i: (i, 0, 0)),
                   pl.BlockSpec((1, c, 2 * KV_WIDTH), lambda i: (i, 0, 0))),
        compiler_params=_params(("arbitrary",)),
        name="ctx_kv",
    )(ctx, cmod, n1w, w_kv, gk, knw)


def _rope(t, cos, sin, even):
    width = t.shape[1]
    partner = jnp.where(even, pltpu.roll(t, width - 1, axis=1), pltpu.roll(t, 1, axis=1))
    return t * cos + partner * sin


def _inproj_kernel(x_ref, mod_ref, n1_ref, w_ref, gq_ref, qn_ref, kn_ref, cos_ref, sin_ref,
                   q_ref, k_ref, v_ref, u_ref):
    x = x_ref[0]
    h = _modulated(x, n1_ref[...], mod_ref[0, 0:1, :], mod_ref[0, 1:2, :])
    p = jnp.dot(h.astype(BF16), w_ref[...], preferred_element_type=F32)
    gq = gq_ref[...]
    q = _head_rms(p[:, :Q_END], gq, qn_ref[...])
    k = _head_rms(p[:, Q_END:K_END], gq[:KV_WIDTH, :KV_WIDTH], kn_ref[...])
    v = p[:, K_END:V_END]
    cos = cos_ref[...]
    sin = sin_ref[...]
    reps = Q_END // LANES
    cos_q = jnp.concatenate([cos] * reps, axis=1)
    sin_q = jnp.concatenate([sin] * reps, axis=1)
    even_q = (lax.broadcasted_iota(jnp.int32, (1, Q_END), 1) & 1) == 0
    even_k = (lax.broadcasted_iota(jnp.int32, (1, KV_WIDTH), 1) & 1) == 0
    q = _rope(q, cos_q, sin_q, even_q) * (HEAD_DIM ** -0.5)
    k = _rope(k, cos, sin, even_k)
    q_ref[0] = q.astype(BF16)
    k_ref[0] = k.astype(BF16)
    v_ref[0] = jnp.concatenate([v, jnp.ones_like(v)], axis=1).astype(BF16)
    u_ref[0] = p[:, V_END:]


def _inproj(x, mods, n1w, w_in, gq, qnw, knw, cos, sin, tl):
    b, l, d = x.shape
    ncol = w_in.shape[1]
    nu = ncol - V_END
    return pl.pallas_call(
        _inproj_kernel,
        out_shape=(jax.ShapeDtypeStruct((b, l, Q_END), BF16),
                   jax.ShapeDtypeStruct((b, l, KV_WIDTH), BF16),
                   jax.ShapeDtypeStruct((b, l, 2 * KV_WIDTH), BF16),
                   jax.ShapeDtypeStruct((b, l, nu), F32)),
        grid=(l // tl, b),
        in_specs=[pl.BlockSpec((1, tl, d), lambda i, j: (j, i, 0)),
                  pl.BlockSpec((1, 6, d), lambda i, j: (j, 0, 0)),
                  pl.BlockSpec((1, d), lambda i, j: (0, 0)),
                  pl.BlockSpec((d, ncol), lambda i, j: (0, 0)),
                  pl.BlockSpec(gq.shape, lambda i, j: (0, 0)),
                  pl.BlockSpec((1, Q_END), lambda i, j: (0, 0)),
                  pl.BlockSpec((1, KV_WIDTH), lambda i, j: (0, 0)),
                  pl.BlockSpec((tl, LANES), lambda i, j: (i, 0)),
                  pl.BlockSpec((tl, LANES), lambda i, j: (i, 0))],
        out_specs=(pl.BlockSpec((1, tl, Q_END), lambda i, j: (j, i, 0)),
                   pl.BlockSpec((1, tl, KV_WIDTH), lambda i, j: (j, i, 0)),
                   pl.BlockSpec((1, tl, 2 * KV_WIDTH), lambda i, j: (j, i, 0)),
                   pl.BlockSpec((1, tl, nu), lambda i, j: (j, i, 0))),
        compiler_params=_params(("arbitrary", "arbitrary")),
        name="inproj",
    )(x, mods, n1w, w_in, gq, qnw, knw, cos, sin)


def _attn_kernel(q_ref, k_ref, v_ref, kc_ref, vc_ref, wn_ref, o_ref, qg_ref, m_ref, acc_ref, *, chunk):
    tq = q_ref.shape[1]
    nj = Q_END // LANES
    sub = min(256, tq)
    n_chunks = k_ref.shape[1] // chunk
    low = lax.broadcasted_iota(jnp.int32, (1, LANES), 1) < HEAD_DIM
    nt = (((1,), (1,)), ((), ()))

    for j in range(nj):
        qv = q_ref[0, :, LANES * j:LANES * (j + 1)]
        zero = jnp.zeros_like(qv)
        qg_ref[0, j * tq:(j + 1) * tq, :] = jnp.where(low, qv, zero)
        qg_ref[1, j * tq:(j + 1) * tq, :] = jnp.where(low, zero, qv)
    m_ref[...] = jnp.full(m_ref.shape, -1e30, F32)
    acc_ref[...] = jnp.zeros(acc_ref.shape, F32)

    def step(kk, vv):
        for g in range(N_KV_HEADS):
            for r in range(0, nj * tq, sub):
                rows = slice(r, r + sub)
                s = lax.dot_general(qg_ref[g, rows, :], kk, nt, preferred_element_type=F32)
                m_old = m_ref[g, rows, :]
                m_new = jnp.maximum(m_old, jnp.max(s, axis=-1, keepdims=True))
                p = jnp.exp(s - m_new).astype(BF16)
                pv = jnp.dot(p, vv, preferred_element_type=F32)
                acc_ref[g, rows, :] = acc_ref[g, rows, :] * jnp.exp(m_old - m_new) + pv
                m_ref[g, rows, :] = m_new

    step(kc_ref[0], vc_ref[0])

    def body(c, carry):
        off = pl.multiple_of(c * chunk, chunk)
        step(k_ref[0, pl.ds(off, chunk), :], v_ref[0, pl.ds(off, chunk), :])
        return carry

    lax.fori_loop(0, n_chunks, body, 0)

    outs = []
    for j in range(nj):
        halves = []
        for g in range(N_KV_HEADS):
            blk = acc_ref[g, j * tq:(j + 1) * tq, :]
            halves.append(blk[:, :LANES] / blk[:, LANES:])
        outs.append(jnp.where(low, halves[0], halves[1]))
    a = jnp.concatenate(outs, axis=1)
    o_ref[0] = _rms(a, wn_ref[...]).astype(BF16)


def _attention(q, k, v, kc, vc, wn, tq, chunk):
    b, l, _ = q.shape
    c = kc.shape[1]
    rows = tq * (Q_END // LANES)
    return pl.pallas_call(
        functools.partial(_attn_kernel, chunk=chunk),
        out_shape=jax.ShapeDtypeStruct((b, l, Q_END), BF16),
        grid=(b, l // tq),
        in_specs=[pl.BlockSpec((1, tq, Q_END), lambda i, j: (i, j, 0)),
                  pl.BlockSpec((1, l, KV_WIDTH), lambda i, j: (i, 0, 0)),
                  pl.BlockSpec((1, l, 2 * KV_WIDTH), lambda i, j: (i, 0, 0)),
                  pl.BlockSpec((1, c, KV_WIDTH), lambda i, j: (i, 0, 0)),
                  pl.BlockSpec((1, c, 2 * KV_WIDTH), lambda i, j: (i, 0, 0)),
                  pl.BlockSpec((1, Q_END), lambda i, j: (0, 0))],
        out_specs=pl.BlockSpec((1, tq, Q_END), lambda i, j: (i, j, 0)),
        scratch_shapes=[pltpu.VMEM((N_KV_HEADS, rows, LANES), BF16),
                        pltpu.VMEM((N_KV_HEADS, rows, 1), F32),
                        pltpu.VMEM((N_KV_HEADS, rows, 2 * LANES), F32)],
        compiler_params=_params(("arbitrary", "arbitrary")),
        name="attn",
    )(q, k, v, kc, vc, wn)


def _hyena_pre_kernel(u0_ref, u1_ref, u2_ref, cw_ref, cb_ref, g32_ref, gbf_ref, x0_ref):
    l = u0_ref.shape[1]
    row = lax.broadcasted_iota(jnp.int32, (l, LANES), 0)

    def conv(u_ref, gi):
        u = u_ref[0]
        prev = jnp.where(row == 0, 0.0, pltpu.roll(u, 1, axis=0))
        nxt = jnp.where(row == l - 1, 0.0, pltpu.roll(u, l - 1, axis=0))
        w = cw_ref[gi]
        return w[0:1] * prev + w[1:2] * u + w[2:3] * nxt + cb_ref[gi:gi + 1, :]

    x0 = conv(u0_ref, 0)
    x1 = conv(u1_ref, 1)
    v = conv(u2_ref, 2)
    g = v * x1
    g32_ref[0] = g
    gbf_ref[0] = g.astype(BF16)
    x0_ref[0] = x0


def _hyena_pre(u, cw, cb):
    b, l, _ = u.shape
    nblk = HYENA_WIDTH // LANES
    ublk = lambda gi: pl.BlockSpec((1, l, LANES), lambda i, j: (i, 0, gi * nblk + j))
    oblk = pl.BlockSpec((1, l, LANES), lambda i, j: (i, 0, j))
    return pl.pallas_call(
        _hyena_pre_kernel,
        out_shape=(jax.ShapeDtypeStruct((b, l, HYENA_WIDTH), F32),
                   jax.ShapeDtypeStruct((b, l, HYENA_WIDTH), BF16),
                   jax.ShapeDtypeStruct((b, l, HYENA_WIDTH), F32)),
        grid=(b, nblk),
        in_specs=[ublk(0), ublk(1), ublk(2),
                  pl.BlockSpec((3, 3, LANES), lambda i, j: (0, 0, j)),
                  pl.BlockSpec((3, LANES), lambda i, j: (0, j))],
        out_specs=(oblk, oblk, oblk),
        compiler_params=_params(("arbitrary", "arbitrary")),
        name="hyena_pre",
    )(u, u, u, cw, cb)


def _filter_kernel(z_ref, w1_ref, b1_ref, w2_ref, b2_ref, w3_ref, b3_ref, w4_ref, fr_ref, dl_ref, o_ref):
    tl = z_ref.shape[0]
    z = z_ref[...]
    fr = fr_ref[...]
    dot = lambda a, w: jnp.dot(a, w, precision=HIGHEST, preferred_element_type=F32)
    h = jnp.sin(fr * (dot(z, w1_ref[...]) + b1_ref[...]))
    h = jnp.sin(fr * (dot(h, w2_ref[...]) + b2_ref[...]))
    h = jnp.sin(fr * (dot(h, w3_ref[...]) + b3_ref[...]))
    h = dot(h, w4_ref[...])
    t = z[:, 0:1]
    decay = jnp.exp(-t * jnp.abs(dl_ref[...]))
    hf = h[:, :HYENA_WIDTH] * decay
    hb = h[:, HYENA_WIDTH:] * decay
    row = lax.broadcasted_iota(jnp.int32, (tl, HYENA_WIDTH), 0) + pl.program_id(0) * tl
    hb = jnp.where(row == 0, 0.0, hb)
    o_ref[0] = hf + hb
    o_ref[1] = hf - hb


def _hyena_filter(z, w1, b1, w2, b2, w3, b3, w4, freq, deltas, tl):
    l = z.shape[0]
    full = lambda a: pl.BlockSpec(a.shape, lambda i: (0,) * a.ndim)
    return pl.pallas_call(
        _filter_kernel,
        out_shape=jax.ShapeDtypeStruct((2, l, HYENA_WIDTH), F32),
        grid=(l // tl,),
        in_specs=[pl.BlockSpec((tl, z.shape[1]), lambda i: (i, 0)),
                  full(w1), full(b1), full(w2), full(b2), full(w3), full(b3), full(w4), full(freq), full(deltas)],
        out_specs=pl.BlockSpec((2, tl, HYENA_WIDTH), lambda i: (0, i, 0)),
        compiler_params=_params(("arbitrary",)),
        name="hyena_filter",
    )(z, w1, b1, w2, b2, w3, b3, w4, freq, deltas)


def _dft_kernel(f_ref, x_ref, o_ref):
    o_ref[0] = jnp.dot(f_ref[...], x_ref[0].astype(BF16), preferred_element_type=F32)


def _dft(fmat, x, tf):
    nb, l, w = x.shape
    n = fmat.shape[0]
    return pl.pallas_call(
        _dft_kernel,
        out_shape=jax.ShapeDtypeStruct((nb, n, w), F32),
        grid=(n // tf, nb),
        in_specs=[pl.BlockSpec((tf, l), lambda i, j: (i, 0)),
                  pl.BlockSpec((1, l, w), lambda i, j: (j, 0, 0))],
        out_specs=pl.BlockSpec((1, tf, w), lambda i, j: (j, i, 0)),
        compiler_params=_params(("arbitrary", "arbitrary")),
        name="dft_filter",
    )(fmat, x)


def _dft_mul_kernel(fr_ref, fi_ref, x_ref, a_ref, b_ref, d_ref, zr_ref, zi_ref):
    x = x_ref[0]
    xr = jnp.dot(fr_ref[...], x, preferred_element_type=F32)
    xi = jnp.dot(fi_ref[...], x, preferred_element_type=F32)
    bb = b_ref[...]
    zr_ref[0] = (xr * a_ref[...] - xi * bb).astype(BF16)
    zi_ref[0] = (xr * bb + xi * d_ref[...]).astype(BF16)


def _dft_mul(fmat, g, sa, sb, sd, tf):
    b, l, w = g.shape
    nf = l // tf
    spec = pl.BlockSpec((tf, w), lambda i, j: (i, 0))
    return pl.pallas_call(
        _dft_mul_kernel,
        out_shape=(jax.ShapeDtypeStruct((b, l, w), BF16), jax.ShapeDtypeStruct((b, l, w), BF16)),
        grid=(nf, b),
        in_specs=[pl.BlockSpec((tf, l), lambda i, j: (i, 0)),
                  pl.BlockSpec((tf, l), lambda i, j: (i + nf, 0)),
                  pl.BlockSpec((1, l, w), lambda i, j: (j, 0, 0)),
                  spec, spec, spec],
        out_specs=(pl.BlockSpec((1, tf, w), lambda i, j: (j, i, 0)),
                   pl.BlockSpec((1, tf, w), lambda i, j: (j, i, 0))),
        compiler_params=_params(("arbitrary", "arbitrary")),
        name="dft_mul",
    )(fmat, fmat, g, sa, sb, sd)


def _idft_kernel(fr_ref, fi_ref, zr_ref, zi_ref, g_ref, x0_ref, hb_ref, wn_ref, o_ref):
    conv = (jnp.dot(fr_ref[...], zr_ref[0], preferred_element_type=F32)
            + jnp.dot(fi_ref[...], zi_ref[0], preferred_element_type=F32))
    y = (conv + g_ref[0] * hb_ref[...]) * x0_ref[0]
    o_ref[0] = _rms(y, wn_ref[...]).astype(BF16)


def _idft(finv_r, finv_i, zr, zi, g32, x0, hbias, wn, tt):
    b, l, w = zr.shape
    tile = pl.BlockSpec((1, tt, w), lambda i, j: (j, i, 0))
    return pl.pallas_call(
        _idft_kernel,
        out_shape=jax.ShapeDtypeStruct((b, l, w), BF16),
        grid=(l // tt, b),
        in_specs=[pl.BlockSpec((tt, l), lambda i, j: (i, 0)),
                  pl.BlockSpec((tt, l), lambda i, j: (i, 0)),
                  pl.BlockSpec((1, l, w), lambda i, j: (j, 0, 0)),
                  pl.BlockSpec((1, l, w), lambda i, j: (j, 0, 0)),
                  tile, tile,
                  pl.BlockSpec((1, w), lambda i, j: (0, 0)),
                  pl.BlockSpec((1, w), lambda i, j: (0, 0))],
        out_specs=tile,
        compiler_params=_params(("arbitrary", "arbitrary")),
        name="idft",
    )(finv_r, finv_i, zr, zi, g32, x0, hbias, wn)


def _store_row_tiles(ref, val):
    rows = val.shape[0]
    for j in range(val.shape[1] // LANES):
        ref[pl.ds(j, rows, stride=SUBLANES), :] = val[:, LANES * j:LANES * (j + 1)]


def _load_row_tiles(ref, rows):
    return jnp.concatenate([ref[pl.ds(j, rows, stride=SUBLANES), :] for j in range(SUBLANES)], axis=1)


def _merge_kernel(a_ref, y_ref, x_ref, mod_ref, wa_ref, wy_ref, n2_ref, sg_ref, su_ref, sd_ref,
                  base_ref, hi_ref, lo_ref, rt_ref):
    m = (jnp.dot(a_ref[0], wa_ref[...], preferred_element_type=F32)
         + jnp.dot(y_ref[0], wy_ref[...], preferred_element_type=F32))
    x1 = x_ref[0] + mod_ref[0, 2:3, :] * m
    h2 = _modulated(x1, n2_ref[...], mod_ref[0, 3:4, :], mod_ref[0, 4:5, :])
    hi = h2.astype(BF16)
    hi_ref[0] = hi
    lo_ref[0] = (h2 - hi.astype(F32)).astype(BF16)
    _store_row_tiles(rt_ref, h2)
    gate = jnp.dot(hi, sg_ref[...], preferred_element_type=F32)
    up = jnp.dot(hi, su_ref[...], preferred_element_type=F32)
    act = (gate * _sigmoid(gate) * up).astype(BF16)
    shared = jnp.dot(act, sd_ref[...], preferred_element_type=F32)
    base_ref[0] = x1 + mod_ref[0, 5:6, :] * shared


def _merge(an, yn, x, mods, wa, wy, n2w, sg, su, sd, tl):
    b, l, d = x.shape
    full = lambda a: pl.BlockSpec(a.shape, lambda i, j: (0,) * a.ndim)
    half = pl.BlockSpec((1, tl, an.shape[2]), lambda i, j: (i, j, 0))
    wide = pl.BlockSpec((1, tl, d), lambda i, j: (i, j, 0))
    per_b = l // tl
    assert d == SUBLANES * LANES
    return pl.pallas_call(
        _merge_kernel,
        out_shape=(jax.ShapeDtypeStruct((b, l, d), F32),
                   jax.ShapeDtypeStruct((b, l, d), BF16),
                   jax.ShapeDtypeStruct((b, l, d), BF16),
                   jax.ShapeDtypeStruct((b * l * SUBLANES, LANES), F32)),
        grid=(b, per_b),
        in_specs=[half, half, wide,
                  pl.BlockSpec((1, 6, d), lambda i, j: (i, 0, 0)),
                  full(wa), full(wy), full(n2w), full(sg), full(su), full(sd)],
        out_specs=(wide, wide, wide,
                   pl.BlockSpec((tl * SUBLANES, LANES), lambda i, j: (i * per_b + j, 0))),
        compiler_params=_params(("arbitrary", "arbitrary")),
        name="merge",
    )(an, yn, x, mods, wa, wy, n2w, sg, su, sd)


def _router_kernel(hi_ref, lo_ref, whi_ref, wlo_ref, bias_ref, tri_ref,
                   idx_ref, wgt_ref, rank_ref, cnt_ref, run_ref):
    tt = hi_ref.shape[0]
    per_group = N_EXPERTS // N_GROUPS

    @pl.when(pl.program_id(0) == 0)
    def _():
        run_ref[...] = jnp.zeros_like(run_ref)

    nt = (((1,), (1,)), ((), ()))
    hi = hi_ref[...]
    whi = whi_ref[...]
    logits = (lax.dot_general(whi, hi, nt, preferred_element_type=F32)
              + lax.dot_general(whi, lo_ref[...], nt, preferred_element_type=F32)
              + lax.dot_general(wlo_ref[...], hi, nt, preferred_element_type=F32))
    scores = _sigmoid(logits)
    biased = scores + bias_ref[...]

    ridx = lax.broadcasted_iota(jnp.int32, (per_group, tt), 0)
    groups = [biased[g * per_group:(g + 1) * per_group, :] for g in range(N_GROUPS)]
    gs = []
    for blk in groups:
        m1 = jnp.max(blk, axis=0, keepdims=True)
        i1 = jnp.min(jnp.where(blk == m1, ridx, per_group), axis=0, keepdims=True)
        m2 = jnp.max(jnp.where(ridx == i1, NEG_INF, blk), axis=0, keepdims=True)
        gs.append(m1 + m2)

    kept = []
    for g in range(N_GROUPS):
        ahead = jnp.zeros((1, tt), F32)
        for o in range(N_GROUPS):
            if o != g:
                wins = (gs[o] >= gs[g]) if o < g else (gs[o] > gs[g])
                ahead = ahead + jnp.where(wins, 1.0, 0.0)
        kept.append(jnp.where(ahead < TOPK_GROUPS, groups[g], NEG_INF))
    cur = jnp.concatenate(kept, axis=0)

    eidx = lax.broadcasted_iota(jnp.int32, cur.shape, 0)
    onehot = jnp.zeros(cur.shape, F32)
    picks = []
    wsel = []
    for _ in range(TOP_K):
        mx = jnp.max(cur, axis=0, keepdims=True)
        first = jnp.min(jnp.where(cur == mx, eidx, N_EXPERTS), axis=0, keepdims=True)
        sel = eidx == first
        picks.append(first)
        wsel.append(jnp.sum(jnp.where(sel, scores, 0.0), axis=0, keepdims=True))
        onehot = jnp.where(sel, 1.0, onehot)
        cur = jnp.where(sel, NEG_INF, cur)
    w = jnp.concatenate(wsel, axis=0)
    w = w / jnp.sum(w, axis=0, keepdims=True) * ROUTE_SCALE
    idx = jnp.concatenate(picks, axis=0)

    oh = onehot.astype(BF16)
    before = jnp.dot(oh, tri_ref[0], preferred_element_type=F32)
    total = jnp.dot(oh, tri_ref[1], preferred_element_type=F32)
    pos = run_ref[...] + before
    ranks = [jnp.sum(jnp.where(eidx == p, pos, 0.0), axis=0, keepdims=True) for p in picks]
    run_ref[...] = run_ref[...] + total

    idx_ref[...] = idx
    wgt_ref[...] = w
    rank_ref[...] = jnp.concatenate(ranks, axis=0).astype(jnp.int32)
    cnt_ref[...] = run_ref[...]


def _router(hi, lo, whi, wlo, bias, tri, tt):
    t, d = hi.shape
    tok = pl.BlockSpec((tt, d), lambda i: (i, 0))
    full = lambda a: pl.BlockSpec(a.shape, lambda i: (0,) * a.ndim)
    out = pl.BlockSpec((TOP_K, tt), lambda i: (0, i))
    return pl.pallas_call(
        _router_kernel,
        out_shape=(jax.ShapeDtypeStruct((TOP_K, t), jnp.int32),
                   jax.ShapeDtypeStruct((TOP_K, t), F32),
                   jax.ShapeDtypeStruct((TOP_K, t), jnp.int32),
                   jax.ShapeDtypeStruct((N_EXPERTS, tt), F32)),
        grid=(t // tt,),
        in_specs=[tok, tok, full(whi), full(wlo), full(bias), full(tri)],
        out_specs=(out, out, out, pl.BlockSpec((N_EXPERTS, tt), lambda i: (0, 0))),
        scratch_shapes=[pltpu.VMEM((N_EXPERTS, tt), F32)],
        compiler_params=_params(("arbitrary",)),
        name="router",
    )(hi, lo, whi, wlo, bias, tri)


def _row_tile(ref, r):
    return ref.at[pl.ds(pl.multiple_of(r * SUBLANES, SUBLANES), SUBLANES), :]


def _dest_kernel(idx_ref, rank_ref, start_ref, dest_ref):
    eidx = lax.broadcasted_iota(jnp.int32, start_ref.shape, 0)
    start = start_ref[...]
    rows = [jnp.sum(jnp.where(eidx == idx_ref[k:k + 1, :], start, 0.0), axis=0, keepdims=True)
            for k in range(TOP_K)]
    dest_ref[...] = jnp.concatenate(rows, axis=0).astype(jnp.int32) + rank_ref[...]


def _dest(idx, rank, start, tt):
    t = idx.shape[1]
    blk = pl.BlockSpec((TOP_K, tt), lambda i: (0, i))
    return pl.pallas_call(
        _dest_kernel,
        out_shape=jax.ShapeDtypeStruct((TOP_K, t), jnp.int32),
        grid=(t // tt,),
        in_specs=[blk, blk, pl.BlockSpec(start.shape, lambda i: (0, 0))],
        out_specs=blk,
        compiler_params=_params(("arbitrary",)),
        name="dest",
    )(idx, rank, start)


def _dispatch_kernel(dest_ref, h_ref, xs_in_ref, xs_ref, sem):
    del xs_in_ref
    td = dest_ref.shape[1]

    def row_copy(t, k):
        return pltpu.make_async_copy(_row_tile(h_ref, t), _row_tile(xs_ref, dest_ref[k, t]), sem)

    def issue(t, carry):
        for k in range(TOP_K):
            row_copy(t, k).start()
        return carry

    lax.fori_loop(0, td, issue, 0)

    def drain(t, carry):
        for k in range(TOP_K):
            row_copy(t, k).wait()
        return carry

    lax.fori_loop(0, td, drain, 0)


def _dispatch(dest, h_rt, xs0, td):
    t = dest.shape[1]
    return pl.pallas_call(
        _dispatch_kernel,
        out_shape=jax.ShapeDtypeStruct(xs0.shape, xs0.dtype),
        grid=(t // td,),
        in_specs=[pl.BlockSpec((TOP_K, td), lambda i: (0, i), memory_space=pltpu.SMEM),
                  pl.BlockSpec((td * SUBLANES, LANES), lambda i: (i, 0)),
                  pl.BlockSpec(memory_space=pl.ANY)],
        out_specs=pl.BlockSpec(memory_space=pl.ANY),
        scratch_shapes=[pltpu.SemaphoreType.DMA],
        input_output_aliases={2: 0},
        compiler_params=_params(("arbitrary",)),
        name="dispatch",
    )(dest, h_rt, xs0)


def _experts_kernel(be_ref, nu_ref, x_ref, wg_ref, wu_ref, wd_ref, y_ref):
    del be_ref

    @pl.when(pl.program_id(0) < nu_ref[0])
    def _():
        x = _load_row_tiles(x_ref, EXPERT_ROWS).astype(BF16)
        gate = jnp.dot(x, wg_ref[0], preferred_element_type=F32)
        up = jnp.dot(x, wu_ref[0], preferred_element_type=F32)
        act = (gate * _sigmoid(gate) * up).astype(BF16)
        _store_row_tiles(y_ref, jnp.dot(act, wd_ref[0], preferred_element_type=F32))


def _experts(blk_e, n_used, xs, wg, wu, wd):
    d, ff = wg.shape[1], wg.shape[2]
    nblk = xs.shape[0] // (EXPERT_ROWS * SUBLANES)
    row = lambda i, be, nu: (jnp.minimum(i, nu[0] - 1), 0)
    wsel = lambda i, be, nu: (be[jnp.minimum(i, nu[0] - 1)], 0, 0)
    return pl.pallas_call(
        _experts_kernel,
        out_shape=jax.ShapeDtypeStruct(xs.shape, F32),
        grid_spec=pltpu.PrefetchScalarGridSpec(
            num_scalar_prefetch=2,
            grid=(nblk,),
            in_specs=[pl.BlockSpec((EXPERT_ROWS * SUBLANES, LANES), row),
                      pl.BlockSpec((1, d, ff), wsel),
                      pl.BlockSpec((1, d, ff), wsel),
                      pl.BlockSpec((1, ff, d), wsel)],
            out_specs=pl.BlockSpec((EXPERT_ROWS * SUBLANES, LANES), row)),
        compiler_params=_params(("arbitrary",)),
        name="experts",
    )(blk_e, n_used, xs, wg, wu, wd)


def _combine_kernel(dest_ref, ys_ref, w_ref, base_ref, mod_ref, fw_ref, o_ref, buf, sem):
    tc = dest_ref.shape[1]

    def row_copy(t, k):
        return pltpu.make_async_copy(_row_tile(ys_ref, dest_ref[k, t]), _row_tile(buf.at[k], t), sem)

    def issue(t, carry):
        for k in range(TOP_K):
            row_copy(t, k).start()
        return carry

    lax.fori_loop(0, tc, issue, 0)

    def drain(t, carry):
        for k in range(TOP_K):
            row_copy(t, k).wait()
        return carry

    lax.fori_loop(0, tc, drain, 0)

    w = w_ref[...]
    routed = w[:, 0:1] * _load_row_tiles(buf.at[0], tc)
    for k in range(1, TOP_K):
        routed = routed + w[:, k:k + 1] * _load_row_tiles(buf.at[k], tc)
    x = base_ref[0] + mod_ref[0, 5:6, :] * routed
    o_ref[0] = _rms(x, fw_ref[...])


def _combine(dest, ys, wt, base, mods, fw, tc):
    b, l, d = base.shape
    per_b = l // tc
    return pl.pallas_call(
        _combine_kernel,
        out_shape=jax.ShapeDtypeStruct((b, l, d), F32),
        grid=(b * per_b,),
        in_specs=[pl.BlockSpec((TOP_K, tc), lambda i: (0, i), memory_space=pltpu.SMEM),
                  pl.BlockSpec(memory_space=pl.ANY),
                  pl.BlockSpec((tc, TOP_K), lambda i: (i, 0)),
                  pl.BlockSpec((1, tc, d), lambda i: (i // per_b, i % per_b, 0)),
                  pl.BlockSpec((1, 6, d), lambda i: (i // per_b, 0, 0)),
                  pl.BlockSpec((1, d), lambda i: (0, 0))],
        out_specs=pl.BlockSpec((1, tc, d), lambda i: (i // per_b, i % per_b, 0)),
        scratch_shapes=[pltpu.VMEM((TOP_K, tc * SUBLANES, LANES), F32), pltpu.SemaphoreType.DMA],
        compiler_params=_params(("arbitrary",)),
        name="combine",
    )(dest, ys, wt, base, mods, fw)


def _rope_tables(l):
    t = jnp.arange(l, dtype=jnp.int32)
    row = (t // GRID_W).astype(F32)
    col = (t % GRID_W).astype(F32)
    n_freq = HEAD_DIM // 4
    inv = ROPE_THETA ** (-jnp.arange(n_freq, dtype=F32) / n_freq)
    ang = jnp.concatenate([row[:, None] * inv, col[:, None] * inv], axis=-1)
    cos = jnp.repeat(jnp.cos(ang), 2, axis=1)
    sin = jnp.repeat(jnp.sin(ang), 2, axis=1)
    sign = jnp.tile(jnp.array([-1.0, 1.0], F32), HEAD_DIM // 2)
    reps = LANES // HEAD_DIM
    return jnp.tile(cos, (1, reps)), jnp.tile(sin * sign, (1, reps))


def _filter_features(l):
    t = jnp.linspace(0.0, 1.0, l, dtype=F32)[:, None]
    bands = (FILTER_EMB - 1) // 2
    w = 2.0 * math.pi * jnp.arange(l, dtype=F32)[:, None] / l
    f = jnp.linspace(1e-4, bands - 1, bands, dtype=F32)[None, :]
    z = jnp.concatenate([t, jnp.cos(f * w), -jnp.sin(f * w)], axis=-1)
    min_decay = math.log(FILTER_TARGET) / FILTER_DECAY_FAST
    max_decay = math.log(FILTER_TARGET) / FILTER_DECAY_SLOW
    deltas = jnp.linspace(min_decay, max_decay, HYENA_WIDTH, dtype=F32)[None, :]
    return jnp.pad(z, ((0, 0), (0, LANES - FILTER_EMB))), deltas


def _dft_matrices(l):
    n = 2 * l
    idx = jnp.arange(l, dtype=jnp.int32)
    prod = (idx[:, None] * idx[None, :]) % n
    ang = prod.astype(F32) * (2.0 * math.pi / n)
    c = jnp.cos(ang)
    s = jnp.sin(ang)
    alt = jnp.where(idx % 2 == 0, 1.0, -1.0).astype(F32)
    first = (idx == 0)[:, None]
    fwd = jnp.concatenate([c, jnp.where(first, alt[None, :], -s)], axis=0).astype(BF16)
    firstc = (idx == 0)[None, :]
    inv_r = (jnp.where(firstc, 1.0, 2.0) * c / n).astype(BF16)
    inv_i = (jnp.where(firstc, alt[:, None], -2.0 * s) / n).astype(BF16)
    return fwd, inv_r, inv_i


def _head_perm():
    order = []
    for j in range(N_HEADS // 2):
        order += list(range(j * HEAD_DIM, (j + 1) * HEAD_DIM))
        order += list(range((j + N_HEADS // 2) * HEAD_DIM, (j + 1 + N_HEADS // 2) * HEAD_DIM))
    return jnp.array(order, jnp.int32)


def _pad2(a, rows, cols):
    return jnp.pad(a, ((0, rows - a.shape[0]), (0, cols - a.shape[1])))


def kernel(x, c, ctx, c_ctx, mod_w, mod_b, norm1_w, w_in, q_norm_w, k_norm_w, conv_w, conv_b, filt_w1, filt_b1, filt_w2, filt_b2, filt_w3, filt_b3, filt_w4, filt_freq, hyena_bias, attn_out_norm_w, hyena_out_norm_w, w_out, norm2_w, router_w, router_bias, exp_w_gate, exp_w_up, exp_w_down, sh_w_gate, sh_w_up, sh_w_down, final_norm_w):
    b, l, d = x.shape
    t = b * l
    assert mod_w.shape[0] == 1, "single-layer stack"
    tl = min(512, l)

    cond = jnp.concatenate([c, c_ctx[None, :], jnp.zeros((-(b + 1) % 8, d), F32)], axis=0)
    mod = _adaln(cond, mod_w[0], mod_b[0][None, :])
    mods = mod[:b].reshape(b, 6, d)
    cmod = mod[b].reshape(6, d)

    perm = _head_perm()
    w_in0 = w_in[0]
    w_in_k = jnp.concatenate([w_in0[:, :Q_END][:, perm], w_in0[:, Q_END:]], axis=1).astype(BF16)
    w_kv = w_in0[:, Q_END:V_END].astype(BF16)
    gq = jnp.kron(jnp.eye(N_HEADS, dtype=F32), jnp.full((HEAD_DIM, HEAD_DIM), 1.0 / HEAD_DIM, F32)).astype(BF16)
    qnw = jnp.tile(q_norm_w[0], N_HEADS)[None, :]
    knw = jnp.tile(k_norm_w[0], N_KV_HEADS)[None, :]
    n1w = norm1_w[0][None, :]
    cos, sin = _rope_tables(l)

    kc, vc = _ctx_kv(ctx, cmod, n1w, w_kv, gq[:KV_WIDTH, :KV_WIDTH], knw)
    q, k, v, u = _inproj(x, mods, n1w, w_in_k, gq, qnw, knw, cos, sin, tl)
    an = _attention(q, k, v, kc, vc, attn_out_norm_w[0][perm][None, :], min(256, l), min(512, l))

    cw = conv_w[0].reshape(3, 3, HYENA_WIDTH).transpose(1, 0, 2)
    cb = conv_b[0].reshape(3, HYENA_WIDTH)
    g32, gbf, x0 = _hyena_pre(u, cw, cb)
    z, deltas = _filter_features(l)
    fo = filt_w2.shape[1]
    hsd = _hyena_filter(
        z, _pad2(filt_w1[0], LANES, LANES), _pad2(filt_b1[0][None, :], 1, LANES),
        _pad2(filt_w2[0], LANES, LANES), _pad2(filt_b2[0][None, :], 1, LANES),
        _pad2(filt_w3[0], LANES, LANES), _pad2(filt_b3[0][None, :], 1, LANES),
        _pad2(filt_w4[0], LANES, 2 * HYENA_WIDTH), _pad2(filt_freq[0][None, :], 1, LANES), deltas, tl)
    del fo
    fwd, inv_r, inv_i = _dft_matrices(l)
    spec = _dft(fwd, hsd, tl)
    row0 = (jnp.arange(l) == 0)[:, None]
    sa = spec[0, :l]
    sd = jnp.where(row0, spec[0, l:l + 1], sa)
    sb = jnp.where(row0, 0.0, spec[1, l:])
    zr, zi = _dft_mul(fwd, gbf, sa, sb, sd, tl)
    yn = _idft(inv_r, inv_i, zr, zi, g32, x0, hyena_bias[0][None, :], hyena_out_norm_w[0][None, :], tl)

    w_out0 = w_out[0]
    base, h2hi, h2lo, h2rt = _merge(
        an, yn, x, mods, w_out0[:ATTN_WIDTH][perm].astype(BF16), w_out0[ATTN_WIDTH:].astype(BF16),
        norm2_w[0][None, :], sh_w_gate[0].astype(BF16), sh_w_up[0].astype(BF16), sh_w_down[0].astype(BF16), tl)

    tt = 256
    rwt = router_w[0].T
    rw_hi = rwt.astype(BF16)
    rw_lo = (rwt - rw_hi.astype(F32)).astype(BF16)
    bias = jnp.broadcast_to(router_bias[0][:, None], (N_EXPERTS, tt))
    ti = jnp.arange(tt)
    tri = jnp.stack([(ti[:, None] < ti[None, :]), jnp.ones((tt, tt), bool)]).astype(BF16)
    idx, wgt, rank, cnt = _router(h2hi.reshape(t, d), h2lo.reshape(t, d), rw_hi, rw_lo, bias, tri, tt)

    counts = cnt[:, 0].astype(jnp.int32)
    padded = (counts + EXPERT_ROWS - 1) // EXPERT_ROWS * EXPERT_ROWS
    pad_end = jnp.cumsum(padded)
    pad_start = pad_end - padded
    dest = _dest(idx, rank, jnp.broadcast_to(pad_start.astype(F32)[:, None], (N_EXPERTS, tt)), tt)
    n_rows = (t * TOP_K + N_EXPERTS * (EXPERT_ROWS - 1) + EXPERT_ROWS - 1) // EXPERT_ROWS * EXPERT_ROWS
    nblk = n_rows // EXPERT_ROWS
    blk_e = jnp.minimum(jnp.searchsorted(pad_end, jnp.arange(nblk, dtype=jnp.int32) * EXPERT_ROWS, side="right"),
                        N_EXPERTS - 1).astype(jnp.int32)
    n_used = (pad_end[-1:] // EXPERT_ROWS).astype(jnp.int32)

    xs = _dispatch(dest, h2rt, jnp.zeros((n_rows * SUBLANES, LANES), F32), min(512, t))
    ys = _experts(blk_e, n_used, xs, exp_w_gate[0].astype(BF16), exp_w_up[0].astype(BF16),
                  exp_w_down[0].astype(BF16))
    return _combine(dest, ys, wgt.T, base, mods, final_norm_w[None, :], min(128, l))
```

```python
import functools
import math

import jax
import jax.numpy as jnp
from jax import lax
from jax.experimental import pallas as pl
from jax.experimental.pallas import tpu as pltpu

F32 = jnp.float32
BF16 = jnp.bfloat16
HIGHEST = lax.Precision.HIGHEST

GRID_W = 64
N_HEADS = 8
N_KV_HEADS = 2
HEAD_DIM = 64
ATTN_WIDTH = N_HEADS * HEAD_DIM
KV_WIDTH = N_KV_HEADS * HEAD_DIM
HYENA_WIDTH = 512
Q_END = ATTN_WIDTH
K_END = Q_END + KV_WIDTH
V_END = K_END + KV_WIDTH
ROPE_THETA = 10000.0
FILTER_EMB = 33
FILTER_DECAY_FAST = 0.3
FILTER_DECAY_SLOW = 1.5
FILTER_TARGET = 1e-2
N_EXPERTS = 256
TOP_K = 8
N_GROUPS = 8
TOPK_GROUPS = 4
ROUTE_SCALE = 2.5
EPS = 1e-6

LANES = 128
ROW_SUB = 4
EXPERT_ROWS = 256
NEG_INF = float("-inf")


def _params(semantics, vmem_mb=48):
    return pltpu.CompilerParams(dimension_semantics=semantics, vmem_limit_bytes=vmem_mb * 1024 * 1024)


def _rms(x, w):
    return x * lax.rsqrt(jnp.mean(x * x, axis=-1, keepdims=True) + EPS) * w


def _sigmoid(x):
    return 1.0 / (1.0 + jnp.exp(-x))


def _adaln_kernel(c_ref, w_ref, b_ref, o_ref):
    c = c_ref[...]
    s = c * _sigmoid(c)
    o_ref[...] = jnp.dot(s, w_ref[...], precision=HIGHEST, preferred_element_type=F32) + b_ref[...]


def _adaln(cond, w, b):
    rows, d = cond.shape
    n = w.shape[1]
    tn = 1536
    return pl.pallas_call(
        _adaln_kernel,
        out_shape=jax.ShapeDtypeStruct((rows, n), F32),
        grid=(n // tn,),
        in_specs=[pl.BlockSpec((rows, d), lambda j: (0, 0)),
                  pl.BlockSpec((d, tn), lambda j: (0, j)),
                  pl.BlockSpec((1, tn), lambda j: (0, j))],
        out_specs=pl.BlockSpec((rows, tn), lambda j: (0, j)),
        compiler_params=_params(("arbitrary",)),
        name="adaln",
    )(cond, w, b)


def _head_rms(t, gmat, w):
    ms = jnp.dot((t * t).astype(BF16), gmat, preferred_element_type=F32)
    return t * lax.rsqrt(ms + EPS) * w


def _modulated(x, norm_w, shift, scale):
    return _rms(x, norm_w) * (1.0 + scale) + shift


def _ctx_kv_kernel(ctx_ref, mod_ref, n1_ref, w_ref, g_ref, kn_ref, kc_ref, vc_ref):
    x = ctx_ref[0]
    h = _modulated(x, n1_ref[...], mod_ref[0:1, :], mod_ref[1:2, :])
    kv = jnp.dot(h.astype(BF16), w_ref[...], preferred_element_type=F32)
    k = _head_rms(kv[:, :KV_WIDTH], g_ref[...], kn_ref[...])
    v = kv[:, KV_WIDTH:]
    kc_ref[0] = k.astype(BF16)
    vc_ref[0] = jnp.concatenate([v, jnp.ones_like(v)], axis=1).astype(BF16)


def _ctx_kv(ctx, cmod, n1w, w_kv, gk, knw):
    b, c, d = ctx.shape
    return pl.pallas_call(
        _ctx_kv_kernel,
        out_shape=(jax.ShapeDtypeStruct((b, c, KV_WIDTH), BF16),
                   jax.ShapeDtypeStruct((b, c, 2 * KV_WIDTH), BF16)),
        grid=(b,),
        in_specs=[pl.BlockSpec((1, c, d), lambda i: (i, 0, 0)),
                  pl.BlockSpec(cmod.shape, lambda i: (0, 0)),
                  pl.BlockSpec((1, d), lambda i: (0, 0)),
                  pl.BlockSpec(w_kv.shape, lambda i: (0, 0)),
                  pl.BlockSpec(gk.shape, lambda i: (0, 0)),
                  pl.BlockSpec((1, KV_WIDTH), lambda i: (0, 0))],
        out_specs=(pl.BlockSpec((1, c, KV_WIDTH), lambda i: (i, 0, 0)),
                   pl.BlockSpec((1, c, 2 * KV_WIDTH), lambda i: (i, 0, 0))),
        compiler_params=_params(("arbitrary",)),
        name="ctx_kv",
    )(ctx, cmod, n1w, w_kv, gk, knw)


def _rope(t, cos, sin, even):
    width = t.shape[1]
    partner = jnp.where(even, pltpu.roll(t, width - 1, axis=1), pltpu.roll(t, 1, axis=1))
    return t * cos + partner * sin


def _inproj_kernel(x_ref, mod_ref, n1_ref, w_ref, gq_ref, qn_ref, kn_ref, cos_ref, sin_ref,
                   q_ref, k_ref, v_ref, u_ref):
    x = x_ref[0]
    h = _modulated(x, n1_ref[...], mod_ref[0, 0:1, :], mod_ref[0, 1:2, :])
    p = jnp.dot(h.astype(BF16), w_ref[...], preferred_element_type=F32)
    gq = gq_ref[...]
    q = _head_rms(p[:, :Q_END], gq, qn_ref[...])
    k = _head_rms(p[:, Q_END:K_END], gq[:KV_WIDTH, :KV_WIDTH], kn_ref[...])
    v = p[:, K_END:V_END]
    cos = cos_ref[...]
    sin = sin_ref[...]
    reps = Q_END // LANES
    cos_q = jnp.concatenate([cos] * reps, axis=1)
    sin_q = jnp.concatenate([sin] * reps, axis=1)
    even_q = (lax.broadcasted_iota(jnp.int32, (1, Q_END), 1) & 1) == 0
    even_k = (lax.broadcasted_iota(jnp.int32, (1, KV_WIDTH), 1) & 1) == 0
    q = _rope(q, cos_q, sin_q, even_q) * (HEAD_DIM ** -0.5)
    k = _rope(k, cos, sin, even_k)
    q_ref[0] = q.astype(BF16)
    k_ref[0] = k.astype(BF16)
    v_ref[0] = jnp.concatenate([v, jnp.ones_like(v)], axis=1).astype(BF16)
    u_ref[0] = p[:, V_END:]


def _inproj(x, mods, n1w, w_in, gq, qnw, knw, cos, sin, tl):
    b, l, d = x.shape
    ncol = w_in.shape[1]
    nu = ncol - V_END
    return pl.pallas_call(
        _inproj_kernel,
        out_shape=(jax.ShapeDtypeStruct((b, l, Q_END), BF16),
                   jax.ShapeDtypeStruct((b, l, KV_WIDTH), BF16),
                   jax.ShapeDtypeStruct((b, l, 2 * KV_WIDTH), BF16),
                   jax.ShapeDtypeStruct((b, l, nu), F32)),
        grid=(l // tl, b),
        in_specs=[pl.BlockSpec((1, tl, d), lambda i, j: (j, i, 0)),
                  pl.BlockSpec((1, 6, d), lambda i, j: (j, 0, 0)),
                  pl.BlockSpec((1, d), lambda i, j: (0, 0)),
                  pl.BlockSpec((d, ncol), lambda i, j: (0, 0)),
                  pl.BlockSpec(gq.shape, lambda i, j: (0, 0)),
                  pl.BlockSpec((1, Q_END), lambda i, j: (0, 0)),
                  pl.BlockSpec((1, KV_WIDTH), lambda i, j: (0, 0)),
                  pl.BlockSpec((tl, LANES), lambda i, j: (i, 0)),
                  pl.BlockSpec((tl, LANES), lambda i, j: (i, 0))],
        out_specs=(pl.BlockSpec((1, tl, Q_END), lambda i, j: (j, i, 0)),
                   pl.BlockSpec((1, tl, KV_WIDTH), lambda i, j: (j, i, 0)),
                   pl.BlockSpec((1, tl, 2 * KV_WIDTH), lambda i, j: (j, i, 0)),
                   pl.BlockSpec((1, tl, nu), lambda i, j: (j, i, 0))),
        compiler_params=_params(("arbitrary", "arbitrary")),
        name="inproj",
    )(x, mods, n1w, w_in, gq, qnw, knw, cos, sin)


def _attn_kernel(q_ref, k_ref, v_ref, wn_ref, o_ref, *, sub):
    tq = q_ref.shape[1]
    low = lax.broadcasted_iota(jnp.int32, (1, LANES), 1) < HEAD_DIM
    nt = (((1,), (1,)), ((), ()))
    kk = k_ref[0]
    vv = v_ref[0]
    for r in range(0, tq, sub):
        outs = []
        for j in range(Q_END // LANES):
            qv = q_ref[0, r:r + sub, LANES * j:LANES * (j + 1)]
            zero = jnp.zeros_like(qv)
            halves = []
            for g in range(N_KV_HEADS):
                qh = jnp.where(low, qv, zero) if g == 0 else jnp.where(low, zero, qv)
                s = lax.dot_general(qh, kk, nt, preferred_element_type=F32)
                p = jnp.exp(s - jnp.max(s, axis=-1, keepdims=True)).astype(BF16)
                pv = jnp.dot(p, vv, preferred_element_type=F32)
                halves.append(pv[:, :LANES] / pv[:, LANES:])
            outs.append(jnp.where(low, halves[0], halves[1]))
        a = jnp.concatenate(outs, axis=1)
        o_ref[0, r:r + sub, :] = _rms(a, wn_ref[...]).astype(BF16)


def _attention(q, k, v, wn, tq, sub):
    b, l, _ = q.shape
    n = k.shape[1]
    return pl.pallas_call(
        functools.partial(_attn_kernel, sub=sub),
        out_shape=jax.ShapeDtypeStruct((b, l, Q_END), BF16),
        grid=(b, l // tq),
        in_specs=[pl.BlockSpec((1, tq, Q_END), lambda i, j: (i, j, 0)),
                  pl.BlockSpec((1, n, KV_WIDTH), lambda i, j: (i, 0, 0)),
                  pl.BlockSpec((1, n, 2 * KV_WIDTH), lambda i, j: (i, 0, 0)),
                  pl.BlockSpec((1, Q_END), lambda i, j: (0, 0))],
        out_specs=pl.BlockSpec((1, tq, Q_END), lambda i, j: (i, j, 0)),
        compiler_params=_params(("arbitrary", "arbitrary")),
        name="attn",
    )(q, k, v, wn)


def _hyena_pre_kernel(u0_ref, u1_ref, u2_ref, cw_ref, cb_ref, g32_ref, gbf_ref, x0_ref):
    l = u0_ref.shape[1]
    row = lax.broadcasted_iota(jnp.int32, (l, LANES), 0)

    def conv(u_ref, gi):
        u = u_ref[0]
        prev = jnp.where(row == 0, 0.0, pltpu.roll(u, 1, axis=0))
        nxt = jnp.where(row == l - 1, 0.0, pltpu.roll(u, l - 1, axis=0))
        w = cw_ref[gi]
        return w[0:1] * prev + w[1:2] * u + w[2:3] * nxt + cb_ref[gi:gi + 1, :]

    x0 = conv(u0_ref, 0)
    x1 = conv(u1_ref, 1)
    v = conv(u2_ref, 2)
    g = v * x1
    g32_ref[0] = g
    gbf_ref[0] = g.astype(BF16)
    x0_ref[0] = x0


def _hyena_pre(u, cw, cb):
    b, l, _ = u.shape
    nblk = HYENA_WIDTH // LANES
    ublk = lambda gi: pl.BlockSpec((1, l, LANES), lambda i, j: (i, 0, gi * nblk + j))
    oblk = pl.BlockSpec((1, l, LANES), lambda i, j: (i, 0, j))
    return pl.pallas_call(
        _hyena_pre_kernel,
        out_shape=(jax.ShapeDtypeStruct((b, l, HYENA_WIDTH), F32),
                   jax.ShapeDtypeStruct((b, l, HYENA_WIDTH), BF16),
                   jax.ShapeDtypeStruct((b, l, HYENA_WIDTH), F32)),
        grid=(b, nblk),
        in_specs=[ublk(0), ublk(1), ublk(2),
                  pl.BlockSpec((3, 3, LANES), lambda i, j: (0, 0, j)),
                  pl.BlockSpec((3, LANES), lambda i, j: (0, j))],
        out_specs=(oblk, oblk, oblk),
        compiler_params=_params(("arbitrary", "arbitrary")),
        name="hyena_pre",
    )(u, u, u, cw, cb)


def _filter_kernel(z_ref, w1_ref, b1_ref, w2_ref, b2_ref, w3_ref, b3_ref, w4_ref, fr_ref, dl_ref, o_ref):
    tl = z_ref.shape[0]
    z = z_ref[...]
    fr = fr_ref[...]
    dot = lambda a, w: jnp.dot(a, w, precision=HIGHEST, preferred_element_type=F32)
    h = jnp.sin(fr * (dot(z, w1_ref[...]) + b1_ref[...]))
    h = jnp.sin(fr * (dot(h, w2_ref[...]) + b2_ref[...]))
    h = jnp.sin(fr * (dot(h, w3_ref[...]) + b3_ref[...]))
    h = dot(h, w4_ref[...])
    t = z[:, 0:1]
    decay = jnp.exp(-t * jnp.abs(dl_ref[...]))
    hf = h[:, :HYENA_WIDTH] * decay
    hb = h[:, HYENA_WIDTH:] * decay
    row = lax.broadcasted_iota(jnp.int32, (tl, HYENA_WIDTH), 0) + pl.program_id(0) * tl
    hb = jnp.where(row == 0, 0.0, hb)
    o_ref[0] = hf + hb
    o_ref[1] = hf - hb


def _hyena_filter(z, w1, b1, w2, b2, w3, b3, w4, freq, deltas, tl):
    l = z.shape[0]
    full = lambda a: pl.BlockSpec(a.shape, lambda i: (0,) * a.ndim)
    return pl.pallas_call(
        _filter_kernel,
        out_shape=jax.ShapeDtypeStruct((2, l, HYENA_WIDTH), F32),
        grid=(l // tl,),
        in_specs=[pl.BlockSpec((tl, z.shape[1]), lambda i: (i, 0)),
                  full(w1), full(b1), full(w2), full(b2), full(w3), full(b3), full(w4), full(freq), full(deltas)],
        out_specs=pl.BlockSpec((2, tl, HYENA_WIDTH), lambda i: (0, i, 0)),
        compiler_params=_params(("arbitrary",)),
        name="hyena_filter",
    )(z, w1, b1, w2, b2, w3, b3, w4, freq, deltas)


def _dft_kernel(f_ref, x_ref, o_ref):
    o_ref[0] = jnp.dot(f_ref[...], x_ref[0].astype(BF16), preferred_element_type=F32)


def _dft(fmat, x, tf):
    nb, l, w = x.shape
    n = fmat.shape[0]
    return pl.pallas_call(
        _dft_kernel,
        out_shape=jax.ShapeDtypeStruct((nb, n, w), F32),
        grid=(n // tf, nb),
        in_specs=[pl.BlockSpec((tf, l), lambda i, j: (i, 0)),
                  pl.BlockSpec((1, l, w), lambda i, j: (j, 0, 0))],
        out_specs=pl.BlockSpec((1, tf, w), lambda i, j: (j, i, 0)),
        compiler_params=_params(("arbitrary", "arbitrary")),
        name="dft_filter",
    )(fmat, x)


def _dft_mul_kernel(fr_ref, fi_ref, x_ref, a_ref, b_ref, d_ref, zr_ref, zi_ref):
    x = x_ref[0]
    xr = jnp.dot(fr_ref[...], x, preferred_element_type=F32)
    xi = jnp.dot(fi_ref[...], x, preferred_element_type=F32)
    bb = b_ref[...]
    zr_ref[0] = (xr * a_ref[...] - xi * bb).astype(BF16)
    zi_ref[0] = (xr * bb + xi * d_ref[...]).astype(BF16)


def _dft_mul(fmat, g, sa, sb, sd, tf):
    b, l, w = g.shape
    nf = l // tf
    spec = pl.BlockSpec((tf, w), lambda i, j: (i, 0))
    return pl.pallas_call(
        _dft_mul_kernel,
        out_shape=(jax.ShapeDtypeStruct((b, l, w), BF16), jax.ShapeDtypeStruct((b, l, w), BF16)),
        grid=(nf, b),
        in_specs=[pl.BlockSpec((tf, l), lambda i, j: (i, 0)),
                  pl.BlockSpec((tf, l), lambda i, j: (i + nf, 0)),
                  pl.BlockSpec((1, l, w), lambda i, j: (j, 0, 0)),
                  spec, spec, spec],
        out_specs=(pl.BlockSpec((1, tf, w), lambda i, j: (j, i, 0)),
                   pl.BlockSpec((1, tf, w), lambda i, j: (j, i, 0))),
        compiler_params=_params(("arbitrary", "arbitrary")),
        name="dft_mul",
    )(fmat, fmat, g, sa, sb, sd)


def _idft_kernel(fr_ref, fi_ref, zr_ref, zi_ref, g_ref, x0_ref, hb_ref, wn_ref, o_ref):
    conv = (jnp.dot(fr_ref[...], zr_ref[0], preferred_element_type=F32)
            + jnp.dot(fi_ref[...], zi_ref[0], preferred_element_type=F32))
    y = (conv + g_ref[0] * hb_ref[...]) * x0_ref[0]
    o_ref[0] = _rms(y, wn_ref[...]).astype(BF16)


def _idft(finv_r, finv_i, zr, zi, g32, x0, hbias, wn, tt):
    b, l, w = zr.shape
    tile = pl.BlockSpec((1, tt, w), lambda i, j: (j, i, 0))
    return pl.pallas_call(
        _idft_kernel,
        out_shape=jax.ShapeDtypeStruct((b, l, w), BF16),
        grid=(l // tt, b),
        in_specs=[pl.BlockSpec((tt, l), lambda i, j: (i, 0)),
                  pl.BlockSpec((tt, l), lambda i, j: (i, 0)),
                  pl.BlockSpec((1, l, w), lambda i, j: (j, 0, 0)),
                  pl.BlockSpec((1, l, w), lambda i, j: (j, 0, 0)),
                  tile, tile,
                  pl.BlockSpec((1, w), lambda i, j: (0, 0)),
                  pl.BlockSpec((1, w), lambda i, j: (0, 0))],
        out_specs=tile,
        compiler_params=_params(("arbitrary", "arbitrary")),
        name="idft",
    )(finv_r, finv_i, zr, zi, g32, x0, hbias, wn)


def _bf16_bits(v):
    return lax.bitcast_convert_type(v.astype(BF16).astype(F32), jnp.uint32)


def _store_row_tiles(ref, val):
    rows, half = val.shape[0], val.shape[1] // 2
    assert half == ROW_SUB * LANES
    for j in range(ROW_SUB):
        lo = _bf16_bits(val[:, LANES * j:LANES * (j + 1)]) >> 16
        hi = _bf16_bits(val[:, half + LANES * j:half + LANES * (j + 1)]) & jnp.uint32(0xFFFF0000)
        ref[pl.ds(j, rows, stride=ROW_SUB), :] = lo | hi


def _load_row_tiles(ref, rows):
    words = [ref[pl.ds(j, rows, stride=ROW_SUB), :] for j in range(ROW_SUB)]
    lo = [lax.bitcast_convert_type(w << 16, F32) for w in words]
    hi = [lax.bitcast_convert_type(w & jnp.uint32(0xFFFF0000), F32) for w in words]
    return jnp.concatenate(lo + hi, axis=1)


def _merge_kernel(a_ref, y_ref, x_ref, mod_ref, wa_ref, wy_ref, n2_ref, sg_ref, su_ref, sd_ref,
                  base_ref, hi_ref, lo_ref, rt_ref):
    m = (jnp.dot(a_ref[0], wa_ref[...], preferred_element_type=F32)
         + jnp.dot(y_ref[0], wy_ref[...], preferred_element_type=F32))
    x1 = x_ref[0] + mod_ref[0, 2:3, :] * m
    h2 = _modulated(x1, n2_ref[...], mod_ref[0, 3:4, :], mod_ref[0, 4:5, :])
    hi = h2.astype(BF16)
    hi_ref[0] = hi
    lo_ref[0] = (h2 - hi.astype(F32)).astype(BF16)
    _store_row_tiles(rt_ref, h2)
    gate = jnp.dot(hi, sg_ref[...], preferred_element_type=F32)
    up = jnp.dot(hi, su_ref[...], preferred_element_type=F32)
    act = (gate * _sigmoid(gate) * up).astype(BF16)
    shared = jnp.dot(act, sd_ref[...], preferred_element_type=F32)
    base_ref[0] = x1 + mod_ref[0, 5:6, :] * shared


def _merge(an, yn, x, mods, wa, wy, n2w, sg, su, sd, tl):
    b, l, d = x.shape
    full = lambda a: pl.BlockSpec(a.shape, lambda i, j: (0,) * a.ndim)
    half = pl.BlockSpec((1, tl, an.shape[2]), lambda i, j: (i, j, 0))
    wide = pl.BlockSpec((1, tl, d), lambda i, j: (i, j, 0))
    per_b = l // tl
    return pl.pallas_call(
        _merge_kernel,
        out_shape=(jax.ShapeDtypeStruct((b, l, d), F32),
                   jax.ShapeDtypeStruct((b, l, d), BF16),
                   jax.ShapeDtypeStruct((b, l, d), BF16),
                   jax.ShapeDtypeStruct((b * l * ROW_SUB, LANES), jnp.uint32)),
        grid=(b, per_b),
        in_specs=[half, half, wide,
                  pl.BlockSpec((1, 6, d), lambda i, j: (i, 0, 0)),
                  full(wa), full(wy), full(n2w), full(sg), full(su), full(sd)],
        out_specs=(wide, wide, wide,
                   pl.BlockSpec((tl * ROW_SUB, LANES), lambda i, j: (i * per_b + j, 0))),
        compiler_params=_params(("arbitrary", "arbitrary")),
        name="merge",
    )(an, yn, x, mods, wa, wy, n2w, sg, su, sd)


def _router_kernel(hi_ref, lo_ref, whi_ref, wlo_ref, bias_ref, tri_ref,
                   idx_ref, wgt_ref, rank_ref, cnt_ref, run_ref):
    tt = hi_ref.shape[0]
    per_group = N_EXPERTS // N_GROUPS

    @pl.when(pl.program_id(0) == 0)
    def _():
        run_ref[...] = jnp.zeros_like(run_ref)

    nt = (((1,), (1,)), ((), ()))
    hi = hi_ref[...]
    whi = whi_ref[...]
    logits = (lax.dot_general(whi, hi, nt, preferred_element_type=F32)
              + lax.dot_general(whi, lo_ref[...], nt, preferred_element_type=F32)
              + lax.dot_general(wlo_ref[...], hi, nt, preferred_element_type=F32))
    scores = _sigmoid(logits)
    biased = scores + bias_ref[...]

    ridx = lax.broadcasted_iota(jnp.int32, (per_group, tt), 0)
    groups = [biased[g * per_group:(g + 1) * per_group, :] for g in range(N_GROUPS)]
    gs = []
    for blk in groups:
        m1 = jnp.max(blk, axis=0, keepdims=True)
        i1 = jnp.min(jnp.where(blk == m1, ridx, per_group), axis=0, keepdims=True)
        m2 = jnp.max(jnp.where(ridx == i1, NEG_INF, blk), axis=0, keepdims=True)
        gs.append(m1 + m2)

    kept = []
    for g in range(N_GROUPS):
        ahead = jnp.zeros((1, tt), F32)
        for o in range(N_GROUPS):
            if o != g:
                wins = (gs[o] >= gs[g]) if o < g else (gs[o] > gs[g])
                ahead = ahead + jnp.where(wins, 1.0, 0.0)
        kept.append(jnp.where(ahead < TOPK_GROUPS, groups[g], NEG_INF))
    cur = jnp.concatenate(kept, axis=0)

    eidx = lax.broadcasted_iota(jnp.int32, cur.shape, 0)
    onehot = jnp.zeros(cur.shape, F32)
    picks = []
    wsel = []
    for _ in range(TOP_K):
        mx = jnp.max(cur, axis=0, keepdims=True)
        first = jnp.min(jnp.where(cur == mx, eidx, N_EXPERTS), axis=0, keepdims=True)
        sel = eidx == first
        picks.append(first)
        wsel.append(jnp.sum(jnp.where(sel, scores, 0.0), axis=0, keepdims=True))
        onehot = jnp.where(sel, 1.0, onehot)
        cur = jnp.where(sel, NEG_INF, cur)
    w = jnp.concatenate(wsel, axis=0)
    w = w / jnp.sum(w, axis=0, keepdims=True) * ROUTE_SCALE
    idx = jnp.concatenate(picks, axis=0)

    oh = onehot.astype(BF16)
    before = jnp.dot(oh, tri_ref[0], preferred_element_type=F32)
    total = jnp.dot(oh, tri_ref[1], preferred_element_type=F32)
    pos = run_ref[...] + before
    ranks = [jnp.sum(jnp.where(eidx == p, pos, 0.0), axis=0, keepdims=True) for p in picks]
    run_ref[...] = run_ref[...] + total

    idx_ref[...] = idx
    wgt_ref[...] = w
    rank_ref[...] = jnp.concatenate(ranks, axis=0).astype(jnp.int32)
    cnt_ref[...] = run_ref[...]


def _router(hi, lo, whi, wlo, bias, tri, tt):
    t, d = hi.shape
    tok = pl.BlockSpec((tt, d), lambda i: (i, 0))
    full = lambda a: pl.BlockSpec(a.shape, lambda i: (0,) * a.ndim)
    out = pl.BlockSpec((TOP_K, tt), lambda i: (0, i))
    return pl.pallas_call(
        _router_kernel,
        out_shape=(jax.ShapeDtypeStruct((TOP_K, t), jnp.int32),
                   jax.ShapeDtypeStruct((TOP_K, t), F32),
                   jax.ShapeDtypeStruct((TOP_K, t), jnp.int32),
                   jax.ShapeDtypeStruct((N_EXPERTS, tt), F32)),
        grid=(t // tt,),
        in_specs=[tok, tok, full(whi), full(wlo), full(bias), full(tri)],
        out_specs=(out, out, out, pl.BlockSpec((N_EXPERTS, tt), lambda i: (0, 0))),
        scratch_shapes=[pltpu.VMEM((N_EXPERTS, tt), F32)],
        compiler_params=_params(("arbitrary",)),
        name="router",
    )(hi, lo, whi, wlo, bias, tri)


def _row_tile(ref, r):
    return ref.at[pl.ds(pl.multiple_of(r * ROW_SUB, ROW_SUB), ROW_SUB), :]


def _dest_kernel(idx_ref, rank_ref, start_ref, dest_ref):
    eidx = lax.broadcasted_iota(jnp.int32, start_ref.shape, 0)
    start = start_ref[...]
    rows = [jnp.sum(jnp.where(eidx == idx_ref[k:k + 1, :], start, 0.0), axis=0, keepdims=True)
            for k in range(TOP_K)]
    dest_ref[...] = jnp.concatenate(rows, axis=0).astype(jnp.int32) + rank_ref[...]


def _dest(idx, rank, start, tt):
    t = idx.shape[1]
    blk = pl.BlockSpec((TOP_K, tt), lambda i: (0, i))
    return pl.pallas_call(
        _dest_kernel,
        out_shape=jax.ShapeDtypeStruct((TOP_K, t), jnp.int32),
        grid=(t // tt,),
        in_specs=[blk, blk, pl.BlockSpec(start.shape, lambda i: (0, 0))],
        out_specs=blk,
        compiler_params=_params(("arbitrary",)),
        name="dest",
    )(idx, rank, start)


def _dispatch_kernel(dest_ref, h_ref, xs_in_ref, xs_ref, sem):
    del xs_in_ref
    td = dest_ref.shape[1]

    def row_copy(t, k):
        return pltpu.make_async_copy(_row_tile(h_ref, t), _row_tile(xs_ref, dest_ref[k, t]), sem)

    def issue(t, carry):
        for k in range(TOP_K):
            row_copy(t, k).start()
        return carry

    lax.fori_loop(0, td, issue, 0)

    def drain(t, carry):
        for k in range(TOP_K):
            row_copy(t, k).wait()
        return carry

    lax.fori_loop(0, td, drain, 0)


def _dispatch(dest, h_rt, xs0, td):
    t = dest.shape[1]
    return pl.pallas_call(
        _dispatch_kernel,
        out_shape=jax.ShapeDtypeStruct(xs0.shape, xs0.dtype),
        grid=(t // td,),
        in_specs=[pl.BlockSpec((TOP_K, td), lambda i: (0, i), memory_space=pltpu.SMEM),
                  pl.BlockSpec((td * ROW_SUB, LANES), lambda i: (i, 0)),
                  pl.BlockSpec(memory_space=pl.ANY)],
        out_specs=pl.BlockSpec(memory_space=pl.ANY),
        scratch_shapes=[pltpu.SemaphoreType.DMA],
        input_output_aliases={2: 0},
        compiler_params=_params(("arbitrary",)),
        name="dispatch",
    )(dest, h_rt, xs0)


def _experts_kernel(be_ref, nu_ref, x_ref, wg_ref, wu_ref, wd_ref, y_ref, wgu_bf, wd_bf):
    i = pl.program_id(0)
    used = i < nu_ref[0]
    new_expert = jnp.logical_or(i == 0, be_ref[i] != be_ref[jnp.maximum(i - 1, 0)])

    @pl.when(jnp.logical_and(used, new_expert))
    def _():
        ff = wg_ref.shape[2]
        wgu_bf[:, :ff] = wg_ref[0].astype(BF16)
        wgu_bf[:, ff:] = wu_ref[0].astype(BF16)
        wd_bf[...] = wd_ref[0].astype(BF16)

    @pl.when(used)
    def _():
        ff = wg_ref.shape[2]
        x = _load_row_tiles(x_ref, EXPERT_ROWS).astype(BF16)
        gu = jnp.dot(x, wgu_bf[...], preferred_element_type=F32)
        gate = gu[:, :ff]
        act = (gate * _sigmoid(gate) * gu[:, ff:]).astype(BF16)
        _store_row_tiles(y_ref, jnp.dot(act, wd_bf[...], preferred_element_type=F32))


def _experts(blk_e, n_used, xs, wg, wu, wd):
    d, ff = wg.shape[1], wg.shape[2]
    nblk = xs.shape[0] // (EXPERT_ROWS * ROW_SUB)
    row = lambda i, be, nu: (jnp.minimum(i, nu[0] - 1), 0)
    wsel = lambda i, be, nu: (be[jnp.minimum(i, nu[0] - 1)], 0, 0)
    return pl.pallas_call(
        _experts_kernel,
        out_shape=jax.ShapeDtypeStruct(xs.shape, xs.dtype),
        grid_spec=pltpu.PrefetchScalarGridSpec(
            num_scalar_prefetch=2,
            grid=(nblk,),
            in_specs=[pl.BlockSpec((EXPERT_ROWS * ROW_SUB, LANES), row),
                      pl.BlockSpec((1, d, ff), wsel),
                      pl.BlockSpec((1, d, ff), wsel),
                      pl.BlockSpec((1, ff, d), wsel)],
            out_specs=pl.BlockSpec((EXPERT_ROWS * ROW_SUB, LANES), row),
            scratch_shapes=[pltpu.VMEM((d, 2 * ff), BF16), pltpu.VMEM((ff, d), BF16)]),
        compiler_params=_params(("arbitrary",)),
        name="experts",
    )(blk_e, n_used, xs, wg, wu, wd)


def _combine_kernel(dest_ref, next_ref, ys_ref, w_ref, base_ref, mod_ref, fw_ref, o_ref, buf, sem):
    tc = dest_ref.shape[1]
    i = pl.program_id(0)
    slot = i % 2

    def row_copy(row, s, t, k):
        return pltpu.make_async_copy(_row_tile(ys_ref, row), _row_tile(buf.at[s, k], t), sem.at[s])

    def issue(d_ref, s):
        def body(t, carry):
            for k in range(TOP_K):
                row_copy(d_ref[k, t], s, t, k).start()
            return carry

        lax.fori_loop(0, tc, body, 0)

    @pl.when(i == 0)
    def _():
        issue(dest_ref, 0)

    @pl.when(i + 1 < pl.num_programs(0))
    def _():
        issue(next_ref, 1 - slot)

    def drain(t, carry):
        for k in range(TOP_K):
            row_copy(0, slot, t, k).wait()
        return carry

    lax.fori_loop(0, tc, drain, 0)

    w = w_ref[...]
    routed = w[:, 0:1] * _load_row_tiles(buf.at[slot, 0], tc)
    for k in range(1, TOP_K):
        routed = routed + w[:, k:k + 1] * _load_row_tiles(buf.at[slot, k], tc)
    x = base_ref[0] + mod_ref[0, 5:6, :] * routed
    o_ref[0] = _rms(x, fw_ref[...])


def _combine(dest, ys, wt, base, mods, fw, tc):
    b, l, d = base.shape
    per_b = l // tc
    steps = b * per_b
    return pl.pallas_call(
        _combine_kernel,
        out_shape=jax.ShapeDtypeStruct((b, l, d), F32),
        grid=(steps,),
        in_specs=[pl.BlockSpec((TOP_K, tc), lambda i: (0, i), memory_space=pltpu.SMEM),
                  pl.BlockSpec((TOP_K, tc), lambda i: (0, jnp.minimum(i + 1, steps - 1)), memory_space=pltpu.SMEM),
                  pl.BlockSpec(memory_space=pl.ANY),
                  pl.BlockSpec((tc, TOP_K), lambda i: (i, 0)),
                  pl.BlockSpec((1, tc, d), lambda i: (i // per_b, i % per_b, 0)),
                  pl.BlockSpec((1, 6, d), lambda i: (i // per_b, 0, 0)),
                  pl.BlockSpec((1, d), lambda i: (0, 0))],
        out_specs=pl.BlockSpec((1, tc, d), lambda i: (i // per_b, i % per_b, 0)),
        scratch_shapes=[pltpu.VMEM((2, TOP_K, tc * ROW_SUB, LANES), jnp.uint32), pltpu.SemaphoreType.DMA((2,))],
        compiler_params=_params(("arbitrary",)),
        name="combine",
    )(dest, dest, ys, wt, base, mods, fw)


def _rope_tables(l):
    t = jnp.arange(l, dtype=jnp.int32)
    row = (t // GRID_W).astype(F32)
    col = (t % GRID_W).astype(F32)
    n_freq = HEAD_DIM // 4
    inv = ROPE_THETA ** (-jnp.arange(n_freq, dtype=F32) / n_freq)
    ang = jnp.concatenate([row[:, None] * inv, col[:, None] * inv], axis=-1)
    cos = jnp.repeat(jnp.cos(ang), 2, axis=1)
    sin = jnp.repeat(jnp.sin(ang), 2, axis=1)
    sign = jnp.tile(jnp.array([-1.0, 1.0], F32), HEAD_DIM // 2)
    reps = LANES // HEAD_DIM
    return jnp.tile(cos, (1, reps)), jnp.tile(sin * sign, (1, reps))


def _filter_features(l):
    t = jnp.linspace(0.0, 1.0, l, dtype=F32)[:, None]
    bands = (FILTER_EMB - 1) // 2
    w = 2.0 * math.pi * jnp.arange(l, dtype=F32)[:, None] / l
    f = jnp.linspace(1e-4, bands - 1, bands, dtype=F32)[None, :]
    z = jnp.concatenate([t, jnp.cos(f * w), -jnp.sin(f * w)], axis=-1)
    min_decay = math.log(FILTER_TARGET) / FILTER_DECAY_FAST
    max_decay = math.log(FILTER_TARGET) / FILTER_DECAY_SLOW
    deltas = jnp.linspace(min_decay, max_decay, HYENA_WIDTH, dtype=F32)[None, :]
    return jnp.pad(z, ((0, 0), (0, LANES - FILTER_EMB))), deltas


def _dft_matrices(l):
    n = 2 * l
    idx = jnp.arange(l, dtype=jnp.int32)
    r = math.isqrt(l)
    assert r * r == l
    sub = jnp.arange(r, dtype=jnp.int32)
    hi = ((r * sub[:, None] * idx[None, :]) % n).astype(F32) * (2.0 * math.pi / n)
    lo = ((sub[:, None] * idx[None, :]) % n).astype(F32) * (2.0 * math.pi / n)
    ch, sh, cl, sl = jnp.cos(hi)[:, None, :], jnp.sin(hi)[:, None, :], jnp.cos(lo)[None], jnp.sin(lo)[None]
    c = (ch * cl - sh * sl).reshape(l, l)
    s = (sh * cl + ch * sl).reshape(l, l)
    alt = jnp.where(idx % 2 == 0, 1.0, -1.0).astype(F32)
    first = (idx == 0)[:, None]
    fwd = jnp.concatenate([c, jnp.where(first, alt[None, :], -s)], axis=0).astype(BF16)
    firstc = (idx == 0)[None, :]
    inv_r = (jnp.where(firstc, 1.0, 2.0) * c / n).astype(BF16)
    inv_i = (jnp.where(firstc, alt[:, None], -2.0 * s) / n).astype(BF16)
    return fwd, inv_r, inv_i


def _head_perm():
    order = []
    for j in range(N_HEADS // 2):
        order += list(range(j * HEAD_DIM, (j + 1) * HEAD_DIM))
        order += list(range((j + N_HEADS // 2) * HEAD_DIM, (j + 1 + N_HEADS // 2) * HEAD_DIM))
    return jnp.array(order, jnp.int32)


def _pad2(a, rows, cols):
    return jnp.pad(a, ((0, rows - a.shape[0]), (0, cols - a.shape[1])))


def kernel(x, c, ctx, c_ctx, mod_w, mod_b, norm1_w, w_in, q_norm_w, k_norm_w, conv_w, conv_b, filt_w1, filt_b1, filt_w2, filt_b2, filt_w3, filt_b3, filt_w4, filt_freq, hyena_bias, attn_out_norm_w, hyena_out_norm_w, w_out, norm2_w, router_w, router_bias, exp_w_gate, exp_w_up, exp_w_down, sh_w_gate, sh_w_up, sh_w_down, final_norm_w):
    b, l, d = x.shape
    t = b * l
    assert mod_w.shape[0] == 1, "single-layer stack"
    tl = min(512, l)

    cond = jnp.concatenate([c, c_ctx[None, :], jnp.zeros((-(b + 1) % 8, d), F32)], axis=0)
    mod = _adaln(cond, mod_w[0], mod_b[0][None, :])
    mods = mod[:b].reshape(b, 6, d)
    cmod = mod[b].reshape(6, d)

    perm = _head_perm()
    w_in0 = w_in[0]
    w_in_k = jnp.concatenate([w_in0[:, :Q_END][:, perm], w_in0[:, Q_END:]], axis=1).astype(BF16)
    w_kv = w_in0[:, Q_END:V_END].astype(BF16)
    gq = jnp.kron(jnp.eye(N_HEADS, dtype=F32), jnp.full((HEAD_DIM, HEAD_DIM), 1.0 / HEAD_DIM, F32)).astype(BF16)
    qnw = jnp.tile(q_norm_w[0], N_HEADS)[None, :]
    knw = jnp.tile(k_norm_w[0], N_KV_HEADS)[None, :]
    n1w = norm1_w[0][None, :]
    cos, sin = _rope_tables(l)

    kc, vc = _ctx_kv(ctx, cmod, n1w, w_kv, gq[:KV_WIDTH, :KV_WIDTH], knw)
    q, k, v, u = _inproj(x, mods, n1w, w_in_k, gq, qnw, knw, cos, sin, tl)
    k_all = jnp.concatenate([kc, k], axis=1)
    v_all = jnp.concatenate([vc, v], axis=1)
    an = _attention(q, k_all, v_all, attn_out_norm_w[0][perm][None, :], min(256, l), min(256, l))

    cw = conv_w[0].reshape(3, 3, HYENA_WIDTH).transpose(1, 0, 2)
    cb = conv_b[0].reshape(3, HYENA_WIDTH)
    g32, gbf, x0 = _hyena_pre(u, cw, cb)
    z, deltas = _filter_features(l)
    fo = filt_w2.shape[1]
    hsd = _hyena_filter(
        z, _pad2(filt_w1[0], LANES, LANES), _pad2(filt_b1[0][None, :], 1, LANES),
        _pad2(filt_w2[0], LANES, LANES), _pad2(filt_b2[0][None, :], 1, LANES),
        _pad2(filt_w3[0], LANES, LANES), _pad2(filt_b3[0][None, :], 1, LANES),
        _pad2(filt_w4[0], LANES, 2 * HYENA_WIDTH), _pad2(filt_freq[0][None, :], 1, LANES), deltas, tl)
    del fo
    fwd, inv_r, inv_i = _dft_matrices(l)
    spec = _dft(fwd, hsd, tl)
    row0 = (jnp.arange(l) == 0)[:, None]
    sa = spec[0, :l]
    sd = jnp.where(row0, spec[0, l:l + 1], sa)
    sb = jnp.where(row0, 0.0, spec[1, l:])
    zr, zi = _dft_mul(fwd, gbf, sa, sb, sd, tl)
    yn = _idft(inv_r, inv_i, zr, zi, g32, x0, hyena_bias[0][None, :], hyena_out_norm_w[0][None, :], tl)

    w_out0 = w_out[0]
    base, h2hi, h2lo, h2rt = _merge(
        an, yn, x, mods, w_out0[:ATTN_WIDTH][perm].astype(BF16), w_out0[ATTN_WIDTH:].astype(BF16),
        norm2_w[0][None, :], sh_w_gate[0].astype(BF16), sh_w_up[0].astype(BF16), sh_w_down[0].astype(BF16), tl)

    tt = 256
    rwt = router_w[0].T
    rw_hi = rwt.astype(BF16)
    rw_lo = (rwt - rw_hi.astype(F32)).astype(BF16)
    bias = jnp.broadcast_to(router_bias[0][:, None], (N_EXPERTS, tt))
    ti = jnp.arange(tt)
    tri = jnp.stack([(ti[:, None] < ti[None, :]), jnp.ones((tt, tt), bool)]).astype(BF16)
    idx, wgt, rank, cnt = _router(h2hi.reshape(t, d), h2lo.reshape(t, d), rw_hi, rw_lo, bias, tri, tt)

    counts = cnt[:, 0].astype(jnp.int32)
    padded = (counts + EXPERT_ROWS - 1) // EXPERT_ROWS * EXPERT_ROWS
    pad_end = jnp.cumsum(padded)
    pad_start = pad_end - padded
    dest = _dest(idx, rank, jnp.broadcast_to(pad_start.astype(F32)[:, None], (N_EXPERTS, tt)), tt)
    n_rows = (t * TOP_K + N_EXPERTS * (EXPERT_ROWS - 1) + EXPERT_ROWS - 1) // EXPERT_ROWS * EXPERT_ROWS
    nblk = n_rows // EXPERT_ROWS
    blk_row = jnp.arange(nblk, dtype=jnp.int32) * EXPERT_ROWS
    blk_e = jnp.minimum(jnp.sum((pad_end[None, :] <= blk_row[:, None]).astype(jnp.int32), axis=1), N_EXPERTS - 1)
    n_used = (pad_end[-1:] // EXPERT_ROWS).astype(jnp.int32)

    xs = _dispatch(dest, h2rt, jnp.zeros((n_rows * ROW_SUB, LANES), jnp.uint32), min(512, t))
    ys = _experts(blk_e, n_used, xs, exp_w_gate[0], exp_w_up[0], exp_w_down[0])
    return _combine(dest, ys, wgt.T, base, mods, final_norm_w[None, :], min(128, l))
```

```python
import functools
import math

import jax
import jax.numpy as jnp
from jax import lax
from jax.experimental import pallas as pl
from jax.experimental.pallas import tpu as pltpu

F32 = jnp.float32
BF16 = jnp.bfloat16
HIGHEST = lax.Precision.HIGHEST

GRID_W = 64
N_HEADS = 8
N_KV_HEADS = 2
HEAD_DIM = 64
ATTN_WIDTH = N_HEADS * HEAD_DIM
KV_WIDTH = N_KV_HEADS * HEAD_DIM
HYENA_WIDTH = 512
Q_END = ATTN_WIDTH
K_END = Q_END + KV_WIDTH
V_END = K_END + KV_WIDTH
ROPE_THETA = 10000.0
FILTER_EMB = 33
FILTER_DECAY_FAST = 0.3
FILTER_DECAY_SLOW = 1.5
FILTER_TARGET = 1e-2
N_EXPERTS = 256
TOP_K = 8
N_GROUPS = 8
TOPK_GROUPS = 4
ROUTE_SCALE = 2.5
EPS = 1e-6

LANES = 128
ROW_SUB = 4
EXPERT_ROWS = 512
EXPERT_SUB = 256
NEG_INF = float("-inf")


def _params(semantics, vmem_mb=48):
    return pltpu.CompilerParams(dimension_semantics=semantics, vmem_limit_bytes=vmem_mb * 1024 * 1024)


def _rms(x, w):
    return x * lax.rsqrt(jnp.mean(x * x, axis=-1, keepdims=True) + EPS) * w


def _sigmoid(x):
    return 1.0 / (1.0 + jnp.exp(-x))


def _adaln_kernel(c_ref, w_ref, b_ref, o_ref):
    c = c_ref[...]
    s = c * _sigmoid(c)
    o_ref[...] = jnp.dot(s, w_ref[...], precision=HIGHEST, preferred_element_type=F32) + b_ref[...]


def _adaln(cond, w, b):
    rows, d = cond.shape
    n = w.shape[1]
    tn = 1536
    return pl.pallas_call(
        _adaln_kernel,
        out_shape=jax.ShapeDtypeStruct((rows, n), F32),
        grid=(n // tn,),
        in_specs=[pl.BlockSpec((rows, d), lambda j: (0, 0)),
                  pl.BlockSpec((d, tn), lambda j: (0, j)),
                  pl.BlockSpec((1, tn), lambda j: (0, j))],
        out_specs=pl.BlockSpec((rows, tn), lambda j: (0, j)),
        compiler_params=_params(("arbitrary",)),
        name="adaln",
    )(cond, w, b)


def _head_rms(t, gmat, w):
    ms = jnp.dot((t * t).astype(BF16), gmat, preferred_element_type=F32)
    return t * lax.rsqrt(ms + EPS) * w


def _modulated(x, norm_w, shift, scale):
    return _rms(x, norm_w) * (1.0 + scale) + shift


def _ctx_kv_kernel(ctx_ref, mod_ref, n1_ref, w_ref, g_ref, kn_ref, kc_ref, vc_ref):
    x = ctx_ref[0]
    h = _modulated(x, n1_ref[...], mod_ref[0:1, :], mod_ref[1:2, :])
    kv = jnp.dot(h.astype(BF16), w_ref[...], preferred_element_type=F32)
    k = _head_rms(kv[:, :KV_WIDTH], g_ref[...], kn_ref[...])
    v = kv[:, KV_WIDTH:]
    kc_ref[0] = k.astype(BF16)
    vc_ref[0] = jnp.concatenate([v, jnp.ones_like(v)], axis=1).astype(BF16)


def _ctx_kv(ctx, cmod, n1w, w_kv, gk, knw):
    b, c, d = ctx.shape
    return pl.pallas_call(
        _ctx_kv_kernel,
        out_shape=(jax.ShapeDtypeStruct((b, c, KV_WIDTH), BF16),
                   jax.ShapeDtypeStruct((b, c, 2 * KV_WIDTH), BF16)),
        grid=(b,),
        in_specs=[pl.BlockSpec((1, c, d), lambda i: (i, 0, 0)),
                  pl.BlockSpec(cmod.shape, lambda i: (0, 0)),
                  pl.BlockSpec((1, d), lambda i: (0, 0)),
                  pl.BlockSpec(w_kv.shape, lambda i: (0, 0)),
                  pl.BlockSpec(gk.shape, lambda i: (0, 0)),
                  pl.BlockSpec((1, KV_WIDTH), lambda i: (0, 0))],
        out_specs=(pl.BlockSpec((1, c, KV_WIDTH), lambda i: (i, 0, 0)),
                   pl.BlockSpec((1, c, 2 * KV_WIDTH), lambda i: (i, 0, 0))),
        compiler_params=_params(("arbitrary",)),
        name="ctx_kv",
    )(ctx, cmod, n1w, w_kv, gk, knw)


def _rope(t, cos, sin, even):
    width = t.shape[1]
    partner = jnp.where(even, pltpu.roll(t, width - 1, axis=1), pltpu.roll(t, 1, axis=1))
    return t * cos + partner * sin


def _inproj_kernel(x_ref, mod_ref, n1_ref, w_ref, gq_ref, qn_ref, kn_ref, cos_ref, sin_ref,
                   q_ref, k_ref, v_ref, u_ref):
    x = x_ref[0]
    h = _modulated(x, n1_ref[...], mod_ref[0, 0:1, :], mod_ref[0, 1:2, :])
    p = jnp.dot(h.astype(BF16), w_ref[...], preferred_element_type=F32)
    gq = gq_ref[...]
    q = _head_rms(p[:, :Q_END], gq, qn_ref[...])
    k = _head_rms(p[:, Q_END:K_END], gq[:KV_WIDTH, :KV_WIDTH], kn_ref[...])
    v = p[:, K_END:V_END]
    cos = cos_ref[...]
    sin = sin_ref[...]
    reps = Q_END // LANES
    cos_q = jnp.concatenate([cos] * reps, axis=1)
    sin_q = jnp.concatenate([sin] * reps, axis=1)
    even_q = (lax.broadcasted_iota(jnp.int32, (1, Q_END), 1) & 1) == 0
    even_k = (lax.broadcasted_iota(jnp.int32, (1, KV_WIDTH), 1) & 1) == 0
    q = _rope(q, cos_q, sin_q, even_q) * (HEAD_DIM ** -0.5)
    k = _rope(k, cos, sin, even_k)
    q_ref[0] = q.astype(BF16)
    k_ref[0] = k.astype(BF16)
    v_ref[0] = jnp.concatenate([v, jnp.ones_like(v)], axis=1).astype(BF16)
    u_ref[0] = p[:, V_END:]


def _inproj(x, mods, n1w, w_in, gq, qnw, knw, cos, sin, tl):
    b, l, d = x.shape
    ncol = w_in.shape[1]
    nu = ncol - V_END
    return pl.pallas_call(
        _inproj_kernel,
        out_shape=(jax.ShapeDtypeStruct((b, l, Q_END), BF16),
                   jax.ShapeDtypeStruct((b, l, KV_WIDTH), BF16),
                   jax.ShapeDtypeStruct((b, l, 2 * KV_WIDTH), BF16),
                   jax.ShapeDtypeStruct((b, l, nu), F32)),
        grid=(l // tl, b),
        in_specs=[pl.BlockSpec((1, tl, d), lambda i, j: (j, i, 0)),
                  pl.BlockSpec((1, 6, d), lambda i, j: (j, 0, 0)),
                  pl.BlockSpec((1, d), lambda i, j: (0, 0)),
                  pl.BlockSpec((d, ncol), lambda i, j: (0, 0)),
                  pl.BlockSpec(gq.shape, lambda i, j: (0, 0)),
                  pl.BlockSpec((1, Q_END), lambda i, j: (0, 0)),
                  pl.BlockSpec((1, KV_WIDTH), lambda i, j: (0, 0)),
                  pl.BlockSpec((tl, LANES), lambda i, j: (i, 0)),
                  pl.BlockSpec((tl, LANES), lambda i, j: (i, 0))],
        out_specs=(pl.BlockSpec((1, tl, Q_END), lambda i, j: (j, i, 0)),
                   pl.BlockSpec((1, tl, KV_WIDTH), lambda i, j: (j, i, 0)),
                   pl.BlockSpec((1, tl, 2 * KV_WIDTH), lambda i, j: (j, i, 0)),
                   pl.BlockSpec((1, tl, nu), lambda i, j: (j, i, 0))),
        compiler_params=_params(("arbitrary", "arbitrary")),
        name="inproj",
    )(x, mods, n1w, w_in, gq, qnw, knw, cos, sin)


def _attn_kernel(q_ref, k_ref, v_ref, wn_ref, o_ref, *, sub):
    tq = q_ref.shape[1]
    low = lax.broadcasted_iota(jnp.int32, (1, LANES), 1) < HEAD_DIM
    nt = (((1,), (1,)), ((), ()))
    kk = k_ref[0]
    vv = v_ref[0]
    for r in range(0, tq, sub):
        outs = []
        for j in range(Q_END // LANES):
            qv = q_ref[0, r:r + sub, LANES * j:LANES * (j + 1)]
            zero = jnp.zeros_like(qv)
            halves = []
            for g in range(N_KV_HEADS):
                qh = jnp.where(low, qv, zero) if g == 0 else jnp.where(low, zero, qv)
                s = lax.dot_general(qh, kk, nt, preferred_element_type=F32)
                p = jnp.exp(s - jnp.max(s, axis=-1, keepdims=True)).astype(BF16)
                pv = jnp.dot(p, vv, preferred_element_type=F32)
                halves.append(pv[:, :LANES] / pv[:, LANES:])
            outs.append(jnp.where(low, halves[0], halves[1]))
        a = jnp.concatenate(outs, axis=1)
        o_ref[0, r:r + sub, :] = _rms(a, wn_ref[...]).astype(BF16)


def _attention(q, k, v, wn, tq, sub):
    b, l, _ = q.shape
    n = k.shape[1]
    return pl.pallas_call(
        functools.partial(_attn_kernel, sub=sub),
        out_shape=jax.ShapeDtypeStruct((b, l, Q_END), BF16),
        grid=(b, l // tq),
        in_specs=[pl.BlockSpec((1, tq, Q_END), lambda i, j: (i, j, 0)),
                  pl.BlockSpec((1, n, KV_WIDTH), lambda i, j: (i, 0, 0)),
                  pl.BlockSpec((1, n, 2 * KV_WIDTH), lambda i, j: (i, 0, 0)),
                  pl.BlockSpec((1, Q_END), lambda i, j: (0, 0))],
        out_specs=pl.BlockSpec((1, tq, Q_END), lambda i, j: (i, j, 0)),
        compiler_params=_params(("arbitrary", "arbitrary")),
        name="attn",
    )(q, k, v, wn)


def _hyena_pre_kernel(u0_ref, u1_ref, u2_ref, cw_ref, cb_ref, g32_ref, gbf_ref, x0_ref):
    l = u0_ref.shape[1]
    row = lax.broadcasted_iota(jnp.int32, (l, LANES), 0)

    def conv(u_ref, gi):
        u = u_ref[0]
        prev = jnp.where(row == 0, 0.0, pltpu.roll(u, 1, axis=0))
        nxt = jnp.where(row == l - 1, 0.0, pltpu.roll(u, l - 1, axis=0))
        w = cw_ref[gi]
        return w[0:1] * prev + w[1:2] * u + w[2:3] * nxt + cb_ref[gi:gi + 1, :]

    x0 = conv(u0_ref, 0)
    x1 = conv(u1_ref, 1)
    v = conv(u2_ref, 2)
    g = v * x1
    g32_ref[0] = g
    gbf_ref[0] = g.astype(BF16)
    x0_ref[0] = x0


def _hyena_pre(u, cw, cb):
    b, l, _ = u.shape
    nblk = HYENA_WIDTH // LANES
    ublk = lambda gi: pl.BlockSpec((1, l, LANES), lambda i, j: (i, 0, gi * nblk + j))
    oblk = pl.BlockSpec((1, l, LANES), lambda i, j: (i, 0, j))
    return pl.pallas_call(
        _hyena_pre_kernel,
        out_shape=(jax.ShapeDtypeStruct((b, l, HYENA_WIDTH), F32),
                   jax.ShapeDtypeStruct((b, l, HYENA_WIDTH), BF16),
                   jax.ShapeDtypeStruct((b, l, HYENA_WIDTH), F32)),
        grid=(b, nblk),
        in_specs=[ublk(0), ublk(1), ublk(2),
                  pl.BlockSpec((3, 3, LANES), lambda i, j: (0, 0, j)),
                  pl.BlockSpec((3, LANES), lambda i, j: (0, j))],
        out_specs=(oblk, oblk, oblk),
        compiler_params=_params(("arbitrary", "arbitrary")),
        name="hyena_pre",
    )(u, u, u, cw, cb)


def _filter_kernel(z_ref, w1_ref, b1_ref, w2_ref, b2_ref, w3_ref, b3_ref, w4_ref, fr_ref, dl_ref, o_ref):
    tl = z_ref.shape[0]
    z = z_ref[...]
    fr = fr_ref[...]
    dot = lambda a, w: jnp.dot(a, w, precision=HIGHEST, preferred_element_type=F32)
    h = jnp.sin(fr * (dot(z, w1_ref[...]) + b1_ref[...]))
    h = jnp.sin(fr * (dot(h, w2_ref[...]) + b2_ref[...]))
    h = jnp.sin(fr * (dot(h, w3_ref[...]) + b3_ref[...]))
    h = dot(h, w4_ref[...])
    t = z[:, 0:1]
    decay = jnp.exp(-t * jnp.abs(dl_ref[...]))
    hf = h[:, :HYENA_WIDTH] * decay
    hb = h[:, HYENA_WIDTH:] * decay
    row = lax.broadcasted_iota(jnp.int32, (tl, HYENA_WIDTH), 0) + pl.program_id(0) * tl
    hb = jnp.where(row == 0, 0.0, hb)
    o_ref[0] = hf + hb
    o_ref[1] = hf - hb


def _hyena_filter(z, w1, b1, w2, b2, w3, b3, w4, freq, deltas, tl):
    l = z.shape[0]
    full = lambda a: pl.BlockSpec(a.shape, lambda i: (0,) * a.ndim)
    return pl.pallas_call(
        _filter_kernel,
        out_shape=jax.ShapeDtypeStruct((2, l, HYENA_WIDTH), F32),
        grid=(l // tl,),
        in_specs=[pl.BlockSpec((tl, z.shape[1]), lambda i: (i, 0)),
                  full(w1), full(b1), full(w2), full(b2), full(w3), full(b3), full(w4), full(freq), full(deltas)],
        out_specs=pl.BlockSpec((2, tl, HYENA_WIDTH), lambda i: (0, i, 0)),
        compiler_params=_params(("arbitrary",)),
        name="hyena_filter",
    )(z, w1, b1, w2, b2, w3, b3, w4, freq, deltas)


def _dft_kernel(f_ref, x_ref, o_ref):
    o_ref[0] = jnp.dot(f_ref[...], x_ref[0].astype(BF16), preferred_element_type=F32)


def _dft(fmat, x, tf):
    nb, l, w = x.shape
    n = fmat.shape[0]
    return pl.pallas_call(
        _dft_kernel,
        out_shape=jax.ShapeDtypeStruct((nb, n, w), F32),
        grid=(n // tf, nb),
        in_specs=[pl.BlockSpec((tf, l), lambda i, j: (i, 0)),
                  pl.BlockSpec((1, l, w), lambda i, j: (j, 0, 0))],
        out_specs=pl.BlockSpec((1, tf, w), lambda i, j: (j, i, 0)),
        compiler_params=_params(("arbitrary", "arbitrary")),
        name="dft_filter",
    )(fmat, x)


def _dft_mul_kernel(fr_ref, fi_ref, x_ref, a_ref, b_ref, d_ref, zr_ref, zi_ref):
    x = x_ref[0]
    xr = jnp.dot(fr_ref[...], x, preferred_element_type=F32)
    xi = jnp.dot(fi_ref[...], x, preferred_element_type=F32)
    bb = b_ref[...]
    zr_ref[0] = (xr * a_ref[...] - xi * bb).astype(BF16)
    zi_ref[0] = (xr * bb + xi * d_ref[...]).astype(BF16)


def _dft_mul(fmat, g, sa, sb, sd, tf):
    b, l, w = g.shape
    nf = l // tf
    spec = pl.BlockSpec((tf, w), lambda i, j: (i, 0))
    return pl.pallas_call(
        _dft_mul_kernel,
        out_shape=(jax.ShapeDtypeStruct((b, l, w), BF16), jax.ShapeDtypeStruct((b, l, w), BF16)),
        grid=(nf, b),
        in_specs=[pl.BlockSpec((tf, l), lambda i, j: (i, 0)),
                  pl.BlockSpec((tf, l), lambda i, j: (i + nf, 0)),
                  pl.BlockSpec((1, l, w), lambda i, j: (j, 0, 0)),
                  spec, spec, spec],
        out_specs=(pl.BlockSpec((1, tf, w), lambda i, j: (j, i, 0)),
                   pl.BlockSpec((1, tf, w), lambda i, j: (j, i, 0))),
        compiler_params=_params(("arbitrary", "arbitrary")),
        name="dft_mul",
    )(fmat, fmat, g, sa, sb, sd)


def _idft_kernel(fr_ref, fi_ref, zr_ref, zi_ref, g_ref, x0_ref, hb_ref, wn_ref, o_ref):
    conv = (jnp.dot(fr_ref[...], zr_ref[0], preferred_element_type=F32)
            + jnp.dot(fi_ref[...], zi_ref[0], preferred_element_type=F32))
    y = (conv + g_ref[0] * hb_ref[...]) * x0_ref[0]
    o_ref[0] = _rms(y, wn_ref[...]).astype(BF16)


def _idft(finv_r, finv_i, zr, zi, g32, x0, hbias, wn, tt):
    b, l, w = zr.shape
    tile = pl.BlockSpec((1, tt, w), lambda i, j: (j, i, 0))
    return pl.pallas_call(
        _idft_kernel,
        out_shape=jax.ShapeDtypeStruct((b, l, w), BF16),
        grid=(l // tt, b),
        in_specs=[pl.BlockSpec((tt, l), lambda i, j: (i, 0)),
                  pl.BlockSpec((tt, l), lambda i, j: (i, 0)),
                  pl.BlockSpec((1, l, w), lambda i, j: (j, 0, 0)),
                  pl.BlockSpec((1, l, w), lambda i, j: (j, 0, 0)),
                  tile, tile,
                  pl.BlockSpec((1, w), lambda i, j: (0, 0)),
                  pl.BlockSpec((1, w), lambda i, j: (0, 0))],
        out_specs=tile,
        compiler_params=_params(("arbitrary", "arbitrary")),
        name="idft",
    )(finv_r, finv_i, zr, zi, g32, x0, hbias, wn)


def _bf16_bits(v):
    return lax.bitcast_convert_type(v.astype(BF16).astype(F32), jnp.uint32)


def _store_row_tiles(ref, val):
    rows, half = val.shape[0], val.shape[1] // 2
    assert half == ROW_SUB * LANES
    for j in range(ROW_SUB):
        lo = _bf16_bits(val[:, LANES * j:LANES * (j + 1)]) >> 16
        hi = _bf16_bits(val[:, half + LANES * j:half + LANES * (j + 1)]) & jnp.uint32(0xFFFF0000)
        ref[pl.ds(j, rows, stride=ROW_SUB), :] = lo | hi


def _load_row_tiles(ref, rows):
    words = [ref[pl.ds(j, rows, stride=ROW_SUB), :] for j in range(ROW_SUB)]
    lo = [lax.bitcast_convert_type(w << 16, F32) for w in words]
    hi = [lax.bitcast_convert_type(w & jnp.uint32(0xFFFF0000), F32) for w in words]
    return jnp.concatenate(lo + hi, axis=1)


def _merge_kernel(a_ref, y_ref, x_ref, mod_ref, wa_ref, wy_ref, n2_ref, sg_ref, su_ref, sd_ref,
                  base_ref, hi_ref, lo_ref, rt_ref):
    m = (jnp.dot(a_ref[0], wa_ref[...], preferred_element_type=F32)
         + jnp.dot(y_ref[0], wy_ref[...], preferred_element_type=F32))
    x1 = x_ref[0] + mod_ref[0, 2:3, :] * m
    h2 = _modulated(x1, n2_ref[...], mod_ref[0, 3:4, :], mod_ref[0, 4:5, :])
    hi = h2.astype(BF16)
    hi_ref[0] = hi
    lo_ref[0] = (h2 - hi.astype(F32)).astype(BF16)
    _store_row_tiles(rt_ref, h2)
    gate = jnp.dot(hi, sg_ref[...], preferred_element_type=F32)
    up = jnp.dot(hi, su_ref[...], preferred_element_type=F32)
    act = (gate * _sigmoid(gate) * up).astype(BF16)
    shared = jnp.dot(act, sd_ref[...], preferred_element_type=F32)
    base_ref[0] = x1 + mod_ref[0, 5:6, :] * shared


def _merge(an, yn, x, mods, wa, wy, n2w, sg, su, sd, tl):
    b, l, d = x.shape
    full = lambda a: pl.BlockSpec(a.shape, lambda i, j: (0,) * a.ndim)
    half = pl.BlockSpec((1, tl, an.shape[2]), lambda i, j: (i, j, 0))
    wide = pl.BlockSpec((1, tl, d), lambda i, j: (i, j, 0))
    per_b = l // tl
    return pl.pallas_call(
        _merge_kernel,
        out_shape=(jax.ShapeDtypeStruct((b, l, d), F32),
                   jax.ShapeDtypeStruct((b, l, d), BF16),
                   jax.ShapeDtypeStruct((b, l, d), BF16),
                   jax.ShapeDtypeStruct((b * l * ROW_SUB, LANES), jnp.uint32)),
        grid=(b, per_b),
        in_specs=[half, half, wide,
                  pl.BlockSpec((1, 6, d), lambda i, j: (i, 0, 0)),
                  full(wa), full(wy), full(n2w), full(sg), full(su), full(sd)],
        out_specs=(wide, wide, wide,
                   pl.BlockSpec((tl * ROW_SUB, LANES), lambda i, j: (i * per_b + j, 0))),
        compiler_params=_params(("arbitrary", "arbitrary")),
        name="merge",
    )(an, yn, x, mods, wa, wy, n2w, sg, su, sd)


def _router_kernel(hi_ref, lo_ref, whi_ref, wlo_ref, bias_ref, tri_ref,
                   idx_ref, wgt_ref, rank_ref, cnt_ref, run_ref):
    tt = hi_ref.shape[0]
    per_group = N_EXPERTS // N_GROUPS

    @pl.when(pl.program_id(0) == 0)
    def _():
        run_ref[...] = jnp.zeros_like(run_ref)

    nt = (((1,), (1,)), ((), ()))
    hi = hi_ref[...]
    whi = whi_ref[...]
    logits = (lax.dot_general(whi, hi, nt, preferred_element_type=F32)
              + lax.dot_general(whi, lo_ref[...], nt, preferred_element_type=F32)
              + lax.dot_general(wlo_ref[...], hi, nt, preferred_element_type=F32))
    scores = _sigmoid(logits)
    biased = scores + bias_ref[...]

    ridx = lax.broadcasted_iota(jnp.int32, (per_group, tt), 0)
    groups = [biased[g * per_group:(g + 1) * per_group, :] for g in range(N_GROUPS)]
    gs = []
    for blk in groups:
        m1 = jnp.max(blk, axis=0, keepdims=True)
        i1 = jnp.min(jnp.where(blk == m1, ridx, per_group), axis=0, keepdims=True)
        m2 = jnp.max(jnp.where(ridx == i1, NEG_INF, blk), axis=0, keepdims=True)
        gs.append(m1 + m2)

    kept = []
    for g in range(N_GROUPS):
        ahead = jnp.zeros((1, tt), F32)
        for o in range(N_GROUPS):
            if o != g:
                wins = (gs[o] >= gs[g]) if o < g else (gs[o] > gs[g])
                ahead = ahead + jnp.where(wins, 1.0, 0.0)
        kept.append(jnp.where(ahead < TOPK_GROUPS, groups[g], NEG_INF))
    cur = jnp.concatenate(kept, axis=0)

    eidx = lax.broadcasted_iota(jnp.int32, cur.shape, 0)
    onehot = jnp.zeros(cur.shape, F32)
    picks = []
    wsel = []
    for _ in range(TOP_K):
        mx = jnp.max(cur, axis=0, keepdims=True)
        first = jnp.min(jnp.where(cur == mx, eidx, N_EXPERTS), axis=0, keepdims=True)
        sel = eidx == first
        picks.append(first)
        wsel.append(jnp.sum(jnp.where(sel, scores, 0.0), axis=0, keepdims=True))
        onehot = jnp.where(sel, 1.0, onehot)
        cur = jnp.where(sel, NEG_INF, cur)
    w = jnp.concatenate(wsel, axis=0)
    w = w / jnp.sum(w, axis=0, keepdims=True) * ROUTE_SCALE
    idx = jnp.concatenate(picks, axis=0)

    oh = onehot.astype(BF16)
    before = jnp.dot(oh, tri_ref[0], preferred_element_type=F32)
    total = jnp.dot(oh, tri_ref[1], preferred_element_type=F32)
    pos = run_ref[...] + before
    ranks = [jnp.sum(jnp.where(eidx == p, pos, 0.0), axis=0, keepdims=True) for p in picks]
    run_ref[...] = run_ref[...] + total

    idx_ref[...] = idx
    wgt_ref[...] = w
    rank_ref[...] = jnp.concatenate(ranks, axis=0).astype(jnp.int32)
    cnt_ref[...] = run_ref[...]


def _router(hi, lo, whi, wlo, bias, tri, tt):
    t, d = hi.shape
    tok = pl.BlockSpec((tt, d), lambda i: (i, 0))
    full = lambda a: pl.BlockSpec(a.shape, lambda i: (0,) * a.ndim)
    out = pl.BlockSpec((TOP_K, tt), lambda i: (0, i))
    return pl.pallas_call(
        _router_kernel,
        out_shape=(jax.ShapeDtypeStruct((TOP_K, t), jnp.int32),
                   jax.ShapeDtypeStruct((TOP_K, t), F32),
                   jax.ShapeDtypeStruct((TOP_K, t), jnp.int32),
                   jax.ShapeDtypeStruct((N_EXPERTS, tt), F32)),
        grid=(t // tt,),
        in_specs=[tok, tok, full(whi), full(wlo), full(bias), full(tri)],
        out_specs=(out, out, out, pl.BlockSpec((N_EXPERTS, tt), lambda i: (0, 0))),
        scratch_shapes=[pltpu.VMEM((N_EXPERTS, tt), F32)],
        compiler_params=_params(("arbitrary",)),
        name="router",
    )(hi, lo, whi, wlo, bias, tri)


def _row_tile(ref, r):
    return ref.at[pl.ds(pl.multiple_of(r * ROW_SUB, ROW_SUB), ROW_SUB), :]


def _dest_kernel(idx_ref, rank_ref, start_ref, dest_ref):
    eidx = lax.broadcasted_iota(jnp.int32, start_ref.shape, 0)
    start = start_ref[...]
    rows = [jnp.sum(jnp.where(eidx == idx_ref[k:k + 1, :], start, 0.0), axis=0, keepdims=True)
            for k in range(TOP_K)]
    dest_ref[...] = jnp.concatenate(rows, axis=0).astype(jnp.int32) + rank_ref[...]


def _dest(idx, rank, start, tt):
    t = idx.shape[1]
    blk = pl.BlockSpec((TOP_K, tt), lambda i: (0, i))
    return pl.pallas_call(
        _dest_kernel,
        out_shape=jax.ShapeDtypeStruct((TOP_K, t), jnp.int32),
        grid=(t // tt,),
        in_specs=[blk, blk, pl.BlockSpec(start.shape, lambda i: (0, 0))],
        out_specs=blk,
        compiler_params=_params(("arbitrary",)),
        name="dest",
    )(idx, rank, start)


def _dispatch_kernel(dest_ref, h_ref, xs_in_ref, xs_ref, sem):
    del xs_in_ref
    td = dest_ref.shape[1]

    def row_copy(t, k):
        return pltpu.make_async_copy(_row_tile(h_ref, t), _row_tile(xs_ref, dest_ref[k, t]), sem)

    def issue(t, carry):
        for k in range(TOP_K):
            row_copy(t, k).start()
        return carry

    lax.fori_loop(0, td, issue, 0)

    def drain(t, carry):
        for k in range(TOP_K):
            row_copy(t, k).wait()
        return carry

    lax.fori_loop(0, td, drain, 0)


def _dispatch(dest, h_rt, xs0, td):
    t = dest.shape[1]
    return pl.pallas_call(
        _dispatch_kernel,
        out_shape=jax.ShapeDtypeStruct(xs0.shape, xs0.dtype),
        grid=(t // td,),
        in_specs=[pl.BlockSpec((TOP_K, td), lambda i: (0, i), memory_space=pltpu.SMEM),
                  pl.BlockSpec((td * ROW_SUB, LANES), lambda i: (i, 0)),
                  pl.BlockSpec(memory_space=pl.ANY)],
        out_specs=pl.BlockSpec(memory_space=pl.ANY),
        scratch_shapes=[pltpu.SemaphoreType.DMA],
        input_output_aliases={2: 0},
        compiler_params=_params(("arbitrary",)),
        name="dispatch",
    )(dest, h_rt, xs0)


def _experts_kernel(be_ref, nu_ref, x_ref, wg_ref, wu_ref, wd_ref, y_ref, wgu_bf, wd_bf):
    i = pl.program_id(0)
    used = i < nu_ref[0]
    new_expert = jnp.logical_or(i == 0, be_ref[i] != be_ref[jnp.maximum(i - 1, 0)])

    @pl.when(jnp.logical_and(used, new_expert))
    def _():
        ff = wg_ref.shape[2]
        wgu_bf[:, :ff] = wg_ref[0].astype(BF16)
        wgu_bf[:, ff:] = wu_ref[0].astype(BF16)
        wd_bf[...] = wd_ref[0].astype(BF16)

    @pl.when(used)
    def _():
        ff = wg_ref.shape[2]
        for r in range(0, EXPERT_ROWS, EXPERT_SUB):
            rows = pl.ds(r * ROW_SUB, EXPERT_SUB * ROW_SUB)
            x = _load_row_tiles(x_ref.at[rows, :], EXPERT_SUB).astype(BF16)
            gu = jnp.dot(x, wgu_bf[...], preferred_element_type=F32)
            gate = gu[:, :ff]
            act = (gate * _sigmoid(gate) * gu[:, ff:]).astype(BF16)
            _store_row_tiles(y_ref.at[rows, :], jnp.dot(act, wd_bf[...], preferred_element_type=F32))


def _experts(blk_e, n_used, xs, wg, wu, wd):
    d, ff = wg.shape[1], wg.shape[2]
    nblk = xs.shape[0] // (EXPERT_ROWS * ROW_SUB)
    row = lambda i, be, nu: (jnp.minimum(i, nu[0] - 1), 0)
    wsel = lambda i, be, nu: (be[jnp.minimum(i, nu[0] - 1)], 0, 0)
    return pl.pallas_call(
        _experts_kernel,
        out_shape=jax.ShapeDtypeStruct(xs.shape, xs.dtype),
        grid_spec=pltpu.PrefetchScalarGridSpec(
            num_scalar_prefetch=2,
            grid=(nblk,),
            in_specs=[pl.BlockSpec((EXPERT_ROWS * ROW_SUB, LANES), row),
                      pl.BlockSpec((1, d, ff), wsel),
                      pl.BlockSpec((1, d, ff), wsel),
                      pl.BlockSpec((1, ff, d), wsel)],
            out_specs=pl.BlockSpec((EXPERT_ROWS * ROW_SUB, LANES), row),
            scratch_shapes=[pltpu.VMEM((d, 2 * ff), BF16), pltpu.VMEM((ff, d), BF16)]),
        compiler_params=_params(("arbitrary",)),
        name="experts",
    )(blk_e, n_used, xs, wg, wu, wd)


def _combine_kernel(dest_ref, next_ref, ys_ref, w_ref, base_ref, mod_ref, fw_ref, o_ref, buf, sem):
    tc = dest_ref.shape[1]
    i = pl.program_id(0)
    slot = i % 2

    def row_copy(row, s, t, k):
        return pltpu.make_async_copy(_row_tile(ys_ref, row), _row_tile(buf.at[s, k], t), sem.at[s])

    def issue(d_ref, s):
        def body(t, carry):
            for k in range(TOP_K):
                row_copy(d_ref[k, t], s, t, k).start()
            return carry

        lax.fori_loop(0, tc, body, 0)

    @pl.when(i == 0)
    def _():
        issue(dest_ref, 0)

    @pl.when(i + 1 < pl.num_programs(0))
    def _():
        issue(next_ref, 1 - slot)

    def drain(t, carry):
        for k in range(TOP_K):
            row_copy(0, slot, t, k).wait()
        return carry

    lax.fori_loop(0, tc, drain, 0)

    w = w_ref[...]
    routed = w[:, 0:1] * _load_row_tiles(buf.at[slot, 0], tc)
    for k in range(1, TOP_K):
        routed = routed + w[:, k:k + 1] * _load_row_tiles(buf.at[slot, k], tc)
    x = base_ref[0] + mod_ref[0, 5:6, :] * routed
    o_ref[0] = _rms(x, fw_ref[...])


def _combine(dest, ys, wt, base, mods, fw, tc):
    b, l, d = base.shape
    per_b = l // tc
    steps = b * per_b
    return pl.pallas_call(
        _combine_kernel,
        out_shape=jax.ShapeDtypeStruct((b, l, d), F32),
        grid=(steps,),
        in_specs=[pl.BlockSpec((TOP_K, tc), lambda i: (0, i), memory_space=pltpu.SMEM),
                  pl.BlockSpec((TOP_K, tc), lambda i: (0, jnp.minimum(i + 1, steps - 1)), memory_space=pltpu.SMEM),
                  pl.BlockSpec(memory_space=pl.ANY),
                  pl.BlockSpec((tc, TOP_K), lambda i: (i, 0)),
                  pl.BlockSpec((1, tc, d), lambda i: (i // per_b, i % per_b, 0)),
                  pl.BlockSpec((1, 6, d), lambda i: (i // per_b, 0, 0)),
                  pl.BlockSpec((1, d), lambda i: (0, 0))],
        out_specs=pl.BlockSpec((1, tc, d), lambda i: (i // per_b, i % per_b, 0)),
        scratch_shapes=[pltpu.VMEM((2, TOP_K, tc * ROW_SUB, LANES), jnp.uint32), pltpu.SemaphoreType.DMA((2,))],
        compiler_params=_params(("arbitrary",)),
        name="combine",
    )(dest, dest, ys, wt, base, mods, fw)


def _rope_tables(l):
    t = jnp.arange(l, dtype=jnp.int32)
    row = (t // GRID_W).astype(F32)
    col = (t % GRID_W).astype(F32)
    n_freq = HEAD_DIM // 4
    inv = ROPE_THETA ** (-jnp.arange(n_freq, dtype=F32) / n_freq)
    ang = jnp.concatenate([row[:, None] * inv, col[:, None] * inv], axis=-1)
    cos = jnp.repeat(jnp.cos(ang), 2, axis=1)
    sin = jnp.repeat(jnp.sin(ang), 2, axis=1)
    sign = jnp.tile(jnp.array([-1.0, 1.0], F32), HEAD_DIM // 2)
    reps = LANES // HEAD_DIM
    return jnp.tile(cos, (1, reps)), jnp.tile(sin * sign, (1, reps))


def _filter_features(l):
    t = jnp.linspace(0.0, 1.0, l, dtype=F32)[:, None]
    bands = (FILTER_EMB - 1) // 2
    w = 2.0 * math.pi * jnp.arange(l, dtype=F32)[:, None] / l
    f = jnp.linspace(1e-4, bands - 1, bands, dtype=F32)[None, :]
    z = jnp.concatenate([t, jnp.cos(f * w), -jnp.sin(f * w)], axis=-1)
    min_decay = math.log(FILTER_TARGET) / FILTER_DECAY_FAST
    max_decay = math.log(FILTER_TARGET) / FILTER_DECAY_SLOW
    deltas = jnp.linspace(min_decay, max_decay, HYENA_WIDTH, dtype=F32)[None, :]
    return jnp.pad(z, ((0, 0), (0, LANES - FILTER_EMB))), deltas


def _dft_matrices(l):
    n = 2 * l
    idx = jnp.arange(l, dtype=jnp.int32)
    r = math.isqrt(l)
    assert r * r == l
    sub = jnp.arange(r, dtype=jnp.int32)
    hi = ((r * sub[:, None] * idx[None, :]) % n).astype(F32) * (2.0 * math.pi / n)
    lo = ((sub[:, None] * idx[None, :]) % n).astype(F32) * (2.0 * math.pi / n)
    ch, sh, cl, sl = jnp.cos(hi)[:, None, :], jnp.sin(hi)[:, None, :], jnp.cos(lo)[None], jnp.sin(lo)[None]
    c = (ch * cl - sh * sl).reshape(l, l)
    s = (sh * cl + ch * sl).reshape(l, l)
    alt = jnp.where(idx % 2 == 0, 1.0, -1.0).astype(F32)
    first = (idx == 0)[:, None]
    fwd = jnp.concatenate([c, jnp.where(first, alt[None, :], -s)], axis=0).astype(BF16)
    firstc = (idx == 0)[None, :]
    inv_r = (jnp.where(firstc, 1.0, 2.0) * c / n).astype(BF16)
    inv_i = (jnp.where(firstc, alt[:, None], -2.0 * s) / n).astype(BF16)
    return fwd, inv_r, inv_i


def _head_perm():
    order = []
    for j in range(N_HEADS // 2):
        order += list(range(j * HEAD_DIM, (j + 1) * HEAD_DIM))
        order += list(range((j + N_HEADS // 2) * HEAD_DIM, (j + 1 + N_HEADS // 2) * HEAD_DIM))
    return jnp.array(order, jnp.int32)


def _pad2(a, rows, cols):
    return jnp.pad(a, ((0, rows - a.shape[0]), (0, cols - a.shape[1])))


def kernel(x, c, ctx, c_ctx, mod_w, mod_b, norm1_w, w_in, q_norm_w, k_norm_w, conv_w, conv_b, filt_w1, filt_b1, filt_w2, filt_b2, filt_w3, filt_b3, filt_w4, filt_freq, hyena_bias, attn_out_norm_w, hyena_out_norm_w, w_out, norm2_w, router_w, router_bias, exp_w_gate, exp_w_up, exp_w_down, sh_w_gate, sh_w_up, sh_w_down, final_norm_w):
    b, l, d = x.shape
    t = b * l
    assert mod_w.shape[0] == 1, "single-layer stack"
    tl = min(512, l)

    cond = jnp.concatenate([c, c_ctx[None, :], jnp.zeros((-(b + 1) % 8, d), F32)], axis=0)
    mod = _adaln(cond, mod_w[0], mod_b[0][None, :])
    mods = mod[:b].reshape(b, 6, d)
    cmod = mod[b].reshape(6, d)

    perm = _head_perm()
    w_in0 = w_in[0]
    w_in_k = jnp.concatenate([w_in0[:, :Q_END][:, perm], w_in0[:, Q_END:]], axis=1).astype(BF16)
    w_kv = w_in0[:, Q_END:V_END].astype(BF16)
    gq = jnp.kron(jnp.eye(N_HEADS, dtype=F32), jnp.full((HEAD_DIM, HEAD_DIM), 1.0 / HEAD_DIM, F32)).astype(BF16)
    qnw = jnp.tile(q_norm_w[0], N_HEADS)[None, :]
    knw = jnp.tile(k_norm_w[0], N_KV_HEADS)[None, :]
    n1w = norm1_w[0][None, :]
    cos, sin = _rope_tables(l)

    kc, vc = _ctx_kv(ctx, cmod, n1w, w_kv, gq[:KV_WIDTH, :KV_WIDTH], knw)
    q, k, v, u = _inproj(x, mods, n1w, w_in_k, gq, qnw, knw, cos, sin, tl)
    k_all = jnp.concatenate([kc, k], axis=1)
    v_all = jnp.concatenate([vc, v], axis=1)
    an = _attention(q, k_all, v_all, attn_out_norm_w[0][perm][None, :], min(256, l), min(256, l))

    cw = conv_w[0].reshape(3, 3, HYENA_WIDTH).transpose(1, 0, 2)
    cb = conv_b[0].reshape(3, HYENA_WIDTH)
    g32, gbf, x0 = _hyena_pre(u, cw, cb)
    z, deltas = _filter_features(l)
    fo = filt_w2.shape[1]
    hsd = _hyena_filter(
        z, _pad2(filt_w1[0], LANES, LANES), _pad2(filt_b1[0][None, :], 1, LANES),
        _pad2(filt_w2[0], LANES, LANES), _pad2(filt_b2[0][None, :], 1, LANES),
        _pad2(filt_w3[0], LANES, LANES), _pad2(filt_b3[0][None, :], 1, LANES),
        _pad2(filt_w4[0], LANES, 2 * HYENA_WIDTH), _pad2(filt_freq[0][None, :], 1, LANES), deltas, tl)
    del fo
    fwd, inv_r, inv_i = _dft_matrices(l)
    spec = _dft(fwd, hsd, tl)
    row0 = (jnp.arange(l) == 0)[:, None]
    sa = spec[0, :l]
    sd = jnp.where(row0, spec[0, l:l + 1], sa)
    sb = jnp.where(row0, 0.0, spec[1, l:])
    zr, zi = _dft_mul(fwd, gbf, sa, sb, sd, tl)
    yn = _idft(inv_r, inv_i, zr, zi, g32, x0, hyena_bias[0][None, :], hyena_out_norm_w[0][None, :], tl)

    w_out0 = w_out[0]
    base, h2hi, h2lo, h2rt = _merge(
        an, yn, x, mods, w_out0[:ATTN_WIDTH][perm].astype(BF16), w_out0[ATTN_WIDTH:].astype(BF16),
        norm2_w[0][None, :], sh_w_gate[0].astype(BF16), sh_w_up[0].astype(BF16), sh_w_down[0].astype(BF16), tl)

    tt = 256
    rwt = router_w[0].T
    rw_hi = rwt.astype(BF16)
    rw_lo = (rwt - rw_hi.astype(F32)).astype(BF16)
    bias = jnp.broadcast_to(router_bias[0][:, None], (N_EXPERTS, tt))
    ti = jnp.arange(tt)
    tri = jnp.stack([(ti[:, None] < ti[None, :]), jnp.ones((tt, tt), bool)]).astype(BF16)
    idx, wgt, rank, cnt = _router(h2hi.reshape(t, d), h2lo.reshape(t, d), rw_hi, rw_lo, bias, tri, tt)

    counts = cnt[:, 0].astype(jnp.int32)
    padded = (counts + EXPERT_ROWS - 1) // EXPERT_ROWS * EXPERT_ROWS
    pad_end = jnp.cumsum(padded)
    pad_start = pad_end - padded
    dest = _dest(idx, rank, jnp.broadcast_to(pad_start.astype(F32)[:, None], (N_EXPERTS, tt)), tt)
    n_rows = (t * TOP_K + N_EXPERTS * (EXPERT_ROWS - 1) + EXPERT_ROWS - 1) // EXPERT_ROWS * EXPERT_ROWS
    nblk = n_rows // EXPERT_ROWS
    blk_row = jnp.arange(nblk, dtype=jnp.int32) * EXPERT_ROWS
    blk_e = jnp.minimum(jnp.sum((pad_end[None, :] <= blk_row[:, None]).astype(jnp.int32), axis=1), N_EXPERTS - 1)
    n_used = (pad_end[-1:] // EXPERT_ROWS).astype(jnp.int32)

    xs = _dispatch(dest, h2rt, jnp.zeros((n_rows * ROW_SUB, LANES), jnp.uint32), min(512, t))
    ys = _experts(blk_e, n_used, xs, exp_w_gate[0], exp_w_up[0], exp_w_down[0])
    return _combine(dest, ys, wgt.T, base, mods, final_norm_w[None, :], min(128, l))
```

```python
import functools
import math

import jax
import jax.numpy as jnp
from jax import lax
from jax.experimental import pallas as pl
from jax.experimental.pallas import tpu as pltpu
from jax.experimental.pallas import tpu_sc as plsc

F32 = jnp.float32
BF16 = jnp.bfloat16
HIGHEST = lax.Precision.HIGHEST

GRID_W = 64
N_HEADS = 8
N_KV_HEADS = 2
HEAD_DIM = 64
ATTN_WIDTH = N_HEADS * HEAD_DIM
KV_WIDTH = N_KV_HEADS * HEAD_DIM
HYENA_WIDTH = 512
Q_END = ATTN_WIDTH
K_END = Q_END + KV_WIDTH
V_END = K_END + KV_WIDTH
ROPE_THETA = 10000.0
FILTER_EMB = 33
FILTER_DECAY_FAST = 0.3
FILTER_DECAY_SLOW = 1.5
FILTER_TARGET = 1e-2
N_EXPERTS = 256
TOP_K = 8
N_GROUPS = 8
TOPK_GROUPS = 4
ROUTE_SCALE = 2.5
EPS = 1e-6

LANES = 128
ROW_SUB = 4
EXPERT_ROWS = 512
EXPERT_SUB = 256
NEG_INF = float("-inf")


def _params(semantics, vmem_mb=48):
    return pltpu.CompilerParams(dimension_semantics=semantics, vmem_limit_bytes=vmem_mb * 1024 * 1024)


def _rms(x, w):
    return x * lax.rsqrt(jnp.mean(x * x, axis=-1, keepdims=True) + EPS) * w


def _sigmoid(x):
    return 1.0 / (1.0 + jnp.exp(-x))


def _adaln_kernel(c_ref, w_ref, b_ref, o_ref):
    c = c_ref[...]
    s = c * _sigmoid(c)
    o_ref[...] = jnp.dot(s, w_ref[...], precision=HIGHEST, preferred_element_type=F32) + b_ref[...]


def _adaln(cond, w, b):
    rows, d = cond.shape
    n = w.shape[1]
    tn = 1536
    return pl.pallas_call(
        _adaln_kernel,
        out_shape=jax.ShapeDtypeStruct((rows, n), F32),
        grid=(n // tn,),
        in_specs=[pl.BlockSpec((rows, d), lambda j: (0, 0)),
                  pl.BlockSpec((d, tn), lambda j: (0, j)),
                  pl.BlockSpec((1, tn), lambda j: (0, j))],
        out_specs=pl.BlockSpec((rows, tn), lambda j: (0, j)),
        compiler_params=_params(("arbitrary",)),
        name="adaln",
    )(cond, w, b)


def _head_rms(t, gmat, w):
    ms = jnp.dot((t * t).astype(BF16), gmat, preferred_element_type=F32)
    return t * lax.rsqrt(ms + EPS) * w


def _modulated(x, norm_w, shift, scale):
    return _rms(x, norm_w) * (1.0 + scale) + shift


def _ctx_kv_kernel(ctx_ref, mod_ref, n1_ref, w_ref, g_ref, kn_ref, kc_ref, vc_ref):
    x = ctx_ref[0]
    h = _modulated(x, n1_ref[...], mod_ref[0:1, :], mod_ref[1:2, :])
    kv = jnp.dot(h.astype(BF16), w_ref[...], preferred_element_type=F32)
    k = _head_rms(kv[:, :KV_WIDTH], g_ref[...], kn_ref[...])
    v = kv[:, KV_WIDTH:]
    kc_ref[0] = k.astype(BF16)
    vc_ref[0] = jnp.concatenate([v, jnp.ones_like(v)], axis=1).astype(BF16)


def _ctx_kv(ctx, cmod, n1w, w_kv, gk, knw):
    b, c, d = ctx.shape
    return pl.pallas_call(
        _ctx_kv_kernel,
        out_shape=(jax.ShapeDtypeStruct((b, c, KV_WIDTH), BF16),
                   jax.ShapeDtypeStruct((b, c, 2 * KV_WIDTH), BF16)),
        grid=(b,),
        in_specs=[pl.BlockSpec((1, c, d), lambda i: (i, 0, 0)),
                  pl.BlockSpec(cmod.shape, lambda i: (0, 0)),
                  pl.BlockSpec((1, d), lambda i: (0, 0)),
                  pl.BlockSpec(w_kv.shape, lambda i: (0, 0)),
                  pl.BlockSpec(gk.shape, lambda i: (0, 0)),
                  pl.BlockSpec((1, KV_WIDTH), lambda i: (0, 0))],
        out_specs=(pl.BlockSpec((1, c, KV_WIDTH), lambda i: (i, 0, 0)),
                   pl.BlockSpec((1, c, 2 * KV_WIDTH), lambda i: (i, 0, 0))),
        compiler_params=_params(("arbitrary",)),
        name="ctx_kv",
    )(ctx, cmod, n1w, w_kv, gk, knw)


def _rope(t, cos, sin, even):
    width = t.shape[1]
    partner = jnp.where(even, pltpu.roll(t, width - 1, axis=1), pltpu.roll(t, 1, axis=1))
    return t * cos + partner * sin


def _inproj_kernel(x_ref, mod_ref, n1_ref, w_ref, gq_ref, qn_ref, kn_ref, cos_ref, sin_ref,
                   q_ref, k_ref, v_ref, u_ref):
    x = x_ref[0]
    h = _modulated(x, n1_ref[...], mod_ref[0, 0:1, :], mod_ref[0, 1:2, :])
    p = jnp.dot(h.astype(BF16), w_ref[...], preferred_element_type=F32)
    gq = gq_ref[...]
    q = _head_rms(p[:, :Q_END], gq, qn_ref[...])
    k = _head_rms(p[:, Q_END:K_END], gq[:KV_WIDTH, :KV_WIDTH], kn_ref[...])
    v = p[:, K_END:V_END]
    cos = cos_ref[...]
    sin = sin_ref[...]
    reps = Q_END // LANES
    cos_q = jnp.concatenate([cos] * reps, axis=1)
    sin_q = jnp.concatenate([sin] * reps, axis=1)
    even_q = (lax.broadcasted_iota(jnp.int32, (1, Q_END), 1) & 1) == 0
    even_k = (lax.broadcasted_iota(jnp.int32, (1, KV_WIDTH), 1) & 1) == 0
    q = _rope(q, cos_q, sin_q, even_q) * (HEAD_DIM ** -0.5)
    k = _rope(k, cos, sin, even_k)
    q_ref[0] = q.astype(BF16)
    k_ref[0] = k.astype(BF16)
    v_ref[0] = jnp.concatenate([v, jnp.ones_like(v)], axis=1).astype(BF16)
    u_ref[0] = p[:, V_END:]


def _inproj(x, mods, n1w, w_in, gq, qnw, knw, cos, sin, tl):
    b, l, d = x.shape
    ncol = w_in.shape[1]
    nu = ncol - V_END
    return pl.pallas_call(
        _inproj_kernel,
        out_shape=(jax.ShapeDtypeStruct((b, l, Q_END), BF16),
                   jax.ShapeDtypeStruct((b, l, KV_WIDTH), BF16),
                   jax.ShapeDtypeStruct((b, l, 2 * KV_WIDTH), BF16),
                   jax.ShapeDtypeStruct((b, l, nu), F32)),
        grid=(l // tl, b),
        in_specs=[pl.BlockSpec((1, tl, d), lambda i, j: (j, i, 0)),
                  pl.BlockSpec((1, 6, d), lambda i, j: (j, 0, 0)),
                  pl.BlockSpec((1, d), lambda i, j: (0, 0)),
                  pl.BlockSpec((d, ncol), lambda i, j: (0, 0)),
                  pl.BlockSpec(gq.shape, lambda i, j: (0, 0)),
                  pl.BlockSpec((1, Q_END), lambda i, j: (0, 0)),
                  pl.BlockSpec((1, KV_WIDTH), lambda i, j: (0, 0)),
                  pl.BlockSpec((tl, LANES), lambda i, j: (i, 0)),
                  pl.BlockSpec((tl, LANES), lambda i, j: (i, 0))],
        out_specs=(pl.BlockSpec((1, tl, Q_END), lambda i, j: (j, i, 0)),
                   pl.BlockSpec((1, tl, KV_WIDTH), lambda i, j: (j, i, 0)),
                   pl.BlockSpec((1, tl, 2 * KV_WIDTH), lambda i, j: (j, i, 0)),
                   pl.BlockSpec((1, tl, nu), lambda i, j: (j, i, 0))),
        compiler_params=_params(("arbitrary", "arbitrary")),
        name="inproj",
    )(x, mods, n1w, w_in, gq, qnw, knw, cos, sin)


def _attn_kernel(q_ref, k_ref, v_ref, wn_ref, o_ref, *, sub):
    tq = q_ref.shape[1]
    low = lax.broadcasted_iota(jnp.int32, (1, LANES), 1) < HEAD_DIM
    nt = (((1,), (1,)), ((), ()))
    kk = k_ref[0]
    vv = v_ref[0]
    for r in range(0, tq, sub):
        outs = []
        for j in range(Q_END // LANES):
            qv = q_ref[0, r:r + sub, LANES * j:LANES * (j + 1)]
            zero = jnp.zeros_like(qv)
            halves = []
            for g in range(N_KV_HEADS):
                qh = jnp.where(low, qv, zero) if g == 0 else jnp.where(low, zero, qv)
                s = lax.dot_general(qh, kk, nt, preferred_element_type=F32)
                p = jnp.exp(s - jnp.max(s, axis=-1, keepdims=True)).astype(BF16)
                pv = jnp.dot(p, vv, preferred_element_type=F32)
                halves.append(pv[:, :LANES] / pv[:, LANES:])
            outs.append(jnp.where(low, halves[0], halves[1]))
        a = jnp.concatenate(outs, axis=1)
        o_ref[0, r:r + sub, :] = _rms(a, wn_ref[...]).astype(BF16)


def _attention(q, k, v, wn, tq, sub):
    b, l, _ = q.shape
    n = k.shape[1]
    return pl.pallas_call(
        functools.partial(_attn_kernel, sub=sub),
        out_shape=jax.ShapeDtypeStruct((b, l, Q_END), BF16),
        grid=(b, l // tq),
        in_specs=[pl.BlockSpec((1, tq, Q_END), lambda i, j: (i, j, 0)),
                  pl.BlockSpec((1, n, KV_WIDTH), lambda i, j: (i, 0, 0)),
                  pl.BlockSpec((1, n, 2 * KV_WIDTH), lambda i, j: (i, 0, 0)),
                  pl.BlockSpec((1, Q_END), lambda i, j: (0, 0))],
        out_specs=pl.BlockSpec((1, tq, Q_END), lambda i, j: (i, j, 0)),
        compiler_params=_params(("arbitrary", "arbitrary")),
        name="attn",
    )(q, k, v, wn)


def _hyena_pre_kernel(u0_ref, u1_ref, u2_ref, cw_ref, cb_ref, g32_ref, gbf_ref, x0_ref):
    l = u0_ref.shape[1]
    row = lax.broadcasted_iota(jnp.int32, (l, LANES), 0)

    def conv(u_ref, gi):
        u = u_ref[0]
        prev = jnp.where(row == 0, 0.0, pltpu.roll(u, 1, axis=0))
        nxt = jnp.where(row == l - 1, 0.0, pltpu.roll(u, l - 1, axis=0))
        w = cw_ref[gi]
        return w[0:1] * prev + w[1:2] * u + w[2:3] * nxt + cb_ref[gi:gi + 1, :]

    x0 = conv(u0_ref, 0)
    x1 = conv(u1_ref, 1)
    v = conv(u2_ref, 2)
    g = v * x1
    g32_ref[0] = g
    gbf_ref[0] = g.astype(BF16)
    x0_ref[0] = x0


def _hyena_pre(u, cw, cb):
    b, l, _ = u.shape
    nblk = HYENA_WIDTH // LANES
    ublk = lambda gi: pl.BlockSpec((1, l, LANES), lambda i, j: (i, 0, gi * nblk + j))
    oblk = pl.BlockSpec((1, l, LANES), lambda i, j: (i, 0, j))
    return pl.pallas_call(
        _hyena_pre_kernel,
        out_shape=(jax.ShapeDtypeStruct((b, l, HYENA_WIDTH), F32),
                   jax.ShapeDtypeStruct((b, l, HYENA_WIDTH), BF16),
                   jax.ShapeDtypeStruct((b, l, HYENA_WIDTH), F32)),
        grid=(b, nblk),
        in_specs=[ublk(0), ublk(1), ublk(2),
                  pl.BlockSpec((3, 3, LANES), lambda i, j: (0, 0, j)),
                  pl.BlockSpec((3, LANES), lambda i, j: (0, j))],
        out_specs=(oblk, oblk, oblk),
        compiler_params=_params(("arbitrary", "arbitrary")),
        name="hyena_pre",
    )(u, u, u, cw, cb)


def _filter_kernel(z_ref, w1_ref, b1_ref, w2_ref, b2_ref, w3_ref, b3_ref, w4_ref, fr_ref, dl_ref, o_ref):
    tl = z_ref.shape[0]
    z = z_ref[...]
    fr = fr_ref[...]
    dot = lambda a, w: jnp.dot(a, w, precision=HIGHEST, preferred_element_type=F32)
    h = jnp.sin(fr * (dot(z, w1_ref[...]) + b1_ref[...]))
    h = jnp.sin(fr * (dot(h, w2_ref[...]) + b2_ref[...]))
    h = jnp.sin(fr * (dot(h, w3_ref[...]) + b3_ref[...]))
    h = dot(h, w4_ref[...])
    t = z[:, 0:1]
    decay = jnp.exp(-t * jnp.abs(dl_ref[...]))
    hf = h[:, :HYENA_WIDTH] * decay
    hb = h[:, HYENA_WIDTH:] * decay
    row = lax.broadcasted_iota(jnp.int32, (tl, HYENA_WIDTH), 0) + pl.program_id(0) * tl
    hb = jnp.where(row == 0, 0.0, hb)
    o_ref[0] = hf + hb
    o_ref[1] = hf - hb


def _hyena_filter(z, w1, b1, w2, b2, w3, b3, w4, freq, deltas, tl):
    l = z.shape[0]
    full = lambda a: pl.BlockSpec(a.shape, lambda i: (0,) * a.ndim)
    return pl.pallas_call(
        _filter_kernel,
        out_shape=jax.ShapeDtypeStruct((2, l, HYENA_WIDTH), F32),
        grid=(l // tl,),
        in_specs=[pl.BlockSpec((tl, z.shape[1]), lambda i: (i, 0)),
                  full(w1), full(b1), full(w2), full(b2), full(w3), full(b3), full(w4), full(freq), full(deltas)],
        out_specs=pl.BlockSpec((2, tl, HYENA_WIDTH), lambda i: (0, i, 0)),
        compiler_params=_params(("arbitrary",)),
        name="hyena_filter",
    )(z, w1, b1, w2, b2, w3, b3, w4, freq, deltas)


def _dft_kernel(f_ref, x_ref, o_ref):
    o_ref[0] = jnp.dot(f_ref[...], x_ref[0].astype(BF16), preferred_element_type=F32)


def _dft(fmat, x, tf):
    nb, l, w = x.shape
    n = fmat.shape[0]
    return pl.pallas_call(
        _dft_kernel,
        out_shape=jax.ShapeDtypeStruct((nb, n, w), F32),
        grid=(n // tf, nb),
        in_specs=[pl.BlockSpec((tf, l), lambda i, j: (i, 0)),
                  pl.BlockSpec((1, l, w), lambda i, j: (j, 0, 0))],
        out_specs=pl.BlockSpec((1, tf, w), lambda i, j: (j, i, 0)),
        compiler_params=_params(("arbitrary", "arbitrary")),
        name="dft_filter",
    )(fmat, x)


def _dft_mul_kernel(fr_ref, fi_ref, x_ref, a_ref, b_ref, d_ref, zr_ref, zi_ref):
    x = x_ref[0]
    xr = jnp.dot(fr_ref[...], x, preferred_element_type=F32)
    xi = jnp.dot(fi_ref[...], x, preferred_element_type=F32)
    bb = b_ref[...]
    zr_ref[0] = (xr * a_ref[...] - xi * bb).astype(BF16)
    zi_ref[0] = (xr * bb + xi * d_ref[...]).astype(BF16)


def _dft_mul(fmat, g, sa, sb, sd, tf):
    b, l, w = g.shape
    nf = l // tf
    spec = pl.BlockSpec((tf, w), lambda i, j: (i, 0))
    return pl.pallas_call(
        _dft_mul_kernel,
        out_shape=(jax.ShapeDtypeStruct((b, l, w), BF16), jax.ShapeDtypeStruct((b, l, w), BF16)),
        grid=(nf, b),
        in_specs=[pl.BlockSpec((tf, l), lambda i, j: (i, 0)),
                  pl.BlockSpec((tf, l), lambda i, j: (i + nf, 0)),
                  pl.BlockSpec((1, l, w), lambda i, j: (j, 0, 0)),
                  spec, spec, spec],
        out_specs=(pl.BlockSpec((1, tf, w), lambda i, j: (j, i, 0)),
                   pl.BlockSpec((1, tf, w), lambda i, j: (j, i, 0))),
        compiler_params=_params(("arbitrary", "arbitrary")),
        name="dft_mul",
    )(fmat, fmat, g, sa, sb, sd)


def _idft_kernel(fr_ref, fi_ref, zr_ref, zi_ref, g_ref, x0_ref, hb_ref, wn_ref, o_ref):
    conv = (jnp.dot(fr_ref[...], zr_ref[0], preferred_element_type=F32)
            + jnp.dot(fi_ref[...], zi_ref[0], preferred_element_type=F32))
    y = (conv + g_ref[0] * hb_ref[...]) * x0_ref[0]
    o_ref[0] = _rms(y, wn_ref[...]).astype(BF16)


def _idft(finv_r, finv_i, zr, zi, g32, x0, hbias, wn, tt):
    b, l, w = zr.shape
    tile = pl.BlockSpec((1, tt, w), lambda i, j: (j, i, 0))
    return pl.pallas_call(
        _idft_kernel,
        out_shape=jax.ShapeDtypeStruct((b, l, w), BF16),
        grid=(l // tt, b),
        in_specs=[pl.BlockSpec((tt, l), lambda i, j: (i, 0)),
                  pl.BlockSpec((tt, l), lambda i, j: (i, 0)),
                  pl.BlockSpec((1, l, w), lambda i, j: (j, 0, 0)),
                  pl.BlockSpec((1, l, w), lambda i, j: (j, 0, 0)),
                  tile, tile,
                  pl.BlockSpec((1, w), lambda i, j: (0, 0)),
                  pl.BlockSpec((1, w), lambda i, j: (0, 0))],
        out_specs=tile,
        compiler_params=_params(("arbitrary", "arbitrary")),
        name="idft",
    )(finv_r, finv_i, zr, zi, g32, x0, hbias, wn)


def _bf16_bits(v):
    return lax.bitcast_convert_type(v.astype(BF16).astype(F32), jnp.uint32)


def _store_row_tiles(ref, val):
    rows, half = val.shape[0], val.shape[1] // 2
    assert half == ROW_SUB * LANES
    for j in range(ROW_SUB):
        lo = _bf16_bits(val[:, LANES * j:LANES * (j + 1)]) >> 16
        hi = _bf16_bits(val[:, half + LANES * j:half + LANES * (j + 1)]) & jnp.uint32(0xFFFF0000)
        ref[pl.ds(j, rows, stride=ROW_SUB), :] = lo | hi


def _load_row_tiles(ref, rows):
    words = [ref[pl.ds(j, rows, stride=ROW_SUB), :] for j in range(ROW_SUB)]
    lo = [lax.bitcast_convert_type(w << 16, F32) for w in words]
    hi = [lax.bitcast_convert_type(w & jnp.uint32(0xFFFF0000), F32) for w in words]
    return jnp.concatenate(lo + hi, axis=1)


def _merge_kernel(a_ref, y_ref, x_ref, mod_ref, wa_ref, wy_ref, n2_ref, sg_ref, su_ref, sd_ref,
                  base_ref, hi_ref, lo_ref, rt_ref):
    m = (jnp.dot(a_ref[0], wa_ref[...], preferred_element_type=F32)
         + jnp.dot(y_ref[0], wy_ref[...], preferred_element_type=F32))
    x1 = x_ref[0] + mod_ref[0, 2:3, :] * m
    h2 = _modulated(x1, n2_ref[...], mod_ref[0, 3:4, :], mod_ref[0, 4:5, :])
    hi = h2.astype(BF16)
    hi_ref[0] = hi
    lo_ref[0] = (h2 - hi.astype(F32)).astype(BF16)
    _store_row_tiles(rt_ref, h2)
    gate = jnp.dot(hi, sg_ref[...], preferred_element_type=F32)
    up = jnp.dot(hi, su_ref[...], preferred_element_type=F32)
    act = (gate * _sigmoid(gate) * up).astype(BF16)
    shared = jnp.dot(act, sd_ref[...], preferred_element_type=F32)
    base_ref[0] = x1 + mod_ref[0, 5:6, :] * shared


def _merge(an, yn, x, mods, wa, wy, n2w, sg, su, sd, tl):
    b, l, d = x.shape
    full = lambda a: pl.BlockSpec(a.shape, lambda i, j: (0,) * a.ndim)
    half = pl.BlockSpec((1, tl, an.shape[2]), lambda i, j: (i, j, 0))
    wide = pl.BlockSpec((1, tl, d), lambda i, j: (i, j, 0))
    per_b = l // tl
    return pl.pallas_call(
        _merge_kernel,
        out_shape=(jax.ShapeDtypeStruct((b, l, d), F32),
                   jax.ShapeDtypeStruct((b, l, d), BF16),
                   jax.ShapeDtypeStruct((b, l, d), BF16),
                   jax.ShapeDtypeStruct((b * l * ROW_SUB, LANES), jnp.uint32)),
        grid=(b, per_b),
        in_specs=[half, half, wide,
                  pl.BlockSpec((1, 6, d), lambda i, j: (i, 0, 0)),
                  full(wa), full(wy), full(n2w), full(sg), full(su), full(sd)],
        out_specs=(wide, wide, wide,
                   pl.BlockSpec((tl * ROW_SUB, LANES), lambda i, j: (i * per_b + j, 0))),
        compiler_params=_params(("arbitrary", "arbitrary")),
        name="merge",
    )(an, yn, x, mods, wa, wy, n2w, sg, su, sd)


def _router_kernel(hi_ref, lo_ref, whi_ref, wlo_ref, bias_ref, tri_ref,
                   idx_ref, wgt_ref, rank_ref, cnt_ref, run_ref):
    tt = hi_ref.shape[0]
    per_group = N_EXPERTS // N_GROUPS

    @pl.when(pl.program_id(0) == 0)
    def _():
        run_ref[...] = jnp.zeros_like(run_ref)

    nt = (((1,), (1,)), ((), ()))
    hi = hi_ref[...]
    whi = whi_ref[...]
    logits = (lax.dot_general(whi, hi, nt, preferred_element_type=F32)
              + lax.dot_general(whi, lo_ref[...], nt, preferred_element_type=F32)
              + lax.dot_general(wlo_ref[...], hi, nt, preferred_element_type=F32))
    scores = _sigmoid(logits)
    biased = scores + bias_ref[...]

    ridx = lax.broadcasted_iota(jnp.int32, (per_group, tt), 0)
    groups = [biased[g * per_group:(g + 1) * per_group, :] for g in range(N_GROUPS)]
    gs = []
    for blk in groups:
        m1 = jnp.max(blk, axis=0, keepdims=True)
        i1 = jnp.min(jnp.where(blk == m1, ridx, per_group), axis=0, keepdims=True)
        m2 = jnp.max(jnp.where(ridx == i1, NEG_INF, blk), axis=0, keepdims=True)
        gs.append(m1 + m2)

    kept = []
    for g in range(N_GROUPS):
        ahead = jnp.zeros((1, tt), F32)
        for o in range(N_GROUPS):
            if o != g:
                wins = (gs[o] >= gs[g]) if o < g else (gs[o] > gs[g])
                ahead = ahead + jnp.where(wins, 1.0, 0.0)
        kept.append(jnp.where(ahead < TOPK_GROUPS, groups[g], NEG_INF))
    cur = jnp.concatenate(kept, axis=0)

    eidx = lax.broadcasted_iota(jnp.int32, cur.shape, 0)
    onehot = jnp.zeros(cur.shape, F32)
    picks = []
    wsel = []
    for _ in range(TOP_K):
        mx = jnp.max(cur, axis=0, keepdims=True)
        first = jnp.min(jnp.where(cur == mx, eidx, N_EXPERTS), axis=0, keepdims=True)
        sel = eidx == first
        picks.append(first)
        wsel.append(jnp.sum(jnp.where(sel, scores, 0.0), axis=0, keepdims=True))
        onehot = jnp.where(sel, 1.0, onehot)
        cur = jnp.where(sel, NEG_INF, cur)
    w = jnp.concatenate(wsel, axis=0)
    w = w / jnp.sum(w, axis=0, keepdims=True) * ROUTE_SCALE
    idx = jnp.concatenate(picks, axis=0)

    oh = onehot.astype(BF16)
    before = jnp.dot(oh, tri_ref[0], preferred_element_type=F32)
    total = jnp.dot(oh, tri_ref[1], preferred_element_type=F32)
    pos = run_ref[...] + before
    ranks = [jnp.sum(jnp.where(eidx == p, pos, 0.0), axis=0, keepdims=True) for p in picks]
    run_ref[...] = run_ref[...] + total

    idx_ref[...] = idx
    wgt_ref[...] = w
    rank_ref[...] = jnp.concatenate(ranks, axis=0).astype(jnp.int32)
    cnt_ref[...] = run_ref[...]


def _router(hi, lo, whi, wlo, bias, tri, tt):
    t, d = hi.shape
    tok = pl.BlockSpec((tt, d), lambda i: (i, 0))
    full = lambda a: pl.BlockSpec(a.shape, lambda i: (0,) * a.ndim)
    out = pl.BlockSpec((TOP_K, tt), lambda i: (0, i))
    return pl.pallas_call(
        _router_kernel,
        out_shape=(jax.ShapeDtypeStruct((TOP_K, t), jnp.int32),
                   jax.ShapeDtypeStruct((TOP_K, t), F32),
                   jax.ShapeDtypeStruct((TOP_K, t), jnp.int32),
                   jax.ShapeDtypeStruct((N_EXPERTS, tt), F32)),
        grid=(t // tt,),
        in_specs=[tok, tok, full(whi), full(wlo), full(bias), full(tri)],
        out_specs=(out, out, out, pl.BlockSpec((N_EXPERTS, tt), lambda i: (0, 0))),
        scratch_shapes=[pltpu.VMEM((N_EXPERTS, tt), F32)],
        compiler_params=_params(("arbitrary",)),
        name="router",
    )(hi, lo, whi, wlo, bias, tri)


def _row_tile(ref, r):
    return ref.at[pl.ds(pl.multiple_of(r * ROW_SUB, ROW_SUB), ROW_SUB), :]


def _dest_kernel(idx_ref, rank_ref, start_ref, dest_ref):
    eidx = lax.broadcasted_iota(jnp.int32, start_ref.shape, 0)
    start = start_ref[...]
    rows = [jnp.sum(jnp.where(eidx == idx_ref[k:k + 1, :], start, 0.0), axis=0, keepdims=True)
            for k in range(TOP_K)]
    dest_ref[...] = jnp.concatenate(rows, axis=0).astype(jnp.int32) + rank_ref[...]


def _dest(idx, rank, start, tt):
    t = idx.shape[1]
    blk = pl.BlockSpec((TOP_K, tt), lambda i: (0, i))
    return pl.pallas_call(
        _dest_kernel,
        out_shape=jax.ShapeDtypeStruct((TOP_K, t), jnp.int32),
        grid=(t // tt,),
        in_specs=[blk, blk, pl.BlockSpec(start.shape, lambda i: (0, 0))],
        out_specs=blk,
        compiler_params=_params(("arbitrary",)),
        name="dest",
    )(idx, rank, start)


def _row_assignment(dest_flat, n_rows, t):
    info = plsc.get_sparse_core_info()
    lanes = info.num_lanes
    workers = info.num_cores * info.num_subcores
    own = n_rows // workers
    chunk = min(8192, t)
    assert n_rows % (workers * lanes) == 0 and t % chunk == 0 and chunk % lanes == 0

    def body(dest_hbm, out_hbm, buf, part):
        wid = lax.axis_index("s") * info.num_cores + lax.axis_index("c")
        base = wid * own
        lane = lax.iota(jnp.int32, lanes)

        @pl.loop(0, own, step=lanes)
        def _(r):
            buf[pl.ds(r, lanes)] = jnp.zeros((lanes,), jnp.int32)

        @pl.loop(0, TOP_K)
        def _(k):
            @pl.loop(0, t, step=chunk)
            def _(c):
                pltpu.sync_copy(dest_hbm.at[pl.ds(k * t + c, chunk)], part)

                @pl.loop(0, chunk, step=lanes)
                def _(j):
                    rel = part[pl.ds(j, lanes)] - base
                    mine = jnp.logical_and(rel >= 0, rel < own)
                    val = (c + j + lane) * TOP_K + k
                    plsc.store_scatter(buf, [jnp.where(mine, rel, 0)], val, mask=mine)

        pltpu.sync_copy(buf, out_hbm.at[pl.ds(base, own)])

    return pl.kernel(
        body,
        out_type=jax.ShapeDtypeStruct((n_rows,), jnp.int32),
        mesh=plsc.VectorSubcoreMesh(core_axis_name="c", subcore_axis_name="s"),
        scratch_types=[pltpu.VMEM((own,), jnp.int32), pltpu.VMEM((chunk,), jnp.int32)],
        compiler_params=pltpu.CompilerParams(needs_layout_passes=False),
        name="row_assignment",
    )(dest_flat)


def _experts_kernel(be_ref, nu_ref, nv_ref, cur_ref, nxt_ref, h_ref, wg_ref, wu_ref, wd_ref, y_ref,
                    xbuf, sem, wgu_bf, wd_bf):
    i = pl.program_id(0)
    n_used = nu_ref[0]
    used = i < n_used
    slot = i % 2
    new_expert = jnp.logical_or(i == 0, be_ref[i] != be_ref[jnp.maximum(i - 1, 0)])

    def row_copy(token, s, r):
        return pltpu.make_async_copy(_row_tile(h_ref, token), _row_tile(xbuf.at[s], r), sem.at[s])

    def issue(a_ref, count, s):
        def body(r, carry):
            row_copy(a_ref[0, 0, r] // TOP_K, s, r).start()
            return carry

        lax.fori_loop(0, count, body, 0)

    @pl.when(i == 0)
    def _():
        xbuf[...] = jnp.zeros(xbuf.shape, xbuf.dtype)
        issue(cur_ref, nv_ref[0], 0)

    @pl.when(i + 1 < n_used)
    def _():
        issue(nxt_ref, nv_ref[i + 1], 1 - slot)

    @pl.when(jnp.logical_and(used, new_expert))
    def _():
        ff = wg_ref.shape[2]
        wgu_bf[:, :ff] = wg_ref[0].astype(BF16)
        wgu_bf[:, ff:] = wu_ref[0].astype(BF16)
        wd_bf[...] = wd_ref[0].astype(BF16)

    @pl.when(used)
    def _():
        def drain(r, carry):
            row_copy(0, slot, r).wait()
            return carry

        lax.fori_loop(0, nv_ref[i], drain, 0)
        ff = wg_ref.shape[2]
        for r in range(0, EXPERT_ROWS, EXPERT_SUB):
            rows = pl.ds(r * ROW_SUB, EXPERT_SUB * ROW_SUB)
            x = _load_row_tiles(xbuf.at[slot, rows, :], EXPERT_SUB).astype(BF16)
            gu = jnp.dot(x, wgu_bf[...], preferred_element_type=F32)
            gate = gu[:, :ff]
            act = (gate * _sigmoid(gate) * gu[:, ff:]).astype(BF16)
            _store_row_tiles(y_ref.at[rows, :], jnp.dot(act, wd_bf[...], preferred_element_type=F32))


def _experts(blk_e, n_used, n_valid, assign, h_rt, wg, wu, wd):
    d, ff = wg.shape[1], wg.shape[2]
    nblk = assign.shape[0]
    last = lambda i, nu: jnp.minimum(i, nu[0] - 1)
    wsel = lambda i, be, nu, nv: (be[last(i, nu)], 0, 0)
    return pl.pallas_call(
        _experts_kernel,
        out_shape=jax.ShapeDtypeStruct((nblk * EXPERT_ROWS * ROW_SUB, LANES), jnp.uint32),
        grid_spec=pltpu.PrefetchScalarGridSpec(
            num_scalar_prefetch=3,
            grid=(nblk,),
            in_specs=[pl.BlockSpec((1, 1, EXPERT_ROWS), lambda i, be, nu, nv: (last(i, nu), 0, 0),
                                   memory_space=pltpu.SMEM),
                      pl.BlockSpec((1, 1, EXPERT_ROWS), lambda i, be, nu, nv: (last(i + 1, nu), 0, 0),
                                   memory_space=pltpu.SMEM),
                      pl.BlockSpec(memory_space=pl.ANY),
                      pl.BlockSpec((1, d, ff), wsel),
                      pl.BlockSpec((1, d, ff), wsel),
                      pl.BlockSpec((1, ff, d), wsel)],
            out_specs=pl.BlockSpec((EXPERT_ROWS * ROW_SUB, LANES), lambda i, be, nu, nv: (last(i, nu), 0)),
            scratch_shapes=[pltpu.VMEM((2, EXPERT_ROWS * ROW_SUB, LANES), jnp.uint32),
                            pltpu.SemaphoreType.DMA((2,)),
                            pltpu.VMEM((d, 2 * ff), BF16), pltpu.VMEM((ff, d), BF16)]),
        compiler_params=_params(("arbitrary",)),
        name="experts",
    )(blk_e, n_used, n_valid, assign, assign, h_rt, wg, wu, wd)


def _combine_kernel(dest_ref, next_ref, ys_ref, w_ref, base_ref, mod_ref, fw_ref, o_ref, buf, sem):
    tc = dest_ref.shape[1]
    i = pl.program_id(0)
    slot = i % 2

    def row_copy(row, s, t, k):
        return pltpu.make_async_copy(_row_tile(ys_ref, row), _row_tile(buf.at[s, k], t), sem.at[s])

    def issue(d_ref, s):
        def body(t, carry):
            for k in range(TOP_K):
                row_copy(d_ref[k, t], s, t, k).start()
            return carry

        lax.fori_loop(0, tc, body, 0)

    @pl.when(i == 0)
    def _():
        issue(dest_ref, 0)

    @pl.when(i + 1 < pl.num_programs(0))
    def _():
        issue(next_ref, 1 - slot)

    def drain(t, carry):
        for k in range(TOP_K):
            row_copy(0, slot, t, k).wait()
        return carry

    lax.fori_loop(0, tc, drain, 0)

    w = w_ref[...]
    routed = w[:, 0:1] * _load_row_tiles(buf.at[slot, 0], tc)
    for k in range(1, TOP_K):
        routed = routed + w[:, k:k + 1] * _load_row_tiles(buf.at[slot, k], tc)
    x = base_ref[0] + mod_ref[0, 5:6, :] * routed
    o_ref[0] = _rms(x, fw_ref[...])


def _combine(dest, ys, wt, base, mods, fw, tc):
    b, l, d = base.shape
    per_b = l // tc
    steps = b * per_b
    return pl.pallas_call(
        _combine_kernel,
        out_shape=jax.ShapeDtypeStruct((b, l, d), F32),
        grid=(steps,),
        in_specs=[pl.BlockSpec((TOP_K, tc), lambda i: (0, i), memory_space=pltpu.SMEM),
                  pl.BlockSpec((TOP_K, tc), lambda i: (0, jnp.minimum(i + 1, steps - 1)), memory_space=pltpu.SMEM),
                  pl.BlockSpec(memory_space=pl.ANY),
                  pl.BlockSpec((tc, TOP_K), lambda i: (i, 0)),
                  pl.BlockSpec((1, tc, d), lambda i: (i // per_b, i % per_b, 0)),
                  pl.BlockSpec((1, 6, d), lambda i: (i // per_b, 0, 0)),
                  pl.BlockSpec((1, d), lambda i: (0, 0))],
        out_specs=pl.BlockSpec((1, tc, d), lambda i: (i // per_b, i % per_b, 0)),
        scratch_shapes=[pltpu.VMEM((2, TOP_K, tc * ROW_SUB, LANES), jnp.uint32), pltpu.SemaphoreType.DMA((2,))],
        compiler_params=_params(("arbitrary",)),
        name="combine",
    )(dest, dest, ys, wt, base, mods, fw)


def _rope_tables(l):
    t = jnp.arange(l, dtype=jnp.int32)
    row = (t // GRID_W).astype(F32)
    col = (t % GRID_W).astype(F32)
    n_freq = HEAD_DIM // 4
    inv = ROPE_THETA ** (-jnp.arange(n_freq, dtype=F32) / n_freq)
    ang = jnp.concatenate([row[:, None] * inv, col[:, None] * inv], axis=-1)
    cos = jnp.repeat(jnp.cos(ang), 2, axis=1)
    sin = jnp.repeat(jnp.sin(ang), 2, axis=1)
    sign = jnp.tile(jnp.array([-1.0, 1.0], F32), HEAD_DIM // 2)
    reps = LANES // HEAD_DIM
    return jnp.tile(cos, (1, reps)), jnp.tile(sin * sign, (1, reps))


def _filter_features(l):
    t = jnp.linspace(0.0, 1.0, l, dtype=F32)[:, None]
    bands = (FILTER_EMB - 1) // 2
    w = 2.0 * math.pi * jnp.arange(l, dtype=F32)[:, None] / l
    f = jnp.linspace(1e-4, bands - 1, bands, dtype=F32)[None, :]
    z = jnp.concatenate([t, jnp.cos(f * w), -jnp.sin(f * w)], axis=-1)
    min_decay = math.log(FILTER_TARGET) / FILTER_DECAY_FAST
    max_decay = math.log(FILTER_TARGET) / FILTER_DECAY_SLOW
    deltas = jnp.linspace(min_decay, max_decay, HYENA_WIDTH, dtype=F32)[None, :]
    return jnp.pad(z, ((0, 0), (0, LANES - FILTER_EMB))), deltas


def _dft_matrices(l):
    n = 2 * l
    idx = jnp.arange(l, dtype=jnp.int32)
    r = math.isqrt(l)
    assert r * r == l
    sub = jnp.arange(r, dtype=jnp.int32)
    hi = ((r * sub[:, None] * idx[None, :]) % n).astype(F32) * (2.0 * math.pi / n)
    lo = ((sub[:, None] * idx[None, :]) % n).astype(F32) * (2.0 * math.pi / n)
    ch, sh, cl, sl = jnp.cos(hi)[:, None, :], jnp.sin(hi)[:, None, :], jnp.cos(lo)[None], jnp.sin(lo)[None]
    c = (ch * cl - sh * sl).reshape(l, l)
    s = (sh * cl + ch * sl).reshape(l, l)
    alt = jnp.where(idx % 2 == 0, 1.0, -1.0).astype(F32)
    first = (idx == 0)[:, None]
    fwd = jnp.concatenate([c, jnp.where(first, alt[None, :], -s)], axis=0).astype(BF16)
    firstc = (idx == 0)[None, :]
    inv_r = (jnp.where(firstc, 1.0, 2.0) * c / n).astype(BF16)
    inv_i = (jnp.where(firstc, alt[:, None], -2.0 * s) / n).astype(BF16)
    return fwd, inv_r, inv_i


def _head_perm():
    order = []
    for j in range(N_HEADS // 2):
        order += list(range(j * HEAD_DIM, (j + 1) * HEAD_DIM))
        order += list(range((j + N_HEADS // 2) * HEAD_DIM, (j + 1 + N_HEADS // 2) * HEAD_DIM))
    return jnp.array(order, jnp.int32)


def _pad2(a, rows, cols):
    return jnp.pad(a, ((0, rows - a.shape[0]), (0, cols - a.shape[1])))


def kernel(x, c, ctx, c_ctx, mod_w, mod_b, norm1_w, w_in, q_norm_w, k_norm_w, conv_w, conv_b, filt_w1, filt_b1, filt_w2, filt_b2, filt_w3, filt_b3, filt_w4, filt_freq, hyena_bias, attn_out_norm_w, hyena_out_norm_w, w_out, norm2_w, router_w, router_bias, exp_w_gate, exp_w_up, exp_w_down, sh_w_gate, sh_w_up, sh_w_down, final_norm_w):
    b, l, d = x.shape
    t = b * l
    assert mod_w.shape[0] == 1, "single-layer stack"
    tl = min(512, l)

    cond = jnp.concatenate([c, c_ctx[None, :], jnp.zeros((-(b + 1) % 8, d), F32)], axis=0)
    mod = _adaln(cond, mod_w[0], mod_b[0][None, :])
    mods = mod[:b].reshape(b, 6, d)
    cmod = mod[b].reshape(6, d)

    perm = _head_perm()
    w_in0 = w_in[0]
    w_in_k = jnp.concatenate([w_in0[:, :Q_END][:, perm], w_in0[:, Q_END:]], axis=1).astype(BF16)
    w_kv = w_in0[:, Q_END:V_END].astype(BF16)
    gq = jnp.kron(jnp.eye(N_HEADS, dtype=F32), jnp.full((HEAD_DIM, HEAD_DIM), 1.0 / HEAD_DIM, F32)).astype(BF16)
    qnw = jnp.tile(q_norm_w[0], N_HEADS)[None, :]
    knw = jnp.tile(k_norm_w[0], N_KV_HEADS)[None, :]
    n1w = norm1_w[0][None, :]
    cos, sin = _rope_tables(l)

    kc, vc = _ctx_kv(ctx, cmod, n1w, w_kv, gq[:KV_WIDTH, :KV_WIDTH], knw)
    q, k, v, u = _inproj(x, mods, n1w, w_in_k, gq, qnw, knw, cos, sin, tl)
    k_all = jnp.concatenate([kc, k], axis=1)
    v_all = jnp.concatenate([vc, v], axis=1)
    an = _attention(q, k_all, v_all, attn_out_norm_w[0][perm][None, :], min(256, l), min(256, l))

    cw = conv_w[0].reshape(3, 3, HYENA_WIDTH).transpose(1, 0, 2)
    cb = conv_b[0].reshape(3, HYENA_WIDTH)
    g32, gbf, x0 = _hyena_pre(u, cw, cb)
    z, deltas = _filter_features(l)
    fo = filt_w2.shape[1]
    hsd = _hyena_filter(
        z, _pad2(filt_w1[0], LANES, LANES), _pad2(filt_b1[0][None, :], 1, LANES),
        _pad2(filt_w2[0], LANES, LANES), _pad2(filt_b2[0][None, :], 1, LANES),
        _pad2(filt_w3[0], LANES, LANES), _pad2(filt_b3[0][None, :], 1, LANES),
        _pad2(filt_w4[0], LANES, 2 * HYENA_WIDTH), _pad2(filt_freq[0][None, :], 1, LANES), deltas, tl)
    del fo
    fwd, inv_r, inv_i = _dft_matrices(l)
    spec = _dft(fwd, hsd, tl)
    row0 = (jnp.arange(l) == 0)[:, None]
    sa = spec[0, :l]
    sd = jnp.where(row0, spec[0, l:l + 1], sa)
    sb = jnp.where(row0, 0.0, spec[1, l:])
    zr, zi = _dft_mul(fwd, gbf, sa, sb, sd, tl)
    yn = _idft(inv_r, inv_i, zr, zi, g32, x0, hyena_bias[0][None, :], hyena_out_norm_w[0][None, :], tl)

    w_out0 = w_out[0]
    base, h2hi, h2lo, h2rt = _merge(
        an, yn, x, mods, w_out0[:ATTN_WIDTH][perm].astype(BF16), w_out0[ATTN_WIDTH:].astype(BF16),
        norm2_w[0][None, :], sh_w_gate[0].astype(BF16), sh_w_up[0].astype(BF16), sh_w_down[0].astype(BF16), tl)

    tt = 256
    rwt = router_w[0].T
    rw_hi = rwt.astype(BF16)
    rw_lo = (rwt - rw_hi.astype(F32)).astype(BF16)
    bias = jnp.broadcast_to(router_bias[0][:, None], (N_EXPERTS, tt))
    ti = jnp.arange(tt)
    tri = jnp.stack([(ti[:, None] < ti[None, :]), jnp.ones((tt, tt), bool)]).astype(BF16)
    idx, wgt, rank, cnt = _router(h2hi.reshape(t, d), h2lo.reshape(t, d), rw_hi, rw_lo, bias, tri, tt)

    counts = cnt[:, 0].astype(jnp.int32)
    padded = (counts + EXPERT_ROWS - 1) // EXPERT_ROWS * EXPERT_ROWS
    pad_end = jnp.cumsum(padded)
    pad_start = pad_end - padded
    dest = _dest(idx, rank, jnp.broadcast_to(pad_start.astype(F32)[:, None], (N_EXPERTS, tt)), tt)
    n_rows = (t * TOP_K + N_EXPERTS * (EXPERT_ROWS - 1) + EXPERT_ROWS - 1) // EXPERT_ROWS * EXPERT_ROWS
    nblk = n_rows // EXPERT_ROWS
    blk_row = jnp.arange(nblk, dtype=jnp.int32) * EXPERT_ROWS
    blk_e = jnp.minimum(jnp.sum((pad_end[None, :] <= blk_row[:, None]).astype(jnp.int32), axis=1), N_EXPERTS - 1)
    n_used = (pad_end[-1:] // EXPERT_ROWS).astype(jnp.int32)

    n_valid = jnp.clip(counts[blk_e] - (blk_row - pad_start[blk_e]), 0, EXPERT_ROWS).astype(jnp.int32)

    assign = _row_assignment(dest.reshape(TOP_K * t), n_rows, t).reshape(nblk, 1, EXPERT_ROWS)
    ys = _experts(blk_e, n_used, n_valid, assign, h2rt, exp_w_gate[0], exp_w_up[0], exp_w_down[0])
    return _combine(dest, ys, wgt.T, base, mods, final_norm_w[None, :], min(128, l))
```

```python
import functools
import math

import jax
import jax.numpy as jnp
from jax import lax
from jax.experimental import pallas as pl
from jax.experimental.pallas import tpu as pltpu
from jax.experimental.pallas import tpu_sc as plsc

F32 = jnp.float32
BF16 = jnp.bfloat16
HIGHEST = lax.Precision.HIGHEST

GRID_W = 64
N_HEADS = 8
N_KV_HEADS = 2
HEAD_DIM = 64
ATTN_WIDTH = N_HEADS * HEAD_DIM
KV_WIDTH = N_KV_HEADS * HEAD_DIM
HYENA_WIDTH = 512
Q_END = ATTN_WIDTH
K_END = Q_END + KV_WIDTH
V_END = K_END + KV_WIDTH
ROPE_THETA = 10000.0
FILTER_EMB = 33
FILTER_DECAY_FAST = 0.3
FILTER_DECAY_SLOW = 1.5
FILTER_TARGET = 1e-2
N_EXPERTS = 256
TOP_K = 8
N_GROUPS = 8
TOPK_GROUPS = 4
ROUTE_SCALE = 2.5
EPS = 1e-6

LANES = 128
ROW_SUB = 4
EXPERT_ROWS = 512
EXPERT_SUB = 256
NEG_INF = float("-inf")


def _params(semantics, vmem_mb=48, **kw):
    return pltpu.CompilerParams(dimension_semantics=semantics, vmem_limit_bytes=vmem_mb * 1024 * 1024, **kw)


def _rms(x, w):
    return x * lax.rsqrt(jnp.mean(x * x, axis=-1, keepdims=True) + EPS) * w


def _sigmoid(x):
    return 1.0 / (1.0 + jnp.exp(-x))


def _adaln_kernel(c_ref, w_ref, b_ref, o_ref):
    c = c_ref[...]
    s = c * _sigmoid(c)
    o_ref[...] = jnp.dot(s, w_ref[...], precision=HIGHEST, preferred_element_type=F32) + b_ref[...]


def _adaln(cond, w, b):
    rows, d = cond.shape
    n = w.shape[1]
    tn = 1536
    return pl.pallas_call(
        _adaln_kernel,
        out_shape=jax.ShapeDtypeStruct((rows, n), F32),
        grid=(n // tn,),
        in_specs=[pl.BlockSpec((rows, d), lambda j: (0, 0)),
                  pl.BlockSpec((d, tn), lambda j: (0, j)),
                  pl.BlockSpec((1, tn), lambda j: (0, j))],
        out_specs=pl.BlockSpec((rows, tn), lambda j: (0, j)),
        compiler_params=_params(("arbitrary",)),
        name="adaln",
    )(cond, w, b)


def _head_rms(t, gmat, w):
    ms = jnp.dot((t * t).astype(BF16), gmat, preferred_element_type=F32)
    return t * lax.rsqrt(ms + EPS) * w


def _modulated(x, norm_w, shift, scale):
    return _rms(x, norm_w) * (1.0 + scale) + shift


def _ctx_kv_kernel(ctx_ref, mod_ref, n1_ref, w_ref, g_ref, kn_ref, kc_ref, vc_ref):
    x = ctx_ref[0]
    h = _modulated(x, n1_ref[...], mod_ref[0:1, :], mod_ref[1:2, :])
    kv = jnp.dot(h.astype(BF16), w_ref[...], preferred_element_type=F32)
    k = _head_rms(kv[:, :KV_WIDTH], g_ref[...], kn_ref[...])
    v = kv[:, KV_WIDTH:]
    kc_ref[0] = k.astype(BF16)
    vc_ref[0] = jnp.concatenate([v, jnp.ones_like(v)], axis=1).astype(BF16)


def _ctx_kv(ctx, cmod, n1w, w_kv, gk, knw):
    b, c, d = ctx.shape
    return pl.pallas_call(
        _ctx_kv_kernel,
        out_shape=(jax.ShapeDtypeStruct((b, c, KV_WIDTH), BF16),
                   jax.ShapeDtypeStruct((b, c, 2 * KV_WIDTH), BF16)),
        grid=(b,),
        in_specs=[pl.BlockSpec((1, c, d), lambda i: (i, 0, 0)),
                  pl.BlockSpec(cmod.shape, lambda i: (0, 0)),
                  pl.BlockSpec((1, d), lambda i: (0, 0)),
                  pl.BlockSpec(w_kv.shape, lambda i: (0, 0)),
                  pl.BlockSpec(gk.shape, lambda i: (0, 0)),
                  pl.BlockSpec((1, KV_WIDTH), lambda i: (0, 0))],
        out_specs=(pl.BlockSpec((1, c, KV_WIDTH), lambda i: (i, 0, 0)),
                   pl.BlockSpec((1, c, 2 * KV_WIDTH), lambda i: (i, 0, 0))),
        compiler_params=_params(("arbitrary",)),
        name="ctx_kv",
    )(ctx, cmod, n1w, w_kv, gk, knw)


def _rope(t, cos, sin, even):
    width = t.shape[1]
    partner = jnp.where(even, pltpu.roll(t, width - 1, axis=1), pltpu.roll(t, 1, axis=1))
    return t * cos + partner * sin


def _inproj_kernel(x_ref, mod_ref, n1_ref, w_ref, gq_ref, qn_ref, kn_ref, cos_ref, sin_ref,
                   q_ref, k_ref, v_ref, u_ref):
    x = x_ref[0]
    h = _modulated(x, n1_ref[...], mod_ref[0, 0:1, :], mod_ref[0, 1:2, :])
    p = jnp.dot(h.astype(BF16), w_ref[...], preferred_element_type=F32)
    gq = gq_ref[...]
    q = _head_rms(p[:, :Q_END], gq, qn_ref[...])
    k = _head_rms(p[:, Q_END:K_END], gq[:KV_WIDTH, :KV_WIDTH], kn_ref[...])
    v = p[:, K_END:V_END]
    cos = cos_ref[...]
    sin = sin_ref[...]
    reps = Q_END // LANES
    cos_q = jnp.concatenate([cos] * reps, axis=1)
    sin_q = jnp.concatenate([sin] * reps, axis=1)
    even_q = (lax.broadcasted_iota(jnp.int32, (1, Q_END), 1) & 1) == 0
    even_k = (lax.broadcasted_iota(jnp.int32, (1, KV_WIDTH), 1) & 1) == 0
    q = _rope(q, cos_q, sin_q, even_q) * (HEAD_DIM ** -0.5)
    k = _rope(k, cos, sin, even_k)
    q_ref[0] = q.astype(BF16)
    k_ref[0] = k.astype(BF16)
    v_ref[0] = jnp.concatenate([v, jnp.ones_like(v)], axis=1).astype(BF16)
    u_ref[0] = p[:, V_END:]


def _inproj(x, mods, n1w, w_in, gq, qnw, knw, cos, sin, tl):
    b, l, d = x.shape
    ncol = w_in.shape[1]
    nu = ncol - V_END
    return pl.pallas_call(
        _inproj_kernel,
        out_shape=(jax.ShapeDtypeStruct((b, l, Q_END), BF16),
                   jax.ShapeDtypeStruct((b, l, KV_WIDTH), BF16),
                   jax.ShapeDtypeStruct((b, l, 2 * KV_WIDTH), BF16),
                   jax.ShapeDtypeStruct((b, l, nu), F32)),
        grid=(l // tl, b),
        in_specs=[pl.BlockSpec((1, tl, d), lambda i, j: (j, i, 0)),
                  pl.BlockSpec((1, 6, d), lambda i, j: (j, 0, 0)),
                  pl.BlockSpec((1, d), lambda i, j: (0, 0)),
                  pl.BlockSpec((d, ncol), lambda i, j: (0, 0)),
                  pl.BlockSpec(gq.shape, lambda i, j: (0, 0)),
                  pl.BlockSpec((1, Q_END), lambda i, j: (0, 0)),
                  pl.BlockSpec((1, KV_WIDTH), lambda i, j: (0, 0)),
                  pl.BlockSpec((tl, LANES), lambda i, j: (i, 0)),
                  pl.BlockSpec((tl, LANES), lambda i, j: (i, 0))],
        out_specs=(pl.BlockSpec((1, tl, Q_END), lambda i, j: (j, i, 0)),
                   pl.BlockSpec((1, tl, KV_WIDTH), lambda i, j: (j, i, 0)),
                   pl.BlockSpec((1, tl, 2 * KV_WIDTH), lambda i, j: (j, i, 0)),
                   pl.BlockSpec((1, tl, nu), lambda i, j: (j, i, 0))),
        compiler_params=_params(("arbitrary", "arbitrary")),
        name="inproj",
    )(x, mods, n1w, w_in, gq, qnw, knw, cos, sin)


def _attn_kernel(q_ref, k_ref, v_ref, wn_ref, o_ref, *, sub):
    tq = q_ref.shape[1]
    low = lax.broadcasted_iota(jnp.int32, (1, LANES), 1) < HEAD_DIM
    nt = (((1,), (1,)), ((), ()))
    kk = k_ref[0]
    vv = v_ref[0]
    for r in range(0, tq, sub):
        outs = []
        for j in range(Q_END // LANES):
            qv = q_ref[0, r:r + sub, LANES * j:LANES * (j + 1)]
            zero = jnp.zeros_like(qv)
            halves = []
            for g in range(N_KV_HEADS):
                qh = jnp.where(low, qv, zero) if g == 0 else jnp.where(low, zero, qv)
                s = lax.dot_general(qh, kk, nt, preferred_element_type=F32)
                p = jnp.exp(s - jnp.max(s, axis=-1, keepdims=True)).astype(BF16)
                pv = jnp.dot(p, vv, preferred_element_type=F32)
                halves.append(pv[:, :LANES] / pv[:, LANES:])
            outs.append(jnp.where(low, halves[0], halves[1]))
        a = jnp.concatenate(outs, axis=1)
        o_ref[0, r:r + sub, :] = _rms(a, wn_ref[...]).astype(BF16)


def _attention(q, k, v, wn, tq, sub):
    b, l, _ = q.shape
    n = k.shape[1]
    return pl.pallas_call(
        functools.partial(_attn_kernel, sub=sub),
        out_shape=jax.ShapeDtypeStruct((b, l, Q_END), BF16),
        grid=(b, l // tq),
        in_specs=[pl.BlockSpec((1, tq, Q_END), lambda i, j: (i, j, 0)),
                  pl.BlockSpec((1, n, KV_WIDTH), lambda i, j: (i, 0, 0)),
                  pl.BlockSpec((1, n, 2 * KV_WIDTH), lambda i, j: (i, 0, 0)),
                  pl.BlockSpec((1, Q_END), lambda i, j: (0, 0))],
        out_specs=pl.BlockSpec((1, tq, Q_END), lambda i, j: (i, j, 0)),
        compiler_params=_params(("arbitrary", "arbitrary")),
        name="attn",
    )(q, k, v, wn)


def _hyena_pre_kernel(u0_ref, u1_ref, u2_ref, cw_ref, cb_ref, g32_ref, gbf_ref, x0_ref):
    l = u0_ref.shape[1]
    row = lax.broadcasted_iota(jnp.int32, (l, LANES), 0)

    def conv(u_ref, gi):
        u = u_ref[0]
        prev = jnp.where(row == 0, 0.0, pltpu.roll(u, 1, axis=0))
        nxt = jnp.where(row == l - 1, 0.0, pltpu.roll(u, l - 1, axis=0))
        w = cw_ref[gi]
        return w[0:1] * prev + w[1:2] * u + w[2:3] * nxt + cb_ref[gi:gi + 1, :]

    x0 = conv(u0_ref, 0)
    x1 = conv(u1_ref, 1)
    v = conv(u2_ref, 2)
    g = v * x1
    g32_ref[0] = g
    gbf_ref[0] = g.astype(BF16)
    x0_ref[0] = x0


def _hyena_pre(u, cw, cb):
    b, l, _ = u.shape
    nblk = HYENA_WIDTH // LANES
    ublk = lambda gi: pl.BlockSpec((1, l, LANES), lambda i, j: (i, 0, gi * nblk + j))
    oblk = pl.BlockSpec((1, l, LANES), lambda i, j: (i, 0, j))
    return pl.pallas_call(
        _hyena_pre_kernel,
        out_shape=(jax.ShapeDtypeStruct((b, l, HYENA_WIDTH), F32),
                   jax.ShapeDtypeStruct((b, l, HYENA_WIDTH), BF16),
                   jax.ShapeDtypeStruct((b, l, HYENA_WIDTH), F32)),
        grid=(b, nblk),
        in_specs=[ublk(0), ublk(1), ublk(2),
                  pl.BlockSpec((3, 3, LANES), lambda i, j: (0, 0, j)),
                  pl.BlockSpec((3, LANES), lambda i, j: (0, j))],
        out_specs=(oblk, oblk, oblk),
        compiler_params=_params(("arbitrary", "arbitrary")),
        name="hyena_pre",
    )(u, u, u, cw, cb)


def _filter_kernel(z_ref, w1_ref, b1_ref, w2_ref, b2_ref, w3_ref, b3_ref, w4_ref, fr_ref, dl_ref, o_ref):
    tl = z_ref.shape[0]
    z = z_ref[...]
    fr = fr_ref[...]
    dot = lambda a, w: jnp.dot(a, w, precision=HIGHEST, preferred_element_type=F32)
    h = jnp.sin(fr * (dot(z, w1_ref[...]) + b1_ref[...]))
    h = jnp.sin(fr * (dot(h, w2_ref[...]) + b2_ref[...]))
    h = jnp.sin(fr * (dot(h, w3_ref[...]) + b3_ref[...]))
    h = dot(h, w4_ref[...])
    t = z[:, 0:1]
    decay = jnp.exp(-t * jnp.abs(dl_ref[...]))
    hf = h[:, :HYENA_WIDTH] * decay
    hb = h[:, HYENA_WIDTH:] * decay
    row = lax.broadcasted_iota(jnp.int32, (tl, HYENA_WIDTH), 0) + pl.program_id(0) * tl
    hb = jnp.where(row == 0, 0.0, hb)
    o_ref[0] = hf + hb
    o_ref[1] = hf - hb


def _hyena_filter(z, w1, b1, w2, b2, w3, b3, w4, freq, deltas, tl):
    l = z.shape[0]
    full = lambda a: pl.BlockSpec(a.shape, lambda i: (0,) * a.ndim)
    return pl.pallas_call(
        _filter_kernel,
        out_shape=jax.ShapeDtypeStruct((2, l, HYENA_WIDTH), F32),
        grid=(l // tl,),
        in_specs=[pl.BlockSpec((tl, z.shape[1]), lambda i: (i, 0)),
                  full(w1), full(b1), full(w2), full(b2), full(w3), full(b3), full(w4), full(freq), full(deltas)],
        out_specs=pl.BlockSpec((2, tl, HYENA_WIDTH), lambda i: (0, i, 0)),
        compiler_params=_params(("arbitrary",)),
        name="hyena_filter",
    )(z, w1, b1, w2, b2, w3, b3, w4, freq, deltas)


def _dft_kernel(f_ref, x_ref, o_ref):
    o_ref[0] = jnp.dot(f_ref[...], x_ref[0].astype(BF16), preferred_element_type=F32)


def _dft(fmat, x, tf):
    nb, l, w = x.shape
    n = fmat.shape[0]
    return pl.pallas_call(
        _dft_kernel,
        out_shape=jax.ShapeDtypeStruct((nb, n, w), F32),
        grid=(n // tf, nb),
        in_specs=[pl.BlockSpec((tf, l), lambda i, j: (i, 0)),
                  pl.BlockSpec((1, l, w), lambda i, j: (j, 0, 0))],
        out_specs=pl.BlockSpec((1, tf, w), lambda i, j: (j, i, 0)),
        compiler_params=_params(("arbitrary", "arbitrary")),
        name="dft_filter",
    )(fmat, x)


def _dft_mul_kernel(fr_ref, fi_ref, x_ref, a_ref, b_ref, d_ref, zr_ref, zi_ref):
    x = x_ref[0]
    xr = jnp.dot(fr_ref[...], x, preferred_element_type=F32)
    xi = jnp.dot(fi_ref[...], x, preferred_element_type=F32)
    bb = b_ref[...]
    zr_ref[0] = (xr * a_ref[...] - xi * bb).astype(BF16)
    zi_ref[0] = (xr * bb + xi * d_ref[...]).astype(BF16)


def _dft_mul(fmat, g, sa, sb, sd, tf):
    b, l, w = g.shape
    nf = l // tf
    spec = pl.BlockSpec((tf, w), lambda i, j: (i, 0))
    return pl.pallas_call(
        _dft_mul_kernel,
        out_shape=(jax.ShapeDtypeStruct((b, l, w), BF16), jax.ShapeDtypeStruct((b, l, w), BF16)),
        grid=(nf, b),
        in_specs=[pl.BlockSpec((tf, l), lambda i, j: (i, 0)),
                  pl.BlockSpec((tf, l), lambda i, j: (i + nf, 0)),
                  pl.BlockSpec((1, l, w), lambda i, j: (j, 0, 0)),
                  spec, spec, spec],
        out_specs=(pl.BlockSpec((1, tf, w), lambda i, j: (j, i, 0)),
                   pl.BlockSpec((1, tf, w), lambda i, j: (j, i, 0))),
        compiler_params=_params(("arbitrary", "arbitrary")),
        name="dft_mul",
    )(fmat, fmat, g, sa, sb, sd)


def _idft_kernel(fr_ref, fi_ref, zr_ref, zi_ref, g_ref, x0_ref, hb_ref, wn_ref, o_ref):
    conv = (jnp.dot(fr_ref[...], zr_ref[0], preferred_element_type=F32)
            + jnp.dot(fi_ref[...], zi_ref[0], preferred_element_type=F32))
    y = (conv + g_ref[0] * hb_ref[...]) * x0_ref[0]
    o_ref[0] = _rms(y, wn_ref[...]).astype(BF16)


def _idft(finv_r, finv_i, zr, zi, g32, x0, hbias, wn, tt):
    b, l, w = zr.shape
    tile = pl.BlockSpec((1, tt, w), lambda i, j: (j, i, 0))
    return pl.pallas_call(
        _idft_kernel,
        out_shape=jax.ShapeDtypeStruct((b, l, w), BF16),
        grid=(l // tt, b),
        in_specs=[pl.BlockSpec((tt, l), lambda i, j: (i, 0)),
                  pl.BlockSpec((tt, l), lambda i, j: (i, 0)),
                  pl.BlockSpec((1, l, w), lambda i, j: (j, 0, 0)),
                  pl.BlockSpec((1, l, w), lambda i, j: (j, 0, 0)),
                  tile, tile,
                  pl.BlockSpec((1, w), lambda i, j: (0, 0)),
                  pl.BlockSpec((1, w), lambda i, j: (0, 0))],
        out_specs=tile,
        compiler_params=_params(("arbitrary", "arbitrary")),
        name="idft",
    )(finv_r, finv_i, zr, zi, g32, x0, hbias, wn)


def _bf16_bits(v):
    return lax.bitcast_convert_type(v.astype(BF16).astype(F32), jnp.uint32)


def _store_row_tiles(ref, val):
    rows, half = val.shape[0], val.shape[1] // 2
    assert half == ROW_SUB * LANES
    for j in range(ROW_SUB):
        lo = _bf16_bits(val[:, LANES * j:LANES * (j + 1)]) >> 16
        hi = _bf16_bits(val[:, half + LANES * j:half + LANES * (j + 1)]) & jnp.uint32(0xFFFF0000)
        ref[pl.ds(j, rows, stride=ROW_SUB), :] = lo | hi


def _load_row_tiles(ref, rows):
    words = [ref[pl.ds(j, rows, stride=ROW_SUB), :] for j in range(ROW_SUB)]
    lo = [lax.bitcast_convert_type(w << 16, F32) for w in words]
    hi = [lax.bitcast_convert_type(w & jnp.uint32(0xFFFF0000), F32) for w in words]
    return jnp.concatenate(lo + hi, axis=1)


def _merge_kernel(a_ref, y_ref, x_ref, mod_ref, wa_ref, wy_ref, n2_ref, sg_ref, su_ref, sd_ref,
                  base_ref, hi_ref, lo_ref, rt_ref):
    m = (jnp.dot(a_ref[0], wa_ref[...], preferred_element_type=F32)
         + jnp.dot(y_ref[0], wy_ref[...], preferred_element_type=F32))
    x1 = x_ref[0] + mod_ref[0, 2:3, :] * m
    h2 = _modulated(x1, n2_ref[...], mod_ref[0, 3:4, :], mod_ref[0, 4:5, :])
    hi = h2.astype(BF16)
    hi_ref[0] = hi
    lo_ref[0] = (h2 - hi.astype(F32)).astype(BF16)
    _store_row_tiles(rt_ref, h2)
    gate = jnp.dot(hi, sg_ref[...], preferred_element_type=F32)
    up = jnp.dot(hi, su_ref[...], preferred_element_type=F32)
    act = (gate * _sigmoid(gate) * up).astype(BF16)
    shared = jnp.dot(act, sd_ref[...], preferred_element_type=F32)
    base_ref[0] = x1 + mod_ref[0, 5:6, :] * shared


def _merge(an, yn, x, mods, wa, wy, n2w, sg, su, sd, tl):
    b, l, d = x.shape
    full = lambda a: pl.BlockSpec(a.shape, lambda i, j: (0,) * a.ndim)
    half = pl.BlockSpec((1, tl, an.shape[2]), lambda i, j: (i, j, 0))
    wide = pl.BlockSpec((1, tl, d), lambda i, j: (i, j, 0))
    per_b = l // tl
    return pl.pallas_call(
        _merge_kernel,
        out_shape=(jax.ShapeDtypeStruct((b, l, d), F32),
                   jax.ShapeDtypeStruct((b, l, d), BF16),
                   jax.ShapeDtypeStruct((b, l, d), BF16),
                   jax.ShapeDtypeStruct((b * l * ROW_SUB, LANES), jnp.uint32)),
        grid=(b, per_b),
        in_specs=[half, half, wide,
                  pl.BlockSpec((1, 6, d), lambda i, j: (i, 0, 0)),
                  full(wa), full(wy), full(n2w), full(sg), full(su), full(sd)],
        out_specs=(wide, wide, wide,
                   pl.BlockSpec((tl * ROW_SUB, LANES), lambda i, j: (i * per_b + j, 0))),
        compiler_params=_params(("arbitrary", "arbitrary")),
        name="merge",
    )(an, yn, x, mods, wa, wy, n2w, sg, su, sd)


def _router_kernel(hi_ref, lo_ref, whi_ref, wlo_ref, bias_ref, tri_ref,
                   idx_ref, wgt_ref, rank_ref, cnt_ref, run_ref):
    tt = hi_ref.shape[0]
    per_group = N_EXPERTS // N_GROUPS

    @pl.when(pl.program_id(0) == 0)
    def _():
        run_ref[...] = jnp.zeros_like(run_ref)

    nt = (((1,), (1,)), ((), ()))
    hi = hi_ref[...]
    whi = whi_ref[...]
    logits = (lax.dot_general(whi, hi, nt, preferred_element_type=F32)
              + lax.dot_general(whi, lo_ref[...], nt, preferred_element_type=F32)
              + lax.dot_general(wlo_ref[...], hi, nt, preferred_element_type=F32))
    scores = _sigmoid(logits)
    biased = scores + bias_ref[...]

    ridx = lax.broadcasted_iota(jnp.int32, (per_group, tt), 0)
    groups = [biased[g * per_group:(g + 1) * per_group, :] for g in range(N_GROUPS)]
    gs = []
    for blk in groups:
        m1 = jnp.max(blk, axis=0, keepdims=True)
        i1 = jnp.min(jnp.where(blk == m1, ridx, per_group), axis=0, keepdims=True)
        m2 = jnp.max(jnp.where(ridx == i1, NEG_INF, blk), axis=0, keepdims=True)
        gs.append(m1 + m2)

    kept = []
    for g in range(N_GROUPS):
        ahead = jnp.zeros((1, tt), F32)
        for o in range(N_GROUPS):
            if o != g:
                wins = (gs[o] >= gs[g]) if o < g else (gs[o] > gs[g])
                ahead = ahead + jnp.where(wins, 1.0, 0.0)
        kept.append(jnp.where(ahead < TOPK_GROUPS, groups[g], NEG_INF))
    cur = jnp.concatenate(kept, axis=0)

    eidx = lax.broadcasted_iota(jnp.int32, cur.shape, 0)
    onehot = jnp.zeros(cur.shape, F32)
    picks = []
    wsel = []
    for _ in range(TOP_K):
        mx = jnp.max(cur, axis=0, keepdims=True)
        first = jnp.min(jnp.where(cur == mx, eidx, N_EXPERTS), axis=0, keepdims=True)
        sel = eidx == first
        picks.append(first)
        wsel.append(jnp.sum(jnp.where(sel, scores, 0.0), axis=0, keepdims=True))
        onehot = jnp.where(sel, 1.0, onehot)
        cur = jnp.where(sel, NEG_INF, cur)
    w = jnp.concatenate(wsel, axis=0)
    w = w / jnp.sum(w, axis=0, keepdims=True) * ROUTE_SCALE
    idx = jnp.concatenate(picks, axis=0)

    oh = onehot.astype(BF16)
    before = jnp.dot(oh, tri_ref[0], preferred_element_type=F32)
    total = jnp.dot(oh, tri_ref[1], preferred_element_type=F32)
    pos = run_ref[...] + before
    ranks = [jnp.sum(jnp.where(eidx == p, pos, 0.0), axis=0, keepdims=True) for p in picks]
    run_ref[...] = run_ref[...] + total

    idx_ref[...] = idx
    wgt_ref[...] = w
    rank_ref[...] = jnp.concatenate(ranks, axis=0).astype(jnp.int32)
    cnt_ref[...] = run_ref[...]


def _router(hi, lo, whi, wlo, bias, tri, tt):
    t, d = hi.shape
    tok = pl.BlockSpec((tt, d), lambda i: (i, 0))
    full = lambda a: pl.BlockSpec(a.shape, lambda i: (0,) * a.ndim)
    out = pl.BlockSpec((TOP_K, tt), lambda i: (0, i))
    return pl.pallas_call(
        _router_kernel,
        out_shape=(jax.ShapeDtypeStruct((TOP_K, t), jnp.int32),
                   jax.ShapeDtypeStruct((TOP_K, t), F32),
                   jax.ShapeDtypeStruct((TOP_K, t), jnp.int32),
                   jax.ShapeDtypeStruct((N_EXPERTS, tt), F32)),
        grid=(t // tt,),
        in_specs=[tok, tok, full(whi), full(wlo), full(bias), full(tri)],
        out_specs=(out, out, out, pl.BlockSpec((N_EXPERTS, tt), lambda i: (0, 0))),
        scratch_shapes=[pltpu.VMEM((N_EXPERTS, tt), F32)],
        compiler_params=_params(("arbitrary",)),
        name="router",
    )(hi, lo, whi, wlo, bias, tri)


def _row_tile(ref, r):
    return ref.at[pl.ds(pl.multiple_of(r * ROW_SUB, ROW_SUB), ROW_SUB), :]


def _dest_kernel(idx_ref, rank_ref, start_ref, dest_ref):
    eidx = lax.broadcasted_iota(jnp.int32, start_ref.shape, 0)
    start = start_ref[...]
    rows = [jnp.sum(jnp.where(eidx == idx_ref[k:k + 1, :], start, 0.0), axis=0, keepdims=True)
            for k in range(TOP_K)]
    dest_ref[...] = jnp.concatenate(rows, axis=0).astype(jnp.int32) + rank_ref[...]


def _dest(idx, rank, start, tt):
    t = idx.shape[1]
    blk = pl.BlockSpec((TOP_K, tt), lambda i: (0, i))
    return pl.pallas_call(
        _dest_kernel,
        out_shape=jax.ShapeDtypeStruct((TOP_K, t), jnp.int32),
        grid=(t // tt,),
        in_specs=[blk, blk, pl.BlockSpec(start.shape, lambda i: (0, 0))],
        out_specs=blk,
        compiler_params=_params(("arbitrary",)),
        name="dest",
    )(idx, rank, start)


def _row_assignment(dest_flat, n_rows, t):
    info = plsc.get_sparse_core_info()
    lanes = info.num_lanes
    workers = info.num_cores * info.num_subcores
    own = n_rows // workers
    chunk = min(8192, t)
    assert n_rows % (workers * lanes) == 0 and t % chunk == 0 and chunk % lanes == 0

    def body(dest_hbm, out_hbm, buf, part):
        wid = lax.axis_index("s") * info.num_cores + lax.axis_index("c")
        base = wid * own
        lane = lax.iota(jnp.int32, lanes)

        @pl.loop(0, own, step=lanes)
        def _(r):
            buf[pl.ds(r, lanes)] = jnp.zeros((lanes,), jnp.int32)

        @pl.loop(0, TOP_K)
        def _(k):
            @pl.loop(0, t, step=chunk)
            def _(c):
                pltpu.sync_copy(dest_hbm.at[pl.ds(k * t + c, chunk)], part)

                @pl.loop(0, chunk, step=lanes)
                def _(j):
                    rel = part[pl.ds(j, lanes)] - base
                    mine = jnp.logical_and(rel >= 0, rel < own)
                    val = (c + j + lane) * TOP_K + k
                    plsc.store_scatter(buf, [jnp.where(mine, rel, 0)], val, mask=mine)

        pltpu.sync_copy(buf, out_hbm.at[pl.ds(base, own)])

    return pl.kernel(
        body,
        out_type=jax.ShapeDtypeStruct((n_rows,), jnp.int32),
        mesh=plsc.VectorSubcoreMesh(core_axis_name="c", subcore_axis_name="s"),
        scratch_types=[pltpu.VMEM((own,), jnp.int32), pltpu.VMEM((chunk,), jnp.int32)],
        compiler_params=pltpu.CompilerParams(needs_layout_passes=False),
        name="row_assignment",
    )(dest_flat)


def _experts_kernel(be_ref, nu_ref, cur_ref, nxt_ref, h_ref, wg_ref, wu_ref, wd_ref, y_ref,
                    xbuf, sem, wgu_bf, wd_bf):
    i = pl.program_id(0)
    n_used = nu_ref[0]
    used = i < n_used
    slot = i % 2
    new_expert = jnp.logical_or(i == 0, be_ref[i] != be_ref[jnp.maximum(i - 1, 0)])

    def row_copy(token, s, r):
        return pltpu.make_async_copy(_row_tile(h_ref, token), _row_tile(xbuf.at[s], r), sem.at[s])

    def issue(a_ref, s):
        for r in range(EXPERT_ROWS):
            row_copy(a_ref[0, 0, r], s, r).start()

    def drain(s):
        for r in range(EXPERT_ROWS):
            row_copy(0, s, r).wait()

    @pl.when(i == 0)
    def _():
        issue(cur_ref, 0)

    @pl.when(jnp.logical_and(used, new_expert))
    def _():
        ff = wg_ref.shape[2]
        wgu_bf[:, :ff] = wg_ref[0].astype(BF16)
        wgu_bf[:, ff:] = wu_ref[0].astype(BF16)
        wd_bf[...] = wd_ref[0].astype(BF16)

    @pl.when(used)
    def _():
        drain(slot)
        issue(nxt_ref, 1 - slot)
        ff = wg_ref.shape[2]
        for r in range(0, EXPERT_ROWS, EXPERT_SUB):
            rows = pl.ds(r * ROW_SUB, EXPERT_SUB * ROW_SUB)
            x = _load_row_tiles(xbuf.at[slot, rows, :], EXPERT_SUB).astype(BF16)
            gu = jnp.dot(x, wgu_bf[...], preferred_element_type=F32)
            gate = gu[:, :ff]
            act = (gate * _sigmoid(gate) * gu[:, ff:]).astype(BF16)
            _store_row_tiles(y_ref.at[rows, :], jnp.dot(act, wd_bf[...], preferred_element_type=F32))

    @pl.when(i == n_used - 1)
    def _():
        drain(1 - slot)


def _experts(blk_e, n_used, assign, h_rt, wg, wu, wd):
    d, ff = wg.shape[1], wg.shape[2]
    nblk = assign.shape[0]
    last = lambda i, nu: jnp.minimum(i, nu[0] - 1)
    wsel = lambda i, be, nu: (be[last(i, nu)], 0, 0)
    return pl.pallas_call(
        _experts_kernel,
        out_shape=jax.ShapeDtypeStruct((nblk * EXPERT_ROWS * ROW_SUB, LANES), jnp.uint32),
        grid_spec=pltpu.PrefetchScalarGridSpec(
            num_scalar_prefetch=2,
            grid=(nblk,),
            in_specs=[pl.BlockSpec((1, 1, EXPERT_ROWS), lambda i, be, nu: (last(i, nu), 0, 0),
                                   memory_space=pltpu.SMEM),
                      pl.BlockSpec((1, 1, EXPERT_ROWS), lambda i, be, nu: (last(i + 1, nu), 0, 0),
                                   memory_space=pltpu.SMEM),
                      pl.BlockSpec(memory_space=pl.ANY),
                      pl.BlockSpec((1, d, ff), wsel),
                      pl.BlockSpec((1, d, ff), wsel),
                      pl.BlockSpec((1, ff, d), wsel)],
            out_specs=pl.BlockSpec((EXPERT_ROWS * ROW_SUB, LANES), lambda i, be, nu: (last(i, nu), 0)),
            scratch_shapes=[pltpu.VMEM((2, EXPERT_ROWS * ROW_SUB, LANES), jnp.uint32),
                            pltpu.SemaphoreType.DMA((2,)),
                            pltpu.VMEM((d, 2 * ff), BF16), pltpu.VMEM((ff, d), BF16)]),
        compiler_params=_params(("arbitrary",)),
        name="experts",
    )(blk_e, n_used, assign, assign, h_rt, wg, wu, wd)


def _combine_kernel(dest_ref, next_ref, ys_ref, w_ref, base_ref, mod_ref, fw_ref, o_ref, buf, sem):
    tc = dest_ref.shape[1]
    i = pl.program_id(0)
    slot = i % 2

    def row_copy(row, s, t, k):
        return pltpu.make_async_copy(_row_tile(ys_ref, row), _row_tile(buf.at[s, k], t), sem.at[s])

    def issue(d_ref, s):
        def body(t, carry):
            for k in range(TOP_K):
                row_copy(d_ref[k, t], s, t, k).start()
            return carry

        lax.fori_loop(0, tc, body, 0)

    @pl.when(i == 0)
    def _():
        issue(dest_ref, 0)

    @pl.when(i + 1 < pl.num_programs(0))
    def _():
        issue(next_ref, 1 - slot)

    def drain(t, carry):
        for k in range(TOP_K):
            row_copy(0, slot, t, k).wait()
        return carry

    lax.fori_loop(0, tc, drain, 0)

    w = w_ref[...]
    routed = w[:, 0:1] * _load_row_tiles(buf.at[slot, 0], tc)
    for k in range(1, TOP_K):
        routed = routed + w[:, k:k + 1] * _load_row_tiles(buf.at[slot, k], tc)
    x = base_ref[0] + mod_ref[0, 5:6, :] * routed
    o_ref[0] = _rms(x, fw_ref[...])


def _combine(dest, ys, wt, base, mods, fw, tc):
    b, l, d = base.shape
    per_b = l // tc
    steps = b * per_b
    return pl.pallas_call(
        _combine_kernel,
        out_shape=jax.ShapeDtypeStruct((b, l, d), F32),
        grid=(steps,),
        in_specs=[pl.BlockSpec((TOP_K, tc), lambda i: (0, i), memory_space=pltpu.SMEM),
                  pl.BlockSpec((TOP_K, tc), lambda i: (0, jnp.minimum(i + 1, steps - 1)), memory_space=pltpu.SMEM),
                  pl.BlockSpec(memory_space=pl.ANY),
                  pl.BlockSpec((tc, TOP_K), lambda i: (i, 0)),
                  pl.BlockSpec((1, tc, d), lambda i: (i // per_b, i % per_b, 0)),
                  pl.BlockSpec((1, 6, d), lambda i: (i // per_b, 0, 0)),
                  pl.BlockSpec((1, d), lambda i: (0, 0))],
        out_specs=pl.BlockSpec((1, tc, d), lambda i: (i // per_b, i % per_b, 0)),
        scratch_shapes=[pltpu.VMEM((2, TOP_K, tc * ROW_SUB, LANES), jnp.uint32), pltpu.SemaphoreType.DMA((2,))],
        compiler_params=_params(("arbitrary",)),
        name="combine",
    )(dest, dest, ys, wt, base, mods, fw)


def _rope_tables(l):
    t = jnp.arange(l, dtype=jnp.int32)
    row = (t // GRID_W).astype(F32)
    col = (t % GRID_W).astype(F32)
    n_freq = HEAD_DIM // 4
    inv = ROPE_THETA ** (-jnp.arange(n_freq, dtype=F32) / n_freq)
    ang = jnp.concatenate([row[:, None] * inv, col[:, None] * inv], axis=-1)
    cos = jnp.repeat(jnp.cos(ang), 2, axis=1)
    sin = jnp.repeat(jnp.sin(ang), 2, axis=1)
    sign = jnp.tile(jnp.array([-1.0, 1.0], F32), HEAD_DIM // 2)
    reps = LANES // HEAD_DIM
    return jnp.tile(cos, (1, reps)), jnp.tile(sin * sign, (1, reps))


def _filter_features(l):
    t = jnp.linspace(0.0, 1.0, l, dtype=F32)[:, None]
    bands = (FILTER_EMB - 1) // 2
    w = 2.0 * math.pi * jnp.arange(l, dtype=F32)[:, None] / l
    f = jnp.linspace(1e-4, bands - 1, bands, dtype=F32)[None, :]
    z = jnp.concatenate([t, jnp.cos(f * w), -jnp.sin(f * w)], axis=-1)
    min_decay = math.log(FILTER_TARGET) / FILTER_DECAY_FAST
    max_decay = math.log(FILTER_TARGET) / FILTER_DECAY_SLOW
    deltas = jnp.linspace(min_decay, max_decay, HYENA_WIDTH, dtype=F32)[None, :]
    return jnp.pad(z, ((0, 0), (0, LANES - FILTER_EMB))), deltas


def _dft_matrices(l):
    n = 2 * l
    idx = jnp.arange(l, dtype=jnp.int32)
    r = math.isqrt(l)
    assert r * r == l
    sub = jnp.arange(r, dtype=jnp.int32)
    hi = ((r * sub[:, None] * idx[None, :]) % n).astype(F32) * (2.0 * math.pi / n)
    lo = ((sub[:, None] * idx[None, :]) % n).astype(F32) * (2.0 * math.pi / n)
    ch, sh, cl, sl = jnp.cos(hi)[:, None, :], jnp.sin(hi)[:, None, :], jnp.cos(lo)[None], jnp.sin(lo)[None]
    c = (ch * cl - sh * sl).reshape(l, l)
    s = (sh * cl + ch * sl).reshape(l, l)
    alt = jnp.where(idx % 2 == 0, 1.0, -1.0).astype(F32)
    first = (idx == 0)[:, None]
    fwd = jnp.concatenate([c, jnp.where(first, alt[None, :], -s)], axis=0).astype(BF16)
    firstc = (idx == 0)[None, :]
    inv_r = (jnp.where(firstc, 1.0, 2.0) * c / n).astype(BF16)
    inv_i = (jnp.where(firstc, alt[:, None], -2.0 * s) / n).astype(BF16)
    return fwd, inv_r, inv_i


def _head_perm():
    order = []
    for j in range(N_HEADS // 2):
        order += list(range(j * HEAD_DIM, (j + 1) * HEAD_DIM))
        order += list(range((j + N_HEADS // 2) * HEAD_DIM, (j + 1 + N_HEADS // 2) * HEAD_DIM))
    return jnp.array(order, jnp.int32)


def _pad2(a, rows, cols):
    return jnp.pad(a, ((0, rows - a.shape[0]), (0, cols - a.shape[1])))


def kernel(x, c, ctx, c_ctx, mod_w, mod_b, norm1_w, w_in, q_norm_w, k_norm_w, conv_w, conv_b, filt_w1, filt_b1, filt_w2, filt_b2, filt_w3, filt_b3, filt_w4, filt_freq, hyena_bias, attn_out_norm_w, hyena_out_norm_w, w_out, norm2_w, router_w, router_bias, exp_w_gate, exp_w_up, exp_w_down, sh_w_gate, sh_w_up, sh_w_down, final_norm_w):
    b, l, d = x.shape
    t = b * l
    assert mod_w.shape[0] == 1, "single-layer stack"
    tl = min(512, l)

    cond = jnp.concatenate([c, c_ctx[None, :], jnp.zeros((-(b + 1) % 8, d), F32)], axis=0)
    mod = _adaln(cond, mod_w[0], mod_b[0][None, :])
    mods = mod[:b].reshape(b, 6, d)
    cmod = mod[b].reshape(6, d)

    perm = _head_perm()
    w_in0 = w_in[0]
    w_in_k = jnp.concatenate([w_in0[:, :Q_END][:, perm], w_in0[:, Q_END:]], axis=1).astype(BF16)
    w_kv = w_in0[:, Q_END:V_END].astype(BF16)
    gq = jnp.kron(jnp.eye(N_HEADS, dtype=F32), jnp.full((HEAD_DIM, HEAD_DIM), 1.0 / HEAD_DIM, F32)).astype(BF16)
    qnw = jnp.tile(q_norm_w[0], N_HEADS)[None, :]
    knw = jnp.tile(k_norm_w[0], N_KV_HEADS)[None, :]
    n1w = norm1_w[0][None, :]
    cos, sin = _rope_tables(l)

    kc, vc = _ctx_kv(ctx, cmod, n1w, w_kv, gq[:KV_WIDTH, :KV_WIDTH], knw)
    q, k, v, u = _inproj(x, mods, n1w, w_in_k, gq, qnw, knw, cos, sin, tl)
    k_all = jnp.concatenate([kc, k], axis=1)
    v_all = jnp.concatenate([vc, v], axis=1)
    an = _attention(q, k_all, v_all, attn_out_norm_w[0][perm][None, :], min(256, l), min(256, l))

    cw = conv_w[0].reshape(3, 3, HYENA_WIDTH).transpose(1, 0, 2)
    cb = conv_b[0].reshape(3, HYENA_WIDTH)
    g32, gbf, x0 = _hyena_pre(u, cw, cb)
    z, deltas = _filter_features(l)
    fo = filt_w2.shape[1]
    hsd = _hyena_filter(
        z, _pad2(filt_w1[0], LANES, LANES), _pad2(filt_b1[0][None, :], 1, LANES),
        _pad2(filt_w2[0], LANES, LANES), _pad2(filt_b2[0][None, :], 1, LANES),
        _pad2(filt_w3[0], LANES, LANES), _pad2(filt_b3[0][None, :], 1, LANES),
        _pad2(filt_w4[0], LANES, 2 * HYENA_WIDTH), _pad2(filt_freq[0][None, :], 1, LANES), deltas, tl)
    del fo
    fwd, inv_r, inv_i = _dft_matrices(l)
    spec = _dft(fwd, hsd, tl)
    row0 = (jnp.arange(l) == 0)[:, None]
    sa = spec[0, :l]
    sd = jnp.where(row0, spec[0, l:l + 1], sa)
    sb = jnp.where(row0, 0.0, spec[1, l:])
    zr, zi = _dft_mul(fwd, gbf, sa, sb, sd, tl)
    yn = _idft(inv_r, inv_i, zr, zi, g32, x0, hyena_bias[0][None, :], hyena_out_norm_w[0][None, :], tl)

    w_out0 = w_out[0]
    base, h2hi, h2lo, h2rt = _merge(
        an, yn, x, mods, w_out0[:ATTN_WIDTH][perm].astype(BF16), w_out0[ATTN_WIDTH:].astype(BF16),
        norm2_w[0][None, :], sh_w_gate[0].astype(BF16), sh_w_up[0].astype(BF16), sh_w_down[0].astype(BF16), tl)

    tt = 256
    rwt = router_w[0].T
    rw_hi = rwt.astype(BF16)
    rw_lo = (rwt - rw_hi.astype(F32)).astype(BF16)
    bias = jnp.broadcast_to(router_bias[0][:, None], (N_EXPERTS, tt))
    ti = jnp.arange(tt)
    tri = jnp.stack([(ti[:, None] < ti[None, :]), jnp.ones((tt, tt), bool)]).astype(BF16)
    idx, wgt, rank, cnt = _router(h2hi.reshape(t, d), h2lo.reshape(t, d), rw_hi, rw_lo, bias, tri, tt)

    counts = cnt[:, 0].astype(jnp.int32)
    padded = (counts + EXPERT_ROWS - 1) // EXPERT_ROWS * EXPERT_ROWS
    pad_end = jnp.cumsum(padded)
    pad_start = pad_end - padded
    dest = _dest(idx, rank, jnp.broadcast_to(pad_start.astype(F32)[:, None], (N_EXPERTS, tt)), tt)
    n_rows = (t * TOP_K + N_EXPERTS * (EXPERT_ROWS - 1) + EXPERT_ROWS - 1) // EXPERT_ROWS * EXPERT_ROWS
    nblk = n_rows // EXPERT_ROWS
    blk_row = jnp.arange(nblk, dtype=jnp.int32) * EXPERT_ROWS
    blk_e = jnp.minimum(jnp.sum((pad_end[None, :] <= blk_row[:, None]).astype(jnp.int32), axis=1), N_EXPERTS - 1)
    n_used = (pad_end[-1:] // EXPERT_ROWS).astype(jnp.int32)

    assign = _row_assignment(dest.reshape(TOP_K * t), n_rows, t)
    row_tok = (assign // TOP_K).reshape(nblk, 1, EXPERT_ROWS)
    ys = _experts(blk_e, n_used, row_tok, h2rt, exp_w_gate[0], exp_w_up[0], exp_w_down[0])
    return _combine(dest, ys, wgt.T, base, mods, final_norm_w[None, :], min(128, l))
```

```python
import functools
import math

import jax
import jax.numpy as jnp
from jax import lax
from jax.experimental import pallas as pl
from jax.experimental.pallas import tpu as pltpu
from jax.experimental.pallas import tpu_sc as plsc

F32 = jnp.float32
BF16 = jnp.bfloat16
HIGHEST = lax.Precision.HIGHEST

GRID_W = 64
N_HEADS = 8
N_KV_HEADS = 2
HEAD_DIM = 64
ATTN_WIDTH = N_HEADS * HEAD_DIM
KV_WIDTH = N_KV_HEADS * HEAD_DIM
HYENA_WIDTH = 512
Q_END = ATTN_WIDTH
K_END = Q_END + KV_WIDTH
V_END = K_END + KV_WIDTH
ROPE_THETA = 10000.0
FILTER_EMB = 33
FILTER_DECAY_FAST = 0.3
FILTER_DECAY_SLOW = 1.5
FILTER_TARGET = 1e-2
N_EXPERTS = 256
TOP_K = 8
N_GROUPS = 8
TOPK_GROUPS = 4
ROUTE_SCALE = 2.5
EPS = 1e-6

LANES = 128
ROW_SUB = 4
EXPERT_ROWS = 512
EXPERT_SUB = 256
NEG_INF = float("-inf")


def _params(semantics, vmem_mb=48, **kw):
    return pltpu.CompilerParams(dimension_semantics=semantics, vmem_limit_bytes=vmem_mb * 1024 * 1024, **kw)


def _rms(x, w):
    return x * lax.rsqrt(jnp.mean(x * x, axis=-1, keepdims=True) + EPS) * w


def _sigmoid(x):
    return 1.0 / (1.0 + jnp.exp(-x))


def _adaln_kernel(c_ref, w_ref, b_ref, o_ref):
    c = c_ref[...]
    s = c * _sigmoid(c)
    o_ref[...] = jnp.dot(s, w_ref[...], precision=HIGHEST, preferred_element_type=F32) + b_ref[...]


def _adaln(cond, w, b):
    rows, d = cond.shape
    n = w.shape[1]
    tn = 1536
    return pl.pallas_call(
        _adaln_kernel,
        out_shape=jax.ShapeDtypeStruct((rows, n), F32),
        grid=(n // tn,),
        in_specs=[pl.BlockSpec((rows, d), lambda j: (0, 0)),
                  pl.BlockSpec((d, tn), lambda j: (0, j)),
                  pl.BlockSpec((1, tn), lambda j: (0, j))],
        out_specs=pl.BlockSpec((rows, tn), lambda j: (0, j)),
        compiler_params=_params(("arbitrary",)),
        name="adaln",
    )(cond, w, b)


def _head_rms(t, gmat, w):
    ms = jnp.dot((t * t).astype(BF16), gmat, preferred_element_type=F32)
    return t * lax.rsqrt(ms + EPS) * w


def _modulated(x, norm_w, shift, scale):
    return _rms(x, norm_w) * (1.0 + scale) + shift


def _ctx_kv_kernel(ctx_ref, mod_ref, n1_ref, w_ref, g_ref, kn_ref, kc_ref, vc_ref):
    x = ctx_ref[0]
    h = _modulated(x, n1_ref[...], mod_ref[0:1, :], mod_ref[1:2, :])
    kv = jnp.dot(h.astype(BF16), w_ref[...], preferred_element_type=F32)
    k = _head_rms(kv[:, :KV_WIDTH], g_ref[...], kn_ref[...])
    v = kv[:, KV_WIDTH:]
    kc_ref[0] = k.astype(BF16)
    vc_ref[0] = jnp.concatenate([v, jnp.ones_like(v)], axis=1).astype(BF16)


def _ctx_kv(ctx, cmod, n1w, w_kv, gk, knw):
    b, c, d = ctx.shape
    return pl.pallas_call(
        _ctx_kv_kernel,
        out_shape=(jax.ShapeDtypeStruct((b, c, KV_WIDTH), BF16),
                   jax.ShapeDtypeStruct((b, c, 2 * KV_WIDTH), BF16)),
        grid=(b,),
        in_specs=[pl.BlockSpec((1, c, d), lambda i: (i, 0, 0)),
                  pl.BlockSpec(cmod.shape, lambda i: (0, 0)),
                  pl.BlockSpec((1, d), lambda i: (0, 0)),
                  pl.BlockSpec(w_kv.shape, lambda i: (0, 0)),
                  pl.BlockSpec(gk.shape, lambda i: (0, 0)),
                  pl.BlockSpec((1, KV_WIDTH), lambda i: (0, 0))],
        out_specs=(pl.BlockSpec((1, c, KV_WIDTH), lambda i: (i, 0, 0)),
                   pl.BlockSpec((1, c, 2 * KV_WIDTH), lambda i: (i, 0, 0))),
        compiler_params=_params(("arbitrary",)),
        name="ctx_kv",
    )(ctx, cmod, n1w, w_kv, gk, knw)


def _rope(t, cos, sin, even):
    width = t.shape[1]
    partner = jnp.where(even, pltpu.roll(t, width - 1, axis=1), pltpu.roll(t, 1, axis=1))
    return t * cos + partner * sin


def _inproj_kernel(x_ref, mod_ref, n1_ref, w_ref, gq_ref, qn_ref, kn_ref, cos_ref, sin_ref,
                   q_ref, k_ref, v_ref, u_ref):
    x = x_ref[0]
    h = _modulated(x, n1_ref[...], mod_ref[0, 0:1, :], mod_ref[0, 1:2, :])
    p = jnp.dot(h.astype(BF16), w_ref[...], preferred_element_type=F32)
    gq = gq_ref[...]
    q = _head_rms(p[:, :Q_END], gq, qn_ref[...])
    k = _head_rms(p[:, Q_END:K_END], gq[:KV_WIDTH, :KV_WIDTH], kn_ref[...])
    v = p[:, K_END:V_END]
    cos = cos_ref[...]
    sin = sin_ref[...]
    reps = Q_END // LANES
    cos_q = jnp.concatenate([cos] * reps, axis=1)
    sin_q = jnp.concatenate([sin] * reps, axis=1)
    even_q = (lax.broadcasted_iota(jnp.int32, (1, Q_END), 1) & 1) == 0
    even_k = (lax.broadcasted_iota(jnp.int32, (1, KV_WIDTH), 1) & 1) == 0
    q = _rope(q, cos_q, sin_q, even_q) * (HEAD_DIM ** -0.5)
    k = _rope(k, cos, sin, even_k)
    q_ref[0] = q.astype(BF16)
    k_ref[0] = k.astype(BF16)
    v_ref[0] = jnp.concatenate([v, jnp.ones_like(v)], axis=1).astype(BF16)
    u_ref[0] = p[:, V_END:]


def _inproj(x, mods, n1w, w_in, gq, qnw, knw, cos, sin, tl):
    b, l, d = x.shape
    ncol = w_in.shape[1]
    nu = ncol - V_END
    return pl.pallas_call(
        _inproj_kernel,
        out_shape=(jax.ShapeDtypeStruct((b, l, Q_END), BF16),
                   jax.ShapeDtypeStruct((b, l, KV_WIDTH), BF16),
                   jax.ShapeDtypeStruct((b, l, 2 * KV_WIDTH), BF16),
                   jax.ShapeDtypeStruct((b, l, nu), F32)),
        grid=(l // tl, b),
        in_specs=[pl.BlockSpec((1, tl, d), lambda i, j: (j, i, 0)),
                  pl.BlockSpec((1, 6, d), lambda i, j: (j, 0, 0)),
                  pl.BlockSpec((1, d), lambda i, j: (0, 0)),
                  pl.BlockSpec((d, ncol), lambda i, j: (0, 0)),
                  pl.BlockSpec(gq.shape, lambda i, j: (0, 0)),
                  pl.BlockSpec((1, Q_END), lambda i, j: (0, 0)),
                  pl.BlockSpec((1, KV_WIDTH), lambda i, j: (0, 0)),
                  pl.BlockSpec((tl, LANES), lambda i, j: (i, 0)),
                  pl.BlockSpec((tl, LANES), lambda i, j: (i, 0))],
        out_specs=(pl.BlockSpec((1, tl, Q_END), lambda i, j: (j, i, 0)),
                   pl.BlockSpec((1, tl, KV_WIDTH), lambda i, j: (j, i, 0)),
                   pl.BlockSpec((1, tl, 2 * KV_WIDTH), lambda i, j: (j, i, 0)),
                   pl.BlockSpec((1, tl, nu), lambda i, j: (j, i, 0))),
        compiler_params=_params(("arbitrary", "arbitrary")),
        name="inproj",
    )(x, mods, n1w, w_in, gq, qnw, knw, cos, sin)


def _attn_kernel(q_ref, k_ref, v_ref, wn_ref, o_ref, *, sub):
    tq = q_ref.shape[1]
    low = lax.broadcasted_iota(jnp.int32, (1, LANES), 1) < HEAD_DIM
    nt = (((1,), (1,)), ((), ()))
    kk = k_ref[0]
    vv = v_ref[0]
    for r in range(0, tq, sub):
        outs = []
        for j in range(Q_END // LANES):
            qv = q_ref[0, r:r + sub, LANES * j:LANES * (j + 1)]
            zero = jnp.zeros_like(qv)
            halves = []
            for g in range(N_KV_HEADS):
                qh = jnp.where(low, qv, zero) if g == 0 else jnp.where(low, zero, qv)
                s = lax.dot_general(qh, kk, nt, preferred_element_type=F32)
                p = jnp.exp(s - jnp.max(s, axis=-1, keepdims=True)).astype(BF16)
                pv = jnp.dot(p, vv, preferred_element_type=F32)
                halves.append(pv[:, :LANES] / pv[:, LANES:])
            outs.append(jnp.where(low, halves[0], halves[1]))
        a = jnp.concatenate(outs, axis=1)
        o_ref[0, r:r + sub, :] = _rms(a, wn_ref[...]).astype(BF16)


def _attention(q, k, v, wn, tq, sub):
    b, l, _ = q.shape
    n = k.shape[1]
    return pl.pallas_call(
        functools.partial(_attn_kernel, sub=sub),
        out_shape=jax.ShapeDtypeStruct((b, l, Q_END), BF16),
        grid=(b, l // tq),
        in_specs=[pl.BlockSpec((1, tq, Q_END), lambda i, j: (i, j, 0)),
                  pl.BlockSpec((1, n, KV_WIDTH), lambda i, j: (i, 0, 0)),
                  pl.BlockSpec((1, n, 2 * KV_WIDTH), lambda i, j: (i, 0, 0)),
                  pl.BlockSpec((1, Q_END), lambda i, j: (0, 0))],
        out_specs=pl.BlockSpec((1, tq, Q_END), lambda i, j: (i, j, 0)),
        compiler_params=_params(("arbitrary", "arbitrary")),
        name="attn",
    )(q, k, v, wn)


def _hyena_pre_kernel(u0_ref, u1_ref, u2_ref, cw_ref, cb_ref, g32_ref, gbf_ref, x0_ref):
    l = u0_ref.shape[1]
    row = lax.broadcasted_iota(jnp.int32, (l, LANES), 0)

    def conv(u_ref, gi):
        u = u_ref[0]
        prev = jnp.where(row == 0, 0.0, pltpu.roll(u, 1, axis=0))
        nxt = jnp.where(row == l - 1, 0.0, pltpu.roll(u, l - 1, axis=0))
        w = cw_ref[gi]
        return w[0:1] * prev + w[1:2] * u + w[2:3] * nxt + cb_ref[gi:gi + 1, :]

    x0 = conv(u0_ref, 0)
    x1 = conv(u1_ref, 1)
    v = conv(u2_ref, 2)
    g = v * x1
    g32_ref[0] = g
    gbf_ref[0] = g.astype(BF16)
    x0_ref[0] = x0


def _hyena_pre(u, cw, cb):
    b, l, _ = u.shape
    nblk = HYENA_WIDTH // LANES
    ublk = lambda gi: pl.BlockSpec((1, l, LANES), lambda i, j: (i, 0, gi * nblk + j))
    oblk = pl.BlockSpec((1, l, LANES), lambda i, j: (i, 0, j))
    return pl.pallas_call(
        _hyena_pre_kernel,
        out_shape=(jax.ShapeDtypeStruct((b, l, HYENA_WIDTH), F32),
                   jax.ShapeDtypeStruct((b, l, HYENA_WIDTH), BF16),
                   jax.ShapeDtypeStruct((b, l, HYENA_WIDTH), F32)),
        grid=(b, nblk),
        in_specs=[ublk(0), ublk(1), ublk(2),
                  pl.BlockSpec((3, 3, LANES), lambda i, j: (0, 0, j)),
                  pl.BlockSpec((3, LANES), lambda i, j: (0, j))],
        out_specs=(oblk, oblk, oblk),
        compiler_params=_params(("arbitrary", "arbitrary")),
        name="hyena_pre",
    )(u, u, u, cw, cb)


def _filter_kernel(z_ref, w1_ref, b1_ref, w2_ref, b2_ref, w3_ref, b3_ref, w4_ref, fr_ref, dl_ref, o_ref):
    tl = z_ref.shape[0]
    z = z_ref[...]
    fr = fr_ref[...]
    dot = lambda a, w: jnp.dot(a, w, precision=HIGHEST, preferred_element_type=F32)
    h = jnp.sin(fr * (dot(z, w1_ref[...]) + b1_ref[...]))
    h = jnp.sin(fr * (dot(h, w2_ref[...]) + b2_ref[...]))
    h = jnp.sin(fr * (dot(h, w3_ref[...]) + b3_ref[...]))
    h = dot(h, w4_ref[...])
    t = z[:, 0:1]
    decay = jnp.exp(-t * jnp.abs(dl_ref[...]))
    hf = h[:, :HYENA_WIDTH] * decay
    hb = h[:, HYENA_WIDTH:] * decay
    row = lax.broadcasted_iota(jnp.int32, (tl, HYENA_WIDTH), 0) + pl.program_id(0) * tl
    hb = jnp.where(row == 0, 0.0, hb)
    o_ref[0] = hf + hb
    o_ref[1] = hf - hb


def _hyena_filter(z, w1, b1, w2, b2, w3, b3, w4, freq, deltas, tl):
    l = z.shape[0]
    full = lambda a: pl.BlockSpec(a.shape, lambda i: (0,) * a.ndim)
    return pl.pallas_call(
        _filter_kernel,
        out_shape=jax.ShapeDtypeStruct((2, l, HYENA_WIDTH), F32),
        grid=(l // tl,),
        in_specs=[pl.BlockSpec((tl, z.shape[1]), lambda i: (i, 0)),
                  full(w1), full(b1), full(w2), full(b2), full(w3), full(b3), full(w4), full(freq), full(deltas)],
        out_specs=pl.BlockSpec((2, tl, HYENA_WIDTH), lambda i: (0, i, 0)),
        compiler_params=_params(("arbitrary",)),
        name="hyena_filter",
    )(z, w1, b1, w2, b2, w3, b3, w4, freq, deltas)


def _dft_kernel(f_ref, x_ref, o_ref):
    o_ref[0] = jnp.dot(f_ref[...], x_ref[0].astype(BF16), preferred_element_type=F32)


def _dft(fmat, x, tf):
    nb, l, w = x.shape
    n = fmat.shape[0]
    return pl.pallas_call(
        _dft_kernel,
        out_shape=jax.ShapeDtypeStruct((nb, n, w), F32),
        grid=(n // tf, nb),
        in_specs=[pl.BlockSpec((tf, l), lambda i, j: (i, 0)),
                  pl.BlockSpec((1, l, w), lambda i, j: (j, 0, 0))],
        out_specs=pl.BlockSpec((1, tf, w), lambda i, j: (j, i, 0)),
        compiler_params=_params(("arbitrary", "arbitrary")),
        name="dft_filter",
    )(fmat, x)


def _dft_mul_kernel(fr_ref, fi_ref, x_ref, a_ref, b_ref, d_ref, zr_ref, zi_ref):
    x = x_ref[0]
    xr = jnp.dot(fr_ref[...], x, preferred_element_type=F32)
    xi = jnp.dot(fi_ref[...], x, preferred_element_type=F32)
    bb = b_ref[...]
    zr_ref[0] = (xr * a_ref[...] - xi * bb).astype(BF16)
    zi_ref[0] = (xr * bb + xi * d_ref[...]).astype(BF16)


def _dft_mul(fmat, g, sa, sb, sd, tf):
    b, l, w = g.shape
    nf = l // tf
    spec = pl.BlockSpec((tf, w), lambda i, j: (i, 0))
    return pl.pallas_call(
        _dft_mul_kernel,
        out_shape=(jax.ShapeDtypeStruct((b, l, w), BF16), jax.ShapeDtypeStruct((b, l, w), BF16)),
        grid=(nf, b),
        in_specs=[pl.BlockSpec((tf, l), lambda i, j: (i, 0)),
                  pl.BlockSpec((tf, l), lambda i, j: (i + nf, 0)),
                  pl.BlockSpec((1, l, w), lambda i, j: (j, 0, 0)),
                  spec, spec, spec],
        out_specs=(pl.BlockSpec((1, tf, w), lambda i, j: (j, i, 0)),
                   pl.BlockSpec((1, tf, w), lambda i, j: (j, i, 0))),
        compiler_params=_params(("arbitrary", "arbitrary")),
        name="dft_mul",
    )(fmat, fmat, g, sa, sb, sd)


def _idft_kernel(fr_ref, fi_ref, zr_ref, zi_ref, g_ref, x0_ref, hb_ref, wn_ref, o_ref):
    conv = (jnp.dot(fr_ref[...], zr_ref[0], preferred_element_type=F32)
            + jnp.dot(fi_ref[...], zi_ref[0], preferred_element_type=F32))
    y = (conv + g_ref[0] * hb_ref[...]) * x0_ref[0]
    o_ref[0] = _rms(y, wn_ref[...]).astype(BF16)


def _idft(finv_r, finv_i, zr, zi, g32, x0, hbias, wn, tt):
    b, l, w = zr.shape
    tile = pl.BlockSpec((1, tt, w), lambda i, j: (j, i, 0))
    return pl.pallas_call(
        _idft_kernel,
        out_shape=jax.ShapeDtypeStruct((b, l, w), BF16),
        grid=(l // tt, b),
        in_specs=[pl.BlockSpec((tt, l), lambda i, j: (i, 0)),
                  pl.BlockSpec((tt, l), lambda i, j: (i, 0)),
                  pl.BlockSpec((1, l, w), lambda i, j: (j, 0, 0)),
                  pl.BlockSpec((1, l, w), lambda i, j: (j, 0, 0)),
                  tile, tile,
                  pl.BlockSpec((1, w), lambda i, j: (0, 0)),
                  pl.BlockSpec((1, w), lambda i, j: (0, 0))],
        out_specs=tile,
        compiler_params=_params(("arbitrary", "arbitrary")),
        name="idft",
    )(finv_r, finv_i, zr, zi, g32, x0, hbias, wn)


def _bf16_bits(v):
    return lax.bitcast_convert_type(v.astype(BF16).astype(F32), jnp.uint32)


def _store_row_tiles(ref, val):
    rows, half = val.shape[0], val.shape[1] // 2
    assert half == ROW_SUB * LANES
    for j in range(ROW_SUB):
        lo = _bf16_bits(val[:, LANES * j:LANES * (j + 1)]) >> 16
        hi = _bf16_bits(val[:, half + LANES * j:half + LANES * (j + 1)]) & jnp.uint32(0xFFFF0000)
        ref[pl.ds(j, rows, stride=ROW_SUB), :] = lo | hi


def _load_row_tiles(ref, rows):
    words = [ref[pl.ds(j, rows, stride=ROW_SUB), :] for j in range(ROW_SUB)]
    lo = [lax.bitcast_convert_type(w << 16, F32) for w in words]
    hi = [lax.bitcast_convert_type(w & jnp.uint32(0xFFFF0000), F32) for w in words]
    return jnp.concatenate(lo + hi, axis=1)


def _merge_kernel(a_ref, y_ref, x_ref, mod_ref, wa_ref, wy_ref, n2_ref, sg_ref, su_ref, sd_ref,
                  base_ref, hi_ref, lo_ref, rt_ref):
    m = (jnp.dot(a_ref[0], wa_ref[...], preferred_element_type=F32)
         + jnp.dot(y_ref[0], wy_ref[...], preferred_element_type=F32))
    x1 = x_ref[0] + mod_ref[0, 2:3, :] * m
    h2 = _modulated(x1, n2_ref[...], mod_ref[0, 3:4, :], mod_ref[0, 4:5, :])
    hi = h2.astype(BF16)
    hi_ref[0] = hi
    lo_ref[0] = (h2 - hi.astype(F32)).astype(BF16)
    _store_row_tiles(rt_ref, h2)
    gate = jnp.dot(hi, sg_ref[...], preferred_element_type=F32)
    up = jnp.dot(hi, su_ref[...], preferred_element_type=F32)
    act = (gate * _sigmoid(gate) * up).astype(BF16)
    shared = jnp.dot(act, sd_ref[...], preferred_element_type=F32)
    base_ref[0] = x1 + mod_ref[0, 5:6, :] * shared


def _merge(an, yn, x, mods, wa, wy, n2w, sg, su, sd, tl):
    b, l, d = x.shape
    full = lambda a: pl.BlockSpec(a.shape, lambda i, j: (0,) * a.ndim)
    half = pl.BlockSpec((1, tl, an.shape[2]), lambda i, j: (i, j, 0))
    wide = pl.BlockSpec((1, tl, d), lambda i, j: (i, j, 0))
    per_b = l // tl
    return pl.pallas_call(
        _merge_kernel,
        out_shape=(jax.ShapeDtypeStruct((b, l, d), F32),
                   jax.ShapeDtypeStruct((b, l, d), BF16),
                   jax.ShapeDtypeStruct((b, l, d), BF16),
                   jax.ShapeDtypeStruct((b * l * ROW_SUB, LANES), jnp.uint32)),
        grid=(b, per_b),
        in_specs=[half, half, wide,
                  pl.BlockSpec((1, 6, d), lambda i, j: (i, 0, 0)),
                  full(wa), full(wy), full(n2w), full(sg), full(su), full(sd)],
        out_specs=(wide, wide, wide,
                   pl.BlockSpec((tl * ROW_SUB, LANES), lambda i, j: (i * per_b + j, 0))),
        compiler_params=_params(("arbitrary", "arbitrary")),
        name="merge",
    )(an, yn, x, mods, wa, wy, n2w, sg, su, sd)


def _router_kernel(hi_ref, lo_ref, whi_ref, wlo_ref, bias_ref, tri_ref,
                   idx_ref, wgt_ref, rank_ref, cnt_ref, run_ref):
    tt = hi_ref.shape[0]
    per_group = N_EXPERTS // N_GROUPS

    @pl.when(pl.program_id(0) == 0)
    def _():
        run_ref[...] = jnp.zeros_like(run_ref)

    nt = (((1,), (1,)), ((), ()))
    hi = hi_ref[...]
    whi = whi_ref[...]
    logits = (lax.dot_general(whi, hi, nt, preferred_element_type=F32)
              + lax.dot_general(whi, lo_ref[...], nt, preferred_element_type=F32)
              + lax.dot_general(wlo_ref[...], hi, nt, preferred_element_type=F32))
    scores = _sigmoid(logits)
    biased = scores + bias_ref[...]

    ridx = lax.broadcasted_iota(jnp.int32, (per_group, tt), 0)
    groups = [biased[g * per_group:(g + 1) * per_group, :] for g in range(N_GROUPS)]
    gs = []
    for blk in groups:
        m1 = jnp.max(blk, axis=0, keepdims=True)
        i1 = jnp.min(jnp.where(blk == m1, ridx, per_group), axis=0, keepdims=True)
        m2 = jnp.max(jnp.where(ridx == i1, NEG_INF, blk), axis=0, keepdims=True)
        gs.append(m1 + m2)

    kept = []
    for g in range(N_GROUPS):
        ahead = jnp.zeros((1, tt), F32)
        for o in range(N_GROUPS):
            if o != g:
                wins = (gs[o] >= gs[g]) if o < g else (gs[o] > gs[g])
                ahead = ahead + jnp.where(wins, 1.0, 0.0)
        kept.append(jnp.where(ahead < TOPK_GROUPS, groups[g], NEG_INF))
    cur = jnp.concatenate(kept, axis=0)

    eidx = lax.broadcasted_iota(jnp.int32, cur.shape, 0)
    onehot = jnp.zeros(cur.shape, F32)
    picks = []
    wsel = []
    for _ in range(TOP_K):
        mx = jnp.max(cur, axis=0, keepdims=True)
        first = jnp.min(jnp.where(cur == mx, eidx, N_EXPERTS), axis=0, keepdims=True)
        sel = eidx == first
        picks.append(first)
        wsel.append(jnp.sum(jnp.where(sel, scores, 0.0), axis=0, keepdims=True))
        onehot = jnp.where(sel, 1.0, onehot)
        cur = jnp.where(sel, NEG_INF, cur)
    w = jnp.concatenate(wsel, axis=0)
    w = w / jnp.sum(w, axis=0, keepdims=True) * ROUTE_SCALE
    idx = jnp.concatenate(picks, axis=0)

    oh = onehot.astype(BF16)
    before = jnp.dot(oh, tri_ref[0], preferred_element_type=F32)
    total = jnp.dot(oh, tri_ref[1], preferred_element_type=F32)
    pos = run_ref[...] + before
    ranks = [jnp.sum(jnp.where(eidx == p, pos, 0.0), axis=0, keepdims=True) for p in picks]
    run_ref[...] = run_ref[...] + total

    idx_ref[...] = idx
    wgt_ref[...] = w
    rank_ref[...] = jnp.concatenate(ranks, axis=0).astype(jnp.int32)
    cnt_ref[...] = run_ref[...]


def _router(hi, lo, whi, wlo, bias, tri, tt):
    t, d = hi.shape
    tok = pl.BlockSpec((tt, d), lambda i: (i, 0))
    full = lambda a: pl.BlockSpec(a.shape, lambda i: (0,) * a.ndim)
    out = pl.BlockSpec((TOP_K, tt), lambda i: (0, i))
    return pl.pallas_call(
        _router_kernel,
        out_shape=(jax.ShapeDtypeStruct((TOP_K, t), jnp.int32),
                   jax.ShapeDtypeStruct((TOP_K, t), F32),
                   jax.ShapeDtypeStruct((TOP_K, t), jnp.int32),
                   jax.ShapeDtypeStruct((N_EXPERTS, tt), F32)),
        grid=(t // tt,),
        in_specs=[tok, tok, full(whi), full(wlo), full(bias), full(tri)],
        out_specs=(out, out, out, pl.BlockSpec((N_EXPERTS, tt), lambda i: (0, 0))),
        scratch_shapes=[pltpu.VMEM((N_EXPERTS, tt), F32)],
        compiler_params=_params(("arbitrary",)),
        name="router",
    )(hi, lo, whi, wlo, bias, tri)


def _row_tile(ref, r):
    return ref.at[pl.ds(pl.multiple_of(r * ROW_SUB, ROW_SUB), ROW_SUB), :]


def _dest_kernel(idx_ref, rank_ref, start_ref, dest_ref):
    eidx = lax.broadcasted_iota(jnp.int32, start_ref.shape, 0)
    start = start_ref[...]
    rows = [jnp.sum(jnp.where(eidx == idx_ref[k:k + 1, :], start, 0.0), axis=0, keepdims=True)
            for k in range(TOP_K)]
    dest_ref[...] = jnp.concatenate(rows, axis=0).astype(jnp.int32) + rank_ref[...]


def _dest(idx, rank, start, tt):
    t = idx.shape[1]
    blk = pl.BlockSpec((TOP_K, tt), lambda i: (0, i))
    return pl.pallas_call(
        _dest_kernel,
        out_shape=jax.ShapeDtypeStruct((TOP_K, t), jnp.int32),
        grid=(t // tt,),
        in_specs=[blk, blk, pl.BlockSpec(start.shape, lambda i: (0, 0))],
        out_specs=blk,
        compiler_params=_params(("arbitrary",)),
        name="dest",
    )(idx, rank, start)


def _row_assignment(dest_flat, n_rows, t):
    info = plsc.get_sparse_core_info()
    lanes = info.num_lanes
    workers = info.num_cores * info.num_subcores
    own = n_rows // workers
    chunk = min(8192, t)
    assert n_rows % (workers * lanes) == 0 and t % chunk == 0 and chunk % lanes == 0

    def body(dest_hbm, out_hbm, buf, part):
        wid = lax.axis_index("s") * info.num_cores + lax.axis_index("c")
        base = wid * own
        lane = lax.iota(jnp.int32, lanes)

        @pl.loop(0, own, step=lanes)
        def _(r):
            buf[pl.ds(r, lanes)] = jnp.zeros((lanes,), jnp.int32)

        @pl.loop(0, TOP_K)
        def _(k):
            @pl.loop(0, t, step=chunk)
            def _(c):
                pltpu.sync_copy(dest_hbm.at[pl.ds(k * t + c, chunk)], part)

                @pl.loop(0, chunk, step=lanes)
                def _(j):
                    rel = part[pl.ds(j, lanes)] - base
                    mine = jnp.logical_and(rel >= 0, rel < own)
                    val = (c + j + lane) * TOP_K + k
                    plsc.store_scatter(buf, [jnp.where(mine, rel, 0)], val, mask=mine)

        pltpu.sync_copy(buf, out_hbm.at[pl.ds(base, own)])

    return pl.kernel(
        body,
        out_type=jax.ShapeDtypeStruct((n_rows,), jnp.int32),
        mesh=plsc.VectorSubcoreMesh(core_axis_name="c", subcore_axis_name="s"),
        scratch_types=[pltpu.VMEM((own,), jnp.int32), pltpu.VMEM((chunk,), jnp.int32)],
        compiler_params=pltpu.CompilerParams(needs_layout_passes=False),
        name="row_assignment",
    )(dest_flat)


def _experts_kernel(be_ref, nu_ref, cur_ref, nxt_ref, h_ref, wg_ref, wu_ref, wd_ref, y_ref,
                    xbuf, sem, wgu_bf, wd_bf):
    i = pl.program_id(0)
    n_used = nu_ref[0]
    used = i < n_used
    slot = i % 2
    new_expert = jnp.logical_or(i == 0, be_ref[i] != be_ref[jnp.maximum(i - 1, 0)])

    def row_copy(token, s, r):
        return pltpu.make_async_copy(_row_tile(h_ref, token), _row_tile(xbuf.at[s], r), sem.at[s])

    def issue(a_ref, s):
        for r in range(EXPERT_ROWS):
            row_copy(a_ref[0, 0, r], s, r).start(priority=1)

    def drain(s):
        for r in range(EXPERT_ROWS):
            row_copy(0, s, r).wait()

    @pl.when(i == 0)
    def _():
        issue(cur_ref, 0)

    @pl.when(jnp.logical_and(used, new_expert))
    def _():
        ff = wg_ref.shape[2]
        wgu_bf[:, :ff] = wg_ref[0].astype(BF16)
        wgu_bf[:, ff:] = wu_ref[0].astype(BF16)
        wd_bf[...] = wd_ref[0].astype(BF16)

    @pl.when(used)
    def _():
        drain(slot)
        issue(nxt_ref, 1 - slot)
        ff = wg_ref.shape[2]
        for r in range(0, EXPERT_ROWS, EXPERT_SUB):
            rows = pl.ds(r * ROW_SUB, EXPERT_SUB * ROW_SUB)
            x = _load_row_tiles(xbuf.at[slot, rows, :], EXPERT_SUB).astype(BF16)
            gu = jnp.dot(x, wgu_bf[...], preferred_element_type=F32)
            gate = gu[:, :ff]
            act = (gate * _sigmoid(gate) * gu[:, ff:]).astype(BF16)
            _store_row_tiles(y_ref.at[rows, :], jnp.dot(act, wd_bf[...], preferred_element_type=F32))

    @pl.when(i == n_used - 1)
    def _():
        drain(1 - slot)


def _experts(blk_e, n_used, assign, h_rt, wg, wu, wd):
    d, ff = wg.shape[1], wg.shape[2]
    nblk = assign.shape[0]
    last = lambda i, nu: jnp.minimum(i, nu[0] - 1)
    wsel = lambda i, be, nu: (be[last(i, nu)], 0, 0)
    return pl.pallas_call(
        _experts_kernel,
        out_shape=jax.ShapeDtypeStruct((nblk * EXPERT_ROWS * ROW_SUB, LANES), jnp.uint32),
        grid_spec=pltpu.PrefetchScalarGridSpec(
            num_scalar_prefetch=2,
            grid=(nblk,),
            in_specs=[pl.BlockSpec((1, 1, EXPERT_ROWS), lambda i, be, nu: (last(i, nu), 0, 0),
                                   memory_space=pltpu.SMEM),
                      pl.BlockSpec((1, 1, EXPERT_ROWS), lambda i, be, nu: (last(i + 1, nu), 0, 0),
                                   memory_space=pltpu.SMEM),
                      pl.BlockSpec(memory_space=pl.ANY),
                      pl.BlockSpec((1, d, ff), wsel),
                      pl.BlockSpec((1, d, ff), wsel),
                      pl.BlockSpec((1, ff, d), wsel)],
            out_specs=pl.BlockSpec((EXPERT_ROWS * ROW_SUB, LANES), lambda i, be, nu: (last(i, nu), 0)),
            scratch_shapes=[pltpu.VMEM((2, EXPERT_ROWS * ROW_SUB, LANES), jnp.uint32),
                            pltpu.SemaphoreType.DMA((2,)),
                            pltpu.VMEM((d, 2 * ff), BF16), pltpu.VMEM((ff, d), BF16)]),
        compiler_params=_params(("arbitrary",)),
        name="experts",
    )(blk_e, n_used, assign, assign, h_rt, wg, wu, wd)


def _combine_kernel(dest_ref, next_ref, ys_ref, w_ref, base_ref, mod_ref, fw_ref, o_ref, buf, sem):
    tc = dest_ref.shape[1]
    i = pl.program_id(0)
    slot = i % 2

    def row_copy(row, s, t, k):
        return pltpu.make_async_copy(_row_tile(ys_ref, row), _row_tile(buf.at[s, k], t), sem.at[s])

    def issue(d_ref, s):
        def body(t, carry):
            for k in range(TOP_K):
                row_copy(d_ref[k, t], s, t, k).start()
            return carry

        lax.fori_loop(0, tc, body, 0)

    @pl.when(i == 0)
    def _():
        issue(dest_ref, 0)

    @pl.when(i + 1 < pl.num_programs(0))
    def _():
        issue(next_ref, 1 - slot)

    def drain(t, carry):
        for k in range(TOP_K):
            row_copy(0, slot, t, k).wait()
        return carry

    lax.fori_loop(0, tc, drain, 0)

    w = w_ref[...]
    routed = w[:, 0:1] * _load_row_tiles(buf.at[slot, 0], tc)
    for k in range(1, TOP_K):
        routed = routed + w[:, k:k + 1] * _load_row_tiles(buf.at[slot, k], tc)
    x = base_ref[0] + mod_ref[0, 5:6, :] * routed
    o_ref[0] = _rms(x, fw_ref[...])


def _combine(dest, ys, wt, base, mods, fw, tc):
    b, l, d = base.shape
    per_b = l // tc
    steps = b * per_b
    return pl.pallas_call(
        _combine_kernel,
        out_shape=jax.ShapeDtypeStruct((b, l, d), F32),
        grid=(steps,),
        in_specs=[pl.BlockSpec((TOP_K, tc), lambda i: (0, i), memory_space=pltpu.SMEM),
                  pl.BlockSpec((TOP_K, tc), lambda i: (0, jnp.minimum(i + 1, steps - 1)), memory_space=pltpu.SMEM),
                  pl.BlockSpec(memory_space=pl.ANY),
                  pl.BlockSpec((tc, TOP_K), lambda i: (i, 0)),
                  pl.BlockSpec((1, tc, d), lambda i: (i // per_b, i % per_b, 0)),
                  pl.BlockSpec((1, 6, d), lambda i: (i // per_b, 0, 0)),
                  pl.BlockSpec((1, d), lambda i: (0, 0))],
        out_specs=pl.BlockSpec((1, tc, d), lambda i: (i // per_b, i % per_b, 0)),
        scratch_shapes=[pltpu.VMEM((2, TOP_K, tc * ROW_SUB, LANES), jnp.uint32), pltpu.SemaphoreType.DMA((2,))],
        compiler_params=_params(("arbitrary",)),
        name="combine",
    )(dest, dest, ys, wt, base, mods, fw)


def _rope_tables(l):
    t = jnp.arange(l, dtype=jnp.int32)
    row = (t // GRID_W).astype(F32)
    col = (t % GRID_W).astype(F32)
    n_freq = HEAD_DIM // 4
    inv = ROPE_THETA ** (-jnp.arange(n_freq, dtype=F32) / n_freq)
    ang = jnp.concatenate([row[:, None] * inv, col[:, None] * inv], axis=-1)
    cos = jnp.repeat(jnp.cos(ang), 2, axis=1)
    sin = jnp.repeat(jnp.sin(ang), 2, axis=1)
    sign = jnp.tile(jnp.array([-1.0, 1.0], F32), HEAD_DIM // 2)
    reps = LANES // HEAD_DIM
    return jnp.tile(cos, (1, reps)), jnp.tile(sin * sign, (1, reps))


def _filter_features(l):
    t = jnp.linspace(0.0, 1.0, l, dtype=F32)[:, None]
    bands = (FILTER_EMB - 1) // 2
    w = 2.0 * math.pi * jnp.arange(l, dtype=F32)[:, None] / l
    f = jnp.linspace(1e-4, bands - 1, bands, dtype=F32)[None, :]
    z = jnp.concatenate([t, jnp.cos(f * w), -jnp.sin(f * w)], axis=-1)
    min_decay = math.log(FILTER_TARGET) / FILTER_DECAY_FAST
    max_decay = math.log(FILTER_TARGET) / FILTER_DECAY_SLOW
    deltas = jnp.linspace(min_decay, max_decay, HYENA_WIDTH, dtype=F32)[None, :]
    return jnp.pad(z, ((0, 0), (0, LANES - FILTER_EMB))), deltas


def _dft_matrices(l):
    n = 2 * l
    idx = jnp.arange(l, dtype=jnp.int32)
    r = math.isqrt(l)
    assert r * r == l
    sub = jnp.arange(r, dtype=jnp.int32)
    hi = ((r * sub[:, None] * idx[None, :]) % n).astype(F32) * (2.0 * math.pi / n)
    lo = ((sub[:, None] * idx[None, :]) % n).astype(F32) * (2.0 * math.pi / n)
    ch, sh, cl, sl = jnp.cos(hi)[:, None, :], jnp.sin(hi)[:, None, :], jnp.cos(lo)[None], jnp.sin(lo)[None]
    c = (ch * cl - sh * sl).reshape(l, l)
    s = (sh * cl + ch * sl).reshape(l, l)
    alt = jnp.where(idx % 2 == 0, 1.0, -1.0).astype(F32)
    first = (idx == 0)[:, None]
    fwd = jnp.concatenate([c, jnp.where(first, alt[None, :], -s)], axis=0).astype(BF16)
    firstc = (idx == 0)[None, :]
    inv_r = (jnp.where(firstc, 1.0, 2.0) * c / n).astype(BF16)
    inv_i = (jnp.where(firstc, alt[:, None], -2.0 * s) / n).astype(BF16)
    return fwd, inv_r, inv_i


def _head_perm():
    order = []
    for j in range(N_HEADS // 2):
        order += list(range(j * HEAD_DIM, (j + 1) * HEAD_DIM))
        order += list(range((j + N_HEADS // 2) * HEAD_DIM, (j + 1 + N_HEADS // 2) * HEAD_DIM))
    return jnp.array(order, jnp.int32)


def _pad2(a, rows, cols):
    return jnp.pad(a, ((0, rows - a.shape[0]), (0, cols - a.shape[1])))


def kernel(x, c, ctx, c_ctx, mod_w, mod_b, norm1_w, w_in, q_norm_w, k_norm_w, conv_w, conv_b, filt_w1, filt_b1, filt_w2, filt_b2, filt_w3, filt_b3, filt_w4, filt_freq, hyena_bias, attn_out_norm_w, hyena_out_norm_w, w_out, norm2_w, router_w, router_bias, exp_w_gate, exp_w_up, exp_w_down, sh_w_gate, sh_w_up, sh_w_down, final_norm_w):
    b, l, d = x.shape
    t = b * l
    assert mod_w.shape[0] == 1, "single-layer stack"
    tl = min(512, l)

    cond = jnp.concatenate([c, c_ctx[None, :], jnp.zeros((-(b + 1) % 8, d), F32)], axis=0)
    mod = _adaln(cond, mod_w[0], mod_b[0][None, :])
    mods = mod[:b].reshape(b, 6, d)
    cmod = mod[b].reshape(6, d)

    perm = _head_perm()
    w_in0 = w_in[0]
    w_in_k = jnp.concatenate([w_in0[:, :Q_END][:, perm], w_in0[:, Q_END:]], axis=1).astype(BF16)
    w_kv = w_in0[:, Q_END:V_END].astype(BF16)
    gq = jnp.kron(jnp.eye(N_HEADS, dtype=F32), jnp.full((HEAD_DIM, HEAD_DIM), 1.0 / HEAD_DIM, F32)).astype(BF16)
    qnw = jnp.tile(q_norm_w[0], N_HEADS)[None, :]
    knw = jnp.tile(k_norm_w[0], N_KV_HEADS)[None, :]
    n1w = norm1_w[0][None, :]
    cos, sin = _rope_tables(l)

    kc, vc = _ctx_kv(ctx, cmod, n1w, w_kv, gq[:KV_WIDTH, :KV_WIDTH], knw)
    q, k, v, u = _inproj(x, mods, n1w, w_in_k, gq, qnw, knw, cos, sin, tl)
    k_all = jnp.concatenate([kc, k], axis=1)
    v_all = jnp.concatenate([vc, v], axis=1)
    an = _attention(q, k_all, v_all, attn_out_norm_w[0][perm][None, :], min(256, l), min(256, l))

    cw = conv_w[0].reshape(3, 3, HYENA_WIDTH).transpose(1, 0, 2)
    cb = conv_b[0].reshape(3, HYENA_WIDTH)
    g32, gbf, x0 = _hyena_pre(u, cw, cb)
    z, deltas = _filter_features(l)
    fo = filt_w2.shape[1]
    hsd = _hyena_filter(
        z, _pad2(filt_w1[0], LANES, LANES), _pad2(filt_b1[0][None, :], 1, LANES),
        _pad2(filt_w2[0], LANES, LANES), _pad2(filt_b2[0][None, :], 1, LANES),
        _pad2(filt_w3[0], LANES, LANES), _pad2(filt_b3[0][None, :], 1, LANES),
        _pad2(filt_w4[0], LANES, 2 * HYENA_WIDTH), _pad2(filt_freq[0][None, :], 1, LANES), deltas, tl)
    del fo
    fwd, inv_r, inv_i = _dft_matrices(l)
    spec = _dft(fwd, hsd, tl)
    row0 = (jnp.arange(l) == 0)[:, None]
    sa = spec[0, :l]
    sd = jnp.where(row0, spec[0, l:l + 1], sa)
    sb = jnp.where(row0, 0.0, spec[1, l:])
    zr, zi = _dft_mul(fwd, gbf, sa, sb, sd, tl)
    yn = _idft(inv_r, inv_i, zr, zi, g32, x0, hyena_bias[0][None, :], hyena_out_norm_w[0][None, :], tl)

    w_out0 = w_out[0]
    base, h2hi, h2lo, h2rt = _merge(
        an, yn, x, mods, w_out0[:ATTN_WIDTH][perm].astype(BF16), w_out0[ATTN_WIDTH:].astype(BF16),
        norm2_w[0][None, :], sh_w_gate[0].astype(BF16), sh_w_up[0].astype(BF16), sh_w_down[0].astype(BF16), tl)

    tt = 256
    rwt = router_w[0].T
    rw_hi = rwt.astype(BF16)
    rw_lo = (rwt - rw_hi.astype(F32)).astype(BF16)
    bias = jnp.broadcast_to(router_bias[0][:, None], (N_EXPERTS, tt))
    ti = jnp.arange(tt)
    tri = jnp.stack([(ti[:, None] < ti[None, :]), jnp.ones((tt, tt), bool)]).astype(BF16)
    idx, wgt, rank, cnt = _router(h2hi.reshape(t, d), h2lo.reshape(t, d), rw_hi, rw_lo, bias, tri, tt)

    counts = cnt[:, 0].astype(jnp.int32)
    padded = (counts + EXPERT_ROWS - 1) // EXPERT_ROWS * EXPERT_ROWS
    pad_end = jnp.cumsum(padded)
    pad_start = pad_end - padded
    dest = _dest(idx, rank, jnp.broadcast_to(pad_start.astype(F32)[:, None], (N_EXPERTS, tt)), tt)
    n_rows = (t * TOP_K + N_EXPERTS * (EXPERT_ROWS - 1) + EXPERT_ROWS - 1) // EXPERT_ROWS * EXPERT_ROWS
    nblk = n_rows // EXPERT_ROWS
    blk_row = jnp.arange(nblk, dtype=jnp.int32) * EXPERT_ROWS
    blk_e = jnp.minimum(jnp.sum((pad_end[None, :] <= blk_row[:, None]).astype(jnp.int32), axis=1), N_EXPERTS - 1)
    n_used = (pad_end[-1:] // EXPERT_ROWS).astype(jnp.int32)

    assign = _row_assignment(dest.reshape(TOP_K * t), n_rows, t)
    row_tok = (assign // TOP_K).reshape(nblk, 1, EXPERT_ROWS)
    ys = _experts(blk_e, n_used, row_tok, h2rt, exp_w_gate[0], exp_w_up[0], exp_w_down[0])
    return _combine(dest, ys, wgt.T, base, mods, final_norm_w[None, :], min(128, l))
```

```python
import functools
import math

import jax
import jax.numpy as jnp
from jax import lax
from jax.experimental import pallas as pl
from jax.experimental.pallas import tpu as pltpu
from jax.experimental.pallas import tpu_sc as plsc

F32 = jnp.float32
BF16 = jnp.bfloat16
HIGHEST = lax.Precision.HIGHEST

GRID_W = 64
N_HEADS = 8
N_KV_HEADS = 2
HEAD_DIM = 64
ATTN_WIDTH = N_HEADS * HEAD_DIM
KV_WIDTH = N_KV_HEADS * HEAD_DIM
HYENA_WIDTH = 512
Q_END = ATTN_WIDTH
K_END = Q_END + KV_WIDTH
V_END = K_END + KV_WIDTH
ROPE_THETA = 10000.0
FILTER_EMB = 33
FILTER_DECAY_FAST = 0.3
FILTER_DECAY_SLOW = 1.5
FILTER_TARGET = 1e-2
N_EXPERTS = 256
TOP_K = 8
N_GROUPS = 8
TOPK_GROUPS = 4
ROUTE_SCALE = 2.5
EPS = 1e-6

LANES = 128
ROW_SUB = 4
EXPERT_ROWS = 512
EXPERT_SUB = 256
NEG_INF = float("-inf")


def _params(semantics, vmem_mb=48, **kw):
    return pltpu.CompilerParams(dimension_semantics=semantics, vmem_limit_bytes=vmem_mb * 1024 * 1024, **kw)


def _rms(x, w):
    return x * lax.rsqrt(jnp.mean(x * x, axis=-1, keepdims=True) + EPS) * w


def _sigmoid(x):
    return 1.0 / (1.0 + jnp.exp(-x))


def _adaln_kernel(c_ref, w_ref, b_ref, o_ref):
    c = c_ref[...]
    s = c * _sigmoid(c)
    o_ref[...] = jnp.dot(s, w_ref[...], precision=HIGHEST, preferred_element_type=F32) + b_ref[...]


def _adaln(cond, w, b):
    rows, d = cond.shape
    n = w.shape[1]
    tn = 1536
    return pl.pallas_call(
        _adaln_kernel,
        out_shape=jax.ShapeDtypeStruct((rows, n), F32),
        grid=(n // tn,),
        in_specs=[pl.BlockSpec((rows, d), lambda j: (0, 0)),
                  pl.BlockSpec((d, tn), lambda j: (0, j)),
                  pl.BlockSpec((1, tn), lambda j: (0, j))],
        out_specs=pl.BlockSpec((rows, tn), lambda j: (0, j)),
        compiler_params=_params(("arbitrary",)),
        name="adaln",
    )(cond, w, b)


def _head_rms(t, gmat, w):
    ms = jnp.dot((t * t).astype(BF16), gmat, preferred_element_type=F32)
    return t * lax.rsqrt(ms + EPS) * w


def _modulated(x, norm_w, shift, scale):
    return _rms(x, norm_w) * (1.0 + scale) + shift


def _ctx_kv_kernel(ctx_ref, mod_ref, n1_ref, w_ref, g_ref, kn_ref, kc_ref, vc_ref):
    x = ctx_ref[0]
    h = _modulated(x, n1_ref[...], mod_ref[0:1, :], mod_ref[1:2, :])
    kv = jnp.dot(h.astype(BF16), w_ref[...], preferred_element_type=F32)
    k = _head_rms(kv[:, :KV_WIDTH], g_ref[...], kn_ref[...])
    v = kv[:, KV_WIDTH:]
    kc_ref[0] = k.astype(BF16)
    vc_ref[0] = jnp.concatenate([v, jnp.ones_like(v)], axis=1).astype(BF16)


def _ctx_kv(ctx, cmod, n1w, w_kv, gk, knw):
    b, c, d = ctx.shape
    return pl.pallas_call(
        _ctx_kv_kernel,
        out_shape=(jax.ShapeDtypeStruct((b, c, KV_WIDTH), BF16),
                   jax.ShapeDtypeStruct((b, c, 2 * KV_WIDTH), BF16)),
        grid=(b,),
        in_specs=[pl.BlockSpec((1, c, d), lambda i: (i, 0, 0)),
                  pl.BlockSpec(cmod.shape, lambda i: (0, 0)),
                  pl.BlockSpec((1, d), lambda i: (0, 0)),
                  pl.BlockSpec(w_kv.shape, lambda i: (0, 0)),
                  pl.BlockSpec(gk.shape, lambda i: (0, 0)),
                  pl.BlockSpec((1, KV_WIDTH), lambda i: (0, 0))],
        out_specs=(pl.BlockSpec((1, c, KV_WIDTH), lambda i: (i, 0, 0)),
                   pl.BlockSpec((1, c, 2 * KV_WIDTH), lambda i: (i, 0, 0))),
        compiler_params=_params(("arbitrary",)),
        name="ctx_kv",
    )(ctx, cmod, n1w, w_kv, gk, knw)


def _rope(t, cos, sin, even):
    width = t.shape[1]
    partner = jnp.where(even, pltpu.roll(t, width - 1, axis=1), pltpu.roll(t, 1, axis=1))
    return t * cos + partner * sin


def _inproj_kernel(x_ref, mod_ref, n1_ref, w_ref, gq_ref, qn_ref, kn_ref, cos_ref, sin_ref,
                   q_ref, k_ref, v_ref, u_ref):
    x = x_ref[0]
    h = _modulated(x, n1_ref[...], mod_ref[0, 0:1, :], mod_ref[0, 1:2, :])
    p = jnp.dot(h.astype(BF16), w_ref[...], preferred_element_type=F32)
    gq = gq_ref[...]
    q = _head_rms(p[:, :Q_END], gq, qn_ref[...])
    k = _head_rms(p[:, Q_END:K_END], gq[:KV_WIDTH, :KV_WIDTH], kn_ref[...])
    v = p[:, K_END:V_END]
    cos = cos_ref[...]
    sin = sin_ref[...]
    reps = Q_END // LANES
    cos_q = jnp.concatenate([cos] * reps, axis=1)
    sin_q = jnp.concatenate([sin] * reps, axis=1)
    even_q = (lax.broadcasted_iota(jnp.int32, (1, Q_END), 1) & 1) == 0
    even_k = (lax.broadcasted_iota(jnp.int32, (1, KV_WIDTH), 1) & 1) == 0
    q = _rope(q, cos_q, sin_q, even_q) * (HEAD_DIM ** -0.5)
    k = _rope(k, cos, sin, even_k)
    q_ref[0] = q.astype(BF16)
    k_ref[0] = k.astype(BF16)
    v_ref[0] = jnp.concatenate([v, jnp.ones_like(v)], axis=1).astype(BF16)
    u_ref[0] = p[:, V_END:]


def _inproj(x, mods, n1w, w_in, gq, qnw, knw, cos, sin, tl):
    b, l, d = x.shape
    ncol = w_in.shape[1]
    nu = ncol - V_END
    return pl.pallas_call(
        _inproj_kernel,
        out_shape=(jax.ShapeDtypeStruct((b, l, Q_END), BF16),
                   jax.ShapeDtypeStruct((b, l, KV_WIDTH), BF16),
                   jax.ShapeDtypeStruct((b, l, 2 * KV_WIDTH), BF16),
                   jax.ShapeDtypeStruct((b, l, nu), F32)),
        grid=(l // tl, b),
        in_specs=[pl.BlockSpec((1, tl, d), lambda i, j: (j, i, 0)),
                  pl.BlockSpec((1, 6, d), lambda i, j: (j, 0, 0)),
                  pl.BlockSpec((1, d), lambda i, j: (0, 0)),
                  pl.BlockSpec((d, ncol), lambda i, j: (0, 0)),
                  pl.BlockSpec(gq.shape, lambda i, j: (0, 0)),
                  pl.BlockSpec((1, Q_END), lambda i, j: (0, 0)),
                  pl.BlockSpec((1, KV_WIDTH), lambda i, j: (0, 0)),
                  pl.BlockSpec((tl, LANES), lambda i, j: (i, 0)),
                  pl.BlockSpec((tl, LANES), lambda i, j: (i, 0))],
        out_specs=(pl.BlockSpec((1, tl, Q_END), lambda i, j: (j, i, 0)),
                   pl.BlockSpec((1, tl, KV_WIDTH), lambda i, j: (j, i, 0)),
                   pl.BlockSpec((1, tl, 2 * KV_WIDTH), lambda i, j: (j, i, 0)),
                   pl.BlockSpec((1, tl, nu), lambda i, j: (j, i, 0))),
        compiler_params=_params(("arbitrary", "arbitrary")),
        name="inproj",
    )(x, mods, n1w, w_in, gq, qnw, knw, cos, sin)


def _attn_kernel(q_ref, k_ref, v_ref, wn_ref, o_ref, *, sub):
    tq = q_ref.shape[1]
    low = lax.broadcasted_iota(jnp.int32, (1, LANES), 1) < HEAD_DIM
    nt = (((1,), (1,)), ((), ()))
    kk = k_ref[0]
    vv = v_ref[0]
    for r in range(0, tq, sub):
        outs = []
        for j in range(Q_END // LANES):
            qv = q_ref[0, r:r + sub, LANES * j:LANES * (j + 1)]
            zero = jnp.zeros_like(qv)
            halves = []
            for g in range(N_KV_HEADS):
                qh = jnp.where(low, qv, zero) if g == 0 else jnp.where(low, zero, qv)
                s = lax.dot_general(qh, kk, nt, preferred_element_type=F32)
                p = jnp.exp(s - jnp.max(s, axis=-1, keepdims=True)).astype(BF16)
                pv = jnp.dot(p, vv, preferred_element_type=F32)
                halves.append(pv[:, :LANES] / pv[:, LANES:])
            outs.append(jnp.where(low, halves[0], halves[1]))
        a = jnp.concatenate(outs, axis=1)
        o_ref[0, r:r + sub, :] = _rms(a, wn_ref[...]).astype(BF16)


def _attention(q, k, v, wn, tq, sub):
    b, l, _ = q.shape
    n = k.shape[1]
    return pl.pallas_call(
        functools.partial(_attn_kernel, sub=sub),
        out_shape=jax.ShapeDtypeStruct((b, l, Q_END), BF16),
        grid=(b, l // tq),
        in_specs=[pl.BlockSpec((1, tq, Q_END), lambda i, j: (i, j, 0)),
                  pl.BlockSpec((1, n, KV_WIDTH), lambda i, j: (i, 0, 0)),
                  pl.BlockSpec((1, n, 2 * KV_WIDTH), lambda i, j: (i, 0, 0)),
                  pl.BlockSpec((1, Q_END), lambda i, j: (0, 0))],
        out_specs=pl.BlockSpec((1, tq, Q_END), lambda i, j: (i, j, 0)),
        compiler_params=_params(("arbitrary", "arbitrary")),
        name="attn",
    )(q, k, v, wn)


def _hyena_pre_kernel(u0_ref, u1_ref, u2_ref, cw_ref, cb_ref, g32_ref, gbf_ref, x0_ref):
    l = u0_ref.shape[1]
    row = lax.broadcasted_iota(jnp.int32, (l, LANES), 0)

    def conv(u_ref, gi):
        u = u_ref[0]
        prev = jnp.where(row == 0, 0.0, pltpu.roll(u, 1, axis=0))
        nxt = jnp.where(row == l - 1, 0.0, pltpu.roll(u, l - 1, axis=0))
        w = cw_ref[gi]
        return w[0:1] * prev + w[1:2] * u + w[2:3] * nxt + cb_ref[gi:gi + 1, :]

    x0 = conv(u0_ref, 0)
    x1 = conv(u1_ref, 1)
    v = conv(u2_ref, 2)
    g = v * x1
    g32_ref[0] = g
    gbf_ref[0] = g.astype(BF16)
    x0_ref[0] = x0


def _hyena_pre(u, cw, cb):
    b, l, _ = u.shape
    nblk = HYENA_WIDTH // LANES
    ublk = lambda gi: pl.BlockSpec((1, l, LANES), lambda i, j: (i, 0, gi * nblk + j))
    oblk = pl.BlockSpec((1, l, LANES), lambda i, j: (i, 0, j))
    return pl.pallas_call(
        _hyena_pre_kernel,
        out_shape=(jax.ShapeDtypeStruct((b, l, HYENA_WIDTH), F32),
                   jax.ShapeDtypeStruct((b, l, HYENA_WIDTH), BF16),
                   jax.ShapeDtypeStruct((b, l, HYENA_WIDTH), F32)),
        grid=(b, nblk),
        in_specs=[ublk(0), ublk(1), ublk(2),
                  pl.BlockSpec((3, 3, LANES), lambda i, j: (0, 0, j)),
                  pl.BlockSpec((3, LANES), lambda i, j: (0, j))],
        out_specs=(oblk, oblk, oblk),
        compiler_params=_params(("arbitrary", "arbitrary")),
        name="hyena_pre",
    )(u, u, u, cw, cb)


def _filter_kernel(z_ref, w1_ref, b1_ref, w2_ref, b2_ref, w3_ref, b3_ref, w4_ref, fr_ref, dl_ref, o_ref):
    tl = z_ref.shape[0]
    z = z_ref[...]
    fr = fr_ref[...]
    dot = lambda a, w: jnp.dot(a, w, precision=HIGHEST, preferred_element_type=F32)
    h = jnp.sin(fr * (dot(z, w1_ref[...]) + b1_ref[...]))
    h = jnp.sin(fr * (dot(h, w2_ref[...]) + b2_ref[...]))
    h = jnp.sin(fr * (dot(h, w3_ref[...]) + b3_ref[...]))
    h = dot(h, w4_ref[...])
    t = z[:, 0:1]
    decay = jnp.exp(-t * jnp.abs(dl_ref[...]))
    hf = h[:, :HYENA_WIDTH] * decay
    hb = h[:, HYENA_WIDTH:] * decay
    row = lax.broadcasted_iota(jnp.int32, (tl, HYENA_WIDTH), 0) + pl.program_id(0) * tl
    hb = jnp.where(row == 0, 0.0, hb)
    o_ref[0] = hf + hb
    o_ref[1] = hf - hb


def _hyena_filter(z, w1, b1, w2, b2, w3, b3, w4, freq, deltas, tl):
    l = z.shape[0]
    full = lambda a: pl.BlockSpec(a.shape, lambda i: (0,) * a.ndim)
    return pl.pallas_call(
        _filter_kernel,
        out_shape=jax.ShapeDtypeStruct((2, l, HYENA_WIDTH), F32),
        grid=(l // tl,),
        in_specs=[pl.BlockSpec((tl, z.shape[1]), lambda i: (i, 0)),
                  full(w1), full(b1), full(w2), full(b2), full(w3), full(b3), full(w4), full(freq), full(deltas)],
        out_specs=pl.BlockSpec((2, tl, HYENA_WIDTH), lambda i: (0, i, 0)),
        compiler_params=_params(("arbitrary",)),
        name="hyena_filter",
    )(z, w1, b1, w2, b2, w3, b3, w4, freq, deltas)


def _dft_kernel(f_ref, x_ref, o_ref):
    o_ref[0] = jnp.dot(f_ref[...], x_ref[0].astype(BF16), preferred_element_type=F32)


def _dft(fmat, x, tf):
    nb, l, w = x.shape
    n = fmat.shape[0]
    return pl.pallas_call(
        _dft_kernel,
        out_shape=jax.ShapeDtypeStruct((nb, n, w), F32),
        grid=(n // tf, nb),
        in_specs=[pl.BlockSpec((tf, l), lambda i, j: (i, 0)),
                  pl.BlockSpec((1, l, w), lambda i, j: (j, 0, 0))],
        out_specs=pl.BlockSpec((1, tf, w), lambda i, j: (j, i, 0)),
        compiler_params=_params(("arbitrary", "arbitrary")),
        name="dft_filter",
    )(fmat, x)


def _dft_mul_kernel(fr_ref, fi_ref, x_ref, a_ref, b_ref, d_ref, zr_ref, zi_ref):
    x = x_ref[0]
    xr = jnp.dot(fr_ref[...], x, preferred_element_type=F32)
    xi = jnp.dot(fi_ref[...], x, preferred_element_type=F32)
    bb = b_ref[...]
    zr_ref[0] = (xr * a_ref[...] - xi * bb).astype(BF16)
    zi_ref[0] = (xr * bb + xi * d_ref[...]).astype(BF16)


def _dft_mul(fmat, g, sa, sb, sd, tf):
    b, l, w = g.shape
    nf = l // tf
    spec = pl.BlockSpec((tf, w), lambda i, j: (i, 0))
    return pl.pallas_call(
        _dft_mul_kernel,
        out_shape=(jax.ShapeDtypeStruct((b, l, w), BF16), jax.ShapeDtypeStruct((b, l, w), BF16)),
        grid=(nf, b),
        in_specs=[pl.BlockSpec((tf, l), lambda i, j: (i, 0)),
                  pl.BlockSpec((tf, l), lambda i, j: (i + nf, 0)),
                  pl.BlockSpec((1, l, w), lambda i, j: (j, 0, 0)),
                  spec, spec, spec],
        out_specs=(pl.BlockSpec((1, tf, w), lambda i, j: (j, i, 0)),
                   pl.BlockSpec((1, tf, w), lambda i, j: (j, i, 0))),
        compiler_params=_params(("arbitrary", "arbitrary")),
        name="dft_mul",
    )(fmat, fmat, g, sa, sb, sd)


def _idft_kernel(fr_ref, fi_ref, zr_ref, zi_ref, g_ref, x0_ref, hb_ref, wn_ref, o_ref):
    conv = (jnp.dot(fr_ref[...], zr_ref[0], preferred_element_type=F32)
            + jnp.dot(fi_ref[...], zi_ref[0], preferred_element_type=F32))
    y = (conv + g_ref[0] * hb_ref[...]) * x0_ref[0]
    o_ref[0] = _rms(y, wn_ref[...]).astype(BF16)


def _idft(finv_r, finv_i, zr, zi, g32, x0, hbias, wn, tt):
    b, l, w = zr.shape
    tile = pl.BlockSpec((1, tt, w), lambda i, j: (j, i, 0))
    return pl.pallas_call(
        _idft_kernel,
        out_shape=jax.ShapeDtypeStruct((b, l, w), BF16),
        grid=(l // tt, b),
        in_specs=[pl.BlockSpec((tt, l), lambda i, j: (i, 0)),
                  pl.BlockSpec((tt, l), lambda i, j: (i, 0)),
                  pl.BlockSpec((1, l, w), lambda i, j: (j, 0, 0)),
                  pl.BlockSpec((1, l, w), lambda i, j: (j, 0, 0)),
                  tile, tile,
                  pl.BlockSpec((1, w), lambda i, j: (0, 0)),
                  pl.BlockSpec((1, w), lambda i, j: (0, 0))],
        out_specs=tile,
        compiler_params=_params(("arbitrary", "arbitrary")),
        name="idft",
    )(finv_r, finv_i, zr, zi, g32, x0, hbias, wn)


def _bf16_bits(v):
    return lax.bitcast_convert_type(v.astype(BF16).astype(F32), jnp.uint32)


def _store_row_tiles(ref, val):
    rows, half = val.shape[0], val.shape[1] // 2
    assert half == ROW_SUB * LANES
    for j in range(ROW_SUB):
        lo = _bf16_bits(val[:, LANES * j:LANES * (j + 1)]) >> 16
        hi = _bf16_bits(val[:, half + LANES * j:half + LANES * (j + 1)]) & jnp.uint32(0xFFFF0000)
        ref[pl.ds(j, rows, stride=ROW_SUB), :] = lo | hi


def _load_row_tiles(ref, rows):
    words = [ref[pl.ds(j, rows, stride=ROW_SUB), :] for j in range(ROW_SUB)]
    lo = [lax.bitcast_convert_type(w << 16, F32) for w in words]
    hi = [lax.bitcast_convert_type(w & jnp.uint32(0xFFFF0000), F32) for w in words]
    return jnp.concatenate(lo + hi, axis=1)


def _merge_kernel(a_ref, y_ref, x_ref, mod_ref, wa_ref, wy_ref, n2_ref, sg_ref, su_ref, sd_ref,
                  base_ref, hi_ref, lo_ref, rt_ref):
    m = (jnp.dot(a_ref[0], wa_ref[...], preferred_element_type=F32)
         + jnp.dot(y_ref[0], wy_ref[...], preferred_element_type=F32))
    x1 = x_ref[0] + mod_ref[0, 2:3, :] * m
    h2 = _modulated(x1, n2_ref[...], mod_ref[0, 3:4, :], mod_ref[0, 4:5, :])
    hi = h2.astype(BF16)
    hi_ref[0] = hi
    lo_ref[0] = (h2 - hi.astype(F32)).astype(BF16)
    _store_row_tiles(rt_ref, h2)
    gate = jnp.dot(hi, sg_ref[...], preferred_element_type=F32)
    up = jnp.dot(hi, su_ref[...], preferred_element_type=F32)
    act = (gate * _sigmoid(gate) * up).astype(BF16)
    shared = jnp.dot(act, sd_ref[...], preferred_element_type=F32)
    base_ref[0] = x1 + mod_ref[0, 5:6, :] * shared


def _merge(an, yn, x, mods, wa, wy, n2w, sg, su, sd, tl):
    b, l, d = x.shape
    full = lambda a: pl.BlockSpec(a.shape, lambda i, j: (0,) * a.ndim)
    half = pl.BlockSpec((1, tl, an.shape[2]), lambda i, j: (i, j, 0))
    wide = pl.BlockSpec((1, tl, d), lambda i, j: (i, j, 0))
    per_b = l // tl
    return pl.pallas_call(
        _merge_kernel,
        out_shape=(jax.ShapeDtypeStruct((b, l, d), F32),
                   jax.ShapeDtypeStruct((b, l, d), BF16),
                   jax.ShapeDtypeStruct((b, l, d), BF16),
                   jax.ShapeDtypeStruct((b * l * ROW_SUB, LANES), jnp.uint32)),
        grid=(b, per_b),
        in_specs=[half, half, wide,
                  pl.BlockSpec((1, 6, d), lambda i, j: (i, 0, 0)),
                  full(wa), full(wy), full(n2w), full(sg), full(su), full(sd)],
        out_specs=(wide, wide, wide,
                   pl.BlockSpec((tl * ROW_SUB, LANES), lambda i, j: (i * per_b + j, 0))),
        compiler_params=_params(("arbitrary", "arbitrary")),
        name="merge",
    )(an, yn, x, mods, wa, wy, n2w, sg, su, sd)


def _router_kernel(hi_ref, lo_ref, whi_ref, wlo_ref, bias_ref, tri_ref,
                   idx_ref, wgt_ref, rank_ref, cnt_ref, run_ref):
    tt = hi_ref.shape[0]
    per_group = N_EXPERTS // N_GROUPS

    @pl.when(pl.program_id(0) == 0)
    def _():
        run_ref[...] = jnp.zeros_like(run_ref)

    nt = (((1,), (1,)), ((), ()))
    hi = hi_ref[...]
    whi = whi_ref[...]
    logits = (lax.dot_general(whi, hi, nt, preferred_element_type=F32)
              + lax.dot_general(whi, lo_ref[...], nt, preferred_element_type=F32)
              + lax.dot_general(wlo_ref[...], hi, nt, preferred_element_type=F32))
    scores = _sigmoid(logits)
    biased = scores + bias_ref[...]

    ridx = lax.broadcasted_iota(jnp.int32, (per_group, tt), 0)
    groups = [biased[g * per_group:(g + 1) * per_group, :] for g in range(N_GROUPS)]
    gs = []
    for blk in groups:
        m1 = jnp.max(blk, axis=0, keepdims=True)
        i1 = jnp.min(jnp.where(blk == m1, ridx, per_group), axis=0, keepdims=True)
        m2 = jnp.max(jnp.where(ridx == i1, NEG_INF, blk), axis=0, keepdims=True)
        gs.append(m1 + m2)

    kept = []
    for g in range(N_GROUPS):
        ahead = jnp.zeros((1, tt), F32)
        for o in range(N_GROUPS):
            if o != g:
                wins = (gs[o] >= gs[g]) if o < g else (gs[o] > gs[g])
                ahead = ahead + jnp.where(wins, 1.0, 0.0)
        kept.append(jnp.where(ahead < TOPK_GROUPS, groups[g], NEG_INF))
    cur = jnp.concatenate(kept, axis=0)

    eidx = lax.broadcasted_iota(jnp.int32, cur.shape, 0)
    onehot = jnp.zeros(cur.shape, F32)
    picks = []
    wsel = []
    for _ in range(TOP_K):
        mx = jnp.max(cur, axis=0, keepdims=True)
        first = jnp.min(jnp.where(cur == mx, eidx, N_EXPERTS), axis=0, keepdims=True)
        sel = eidx == first
        picks.append(first)
        wsel.append(jnp.sum(jnp.where(sel, scores, 0.0), axis=0, keepdims=True))
        onehot = jnp.where(sel, 1.0, onehot)
        cur = jnp.where(sel, NEG_INF, cur)
    w = jnp.concatenate(wsel, axis=0)
    w = w / jnp.sum(w, axis=0, keepdims=True) * ROUTE_SCALE
    idx = jnp.concatenate(picks, axis=0)

    oh = onehot.astype(BF16)
    before = jnp.dot(oh, tri_ref[0], preferred_element_type=F32)
    total = jnp.dot(oh, tri_ref[1], preferred_element_type=F32)
    pos = run_ref[...] + before
    ranks = [jnp.sum(jnp.where(eidx == p, pos, 0.0), axis=0, keepdims=True) for p in picks]
    run_ref[...] = run_ref[...] + total

    idx_ref[...] = idx
    wgt_ref[...] = w
    rank_ref[...] = jnp.concatenate(ranks, axis=0).astype(jnp.int32)
    cnt_ref[...] = run_ref[...]


def _router(hi, lo, whi, wlo, bias, tri, tt):
    t, d = hi.shape
    tok = pl.BlockSpec((tt, d), lambda i: (i, 0))
    full = lambda a: pl.BlockSpec(a.shape, lambda i: (0,) * a.ndim)
    out = pl.BlockSpec((TOP_K, tt), lambda i: (0, i))
    return pl.pallas_call(
        _router_kernel,
        out_shape=(jax.ShapeDtypeStruct((TOP_K, t), jnp.int32),
                   jax.ShapeDtypeStruct((TOP_K, t), F32),
                   jax.ShapeDtypeStruct((TOP_K, t), jnp.int32),
                   jax.ShapeDtypeStruct((N_EXPERTS, tt), F32)),
        grid=(t // tt,),
        in_specs=[tok, tok, full(whi), full(wlo), full(bias), full(tri)],
        out_specs=(out, out, out, pl.BlockSpec((N_EXPERTS, tt), lambda i: (0, 0))),
        scratch_shapes=[pltpu.VMEM((N_EXPERTS, tt), F32)],
        compiler_params=_params(("arbitrary",)),
        name="router",
    )(hi, lo, whi, wlo, bias, tri)


def _row_tile(ref, r):
    return ref.at[pl.ds(pl.multiple_of(r * ROW_SUB, ROW_SUB), ROW_SUB), :]


def _dest_kernel(idx_ref, rank_ref, start_ref, dest_ref):
    eidx = lax.broadcasted_iota(jnp.int32, start_ref.shape, 0)
    start = start_ref[...]
    rows = [jnp.sum(jnp.where(eidx == idx_ref[k:k + 1, :], start, 0.0), axis=0, keepdims=True)
            for k in range(TOP_K)]
    dest_ref[...] = jnp.concatenate(rows, axis=0).astype(jnp.int32) + rank_ref[...]


def _dest(idx, rank, start, tt):
    t = idx.shape[1]
    blk = pl.BlockSpec((TOP_K, tt), lambda i: (0, i))
    return pl.pallas_call(
        _dest_kernel,
        out_shape=jax.ShapeDtypeStruct((TOP_K, t), jnp.int32),
        grid=(t // tt,),
        in_specs=[blk, blk, pl.BlockSpec(start.shape, lambda i: (0, 0))],
        out_specs=blk,
        compiler_params=_params(("arbitrary",)),
        name="dest",
    )(idx, rank, start)


def _row_assignment(dest_flat, n_rows, t):
    info = plsc.get_sparse_core_info()
    lanes = info.num_lanes
    workers = info.num_cores * info.num_subcores
    own = n_rows // workers
    chunk = min(8192, t)
    assert n_rows % (workers * lanes) == 0 and t % chunk == 0 and chunk % lanes == 0

    def body(dest_hbm, out_hbm, buf, part):
        wid = lax.axis_index("s") * info.num_cores + lax.axis_index("c")
        base = wid * own
        lane = lax.iota(jnp.int32, lanes)

        @pl.loop(0, own, step=lanes)
        def _(r):
            buf[pl.ds(r, lanes)] = jnp.zeros((lanes,), jnp.int32)

        @pl.loop(0, TOP_K)
        def _(k):
            @pl.loop(0, t, step=chunk)
            def _(c):
                pltpu.sync_copy(dest_hbm.at[pl.ds(k * t + c, chunk)], part)

                @pl.loop(0, chunk, step=lanes)
                def _(j):
                    rel = part[pl.ds(j, lanes)] - base
                    mine = jnp.logical_and(rel >= 0, rel < own)
                    val = (c + j + lane) * TOP_K + k
                    plsc.store_scatter(buf, [jnp.where(mine, rel, 0)], val, mask=mine)

        pltpu.sync_copy(buf, out_hbm.at[pl.ds(base, own)])

    return pl.kernel(
        body,
        out_type=jax.ShapeDtypeStruct((n_rows,), jnp.int32),
        mesh=plsc.VectorSubcoreMesh(core_axis_name="c", subcore_axis_name="s"),
        scratch_types=[pltpu.VMEM((own,), jnp.int32), pltpu.VMEM((chunk,), jnp.int32)],
        compiler_params=pltpu.CompilerParams(needs_layout_passes=False),
        name="row_assignment",
    )(dest_flat)


def _experts_kernel(be_ref, nu_ref, cur_ref, nxt_ref, h_ref, wg_ref, wu_ref, wd_ref, y_ref,
                    xbuf, sem, wgu_bf, wd_bf):
    i = pl.program_id(0)
    n_used = nu_ref[0]
    used = i < n_used
    slot = i % 2
    new_expert = jnp.logical_or(i == 0, be_ref[i] != be_ref[jnp.maximum(i - 1, 0)])

    def row_copy(token, s, r):
        return pltpu.make_async_copy(_row_tile(h_ref, token), _row_tile(xbuf.at[s], r), sem.at[s])

    def issue(a_ref, s):
        for r in list(range(0, EXPERT_ROWS, 2)) + list(range(1, EXPERT_ROWS, 2)):
            row_copy(a_ref[0, 0, r], s, r).start()

    def drain(s):
        for r in range(EXPERT_ROWS):
            row_copy(0, s, r).wait()

    @pl.when(i == 0)
    def _():
        issue(cur_ref, 0)

    @pl.when(jnp.logical_and(used, new_expert))
    def _():
        ff = wg_ref.shape[2]
        wgu_bf[:, :ff] = wg_ref[0].astype(BF16)
        wgu_bf[:, ff:] = wu_ref[0].astype(BF16)
        wd_bf[...] = wd_ref[0].astype(BF16)

    @pl.when(used)
    def _():
        drain(slot)
        issue(nxt_ref, 1 - slot)
        ff = wg_ref.shape[2]
        for r in range(0, EXPERT_ROWS, EXPERT_SUB):
            rows = pl.ds(r * ROW_SUB, EXPERT_SUB * ROW_SUB)
            x = _load_row_tiles(xbuf.at[slot, rows, :], EXPERT_SUB).astype(BF16)
            gu = jnp.dot(x, wgu_bf[...], preferred_element_type=F32)
            gate = gu[:, :ff]
            act = (gate * _sigmoid(gate) * gu[:, ff:]).astype(BF16)
            _store_row_tiles(y_ref.at[rows, :], jnp.dot(act, wd_bf[...], preferred_element_type=F32))

    @pl.when(i == n_used - 1)
    def _():
        drain(1 - slot)


def _experts(blk_e, n_used, assign, h_rt, wg, wu, wd):
    d, ff = wg.shape[1], wg.shape[2]
    nblk = assign.shape[0]
    last = lambda i, nu: jnp.minimum(i, nu[0] - 1)
    wsel = lambda i, be, nu: (be[last(i, nu)], 0, 0)
    return pl.pallas_call(
        _experts_kernel,
        out_shape=jax.ShapeDtypeStruct((nblk * EXPERT_ROWS * ROW_SUB, LANES), jnp.uint32),
        grid_spec=pltpu.PrefetchScalarGridSpec(
            num_scalar_prefetch=2,
            grid=(nblk,),
            in_specs=[pl.BlockSpec((1, 1, EXPERT_ROWS), lambda i, be, nu: (last(i, nu), 0, 0),
                                   memory_space=pltpu.SMEM),
                      pl.BlockSpec((1, 1, EXPERT_ROWS), lambda i, be, nu: (last(i + 1, nu), 0, 0),
                                   memory_space=pltpu.SMEM),
                      pl.BlockSpec(memory_space=pl.ANY),
                      pl.BlockSpec((1, d, ff), wsel),
                      pl.BlockSpec((1, d, ff), wsel),
                      pl.BlockSpec((1, ff, d), wsel)],
            out_specs=pl.BlockSpec((EXPERT_ROWS * ROW_SUB, LANES), lambda i, be, nu: (last(i, nu), 0)),
            scratch_shapes=[pltpu.VMEM((2, EXPERT_ROWS * ROW_SUB, LANES), jnp.uint32),
                            pltpu.SemaphoreType.DMA((2,)),
                            pltpu.VMEM((d, 2 * ff), BF16), pltpu.VMEM((ff, d), BF16)]),
        compiler_params=_params(("arbitrary",)),
        name="experts",
    )(blk_e, n_used, assign, assign, h_rt, wg, wu, wd)


def _combine_kernel(dest_ref, next_ref, ys_ref, w_ref, base_ref, mod_ref, fw_ref, o_ref, buf, sem):
    tc = dest_ref.shape[1]
    i = pl.program_id(0)
    slot = i % 2

    def row_copy(row, s, t, k):
        return pltpu.make_async_copy(_row_tile(ys_ref, row), _row_tile(buf.at[s, k], t), sem.at[s])

    def issue(d_ref, s):
        def body(t, carry):
            for k in range(TOP_K):
                row_copy(d_ref[k, t], s, t, k).start()
            return carry

        lax.fori_loop(0, tc, body, 0)

    @pl.when(i == 0)
    def _():
        issue(dest_ref, 0)

    @pl.when(i + 1 < pl.num_programs(0))
    def _():
        issue(next_ref, 1 - slot)

    def drain(t, carry):
        for k in range(TOP_K):
            row_copy(0, slot, t, k).wait()
        return carry

    lax.fori_loop(0, tc, drain, 0)

    w = w_ref[...]
    routed = w[:, 0:1] * _load_row_tiles(buf.at[slot, 0], tc)
    for k in range(1, TOP_K):
        routed = routed + w[:, k:k + 1] * _load_row_tiles(buf.at[slot, k], tc)
    x = base_ref[0] + mod_ref[0, 5:6, :] * routed
    o_ref[0] = _rms(x, fw_ref[...])


def _combine(dest, ys, wt, base, mods, fw, tc):
    b, l, d = base.shape
    per_b = l // tc
    steps = b * per_b
    return pl.pallas_call(
        _combine_kernel,
        out_shape=jax.ShapeDtypeStruct((b, l, d), F32),
        grid=(steps,),
        in_specs=[pl.BlockSpec((TOP_K, tc), lambda i: (0, i), memory_space=pltpu.SMEM),
                  pl.BlockSpec((TOP_K, tc), lambda i: (0, jnp.minimum(i + 1, steps - 1)), memory_space=pltpu.SMEM),
                  pl.BlockSpec(memory_space=pl.ANY),
                  pl.BlockSpec((tc, TOP_K), lambda i: (i, 0)),
                  pl.BlockSpec((1, tc, d), lambda i: (i // per_b, i % per_b, 0)),
                  pl.BlockSpec((1, 6, d), lambda i: (i // per_b, 0, 0)),
                  pl.BlockSpec((1, d), lambda i: (0, 0))],
        out_specs=pl.BlockSpec((1, tc, d), lambda i: (i // per_b, i % per_b, 0)),
        scratch_shapes=[pltpu.VMEM((2, TOP_K, tc * ROW_SUB, LANES), jnp.uint32), pltpu.SemaphoreType.DMA((2,))],
        compiler_params=_params(("arbitrary",)),
        name="combine",
    )(dest, dest, ys, wt, base, mods, fw)


def _rope_tables(l):
    t = jnp.arange(l, dtype=jnp.int32)
    row = (t // GRID_W).astype(F32)
    col = (t % GRID_W).astype(F32)
    n_freq = HEAD_DIM // 4
    inv = ROPE_THETA ** (-jnp.arange(n_freq, dtype=F32) / n_freq)
    ang = jnp.concatenate([row[:, None] * inv, col[:, None] * inv], axis=-1)
    cos = jnp.repeat(jnp.cos(ang), 2, axis=1)
    sin = jnp.repeat(jnp.sin(ang), 2, axis=1)
    sign = jnp.tile(jnp.array([-1.0, 1.0], F32), HEAD_DIM // 2)
    reps = LANES // HEAD_DIM
    return jnp.tile(cos, (1, reps)), jnp.tile(sin * sign, (1, reps))


def _filter_features(l):
    t = jnp.linspace(0.0, 1.0, l, dtype=F32)[:, None]
    bands = (FILTER_EMB - 1) // 2
    w = 2.0 * math.pi * jnp.arange(l, dtype=F32)[:, None] / l
    f = jnp.linspace(1e-4, bands - 1, bands, dtype=F32)[None, :]
    z = jnp.concatenate([t, jnp.cos(f * w), -jnp.sin(f * w)], axis=-1)
    min_decay = math.log(FILTER_TARGET) / FILTER_DECAY_FAST
    max_decay = math.log(FILTER_TARGET) / FILTER_DECAY_SLOW
    deltas = jnp.linspace(min_decay, max_decay, HYENA_WIDTH, dtype=F32)[None, :]
    return jnp.pad(z, ((0, 0), (0, LANES - FILTER_EMB))), deltas


def _dft_matrices(l):
    n = 2 * l
    idx = jnp.arange(l, dtype=jnp.int32)
    r = math.isqrt(l)
    assert r * r == l
    sub = jnp.arange(r, dtype=jnp.int32)
    hi = ((r * sub[:, None] * idx[None, :]) % n).astype(F32) * (2.0 * math.pi / n)
    lo = ((sub[:, None] * idx[None, :]) % n).astype(F32) * (2.0 * math.pi / n)
    ch, sh, cl, sl = jnp.cos(hi)[:, None, :], jnp.sin(hi)[:, None, :], jnp.cos(lo)[None], jnp.sin(lo)[None]
    c = (ch * cl - sh * sl).reshape(l, l)
    s = (sh * cl + ch * sl).reshape(l, l)
    alt = jnp.where(idx % 2 == 0, 1.0, -1.0).astype(F32)
    first = (idx == 0)[:, None]
    fwd = jnp.concatenate([c, jnp.where(first, alt[None, :], -s)], axis=0).astype(BF16)
    firstc = (idx == 0)[None, :]
    inv_r = (jnp.where(firstc, 1.0, 2.0) * c / n).astype(BF16)
    inv_i = (jnp.where(firstc, alt[:, None], -2.0 * s) / n).astype(BF16)
    return fwd, inv_r, inv_i


def _head_perm():
    order = []
    for j in range(N_HEADS // 2):
        order += list(range(j * HEAD_DIM, (j + 1) * HEAD_DIM))
        order += list(range((j + N_HEADS // 2) * HEAD_DIM, (j + 1 + N_HEADS // 2) * HEAD_DIM))
    return jnp.array(order, jnp.int32)


def _pad2(a, rows, cols):
    return jnp.pad(a, ((0, rows - a.shape[0]), (0, cols - a.shape[1])))


def kernel(x, c, ctx, c_ctx, mod_w, mod_b, norm1_w, w_in, q_norm_w, k_norm_w, conv_w, conv_b, filt_w1, filt_b1, filt_w2, filt_b2, filt_w3, filt_b3, filt_w4, filt_freq, hyena_bias, attn_out_norm_w, hyena_out_norm_w, w_out, norm2_w, router_w, router_bias, exp_w_gate, exp_w_up, exp_w_down, sh_w_gate, sh_w_up, sh_w_down, final_norm_w):
    b, l, d = x.shape
    t = b * l
    assert mod_w.shape[0] == 1, "single-layer stack"
    tl = min(512, l)

    cond = jnp.concatenate([c, c_ctx[None, :], jnp.zeros((-(b + 1) % 8, d), F32)], axis=0)
    mod = _adaln(cond, mod_w[0], mod_b[0][None, :])
    mods = mod[:b].reshape(b, 6, d)
    cmod = mod[b].reshape(6, d)

    perm = _head_perm()
    w_in0 = w_in[0]
    w_in_k = jnp.concatenate([w_in0[:, :Q_END][:, perm], w_in0[:, Q_END:]], axis=1).astype(BF16)
    w_kv = w_in0[:, Q_END:V_END].astype(BF16)
    gq = jnp.kron(jnp.eye(N_HEADS, dtype=F32), jnp.full((HEAD_DIM, HEAD_DIM), 1.0 / HEAD_DIM, F32)).astype(BF16)
    qnw = jnp.tile(q_norm_w[0], N_HEADS)[None, :]
    knw = jnp.tile(k_norm_w[0], N_KV_HEADS)[None, :]
    n1w = norm1_w[0][None, :]
    cos, sin = _rope_tables(l)

    kc, vc = _ctx_kv(ctx, cmod, n1w, w_kv, gq[:KV_WIDTH, :KV_WIDTH], knw)
    q, k, v, u = _inproj(x, mods, n1w, w_in_k, gq, qnw, knw, cos, sin, tl)
    k_all = jnp.concatenate([kc, k], axis=1)
    v_all = jnp.concatenate([vc, v], axis=1)
    an = _attention(q, k_all, v_all, attn_out_norm_w[0][perm][None, :], min(256, l), min(256, l))

    cw = conv_w[0].reshape(3, 3, HYENA_WIDTH).transpose(1, 0, 2)
    cb = conv_b[0].reshape(3, HYENA_WIDTH)
    g32, gbf, x0 = _hyena_pre(u, cw, cb)
    z, deltas = _filter_features(l)
    fo = filt_w2.shape[1]
    hsd = _hyena_filter(
        z, _pad2(filt_w1[0], LANES, LANES), _pad2(filt_b1[0][None, :], 1, LANES),
        _pad2(filt_w2[0], LANES, LANES), _pad2(filt_b2[0][None, :], 1, LANES),
        _pad2(filt_w3[0], LANES, LANES), _pad2(filt_b3[0][None, :], 1, LANES),
        _pad2(filt_w4[0], LANES, 2 * HYENA_WIDTH), _pad2(filt_freq[0][None, :], 1, LANES), deltas, tl)
    del fo
    fwd, inv_r, inv_i = _dft_matrices(l)
    spec = _dft(fwd, hsd, tl)
    row0 = (jnp.arange(l) == 0)[:, None]
    sa = spec[0, :l]
    sd = jnp.where(row0, spec[0, l:l + 1], sa)
    sb = jnp.where(row0, 0.0, spec[1, l:])
    zr, zi = _dft_mul(fwd, gbf, sa, sb, sd, tl)
    yn = _idft(inv_r, inv_i, zr, zi, g32, x0, hyena_bias[0][None, :], hyena_out_norm_w[0][None, :], tl)

    w_out0 = w_out[0]
    base, h2hi, h2lo, h2rt = _merge(
        an, yn, x, mods, w_out0[:ATTN_WIDTH][perm].astype(BF16), w_out0[ATTN_WIDTH:].astype(BF16),
        norm2_w[0][None, :], sh_w_gate[0].astype(BF16), sh_w_up[0].astype(BF16), sh_w_down[0].astype(BF16), tl)

    tt = 256
    rwt = router_w[0].T
    rw_hi = rwt.astype(BF16)
    rw_lo = (rwt - rw_hi.astype(F32)).astype(BF16)
    bias = jnp.broadcast_to(router_bias[0][:, None], (N_EXPERTS, tt))
    ti = jnp.arange(tt)
    tri = jnp.stack([(ti[:, None] < ti[None, :]), jnp.ones((tt, tt), bool)]).astype(BF16)
    idx, wgt, rank, cnt = _router(h2hi.reshape(t, d), h2lo.reshape(t, d), rw_hi, rw_lo, bias, tri, tt)

    counts = cnt[:, 0].astype(jnp.int32)
    padded = (counts + EXPERT_ROWS - 1) // EXPERT_ROWS * EXPERT_ROWS
    pad_end = jnp.cumsum(padded)
    pad_start = pad_end - padded
    dest = _dest(idx, rank, jnp.broadcast_to(pad_start.astype(F32)[:, None], (N_EXPERTS, tt)), tt)
    n_rows = (t * TOP_K + N_EXPERTS * (EXPERT_ROWS - 1) + EXPERT_ROWS - 1) // EXPERT_ROWS * EXPERT_ROWS
    nblk = n_rows // EXPERT_ROWS
    blk_row = jnp.arange(nblk, dtype=jnp.int32) * EXPERT_ROWS
    blk_e = jnp.minimum(jnp.sum((pad_end[None, :] <= blk_row[:, None]).astype(jnp.int32), axis=1), N_EXPERTS - 1)
    n_used = (pad_end[-1:] // EXPERT_ROWS).astype(jnp.int32)

    assign = _row_assignment(dest.reshape(TOP_K * t), n_rows, t)
    row_tok = (assign // TOP_K).reshape(nblk, 1, EXPERT_ROWS)
    ys = _experts(blk_e, n_used, row_tok, h2rt, exp_w_gate[0], exp_w_up[0], exp_w_down[0])
    return _combine(dest, ys, wgt.T, base, mods, final_norm_w[None, :], min(128, l))
```

```python
import functools
import math

import jax
import jax.numpy as jnp
from jax import lax
from jax.experimental import pallas as pl
from jax.experimental.pallas import tpu as pltpu

F32 = jnp.float32
BF16 = jnp.bfloat16
HIGHEST = lax.Precision.HIGHEST

GRID_W = 64
N_HEADS = 8
N_KV_HEADS = 2
HEAD_DIM = 64
ATTN_WIDTH = N_HEADS * HEAD_DIM
KV_WIDTH = N_KV_HEADS * HEAD_DIM
HYENA_WIDTH = 512
Q_END = ATTN_WIDTH
K_END = Q_END + KV_WIDTH
V_END = K_END + KV_WIDTH
ROPE_THETA = 10000.0
FILTER_EMB = 33
FILTER_DECAY_FAST = 0.3
FILTER_DECAY_SLOW = 1.5
FILTER_TARGET = 1e-2
N_EXPERTS = 256
TOP_K = 8
N_GROUPS = 8
TOPK_GROUPS = 4
ROUTE_SCALE = 2.5
EPS = 1e-6

LANES = 128
ROW_SUB = 4
EXPERT_ROWS = 512
EXPERT_SUB = 256
NEG_INF = float("-inf")


def _params(semantics, vmem_mb=48):
    return pltpu.CompilerParams(dimension_semantics=semantics, vmem_limit_bytes=vmem_mb * 1024 * 1024)


def _rms(x, w):
    return x * lax.rsqrt(jnp.mean(x * x, axis=-1, keepdims=True) + EPS) * w


def _sigmoid(x):
    return 1.0 / (1.0 + jnp.exp(-x))


def _adaln_kernel(c_ref, w_ref, b_ref, o_ref):
    c = c_ref[...]
    s = c * _sigmoid(c)
    o_ref[...] = jnp.dot(s, w_ref[...], precision=HIGHEST, preferred_element_type=F32) + b_ref[...]


def _adaln(cond, w, b):
    rows, d = cond.shape
    n = w.shape[1]
    tn = 1536
    return pl.pallas_call(
        _adaln_kernel,
        out_shape=jax.ShapeDtypeStruct((rows, n), F32),
        grid=(n // tn,),
        in_specs=[pl.BlockSpec((rows, d), lambda j: (0, 0)),
                  pl.BlockSpec((d, tn), lambda j: (0, j)),
                  pl.BlockSpec((1, tn), lambda j: (0, j))],
        out_specs=pl.BlockSpec((rows, tn), lambda j: (0, j)),
        compiler_params=_params(("arbitrary",)),
        name="adaln",
    )(cond, w, b)


def _head_rms(t, gmat, w):
    ms = jnp.dot((t * t).astype(BF16), gmat, preferred_element_type=F32)
    return t * lax.rsqrt(ms + EPS) * w


def _modulated(x, norm_w, shift, scale):
    return _rms(x, norm_w) * (1.0 + scale) + shift


def _ctx_kv_kernel(ctx_ref, mod_ref, n1_ref, w_ref, g_ref, kn_ref, kc_ref, vc_ref):
    x = ctx_ref[0]
    h = _modulated(x, n1_ref[...], mod_ref[0:1, :], mod_ref[1:2, :])
    kv = jnp.dot(h.astype(BF16), w_ref[...], preferred_element_type=F32)
    k = _head_rms(kv[:, :KV_WIDTH], g_ref[...], kn_ref[...])
    v = kv[:, KV_WIDTH:]
    kc_ref[0] = k.astype(BF16)
    vc_ref[0] = jnp.concatenate([v, jnp.ones_like(v)], axis=1).astype(BF16)


def _ctx_kv(ctx, cmod, n1w, w_kv, gk, knw):
    b, c, d = ctx.shape
    return pl.pallas_call(
        _ctx_kv_kernel,
        out_shape=(jax.ShapeDtypeStruct((b, c, KV_WIDTH), BF16),
                   jax.ShapeDtypeStruct((b, c, 2 * KV_WIDTH), BF16)),
        grid=(b,),
        in_specs=[pl.BlockSpec((1, c, d), lambda i: (i, 0, 0)),
                  pl.BlockSpec(cmod.shape, lambda i: (0, 0)),
                  pl.BlockSpec((1, d), lambda i: (0, 0)),
                  pl.BlockSpec(w_kv.shape, lambda i: (0, 0)),
                  pl.BlockSpec(gk.shape, lambda i: (0, 0)),
                  pl.BlockSpec((1, KV_WIDTH), lambda i: (0, 0))],
        out_specs=(pl.BlockSpec((1, c, KV_WIDTH), lambda i: (i, 0, 0)),
                   pl.BlockSpec((1, c, 2 * KV_WIDTH), lambda i: (i, 0, 0))),
        compiler_params=_params(("arbitrary",)),
        name="ctx_kv",
    )(ctx, cmod, n1w, w_kv, gk, knw)


def _rope(t, cos, sin, even):
    width = t.shape[1]
    partner = jnp.where(even, pltpu.roll(t, width - 1, axis=1), pltpu.roll(t, 1, axis=1))
    return t * cos + partner * sin


def _inproj_kernel(x_ref, mod_ref, n1_ref, w_ref, gq_ref, qn_ref, kn_ref, cos_ref, sin_ref,
                   q_ref, k_ref, v_ref, u_ref):
    x = x_ref[0]
    h = _modulated(x, n1_ref[...], mod_ref[0, 0:1, :], mod_ref[0, 1:2, :])
    p = jnp.dot(h.astype(BF16), w_ref[...], preferred_element_type=F32)
    gq = gq_ref[...]
    q = _head_rms(p[:, :Q_END], gq, qn_ref[...])
    k = _head_rms(p[:, Q_END:K_END], gq[:KV_WIDTH, :KV_WIDTH], kn_ref[...])
    v = p[:, K_END:V_END]
    cos = cos_ref[...]
    sin = sin_ref[...]
    reps = Q_END // LANES
    cos_q = jnp.concatenate([cos] * reps, axis=1)
    sin_q = jnp.concatenate([sin] * reps, axis=1)
    even_q = (lax.broadcasted_iota(jnp.int32, (1, Q_END), 1) & 1) == 0
    even_k = (lax.broadcasted_iota(jnp.int32, (1, KV_WIDTH), 1) & 1) == 0
    q = _rope(q, cos_q, sin_q, even_q) * (HEAD_DIM ** -0.5)
    k = _rope(k, cos, sin, even_k)
    q_ref[0] = q.astype(BF16)
    k_ref[0] = k.astype(BF16)
    v_ref[0] = jnp.concatenate([v, jnp.ones_like(v)], axis=1).astype(BF16)
    u_ref[0] = p[:, V_END:].astype(BF16)


def _inproj(x, mods, n1w, w_in, gq, qnw, knw, cos, sin, tl):
    b, l, d = x.shape
    ncol = w_in.shape[1]
    nu = ncol - V_END
    return pl.pallas_call(
        _inproj_kernel,
        out_shape=(jax.ShapeDtypeStruct((b, l, Q_END), BF16),
                   jax.ShapeDtypeStruct((b, l, KV_WIDTH), BF16),
                   jax.ShapeDtypeStruct((b, l, 2 * KV_WIDTH), BF16),
                   jax.ShapeDtypeStruct((b, l, nu), BF16)),
        grid=(l // tl, b),
        in_specs=[pl.BlockSpec((1, tl, d), lambda i, j: (j, i, 0)),
                  pl.BlockSpec((1, 6, d), lambda i, j: (j, 0, 0)),
                  pl.BlockSpec((1, d), lambda i, j: (0, 0)),
                  pl.BlockSpec((d, ncol), lambda i, j: (0, 0)),
                  pl.BlockSpec(gq.shape, lambda i, j: (0, 0)),
                  pl.BlockSpec((1, Q_END), lambda i, j: (0, 0)),
                  pl.BlockSpec((1, KV_WIDTH), lambda i, j: (0, 0)),
                  pl.BlockSpec((tl, LANES), lambda i, j: (i, 0)),
                  pl.BlockSpec((tl, LANES), lambda i, j: (i, 0))],
        out_specs=(pl.BlockSpec((1, tl, Q_END), lambda i, j: (j, i, 0)),
                   pl.BlockSpec((1, tl, KV_WIDTH), lambda i, j: (j, i, 0)),
                   pl.BlockSpec((1, tl, 2 * KV_WIDTH), lambda i, j: (j, i, 0)),
                   pl.BlockSpec((1, tl, nu), lambda i, j: (j, i, 0))),
        compiler_params=_params(("arbitrary", "arbitrary")),
        name="inproj",
    )(x, mods, n1w, w_in, gq, qnw, knw, cos, sin)


def _attn_kernel(q_ref, k_ref, v_ref, wn_ref, o_ref, *, sub):
    tq = q_ref.shape[1]
    low = lax.broadcasted_iota(jnp.int32, (1, LANES), 1) < HEAD_DIM
    nt = (((1,), (1,)), ((), ()))
    kk = k_ref[0]
    vv = v_ref[0]
    for r in range(0, tq, sub):
        outs = []
        for j in range(Q_END // LANES):
            qv = q_ref[0, r:r + sub, LANES * j:LANES * (j + 1)]
            zero = jnp.zeros_like(qv)
            halves = []
            for g in range(N_KV_HEADS):
                qh = jnp.where(low, qv, zero) if g == 0 else jnp.where(low, zero, qv)
                s = lax.dot_general(qh, kk, nt, preferred_element_type=F32)
                p = jnp.exp(s - jnp.max(s, axis=-1, keepdims=True)).astype(BF16)
                pv = jnp.dot(p, vv, preferred_element_type=F32)
                halves.append(pv[:, :LANES] / pv[:, LANES:])
            outs.append(jnp.where(low, halves[0], halves[1]))
        a = jnp.concatenate(outs, axis=1)
        o_ref[0, r:r + sub, :] = _rms(a, wn_ref[...]).astype(BF16)


def _attention(q, k, v, wn, tq, sub):
    b, l, _ = q.shape
    n = k.shape[1]
    return pl.pallas_call(
        functools.partial(_attn_kernel, sub=sub),
        out_shape=jax.ShapeDtypeStruct((b, l, Q_END), BF16),
        grid=(b, l // tq),
        in_specs=[pl.BlockSpec((1, tq, Q_END), lambda i, j: (i, j, 0)),
                  pl.BlockSpec((1, n, KV_WIDTH), lambda i, j: (i, 0, 0)),
                  pl.BlockSpec((1, n, 2 * KV_WIDTH), lambda i, j: (i, 0, 0)),
                  pl.BlockSpec((1, Q_END), lambda i, j: (0, 0))],
        out_specs=pl.BlockSpec((1, tq, Q_END), lambda i, j: (i, j, 0)),
        compiler_params=_params(("arbitrary", "arbitrary")),
        name="attn",
    )(q, k, v, wn)


def _hyena_pre_kernel(u0_ref, u1_ref, u2_ref, cw_ref, cb_ref, g_ref, x0_ref):
    l = u0_ref.shape[1]
    row = lax.broadcasted_iota(jnp.int32, (l, LANES), 0)

    def conv(u_ref, gi):
        u = u_ref[0].astype(F32)
        prev = jnp.where(row == 0, 0.0, pltpu.roll(u, 1, axis=0))
        nxt = jnp.where(row == l - 1, 0.0, pltpu.roll(u, l - 1, axis=0))
        w = cw_ref[gi]
        return w[0:1] * prev + w[1:2] * u + w[2:3] * nxt + cb_ref[gi:gi + 1, :]

    x0 = conv(u0_ref, 0)
    x1 = conv(u1_ref, 1)
    v = conv(u2_ref, 2)
    g = v * x1
    g_ref[0] = g.astype(BF16)
    x0_ref[0] = x0.astype(BF16)


def _hyena_pre(u, cw, cb):
    b, l, _ = u.shape
    nblk = HYENA_WIDTH // LANES
    ublk = lambda gi: pl.BlockSpec((1, l, LANES), lambda i, j: (i, 0, gi * nblk + j))
    oblk = pl.BlockSpec((1, l, LANES), lambda i, j: (i, 0, j))
    return pl.pallas_call(
        _hyena_pre_kernel,
        out_shape=(jax.ShapeDtypeStruct((b, l, HYENA_WIDTH), BF16),
                   jax.ShapeDtypeStruct((b, l, HYENA_WIDTH), BF16)),
        grid=(b, nblk),
        in_specs=[ublk(0), ublk(1), ublk(2),
                  pl.BlockSpec((3, 3, LANES), lambda i, j: (0, 0, j)),
                  pl.BlockSpec((3, LANES), lambda i, j: (0, j))],
        out_specs=(oblk, oblk),
        compiler_params=_params(("arbitrary", "arbitrary")),
        name="hyena_pre",
    )(u, u, u, cw, cb)


def _filter_kernel(z_ref, w1_ref, b1_ref, w2_ref, b2_ref, w3_ref, b3_ref, w4_ref, fr_ref, dl_ref, o_ref):
    tl = z_ref.shape[0]
    z = z_ref[...]
    fr = fr_ref[...]
    dot = lambda a, w: jnp.dot(a, w, precision=HIGHEST, preferred_element_type=F32)
    h = jnp.sin(fr * (dot(z, w1_ref[...]) + b1_ref[...]))
    h = jnp.sin(fr * (dot(h, w2_ref[...]) + b2_ref[...]))
    h = jnp.sin(fr * (dot(h, w3_ref[...]) + b3_ref[...]))
    h = dot(h, w4_ref[...])
    t = z[:, 0:1]
    decay = jnp.exp(-t * jnp.abs(dl_ref[...]))
    hf = h[:, :HYENA_WIDTH] * decay
    hb = h[:, HYENA_WIDTH:] * decay
    row = lax.broadcasted_iota(jnp.int32, (tl, HYENA_WIDTH), 0) + pl.program_id(0) * tl
    hb = jnp.where(row == 0, 0.0, hb)
    o_ref[0] = hf + hb
    o_ref[1] = hf - hb


def _hyena_filter(z, w1, b1, w2, b2, w3, b3, w4, freq, deltas, tl):
    l = z.shape[0]
    full = lambda a: pl.BlockSpec(a.shape, lambda i: (0,) * a.ndim)
    return pl.pallas_call(
        _filter_kernel,
        out_shape=jax.ShapeDtypeStruct((2, l, HYENA_WIDTH), F32),
        grid=(l // tl,),
        in_specs=[pl.BlockSpec((tl, z.shape[1]), lambda i: (i, 0)),
                  full(w1), full(b1), full(w2), full(b2), full(w3), full(b3), full(w4), full(freq), full(deltas)],
        out_specs=pl.BlockSpec((2, tl, HYENA_WIDTH), lambda i: (0, i, 0)),
        compiler_params=_params(("arbitrary",)),
        name="hyena_filter",
    )(z, w1, b1, w2, b2, w3, b3, w4, freq, deltas)


def _dft_kernel(f_ref, x_ref, o_ref):
    o_ref[0] = jnp.dot(f_ref[...], x_ref[0].astype(BF16), preferred_element_type=F32)


def _dft(fmat, x, tf):
    nb, l, w = x.shape
    n = fmat.shape[0]
    return pl.pallas_call(
        _dft_kernel,
        out_shape=jax.ShapeDtypeStruct((nb, n, w), F32),
        grid=(n // tf, nb),
        in_specs=[pl.BlockSpec((tf, l), lambda i, j: (i, 0)),
                  pl.BlockSpec((1, l, w), lambda i, j: (j, 0, 0))],
        out_specs=pl.BlockSpec((1, tf, w), lambda i, j: (j, i, 0)),
        compiler_params=_params(("arbitrary", "arbitrary")),
        name="dft_filter",
    )(fmat, x)


def _dft_mul_kernel(fr_ref, fi_ref, x_ref, a_ref, b_ref, d_ref, zr_ref, zi_ref):
    x = x_ref[0]
    xr = jnp.dot(fr_ref[...], x, preferred_element_type=F32)
    xi = jnp.dot(fi_ref[...], x, preferred_element_type=F32)
    bb = b_ref[...]
    zr_ref[0] = (xr * a_ref[...] - xi * bb).astype(BF16)
    zi_ref[0] = (xr * bb + xi * d_ref[...]).astype(BF16)


def _dft_mul(fmat, g, sa, sb, sd, tf):
    b, l, w = g.shape
    nf = l // tf
    spec = pl.BlockSpec((tf, w), lambda i, j: (i, 0))
    return pl.pallas_call(
        _dft_mul_kernel,
        out_shape=(jax.ShapeDtypeStruct((b, l, w), BF16), jax.ShapeDtypeStruct((b, l, w), BF16)),
        grid=(nf, b),
        in_specs=[pl.BlockSpec((tf, l), lambda i, j: (i, 0)),
                  pl.BlockSpec((tf, l), lambda i, j: (i + nf, 0)),
                  pl.BlockSpec((1, l, w), lambda i, j: (j, 0, 0)),
                  spec, spec, spec],
        out_specs=(pl.BlockSpec((1, tf, w), lambda i, j: (j, i, 0)),
                   pl.BlockSpec((1, tf, w), lambda i, j: (j, i, 0))),
        compiler_params=_params(("arbitrary", "arbitrary")),
        name="dft_mul",
    )(fmat, fmat, g, sa, sb, sd)


def _idft_kernel(fr_ref, fi_ref, zr_ref, zi_ref, g_ref, x0_ref, hb_ref, wn_ref, o_ref):
    conv = (jnp.dot(fr_ref[...], zr_ref[0], preferred_element_type=F32)
            + jnp.dot(fi_ref[...], zi_ref[0], preferred_element_type=F32))
    y = (conv + g_ref[0].astype(F32) * hb_ref[...]) * x0_ref[0].astype(F32)
    o_ref[0] = _rms(y, wn_ref[...]).astype(BF16)


def _idft(finv_r, finv_i, zr, zi, g, x0, hbias, wn, tt):
    b, l, w = zr.shape
    tile = pl.BlockSpec((1, tt, w), lambda i, j: (j, i, 0))
    return pl.pallas_call(
        _idft_kernel,
        out_shape=jax.ShapeDtypeStruct((b, l, w), BF16),
        grid=(l // tt, b),
        in_specs=[pl.BlockSpec((tt, l), lambda i, j: (i, 0)),
                  pl.BlockSpec((tt, l), lambda i, j: (i, 0)),
                  pl.BlockSpec((1, l, w), lambda i, j: (j, 0, 0)),
                  pl.BlockSpec((1, l, w), lambda i, j: (j, 0, 0)),
                  tile, tile,
                  pl.BlockSpec((1, w), lambda i, j: (0, 0)),
                  pl.BlockSpec((1, w), lambda i, j: (0, 0))],
        out_specs=tile,
        compiler_params=_params(("arbitrary", "arbitrary")),
        name="idft",
    )(finv_r, finv_i, zr, zi, g, x0, hbias, wn)


def _bf16_bits(v):
    return lax.bitcast_convert_type(v.astype(BF16).astype(F32), jnp.uint32)


def _store_row_tiles(ref, val):
    rows, half = val.shape[0], val.shape[1] // 2
    assert half == ROW_SUB * LANES
    for j in range(ROW_SUB):
        lo = _bf16_bits(val[:, LANES * j:LANES * (j + 1)]) >> 16
        hi = _bf16_bits(val[:, half + LANES * j:half + LANES * (j + 1)]) & jnp.uint32(0xFFFF0000)
        ref[pl.ds(j, rows, stride=ROW_SUB), :] = lo | hi


def _load_row_tiles(ref, rows):
    words = [ref[pl.ds(j, rows, stride=ROW_SUB), :] for j in range(ROW_SUB)]
    lo = [lax.bitcast_convert_type(w << 16, F32) for w in words]
    hi = [lax.bitcast_convert_type(w & jnp.uint32(0xFFFF0000), F32) for w in words]
    return jnp.concatenate(lo + hi, axis=1)


def _merge_kernel(a_ref, y_ref, x_ref, mod_ref, wa_ref, wy_ref, n2_ref, sg_ref, su_ref, sd_ref,
                  base_ref, hi_ref, lo_ref, rt_ref):
    m = (jnp.dot(a_ref[0], wa_ref[...], preferred_element_type=F32)
         + jnp.dot(y_ref[0], wy_ref[...], preferred_element_type=F32))
    x1 = x_ref[0] + mod_ref[0, 2:3, :] * m
    h2 = _modulated(x1, n2_ref[...], mod_ref[0, 3:4, :], mod_ref[0, 4:5, :])
    hi = h2.astype(BF16)
    hi_ref[0] = hi
    lo_ref[0] = (h2 - hi.astype(F32)).astype(BF16)
    _store_row_tiles(rt_ref, h2)
    gate = jnp.dot(hi, sg_ref[...], preferred_element_type=F32)
    up = jnp.dot(hi, su_ref[...], preferred_element_type=F32)
    act = (gate * _sigmoid(gate) * up).astype(BF16)
    shared = jnp.dot(act, sd_ref[...], preferred_element_type=F32)
    base_ref[0] = x1 + mod_ref[0, 5:6, :] * shared


def _merge(an, yn, x, mods, wa, wy, n2w, sg, su, sd, tl):
    b, l, d = x.shape
    full = lambda a: pl.BlockSpec(a.shape, lambda i, j: (0,) * a.ndim)
    half = pl.BlockSpec((1, tl, an.shape[2]), lambda i, j: (i, j, 0))
    wide = pl.BlockSpec((1, tl, d), lambda i, j: (i, j, 0))
    per_b = l // tl
    return pl.pallas_call(
        _merge_kernel,
        out_shape=(jax.ShapeDtypeStruct((b, l, d), F32),
                   jax.ShapeDtypeStruct((b, l, d), BF16),
                   jax.ShapeDtypeStruct((b, l, d), BF16),
                   jax.ShapeDtypeStruct((b * l * ROW_SUB, LANES), jnp.uint32)),
        grid=(b, per_b),
        in_specs=[half, half, wide,
                  pl.BlockSpec((1, 6, d), lambda i, j: (i, 0, 0)),
                  full(wa), full(wy), full(n2w), full(sg), full(su), full(sd)],
        out_specs=(wide, wide, wide,
                   pl.BlockSpec((tl * ROW_SUB, LANES), lambda i, j: (i * per_b + j, 0))),
        compiler_params=_params(("arbitrary", "arbitrary")),
        name="merge",
    )(an, yn, x, mods, wa, wy, n2w, sg, su, sd)


def _router_kernel(hi_ref, lo_ref, whi_ref, wlo_ref, bias_ref, tri_ref,
                   idx_ref, wgt_ref, pos_ref, cnt_ref, run_ref):
    tt = hi_ref.shape[0]
    per_group = N_EXPERTS // N_GROUPS

    @pl.when(pl.program_id(0) == 0)
    def _():
        run_ref[...] = jnp.zeros_like(run_ref)

    nt = (((1,), (1,)), ((), ()))
    hi = hi_ref[...]
    whi = whi_ref[...]
    logits = (lax.dot_general(whi, hi, nt, preferred_element_type=F32)
              + lax.dot_general(whi, lo_ref[...], nt, preferred_element_type=F32)
              + lax.dot_general(wlo_ref[...], hi, nt, preferred_element_type=F32))
    scores = _sigmoid(logits)
    biased = scores + bias_ref[...]

    ridx = lax.broadcasted_iota(jnp.int32, (per_group, tt), 0)
    groups = [biased[g * per_group:(g + 1) * per_group, :] for g in range(N_GROUPS)]
    gs = []
    for blk in groups:
        m1 = jnp.max(blk, axis=0, keepdims=True)
        i1 = jnp.min(jnp.where(blk == m1, ridx, per_group), axis=0, keepdims=True)
        m2 = jnp.max(jnp.where(ridx == i1, NEG_INF, blk), axis=0, keepdims=True)
        gs.append(m1 + m2)

    kept = []
    for g in range(N_GROUPS):
        ahead = jnp.zeros((1, tt), F32)
        for o in range(N_GROUPS):
            if o != g:
                wins = (gs[o] >= gs[g]) if o < g else (gs[o] > gs[g])
                ahead = ahead + jnp.where(wins, 1.0, 0.0)
        kept.append(jnp.where(ahead < TOPK_GROUPS, groups[g], NEG_INF))
    cur = jnp.concatenate(kept, axis=0)

    eidx = lax.broadcasted_iota(jnp.int32, cur.shape, 0)
    kept_mask = cur
    picks = []
    wsel = []
    for _ in range(TOP_K):
        mx = jnp.max(cur, axis=0, keepdims=True)
        first = jnp.min(jnp.where(cur == mx, eidx, N_EXPERTS), axis=0, keepdims=True)
        sel = eidx == first
        picks.append(first)
        wsel.append(jnp.sum(jnp.where(sel, scores, 0.0), axis=0, keepdims=True))
        cur = jnp.where(sel, NEG_INF, cur)
    w = jnp.concatenate(wsel, axis=0)
    w = w / jnp.sum(w, axis=0, keepdims=True) * ROUTE_SCALE
    idx = jnp.concatenate(picks, axis=0)

    oh = jnp.where(cur == NEG_INF, jnp.where(kept_mask == NEG_INF, 0.0, 1.0), 0.0).astype(BF16)
    before = jnp.dot(oh, tri_ref[0], preferred_element_type=F32)
    total = jnp.dot(oh, tri_ref[1], preferred_element_type=F32)
    pos_ref[...] = run_ref[...] + before
    run_ref[...] = run_ref[...] + total

    idx_ref[...] = idx
    wgt_ref[...] = w
    cnt_ref[...] = run_ref[...]


def _router(hi, lo, whi, wlo, bias, tri, tt):
    t, d = hi.shape
    tok = pl.BlockSpec((tt, d), lambda i: (i, 0))
    full = lambda a: pl.BlockSpec(a.shape, lambda i: (0,) * a.ndim)
    out = pl.BlockSpec((TOP_K, tt), lambda i: (0, i))
    return pl.pallas_call(
        _router_kernel,
        out_shape=(jax.ShapeDtypeStruct((TOP_K, t), jnp.int32),
                   jax.ShapeDtypeStruct((TOP_K, t), F32),
                   jax.ShapeDtypeStruct((N_EXPERTS, t), F32),
                   jax.ShapeDtypeStruct((N_EXPERTS, tt), F32)),
        grid=(t // tt,),
        in_specs=[tok, tok, full(whi), full(wlo), full(bias), full(tri)],
        out_specs=(out, out, pl.BlockSpec((N_EXPERTS, tt), lambda i: (0, i)),
                   pl.BlockSpec((N_EXPERTS, tt), lambda i: (0, 0))),
        scratch_shapes=[pltpu.VMEM((N_EXPERTS, tt), F32)],
        compiler_params=_params(("arbitrary",)),
        name="router",
    )(hi, lo, whi, wlo, bias, tri)


def _row_tile(ref, r):
    return ref.at[pl.ds(pl.multiple_of(r * ROW_SUB, ROW_SUB), ROW_SUB), :]


def _dest_kernel(idx_ref, pos_ref, start_ref, dest_ref):
    eidx = lax.broadcasted_iota(jnp.int32, start_ref.shape, 0)
    row = start_ref[...] + pos_ref[...]
    rows = [jnp.sum(jnp.where(eidx == idx_ref[k:k + 1, :], row, 0.0), axis=0, keepdims=True)
            for k in range(TOP_K)]
    dest_ref[...] = jnp.concatenate(rows, axis=0).astype(jnp.int32)


def _dest(idx, pos, start, tt):
    t = idx.shape[1]
    blk = pl.BlockSpec((TOP_K, tt), lambda i: (0, i))
    return pl.pallas_call(
        _dest_kernel,
        out_shape=jax.ShapeDtypeStruct((TOP_K, t), jnp.int32),
        grid=(t // tt,),
        in_specs=[blk, pl.BlockSpec((N_EXPERTS, tt), lambda i: (0, i)), pl.BlockSpec(start.shape, lambda i: (0, 0))],
        out_specs=blk,
        compiler_params=_params(("arbitrary",)),
        name="dest",
    )(idx, pos, start)


def _zero_tails_kernel(last_ref, o_ref):
    del last_ref
    o_ref[...] = jnp.zeros(o_ref.shape, o_ref.dtype)


def _zero_tails(last_blk, n_rows):
    return pl.pallas_call(
        _zero_tails_kernel,
        out_shape=jax.ShapeDtypeStruct((n_rows * ROW_SUB, LANES), jnp.uint32),
        grid_spec=pltpu.PrefetchScalarGridSpec(
            num_scalar_prefetch=1,
            grid=(last_blk.shape[0],),
            in_specs=[],
            out_specs=pl.BlockSpec((EXPERT_ROWS * ROW_SUB, LANES), lambda e, last: (last[e], 0))),
        compiler_params=_params(("arbitrary",)),
        name="zero_tails",
    )(last_blk)


def _dispatch_kernel(dest_ref, h_ref, xs_in_ref, xs_ref, sem):
    del xs_in_ref
    td = dest_ref.shape[1]

    def row_copy(t, k):
        return pltpu.make_async_copy(_row_tile(h_ref, t), _row_tile(xs_ref, dest_ref[k, t]), sem)

    def issue(t, carry):
        for k in range(TOP_K):
            row_copy(t, k).start()
        return carry

    lax.fori_loop(0, td, issue, 0)

    def drain(t, carry):
        for k in range(TOP_K):
            row_copy(t, k).wait()
        return carry

    lax.fori_loop(0, td, drain, 0)


def _dispatch(dest, h_rt, xs0, td):
    t = dest.shape[1]
    return pl.pallas_call(
        _dispatch_kernel,
        out_shape=jax.ShapeDtypeStruct(xs0.shape, xs0.dtype),
        grid=(t // td,),
        in_specs=[pl.BlockSpec((TOP_K, td), lambda i: (0, i), memory_space=pltpu.SMEM),
                  pl.BlockSpec((td * ROW_SUB, LANES), lambda i: (i, 0)),
                  pl.BlockSpec(memory_space=pl.ANY)],
        out_specs=pl.BlockSpec(memory_space=pl.ANY),
        scratch_shapes=[pltpu.SemaphoreType.DMA],
        input_output_aliases={2: 0},
        compiler_params=_params(("arbitrary",)),
        name="dispatch",
    )(dest, h_rt, xs0)


def _experts_kernel(be_ref, nu_ref, x_ref, wg_ref, wu_ref, wd_ref, y_ref, wgu_bf, wd_bf):
    i = pl.program_id(0)
    used = i < nu_ref[0]
    new_expert = jnp.logical_or(i == 0, be_ref[i] != be_ref[jnp.maximum(i - 1, 0)])

    @pl.when(jnp.logical_and(used, new_expert))
    def _():
        ff = wg_ref.shape[2]
        wgu_bf[:, :ff] = wg_ref[0].astype(BF16)
        wgu_bf[:, ff:] = wu_ref[0].astype(BF16)
        wd_bf[...] = wd_ref[0].astype(BF16)

    @pl.when(used)
    def _():
        ff = wg_ref.shape[2]
        for r in range(0, EXPERT_ROWS, EXPERT_SUB):
            rows = pl.ds(r * ROW_SUB, EXPERT_SUB * ROW_SUB)
            x = _load_row_tiles(x_ref.at[rows, :], EXPERT_SUB).astype(BF16)
            gu = jnp.dot(x, wgu_bf[...], preferred_element_type=F32)
            gate = gu[:, :ff]
            act = (gate * _sigmoid(gate) * gu[:, ff:]).astype(BF16)
            _store_row_tiles(y_ref.at[rows, :], jnp.dot(act, wd_bf[...], preferred_element_type=F32))


def _experts(blk_e, n_used, xs, wg, wu, wd):
    d, ff = wg.shape[1], wg.shape[2]
    nblk = xs.shape[0] // (EXPERT_ROWS * ROW_SUB)
    row = lambda i, be, nu: (jnp.minimum(i, nu[0] - 1), 0)
    wsel = lambda i, be, nu: (be[jnp.minimum(i, nu[0] - 1)], 0, 0)
    return pl.pallas_call(
        _experts_kernel,
        out_shape=jax.ShapeDtypeStruct(xs.shape, xs.dtype),
        grid_spec=pltpu.PrefetchScalarGridSpec(
            num_scalar_prefetch=2,
            grid=(nblk,),
            in_specs=[pl.BlockSpec((EXPERT_ROWS * ROW_SUB, LANES), row),
                      pl.BlockSpec((1, d, ff), wsel),
                      pl.BlockSpec((1, d, ff), wsel),
                      pl.BlockSpec((1, ff, d), wsel)],
            out_specs=pl.BlockSpec((EXPERT_ROWS * ROW_SUB, LANES), row),
            scratch_shapes=[pltpu.VMEM((d, 2 * ff), BF16), pltpu.VMEM((ff, d), BF16)]),
        compiler_params=_params(("arbitrary",)),
        name="experts",
    )(blk_e, n_used, xs, wg, wu, wd)


def _combine_kernel(dest_ref, next_ref, ys_ref, w_ref, base_ref, mod_ref, fw_ref, o_ref, buf, sem):
    tc = dest_ref.shape[1]
    i = pl.program_id(0)
    slot = i % 2

    def row_copy(row, s, t, k):
        return pltpu.make_async_copy(_row_tile(ys_ref, row), _row_tile(buf.at[s, k], t), sem.at[s])

    def issue(d_ref, s):
        def body(t, carry):
            for k in range(TOP_K):
                row_copy(d_ref[k, t], s, t, k).start()
            return carry

        lax.fori_loop(0, tc, body, 0)

    @pl.when(i == 0)
    def _():
        issue(dest_ref, 0)

    @pl.when(i + 1 < pl.num_programs(0))
    def _():
        issue(next_ref, 1 - slot)

    def drain(t, carry):
        for k in range(TOP_K):
            row_copy(0, slot, t, k).wait()
        return carry

    lax.fori_loop(0, tc, drain, 0)

    w = w_ref[...]
    routed = w[:, 0:1] * _load_row_tiles(buf.at[slot, 0], tc)
    for k in range(1, TOP_K):
        routed = routed + w[:, k:k + 1] * _load_row_tiles(buf.at[slot, k], tc)
    x = base_ref[0] + mod_ref[0, 5:6, :] * routed
    o_ref[0] = _rms(x, fw_ref[...])


def _combine(dest, ys, wt, base, mods, fw, tc):
    b, l, d = base.shape
    per_b = l // tc
    steps = b * per_b
    return pl.pallas_call(
        _combine_kernel,
        out_shape=jax.ShapeDtypeStruct((b, l, d), F32),
        grid=(steps,),
        in_specs=[pl.BlockSpec((TOP_K, tc), lambda i: (0, i), memory_space=pltpu.SMEM),
                  pl.BlockSpec((TOP_K, tc), lambda i: (0, jnp.minimum(i + 1, steps - 1)), memory_space=pltpu.SMEM),
                  pl.BlockSpec(memory_space=pl.ANY),
                  pl.BlockSpec((tc, TOP_K), lambda i: (i, 0)),
                  pl.BlockSpec((1, tc, d), lambda i: (i // per_b, i % per_b, 0)),
                  pl.BlockSpec((1, 6, d), lambda i: (i // per_b, 0, 0)),
                  pl.BlockSpec((1, d), lambda i: (0, 0))],
        out_specs=pl.BlockSpec((1, tc, d), lambda i: (i // per_b, i % per_b, 0)),
        scratch_shapes=[pltpu.VMEM((2, TOP_K, tc * ROW_SUB, LANES), jnp.uint32), pltpu.SemaphoreType.DMA((2,))],
        compiler_params=_params(("arbitrary",)),
        name="combine",
    )(dest, dest, ys, wt, base, mods, fw)


def _rope_tables(l):
    t = jnp.arange(l, dtype=jnp.int32)
    row = (t // GRID_W).astype(F32)
    col = (t % GRID_W).astype(F32)
    n_freq = HEAD_DIM // 4
    inv = ROPE_THETA ** (-jnp.arange(n_freq, dtype=F32) / n_freq)
    ang = jnp.concatenate([row[:, None] * inv, col[:, None] * inv], axis=-1)
    cos = jnp.repeat(jnp.cos(ang), 2, axis=1)
    sin = jnp.repeat(jnp.sin(ang), 2, axis=1)
    sign = jnp.tile(jnp.array([-1.0, 1.0], F32), HEAD_DIM // 2)
    reps = LANES // HEAD_DIM
    return jnp.tile(cos, (1, reps)), jnp.tile(sin * sign, (1, reps))


def _filter_features(l):
    t = jnp.linspace(0.0, 1.0, l, dtype=F32)[:, None]
    bands = (FILTER_EMB - 1) // 2
    w = 2.0 * math.pi * jnp.arange(l, dtype=F32)[:, None] / l
    f = jnp.linspace(1e-4, bands - 1, bands, dtype=F32)[None, :]
    z = jnp.concatenate([t, jnp.cos(f * w), -jnp.sin(f * w)], axis=-1)
    min_decay = math.log(FILTER_TARGET) / FILTER_DECAY_FAST
    max_decay = math.log(FILTER_TARGET) / FILTER_DECAY_SLOW
    deltas = jnp.linspace(min_decay, max_decay, HYENA_WIDTH, dtype=F32)[None, :]
    return jnp.pad(z, ((0, 0), (0, LANES - FILTER_EMB))), deltas


def _dft_matrices(l):
    n = 2 * l
    idx = jnp.arange(l, dtype=jnp.int32)
    r = math.isqrt(l)
    assert r * r == l
    sub = jnp.arange(r, dtype=jnp.int32)
    hi = ((r * sub[:, None] * idx[None, :]) % n).astype(F32) * (2.0 * math.pi / n)
    lo = ((sub[:, None] * idx[None, :]) % n).astype(F32) * (2.0 * math.pi / n)
    ch, sh, cl, sl = jnp.cos(hi)[:, None, :], jnp.sin(hi)[:, None, :], jnp.cos(lo)[None], jnp.sin(lo)[None]
    c = (ch * cl - sh * sl).reshape(l, l)
    s = (sh * cl + ch * sl).reshape(l, l)
    alt = jnp.where(idx % 2 == 0, 1.0, -1.0).astype(F32)
    first = (idx == 0)[:, None]
    fwd = jnp.concatenate([c, jnp.where(first, alt[None, :], -s)], axis=0).astype(BF16)
    firstc = (idx == 0)[None, :]
    inv_r = (jnp.where(firstc, 1.0, 2.0) * c / n).astype(BF16)
    inv_i = (jnp.where(firstc, alt[:, None], -2.0 * s) / n).astype(BF16)
    return fwd, inv_r, inv_i


def _head_perm():
    order = []
    for j in range(N_HEADS // 2):
        order += list(range(j * HEAD_DIM, (j + 1) * HEAD_DIM))
        order += list(range((j + N_HEADS // 2) * HEAD_DIM, (j + 1 + N_HEADS // 2) * HEAD_DIM))
    return jnp.array(order, jnp.int32)


def _pad2(a, rows, cols):
    return jnp.pad(a, ((0, rows - a.shape[0]), (0, cols - a.shape[1])))


def kernel(x, c, ctx, c_ctx, mod_w, mod_b, norm1_w, w_in, q_norm_w, k_norm_w, conv_w, conv_b, filt_w1, filt_b1, filt_w2, filt_b2, filt_w3, filt_b3, filt_w4, filt_freq, hyena_bias, attn_out_norm_w, hyena_out_norm_w, w_out, norm2_w, router_w, router_bias, exp_w_gate, exp_w_up, exp_w_down, sh_w_gate, sh_w_up, sh_w_down, final_norm_w):
    b, l, d = x.shape
    t = b * l
    assert mod_w.shape[0] == 1, "single-layer stack"
    tl = min(512, l)

    cond = jnp.concatenate([c, c_ctx[None, :], jnp.zeros((-(b + 1) % 8, d), F32)], axis=0)
    mod = _adaln(cond, mod_w[0], mod_b[0][None, :])
    mods = mod[:b].reshape(b, 6, d)
    cmod = mod[b].reshape(6, d)

    perm = _head_perm()
    w_in0 = w_in[0]
    w_in_k = jnp.concatenate([w_in0[:, :Q_END][:, perm], w_in0[:, Q_END:]], axis=1).astype(BF16)
    w_kv = w_in0[:, Q_END:V_END].astype(BF16)
    gq = jnp.kron(jnp.eye(N_HEADS, dtype=F32), jnp.full((HEAD_DIM, HEAD_DIM), 1.0 / HEAD_DIM, F32)).astype(BF16)
    qnw = jnp.tile(q_norm_w[0], N_HEADS)[None, :]
    knw = jnp.tile(k_norm_w[0], N_KV_HEADS)[None, :]
    n1w = norm1_w[0][None, :]
    cos, sin = _rope_tables(l)

    kc, vc = _ctx_kv(ctx, cmod, n1w, w_kv, gq[:KV_WIDTH, :KV_WIDTH], knw)
    q, k, v, u = _inproj(x, mods, n1w, w_in_k, gq, qnw, knw, cos, sin, tl)
    k_all = jnp.concatenate([kc, k], axis=1)
    v_all = jnp.concatenate([vc, v], axis=1)
    an = _attention(q, k_all, v_all, attn_out_norm_w[0][perm][None, :], min(256, l), min(256, l))

    cw = conv_w[0].reshape(3, 3, HYENA_WIDTH).transpose(1, 0, 2)
    cb = conv_b[0].reshape(3, HYENA_WIDTH)
    gbf, x0 = _hyena_pre(u, cw, cb)
    z, deltas = _filter_features(l)
    fo = filt_w2.shape[1]
    hsd = _hyena_filter(
        z, _pad2(filt_w1[0], LANES, LANES), _pad2(filt_b1[0][None, :], 1, LANES),
        _pad2(filt_w2[0], LANES, LANES), _pad2(filt_b2[0][None, :], 1, LANES),
        _pad2(filt_w3[0], LANES, LANES), _pad2(filt_b3[0][None, :], 1, LANES),
        _pad2(filt_w4[0], LANES, 2 * HYENA_WIDTH), _pad2(filt_freq[0][None, :], 1, LANES), deltas, tl)
    del fo
    fwd, inv_r, inv_i = _dft_matrices(l)
    spec = _dft(fwd, hsd, tl)
    row0 = (jnp.arange(l) == 0)[:, None]
    sa = spec[0, :l]
    sd = jnp.where(row0, spec[0, l:l + 1], sa)
    sb = jnp.where(row0, 0.0, spec[1, l:])
    zr, zi = _dft_mul(fwd, gbf, sa, sb, sd, tl)
    yn = _idft(inv_r, inv_i, zr, zi, gbf, x0, hyena_bias[0][None, :], hyena_out_norm_w[0][None, :], tl)

    w_out0 = w_out[0]
    base, h2hi, h2lo, h2rt = _merge(
        an, yn, x, mods, w_out0[:ATTN_WIDTH][perm].astype(BF16), w_out0[ATTN_WIDTH:].astype(BF16),
        norm2_w[0][None, :], sh_w_gate[0].astype(BF16), sh_w_up[0].astype(BF16), sh_w_down[0].astype(BF16), tl)

    tt = 256
    rwt = router_w[0].T
    rw_hi = rwt.astype(BF16)
    rw_lo = (rwt - rw_hi.astype(F32)).astype(BF16)
    bias = jnp.broadcast_to(router_bias[0][:, None], (N_EXPERTS, tt))
    ti = jnp.arange(tt)
    tri = jnp.stack([(ti[:, None] < ti[None, :]), jnp.ones((tt, tt), bool)]).astype(BF16)
    idx, wgt, pos, cnt = _router(h2hi.reshape(t, d), h2lo.reshape(t, d), rw_hi, rw_lo, bias, tri, tt)

    counts = cnt[:, 0].astype(jnp.int32)
    padded = (counts + EXPERT_ROWS - 1) // EXPERT_ROWS * EXPERT_ROWS
    pad_end = jnp.cumsum(padded)
    pad_start = pad_end - padded
    dest = _dest(idx, pos, jnp.broadcast_to(pad_start.astype(F32)[:, None], (N_EXPERTS, tt)), tt)
    n_rows = (t * TOP_K + N_EXPERTS * (EXPERT_ROWS - 1) + EXPERT_ROWS - 1) // EXPERT_ROWS * EXPERT_ROWS
    nblk = n_rows // EXPERT_ROWS
    blk_row = jnp.arange(nblk, dtype=jnp.int32) * EXPERT_ROWS
    blk_e = jnp.minimum(jnp.sum((pad_end[None, :] <= blk_row[:, None]).astype(jnp.int32), axis=1), N_EXPERTS - 1)
    n_used = (pad_end[-1:] // EXPERT_ROWS).astype(jnp.int32)

    last_blk = jnp.maximum(pad_end // EXPERT_ROWS - 1, 0).astype(jnp.int32)
    xs = _dispatch(dest, h2rt, _zero_tails(last_blk, n_rows), min(512, t))
    ys = _experts(blk_e, n_used, xs, exp_w_gate[0], exp_w_up[0], exp_w_down[0])
    return _combine(dest, ys, wgt.T, base, mods, final_norm_w[None, :], min(128, l))
```

```python
import functools
import math

import jax
import jax.numpy as jnp
from jax import lax
from jax.experimental import pallas as pl
from jax.experimental.pallas import tpu as pltpu
from jax.experimental.pallas import tpu_sc as plsc

F32 = jnp.float32
BF16 = jnp.bfloat16
HIGHEST = lax.Precision.HIGHEST

GRID_W = 64
N_HEADS = 8
N_KV_HEADS = 2
HEAD_DIM = 64
ATTN_WIDTH = N_HEADS * HEAD_DIM
KV_WIDTH = N_KV_HEADS * HEAD_DIM
HYENA_WIDTH = 512
Q_END = ATTN_WIDTH
K_END = Q_END + KV_WIDTH
V_END = K_END + KV_WIDTH
ROPE_THETA = 10000.0
FILTER_EMB = 33
FILTER_DECAY_FAST = 0.3
FILTER_DECAY_SLOW = 1.5
FILTER_TARGET = 1e-2
N_EXPERTS = 256
TOP_K = 8
N_GROUPS = 8
TOPK_GROUPS = 4
ROUTE_SCALE = 2.5
EPS = 1e-6

LANES = 128
ROW_SUB = 4
EXPERT_ROWS = 512
EXPERT_SUB = 256
NEG_INF = float("-inf")


def _params(semantics, vmem_mb=48):
    return pltpu.CompilerParams(dimension_semantics=semantics, vmem_limit_bytes=vmem_mb * 1024 * 1024)


def _rms(x, w):
    return x * lax.rsqrt(jnp.mean(x * x, axis=-1, keepdims=True) + EPS) * w


def _sigmoid(x):
    return 1.0 / (1.0 + jnp.exp(-x))


def _adaln_kernel(c_ref, w_ref, b_ref, o_ref):
    c = c_ref[...]
    s = c * _sigmoid(c)
    o_ref[...] = jnp.dot(s, w_ref[...], precision=HIGHEST, preferred_element_type=F32) + b_ref[...]


def _adaln(cond, w, b):
    rows, d = cond.shape
    n = w.shape[1]
    tn = 1536
    return pl.pallas_call(
        _adaln_kernel,
        out_shape=jax.ShapeDtypeStruct((rows, n), F32),
        grid=(n // tn,),
        in_specs=[pl.BlockSpec((rows, d), lambda j: (0, 0)),
                  pl.BlockSpec((d, tn), lambda j: (0, j)),
                  pl.BlockSpec((1, tn), lambda j: (0, j))],
        out_specs=pl.BlockSpec((rows, tn), lambda j: (0, j)),
        compiler_params=_params(("arbitrary",)),
        name="adaln",
    )(cond, w, b)


def _head_rms(t, gmat, w):
    ms = jnp.dot((t * t).astype(BF16), gmat, preferred_element_type=F32)
    return t * lax.rsqrt(ms + EPS) * w


def _modulated(x, norm_w, shift, scale):
    return _rms(x, norm_w) * (1.0 + scale) + shift


def _ctx_kv_kernel(ctx_ref, mod_ref, n1_ref, w_ref, g_ref, kn_ref, kc_ref, vc_ref):
    x = ctx_ref[0]
    h = _modulated(x, n1_ref[...], mod_ref[0:1, :], mod_ref[1:2, :])
    kv = jnp.dot(h.astype(BF16), w_ref[...], preferred_element_type=F32)
    k = _head_rms(kv[:, :KV_WIDTH], g_ref[...], kn_ref[...])
    v = kv[:, KV_WIDTH:]
    kc_ref[0] = k.astype(BF16)
    vc_ref[0] = jnp.concatenate([v, jnp.ones_like(v)], axis=1).astype(BF16)


def _ctx_kv(ctx, cmod, n1w, w_kv, gk, knw):
    b, c, d = ctx.shape
    return pl.pallas_call(
        _ctx_kv_kernel,
        out_shape=(jax.ShapeDtypeStruct((b, c, KV_WIDTH), BF16),
                   jax.ShapeDtypeStruct((b, c, 2 * KV_WIDTH), BF16)),
        grid=(b,),
        in_specs=[pl.BlockSpec((1, c, d), lambda i: (i, 0, 0)),
                  pl.BlockSpec(cmod.shape, lambda i: (0, 0)),
                  pl.BlockSpec((1, d), lambda i: (0, 0)),
                  pl.BlockSpec(w_kv.shape, lambda i: (0, 0)),
                  pl.BlockSpec(gk.shape, lambda i: (0, 0)),
                  pl.BlockSpec((1, KV_WIDTH), lambda i: (0, 0))],
        out_specs=(pl.BlockSpec((1, c, KV_WIDTH), lambda i: (i, 0, 0)),
                   pl.BlockSpec((1, c, 2 * KV_WIDTH), lambda i: (i, 0, 0))),
        compiler_params=_params(("arbitrary",)),
        name="ctx_kv",
    )(ctx, cmod, n1w, w_kv, gk, knw)


def _rope(t, cos, sin, even):
    width = t.shape[1]
    partner = jnp.where(even, pltpu.roll(t, width - 1, axis=1), pltpu.roll(t, 1, axis=1))
    return t * cos + partner * sin


def _inproj_kernel(x_ref, mod_ref, n1_ref, w_ref, gq_ref, qn_ref, kn_ref, cos_ref, sin_ref,
                   q_ref, k_ref, v_ref, u_ref):
    x = x_ref[0]
    h = _modulated(x, n1_ref[...], mod_ref[0, 0:1, :], mod_ref[0, 1:2, :])
    p = jnp.dot(h.astype(BF16), w_ref[...], preferred_element_type=F32)
    gq = gq_ref[...]
    q = _head_rms(p[:, :Q_END], gq, qn_ref[...])
    k = _head_rms(p[:, Q_END:K_END], gq[:KV_WIDTH, :KV_WIDTH], kn_ref[...])
    v = p[:, K_END:V_END]
    cos = cos_ref[...]
    sin = sin_ref[...]
    reps = Q_END // LANES
    cos_q = jnp.concatenate([cos] * reps, axis=1)
    sin_q = jnp.concatenate([sin] * reps, axis=1)
    even_q = (lax.broadcasted_iota(jnp.int32, (1, Q_END), 1) & 1) == 0
    even_k = (lax.broadcasted_iota(jnp.int32, (1, KV_WIDTH), 1) & 1) == 0
    q = _rope(q, cos_q, sin_q, even_q) * (HEAD_DIM ** -0.5)
    k = _rope(k, cos, sin, even_k)
    q_ref[0] = q.astype(BF16)
    k_ref[0] = k.astype(BF16)
    v_ref[0] = jnp.concatenate([v, jnp.ones_like(v)], axis=1).astype(BF16)
    u_ref[0] = p[:, V_END:].astype(BF16)


def _inproj(x, mods, n1w, w_in, gq, qnw, knw, cos, sin, tl):
    b, l, d = x.shape
    ncol = w_in.shape[1]
    nu = ncol - V_END
    return pl.pallas_call(
        _inproj_kernel,
        out_shape=(jax.ShapeDtypeStruct((b, l, Q_END), BF16),
                   jax.ShapeDtypeStruct((b, l, KV_WIDTH), BF16),
                   jax.ShapeDtypeStruct((b, l, 2 * KV_WIDTH), BF16),
                   jax.ShapeDtypeStruct((b, l, nu), BF16)),
        grid=(l // tl, b),
        in_specs=[pl.BlockSpec((1, tl, d), lambda i, j: (j, i, 0)),
                  pl.BlockSpec((1, 6, d), lambda i, j: (j, 0, 0)),
                  pl.BlockSpec((1, d), lambda i, j: (0, 0)),
                  pl.BlockSpec((d, ncol), lambda i, j: (0, 0)),
                  pl.BlockSpec(gq.shape, lambda i, j: (0, 0)),
                  pl.BlockSpec((1, Q_END), lambda i, j: (0, 0)),
                  pl.BlockSpec((1, KV_WIDTH), lambda i, j: (0, 0)),
                  pl.BlockSpec((tl, LANES), lambda i, j: (i, 0)),
                  pl.BlockSpec((tl, LANES), lambda i, j: (i, 0))],
        out_specs=(pl.BlockSpec((1, tl, Q_END), lambda i, j: (j, i, 0)),
                   pl.BlockSpec((1, tl, KV_WIDTH), lambda i, j: (j, i, 0)),
                   pl.BlockSpec((1, tl, 2 * KV_WIDTH), lambda i, j: (j, i, 0)),
                   pl.BlockSpec((1, tl, nu), lambda i, j: (j, i, 0))),
        compiler_params=_params(("arbitrary", "arbitrary")),
        name="inproj",
    )(x, mods, n1w, w_in, gq, qnw, knw, cos, sin)


def _attn_kernel(q_ref, k_ref, v_ref, wn_ref, o_ref, *, sub):
    tq = q_ref.shape[1]
    low = lax.broadcasted_iota(jnp.int32, (1, LANES), 1) < HEAD_DIM
    nt = (((1,), (1,)), ((), ()))
    kk = k_ref[0]
    vv = v_ref[0]
    for r in range(0, tq, sub):
        outs = []
        for j in range(Q_END // LANES):
            qv = q_ref[0, r:r + sub, LANES * j:LANES * (j + 1)]
            zero = jnp.zeros_like(qv)
            halves = []
            for g in range(N_KV_HEADS):
                qh = jnp.where(low, qv, zero) if g == 0 else jnp.where(low, zero, qv)
                s = lax.dot_general(qh, kk, nt, preferred_element_type=F32)
                p = jnp.exp(s - jnp.max(s, axis=-1, keepdims=True)).astype(BF16)
                pv = jnp.dot(p, vv, preferred_element_type=F32)
                halves.append(pv[:, :LANES] / pv[:, LANES:])
            outs.append(jnp.where(low, halves[0], halves[1]))
        a = jnp.concatenate(outs, axis=1)
        o_ref[0, r:r + sub, :] = _rms(a, wn_ref[...]).astype(BF16)


def _attention(q, k, v, wn, tq, sub):
    b, l, _ = q.shape
    n = k.shape[1]
    return pl.pallas_call(
        functools.partial(_attn_kernel, sub=sub),
        out_shape=jax.ShapeDtypeStruct((b, l, Q_END), BF16),
        grid=(b, l // tq),
        in_specs=[pl.BlockSpec((1, tq, Q_END), lambda i, j: (i, j, 0)),
                  pl.BlockSpec((1, n, KV_WIDTH), lambda i, j: (i, 0, 0)),
                  pl.BlockSpec((1, n, 2 * KV_WIDTH), lambda i, j: (i, 0, 0)),
                  pl.BlockSpec((1, Q_END), lambda i, j: (0, 0))],
        out_specs=pl.BlockSpec((1, tq, Q_END), lambda i, j: (i, j, 0)),
        compiler_params=_params(("arbitrary", "arbitrary")),
        name="attn",
    )(q, k, v, wn)


def _hyena_pre_kernel(u0_ref, u1_ref, u2_ref, cw_ref, cb_ref, g_ref, x0_ref):
    l = u0_ref.shape[1]
    row = lax.broadcasted_iota(jnp.int32, (l, LANES), 0)

    def conv(u_ref, gi):
        u = u_ref[0].astype(F32)
        prev = jnp.where(row == 0, 0.0, pltpu.roll(u, 1, axis=0))
        nxt = jnp.where(row == l - 1, 0.0, pltpu.roll(u, l - 1, axis=0))
        w = cw_ref[gi]
        return w[0:1] * prev + w[1:2] * u + w[2:3] * nxt + cb_ref[gi:gi + 1, :]

    x0 = conv(u0_ref, 0)
    x1 = conv(u1_ref, 1)
    v = conv(u2_ref, 2)
    g = v * x1
    g_ref[0] = g.astype(BF16)
    x0_ref[0] = x0.astype(BF16)


def _hyena_pre(u, cw, cb):
    b, l, _ = u.shape
    nblk = HYENA_WIDTH // LANES
    ublk = lambda gi: pl.BlockSpec((1, l, LANES), lambda i, j: (i, 0, gi * nblk + j))
    oblk = pl.BlockSpec((1, l, LANES), lambda i, j: (i, 0, j))
    return pl.pallas_call(
        _hyena_pre_kernel,
        out_shape=(jax.ShapeDtypeStruct((b, l, HYENA_WIDTH), BF16),
                   jax.ShapeDtypeStruct((b, l, HYENA_WIDTH), BF16)),
        grid=(b, nblk),
        in_specs=[ublk(0), ublk(1), ublk(2),
                  pl.BlockSpec((3, 3, LANES), lambda i, j: (0, 0, j)),
                  pl.BlockSpec((3, LANES), lambda i, j: (0, j))],
        out_specs=(oblk, oblk),
        compiler_params=_params(("arbitrary", "arbitrary")),
        name="hyena_pre",
    )(u, u, u, cw, cb)


def _filter_kernel(z_ref, w1_ref, b1_ref, w2_ref, b2_ref, w3_ref, b3_ref, w4_ref, fr_ref, dl_ref, o_ref):
    tl = z_ref.shape[0]
    z = z_ref[...]
    fr = fr_ref[...]
    dot = lambda a, w: jnp.dot(a, w, precision=HIGHEST, preferred_element_type=F32)
    h = jnp.sin(fr * (dot(z, w1_ref[...]) + b1_ref[...]))
    h = jnp.sin(fr * (dot(h, w2_ref[...]) + b2_ref[...]))
    h = jnp.sin(fr * (dot(h, w3_ref[...]) + b3_ref[...]))
    h = dot(h, w4_ref[...])
    t = z[:, 0:1]
    decay = jnp.exp(-t * jnp.abs(dl_ref[...]))
    hf = h[:, :HYENA_WIDTH] * decay
    hb = h[:, HYENA_WIDTH:] * decay
    row = lax.broadcasted_iota(jnp.int32, (tl, HYENA_WIDTH), 0) + pl.program_id(0) * tl
    hb = jnp.where(row == 0, 0.0, hb)
    o_ref[0] = hf + hb
    o_ref[1] = hf - hb


def _hyena_filter(z, w1, b1, w2, b2, w3, b3, w4, freq, deltas, tl):
    l = z.shape[0]
    full = lambda a: pl.BlockSpec(a.shape, lambda i: (0,) * a.ndim)
    return pl.pallas_call(
        _filter_kernel,
        out_shape=jax.ShapeDtypeStruct((2, l, HYENA_WIDTH), F32),
        grid=(l // tl,),
        in_specs=[pl.BlockSpec((tl, z.shape[1]), lambda i: (i, 0)),
                  full(w1), full(b1), full(w2), full(b2), full(w3), full(b3), full(w4), full(freq), full(deltas)],
        out_specs=pl.BlockSpec((2, tl, HYENA_WIDTH), lambda i: (0, i, 0)),
        compiler_params=_params(("arbitrary",)),
        name="hyena_filter",
    )(z, w1, b1, w2, b2, w3, b3, w4, freq, deltas)


def _dft_kernel(f_ref, x_ref, o_ref):
    o_ref[0] = jnp.dot(f_ref[...], x_ref[0].astype(BF16), preferred_element_type=F32)


def _dft(fmat, x, tf):
    nb, l, w = x.shape
    n = fmat.shape[0]
    return pl.pallas_call(
        _dft_kernel,
        out_shape=jax.ShapeDtypeStruct((nb, n, w), F32),
        grid=(n // tf, nb),
        in_specs=[pl.BlockSpec((tf, l), lambda i, j: (i, 0)),
                  pl.BlockSpec((1, l, w), lambda i, j: (j, 0, 0))],
        out_specs=pl.BlockSpec((1, tf, w), lambda i, j: (j, i, 0)),
        compiler_params=_params(("arbitrary", "arbitrary")),
        name="dft_filter",
    )(fmat, x)


def _dft_mul_kernel(fr_ref, fi_ref, x_ref, a_ref, b_ref, d_ref, zr_ref, zi_ref):
    x = x_ref[0]
    xr = jnp.dot(fr_ref[...], x, preferred_element_type=F32)
    xi = jnp.dot(fi_ref[...], x, preferred_element_type=F32)
    bb = b_ref[...]
    zr_ref[0] = (xr * a_ref[...] - xi * bb).astype(BF16)
    zi_ref[0] = (xr * bb + xi * d_ref[...]).astype(BF16)


def _dft_mul(fmat, g, sa, sb, sd, tf):
    b, l, w = g.shape
    nf = l // tf
    spec = pl.BlockSpec((tf, w), lambda i, j: (i, 0))
    return pl.pallas_call(
        _dft_mul_kernel,
        out_shape=(jax.ShapeDtypeStruct((b, l, w), BF16), jax.ShapeDtypeStruct((b, l, w), BF16)),
        grid=(nf, b),
        in_specs=[pl.BlockSpec((tf, l), lambda i, j: (i, 0)),
                  pl.BlockSpec((tf, l), lambda i, j: (i + nf, 0)),
                  pl.BlockSpec((1, l, w), lambda i, j: (j, 0, 0)),
                  spec, spec, spec],
        out_specs=(pl.BlockSpec((1, tf, w), lambda i, j: (j, i, 0)),
                   pl.BlockSpec((1, tf, w), lambda i, j: (j, i, 0))),
        compiler_params=_params(("arbitrary", "arbitrary")),
        name="dft_mul",
    )(fmat, fmat, g, sa, sb, sd)


def _idft_kernel(fr_ref, fi_ref, zr_ref, zi_ref, g_ref, x0_ref, hb_ref, wn_ref, o_ref):
    conv = (jnp.dot(fr_ref[...], zr_ref[0], preferred_element_type=F32)
            + jnp.dot(fi_ref[...], zi_ref[0], preferred_element_type=F32))
    y = (conv + g_ref[0].astype(F32) * hb_ref[...]) * x0_ref[0].astype(F32)
    o_ref[0] = _rms(y, wn_ref[...]).astype(BF16)


def _idft(finv_r, finv_i, zr, zi, g, x0, hbias, wn, tt):
    b, l, w = zr.shape
    tile = pl.BlockSpec((1, tt, w), lambda i, j: (j, i, 0))
    return pl.pallas_call(
        _idft_kernel,
        out_shape=jax.ShapeDtypeStruct((b, l, w), BF16),
        grid=(l // tt, b),
        in_specs=[pl.BlockSpec((tt, l), lambda i, j: (i, 0)),
                  pl.BlockSpec((tt, l), lambda i, j: (i, 0)),
                  pl.BlockSpec((1, l, w), lambda i, j: (j, 0, 0)),
                  pl.BlockSpec((1, l, w), lambda i, j: (j, 0, 0)),
                  tile, tile,
                  pl.BlockSpec((1, w), lambda i, j: (0, 0)),
                  pl.BlockSpec((1, w), lambda i, j: (0, 0))],
        out_specs=tile,
        compiler_params=_params(("arbitrary", "arbitrary")),
        name="idft",
    )(finv_r, finv_i, zr, zi, g, x0, hbias, wn)


def _bf16_bits(v):
    return lax.bitcast_convert_type(v.astype(BF16).astype(F32), jnp.uint32)


def _store_row_tiles(ref, val):
    rows, half = val.shape[0], val.shape[1] // 2
    assert half == ROW_SUB * LANES
    for j in range(ROW_SUB):
        lo = _bf16_bits(val[:, LANES * j:LANES * (j + 1)]) >> 16
        hi = _bf16_bits(val[:, half + LANES * j:half + LANES * (j + 1)]) & jnp.uint32(0xFFFF0000)
        ref[pl.ds(j, rows, stride=ROW_SUB), :] = lo | hi


def _load_row_tiles(ref, rows):
    words = [ref[pl.ds(j, rows, stride=ROW_SUB), :] for j in range(ROW_SUB)]
    lo = [lax.bitcast_convert_type(w << 16, F32) for w in words]
    hi = [lax.bitcast_convert_type(w & jnp.uint32(0xFFFF0000), F32) for w in words]
    return jnp.concatenate(lo + hi, axis=1)


def _merge_kernel(a_ref, y_ref, x_ref, mod_ref, wa_ref, wy_ref, n2_ref, sg_ref, su_ref, sd_ref,
                  base_ref, hi_ref, lo_ref, rt_ref):
    m = (jnp.dot(a_ref[0], wa_ref[...], preferred_element_type=F32)
         + jnp.dot(y_ref[0], wy_ref[...], preferred_element_type=F32))
    x1 = x_ref[0] + mod_ref[0, 2:3, :] * m
    h2 = _modulated(x1, n2_ref[...], mod_ref[0, 3:4, :], mod_ref[0, 4:5, :])
    hi = h2.astype(BF16)
    hi_ref[0] = hi
    lo_ref[0] = (h2 - hi.astype(F32)).astype(BF16)
    _store_row_tiles(rt_ref, h2)
    gate = jnp.dot(hi, sg_ref[...], preferred_element_type=F32)
    up = jnp.dot(hi, su_ref[...], preferred_element_type=F32)
    act = (gate * _sigmoid(gate) * up).astype(BF16)
    shared = jnp.dot(act, sd_ref[...], preferred_element_type=F32)
    base_ref[0] = x1 + mod_ref[0, 5:6, :] * shared


def _merge(an, yn, x, mods, wa, wy, n2w, sg, su, sd, tl):
    b, l, d = x.shape
    full = lambda a: pl.BlockSpec(a.shape, lambda i, j: (0,) * a.ndim)
    half = pl.BlockSpec((1, tl, an.shape[2]), lambda i, j: (i, j, 0))
    wide = pl.BlockSpec((1, tl, d), lambda i, j: (i, j, 0))
    per_b = l // tl
    return pl.pallas_call(
        _merge_kernel,
        out_shape=(jax.ShapeDtypeStruct((b, l, d), F32),
                   jax.ShapeDtypeStruct((b, l, d), BF16),
                   jax.ShapeDtypeStruct((b, l, d), BF16),
                   jax.ShapeDtypeStruct((b * l * ROW_SUB, LANES), jnp.uint32)),
        grid=(b, per_b),
        in_specs=[half, half, wide,
                  pl.BlockSpec((1, 6, d), lambda i, j: (i, 0, 0)),
                  full(wa), full(wy), full(n2w), full(sg), full(su), full(sd)],
        out_specs=(wide, wide, wide,
                   pl.BlockSpec((tl * ROW_SUB, LANES), lambda i, j: (i * per_b + j, 0))),
        compiler_params=_params(("arbitrary", "arbitrary")),
        name="merge",
    )(an, yn, x, mods, wa, wy, n2w, sg, su, sd)


def _router_kernel(hi_ref, lo_ref, whi_ref, wlo_ref, bias_ref, tri_ref,
                   idx_ref, wgt_ref, pos_ref, cnt_ref, run_ref):
    tt = hi_ref.shape[0]
    per_group = N_EXPERTS // N_GROUPS

    @pl.when(pl.program_id(0) == 0)
    def _():
        run_ref[...] = jnp.zeros_like(run_ref)

    nt = (((1,), (1,)), ((), ()))
    hi = hi_ref[...]
    whi = whi_ref[...]
    logits = (lax.dot_general(whi, hi, nt, preferred_element_type=F32)
              + lax.dot_general(whi, lo_ref[...], nt, preferred_element_type=F32)
              + lax.dot_general(wlo_ref[...], hi, nt, preferred_element_type=F32))
    scores = _sigmoid(logits)
    biased = scores + bias_ref[...]

    ridx = lax.broadcasted_iota(jnp.int32, (per_group, tt), 0)
    groups = [biased[g * per_group:(g + 1) * per_group, :] for g in range(N_GROUPS)]
    gs = []
    for blk in groups:
        m1 = jnp.max(blk, axis=0, keepdims=True)
        i1 = jnp.min(jnp.where(blk == m1, ridx, per_group), axis=0, keepdims=True)
        m2 = jnp.max(jnp.where(ridx == i1, NEG_INF, blk), axis=0, keepdims=True)
        gs.append(m1 + m2)

    kept = []
    for g in range(N_GROUPS):
        ahead = jnp.zeros((1, tt), F32)
        for o in range(N_GROUPS):
            if o != g:
                wins = (gs[o] >= gs[g]) if o < g else (gs[o] > gs[g])
                ahead = ahead + jnp.where(wins, 1.0, 0.0)
        kept.append(jnp.where(ahead < TOPK_GROUPS, groups[g], NEG_INF))
    cur = jnp.concatenate(kept, axis=0)

    eidx = lax.broadcasted_iota(jnp.int32, cur.shape, 0)
    kept_mask = cur
    picks = []
    wsel = []
    for _ in range(TOP_K):
        mx = jnp.max(cur, axis=0, keepdims=True)
        first = jnp.min(jnp.where(cur == mx, eidx, N_EXPERTS), axis=0, keepdims=True)
        sel = eidx == first
        picks.append(first)
        wsel.append(jnp.sum(jnp.where(sel, scores, 0.0), axis=0, keepdims=True))
        cur = jnp.where(sel, NEG_INF, cur)
    w = jnp.concatenate(wsel, axis=0)
    w = w / jnp.sum(w, axis=0, keepdims=True) * ROUTE_SCALE
    idx = jnp.concatenate(picks, axis=0)

    oh = jnp.where(cur == NEG_INF, jnp.where(kept_mask == NEG_INF, 0.0, 1.0), 0.0).astype(BF16)
    before = jnp.dot(oh, tri_ref[0], preferred_element_type=F32)
    total = jnp.dot(oh, tri_ref[1], preferred_element_type=F32)
    pos_ref[...] = run_ref[...] + before
    run_ref[...] = run_ref[...] + total

    idx_ref[...] = idx
    wgt_ref[...] = w
    cnt_ref[...] = run_ref[...]


def _router(hi, lo, whi, wlo, bias, tri, tt):
    t, d = hi.shape
    tok = pl.BlockSpec((tt, d), lambda i: (i, 0))
    full = lambda a: pl.BlockSpec(a.shape, lambda i: (0,) * a.ndim)
    out = pl.BlockSpec((TOP_K, tt), lambda i: (0, i))
    return pl.pallas_call(
        _router_kernel,
        out_shape=(jax.ShapeDtypeStruct((TOP_K, t), jnp.int32),
                   jax.ShapeDtypeStruct((TOP_K, t), F32),
                   jax.ShapeDtypeStruct((N_EXPERTS, t), F32),
                   jax.ShapeDtypeStruct((N_EXPERTS, tt), F32)),
        grid=(t // tt,),
        in_specs=[tok, tok, full(whi), full(wlo), full(bias), full(tri)],
        out_specs=(out, out, pl.BlockSpec((N_EXPERTS, tt), lambda i: (0, i)),
                   pl.BlockSpec((N_EXPERTS, tt), lambda i: (0, 0))),
        scratch_shapes=[pltpu.VMEM((N_EXPERTS, tt), F32)],
        compiler_params=_params(("arbitrary",)),
        name="router",
    )(hi, lo, whi, wlo, bias, tri)


def _row_tile(ref, r):
    return ref.at[pl.ds(pl.multiple_of(r * ROW_SUB, ROW_SUB), ROW_SUB), :]


def _dest_kernel(idx_ref, pos_ref, start_ref, dest_ref):
    eidx = lax.broadcasted_iota(jnp.int32, start_ref.shape, 0)
    row = start_ref[...] + pos_ref[...]
    rows = [jnp.sum(jnp.where(eidx == idx_ref[k:k + 1, :], row, 0.0), axis=0, keepdims=True)
            for k in range(TOP_K)]
    dest_ref[...] = jnp.concatenate(rows, axis=0).astype(jnp.int32)


def _dest(idx, pos, start, tt):
    t = idx.shape[1]
    blk = pl.BlockSpec((TOP_K, tt), lambda i: (0, i))
    return pl.pallas_call(
        _dest_kernel,
        out_shape=jax.ShapeDtypeStruct((TOP_K, t), jnp.int32),
        grid=(t // tt,),
        in_specs=[blk, pl.BlockSpec((N_EXPERTS, tt), lambda i: (0, i)), pl.BlockSpec(start.shape, lambda i: (0, 0))],
        out_specs=blk,
        compiler_params=_params(("arbitrary",)),
        name="dest",
    )(idx, pos, start)


def _zero_tails_kernel(last_ref, o_ref):
    del last_ref
    o_ref[...] = jnp.zeros(o_ref.shape, o_ref.dtype)


def _zero_tails(last_blk, n_rows):
    return pl.pallas_call(
        _zero_tails_kernel,
        out_shape=jax.ShapeDtypeStruct((n_rows * ROW_SUB, LANES), jnp.uint32),
        grid_spec=pltpu.PrefetchScalarGridSpec(
            num_scalar_prefetch=1,
            grid=(last_blk.shape[0],),
            in_specs=[],
            out_specs=pl.BlockSpec((EXPERT_ROWS * ROW_SUB, LANES), lambda e, last: (last[e], 0))),
        compiler_params=_params(("arbitrary",)),
        name="zero_tails",
    )(last_blk)


def _dispatch_kernel(dest_ref, h_ref, xs_in_ref, xs_ref, sem):
    del xs_in_ref
    td = dest_ref.shape[1]

    def row_copy(t, k):
        return pltpu.make_async_copy(_row_tile(h_ref, t), _row_tile(xs_ref, dest_ref[k, t]), sem)

    def issue(t, carry):
        for k in range(TOP_K):
            row_copy(t, k).start()
        return carry

    lax.fori_loop(0, td, issue, 0)

    def drain(t, carry):
        for k in range(TOP_K):
            row_copy(t, k).wait()
        return carry

    lax.fori_loop(0, td, drain, 0)


def _dispatch(dest, h_rt, xs0, td):
    t = dest.shape[1]
    return pl.pallas_call(
        _dispatch_kernel,
        out_shape=jax.ShapeDtypeStruct(xs0.shape, xs0.dtype),
        grid=(t // td,),
        in_specs=[pl.BlockSpec((TOP_K, td), lambda i: (0, i), memory_space=pltpu.SMEM),
                  pl.BlockSpec((td * ROW_SUB, LANES), lambda i: (i, 0)),
                  pl.BlockSpec(memory_space=pl.ANY)],
        out_specs=pl.BlockSpec(memory_space=pl.ANY),
        scratch_shapes=[pltpu.SemaphoreType.DMA],
        input_output_aliases={2: 0},
        compiler_params=_params(("arbitrary",)),
        name="dispatch",
    )(dest, h_rt, xs0)


def _row_assignment(dest_flat, n_rows, t):
    info = plsc.get_sparse_core_info()
    lanes = info.num_lanes
    workers = info.num_cores * info.num_subcores
    own = n_rows // workers
    total = TOP_K * t
    chunk = min(8192, total)
    assert n_rows % (workers * lanes) == 0 and total % chunk == 0 and chunk % lanes == 0

    def body(dest_hbm, out_hbm, buf, part):
        wid = lax.axis_index("s") * info.num_cores + lax.axis_index("c")
        base = wid * own
        lane = lax.iota(jnp.int32, lanes)

        @pl.loop(0, own, step=lanes)
        def _(r):
            buf[pl.ds(r, lanes)] = jnp.full((lanes,), total, jnp.int32)

        @pl.loop(0, total, step=chunk)
        def _(c):
            pltpu.sync_copy(dest_hbm.at[pl.ds(c, chunk)], part)

            @pl.loop(0, chunk, step=lanes)
            def _(j):
                rel = part[pl.ds(j, lanes)] - base
                mine = jnp.logical_and(rel >= 0, rel < own)
                plsc.store_scatter(buf, [jnp.where(mine, rel, 0)], c + j + lane, mask=mine)

        pltpu.sync_copy(buf, out_hbm.at[pl.ds(base, own)])

    return pl.kernel(
        body,
        out_type=jax.ShapeDtypeStruct((n_rows,), jnp.int32),
        mesh=plsc.VectorSubcoreMesh(core_axis_name="c", subcore_axis_name="s"),
        scratch_types=[pltpu.VMEM((own,), jnp.int32), pltpu.VMEM((chunk,), jnp.int32)],
        compiler_params=pltpu.CompilerParams(needs_layout_passes=False),
        name="row_assignment",
    )(dest_flat)


def _experts_kernel(be_ref, nu_ref, prev_ref, cur_ref, x_ref, wg_ref, wu_ref, wd_ref, g_ref,
                    ybuf, sem, wgu_bf, wd_bf):
    i = pl.program_id(0)
    n_used = nu_ref[0]
    used = i < n_used
    slot = i % 2
    new_expert = jnp.logical_or(i == 0, be_ref[i] != be_ref[jnp.maximum(i - 1, 0)])

    def row_copy(s, r, a):
        return pltpu.make_async_copy(_row_tile(ybuf.at[s], r), _row_tile(g_ref, a), sem.at[s])

    def send(a_ref, s):
        for r in range(EXPERT_ROWS):
            row_copy(s, r, a_ref[0, 0, r]).start()

    def drain(s):
        for r in range(EXPERT_ROWS):
            row_copy(s, r, 0).wait()

    def compute():
        ff = wg_ref.shape[2]
        for r in range(0, EXPERT_ROWS, EXPERT_SUB):
            rows = pl.ds(r * ROW_SUB, EXPERT_SUB * ROW_SUB)
            x = _load_row_tiles(x_ref.at[rows, :], EXPERT_SUB).astype(BF16)
            gu = jnp.dot(x, wgu_bf[...], preferred_element_type=F32)
            gate = gu[:, :ff]
            act = (gate * _sigmoid(gate) * gu[:, ff:]).astype(BF16)
            _store_row_tiles(ybuf.at[slot, rows, :], jnp.dot(act, wd_bf[...], preferred_element_type=F32))

    @pl.when(jnp.logical_and(used, new_expert))
    def _():
        ff = wg_ref.shape[2]
        wgu_bf[:, :ff] = wg_ref[0].astype(BF16)
        wgu_bf[:, ff:] = wu_ref[0].astype(BF16)
        wd_bf[...] = wd_ref[0].astype(BF16)

    @pl.when(jnp.logical_and(used, i >= 2))
    def _():
        drain(slot)

    @pl.when(jnp.logical_and(used, i == 0))
    def _():
        compute()

    @pl.when(jnp.logical_and(used, i >= 1))
    def _():
        send(prev_ref, 1 - slot)
        compute()

    @pl.when(i == n_used - 1)
    def _():
        send(cur_ref, slot)
        drain(slot)

        @pl.when(i >= 1)
        def _():
            drain(1 - slot)


def _experts(blk_e, n_used, assign, xs, wg, wu, wd, t):
    d, ff = wg.shape[1], wg.shape[2]
    nblk = xs.shape[0] // (EXPERT_ROWS * ROW_SUB)
    last = lambda i, nu: jnp.minimum(i, nu[0] - 1)
    wsel = lambda i, be, nu: (be[last(i, nu)], 0, 0)
    return pl.pallas_call(
        _experts_kernel,
        out_shape=jax.ShapeDtypeStruct(((TOP_K * t + 8) * ROW_SUB, LANES), jnp.uint32),
        grid_spec=pltpu.PrefetchScalarGridSpec(
            num_scalar_prefetch=2,
            grid=(nblk,),
            in_specs=[pl.BlockSpec((1, 1, EXPERT_ROWS), lambda i, be, nu: (jnp.maximum(last(i, nu) - 1, 0), 0, 0),
                                   memory_space=pltpu.SMEM),
                      pl.BlockSpec((1, 1, EXPERT_ROWS), lambda i, be, nu: (last(i, nu), 0, 0),
                                   memory_space=pltpu.SMEM),
                      pl.BlockSpec((EXPERT_ROWS * ROW_SUB, LANES), lambda i, be, nu: (last(i, nu), 0)),
                      pl.BlockSpec((1, d, ff), wsel),
                      pl.BlockSpec((1, d, ff), wsel),
                      pl.BlockSpec((1, ff, d), wsel)],
            out_specs=pl.BlockSpec(memory_space=pl.ANY),
            scratch_shapes=[pltpu.VMEM((2, EXPERT_ROWS * ROW_SUB, LANES), jnp.uint32),
                            pltpu.SemaphoreType.DMA((2,)),
                            pltpu.VMEM((d, 2 * ff), BF16), pltpu.VMEM((ff, d), BF16)]),
        compiler_params=_params(("arbitrary",)),
        name="experts",
    )(blk_e, n_used, assign, assign, xs, wg, wu, wd)


def _combine_kernel(*refs):
    g_refs = refs[:TOP_K]
    w_ref, base_ref, mod_ref, fw_ref, o_ref = refs[TOP_K:]
    tc = base_ref.shape[1]
    w = w_ref[...]
    routed = w[:, 0:1] * _load_row_tiles(g_refs[0], tc)
    for k in range(1, TOP_K):
        routed = routed + w[:, k:k + 1] * _load_row_tiles(g_refs[k], tc)
    x = base_ref[0] + mod_ref[0, 5:6, :] * routed
    o_ref[0] = _rms(x, fw_ref[...])


def _combine(slots, wt, base, mods, fw, tc):
    b, l, d = base.shape
    per_b = l // tc
    steps = b * per_b
    slot_spec = lambda k: pl.BlockSpec((tc * ROW_SUB, LANES), lambda i: (k * steps + i, 0))
    return pl.pallas_call(
        _combine_kernel,
        out_shape=jax.ShapeDtypeStruct((b, l, d), F32),
        grid=(steps,),
        in_specs=[slot_spec(k) for k in range(TOP_K)] + [
            pl.BlockSpec((tc, TOP_K), lambda i: (i, 0)),
            pl.BlockSpec((1, tc, d), lambda i: (i // per_b, i % per_b, 0)),
            pl.BlockSpec((1, 6, d), lambda i: (i // per_b, 0, 0)),
            pl.BlockSpec((1, d), lambda i: (0, 0))],
        out_specs=pl.BlockSpec((1, tc, d), lambda i: (i // per_b, i % per_b, 0)),
        compiler_params=_params(("arbitrary",)),
        name="combine",
    )(*([slots] * TOP_K), wt, base, mods, fw)


def _rope_tables(l):
    t = jnp.arange(l, dtype=jnp.int32)
    row = (t // GRID_W).astype(F32)
    col = (t % GRID_W).astype(F32)
    n_freq = HEAD_DIM // 4
    inv = ROPE_THETA ** (-jnp.arange(n_freq, dtype=F32) / n_freq)
    ang = jnp.concatenate([row[:, None] * inv, col[:, None] * inv], axis=-1)
    cos = jnp.repeat(jnp.cos(ang), 2, axis=1)
    sin = jnp.repeat(jnp.sin(ang), 2, axis=1)
    sign = jnp.tile(jnp.array([-1.0, 1.0], F32), HEAD_DIM // 2)
    reps = LANES // HEAD_DIM
    return jnp.tile(cos, (1, reps)), jnp.tile(sin * sign, (1, reps))


def _filter_features(l):
    t = jnp.linspace(0.0, 1.0, l, dtype=F32)[:, None]
    bands = (FILTER_EMB - 1) // 2
    w = 2.0 * math.pi * jnp.arange(l, dtype=F32)[:, None] / l
    f = jnp.linspace(1e-4, bands - 1, bands, dtype=F32)[None, :]
    z = jnp.concatenate([t, jnp.cos(f * w), -jnp.sin(f * w)], axis=-1)
    min_decay = math.log(FILTER_TARGET) / FILTER_DECAY_FAST
    max_decay = math.log(FILTER_TARGET) / FILTER_DECAY_SLOW
    deltas = jnp.linspace(min_decay, max_decay, HYENA_WIDTH, dtype=F32)[None, :]
    return jnp.pad(z, ((0, 0), (0, LANES - FILTER_EMB))), deltas


def _dft_matrices(l):
    n = 2 * l
    idx = jnp.arange(l, dtype=jnp.int32)
    r = math.isqrt(l)
    assert r * r == l
    sub = jnp.arange(r, dtype=jnp.int32)
    hi = ((r * sub[:, None] * idx[None, :]) % n).astype(F32) * (2.0 * math.pi / n)
    lo = ((sub[:, None] * idx[None, :]) % n).astype(F32) * (2.0 * math.pi / n)
    ch, sh, cl, sl = jnp.cos(hi)[:, None, :], jnp.sin(hi)[:, None, :], jnp.cos(lo)[None], jnp.sin(lo)[None]
    c = (ch * cl - sh * sl).reshape(l, l)
    s = (sh * cl + ch * sl).reshape(l, l)
    alt = jnp.where(idx % 2 == 0, 1.0, -1.0).astype(F32)
    first = (idx == 0)[:, None]
    fwd = jnp.concatenate([c, jnp.where(first, alt[None, :], -s)], axis=0).astype(BF16)
    firstc = (idx == 0)[None, :]
    inv_r = (jnp.where(firstc, 1.0, 2.0) * c / n).astype(BF16)
    inv_i = (jnp.where(firstc, alt[:, None], -2.0 * s) / n).astype(BF16)
    return fwd, inv_r, inv_i


def _head_perm():
    order = []
    for j in range(N_HEADS // 2):
        order += list(range(j * HEAD_DIM, (j + 1) * HEAD_DIM))
        order += list(range((j + N_HEADS // 2) * HEAD_DIM, (j + 1 + N_HEADS // 2) * HEAD_DIM))
    return jnp.array(order, jnp.int32)


def _pad2(a, rows, cols):
    return jnp.pad(a, ((0, rows - a.shape[0]), (0, cols - a.shape[1])))


def kernel(x, c, ctx, c_ctx, mod_w, mod_b, norm1_w, w_in, q_norm_w, k_norm_w, conv_w, conv_b, filt_w1, filt_b1, filt_w2, filt_b2, filt_w3, filt_b3, filt_w4, filt_freq, hyena_bias, attn_out_norm_w, hyena_out_norm_w, w_out, norm2_w, router_w, router_bias, exp_w_gate, exp_w_up, exp_w_down, sh_w_gate, sh_w_up, sh_w_down, final_norm_w):
    b, l, d = x.shape
    t = b * l
    assert mod_w.shape[0] == 1, "single-layer stack"
    tl = min(512, l)

    cond = jnp.concatenate([c, c_ctx[None, :], jnp.zeros((-(b + 1) % 8, d), F32)], axis=0)
    mod = _adaln(cond, mod_w[0], mod_b[0][None, :])
    mods = mod[:b].reshape(b, 6, d)
    cmod = mod[b].reshape(6, d)

    perm = _head_perm()
    w_in0 = w_in[0]
    w_in_k = jnp.concatenate([w_in0[:, :Q_END][:, perm], w_in0[:, Q_END:]], axis=1).astype(BF16)
    w_kv = w_in0[:, Q_END:V_END].astype(BF16)
    gq = jnp.kron(jnp.eye(N_HEADS, dtype=F32), jnp.full((HEAD_DIM, HEAD_DIM), 1.0 / HEAD_DIM, F32)).astype(BF16)
    qnw = jnp.tile(q_norm_w[0], N_HEADS)[None, :]
    knw = jnp.tile(k_norm_w[0], N_KV_HEADS)[None, :]
    n1w = norm1_w[0][None, :]
    cos, sin = _rope_tables(l)

    kc, vc = _ctx_kv(ctx, cmod, n1w, w_kv, gq[:KV_WIDTH, :KV_WIDTH], knw)
    q, k, v, u = _inproj(x, mods, n1w, w_in_k, gq, qnw, knw, cos, sin, tl)
    k_all = jnp.concatenate([kc, k], axis=1)
    v_all = jnp.concatenate([vc, v], axis=1)
    an = _attention(q, k_all, v_all, attn_out_norm_w[0][perm][None, :], min(256, l), min(256, l))

    cw = conv_w[0].reshape(3, 3, HYENA_WIDTH).transpose(1, 0, 2)
    cb = conv_b[0].reshape(3, HYENA_WIDTH)
    gbf, x0 = _hyena_pre(u, cw, cb)
    z, deltas = _filter_features(l)
    fo = filt_w2.shape[1]
    hsd = _hyena_filter(
        z, _pad2(filt_w1[0], LANES, LANES), _pad2(filt_b1[0][None, :], 1, LANES),
        _pad2(filt_w2[0], LANES, LANES), _pad2(filt_b2[0][None, :], 1, LANES),
        _pad2(filt_w3[0], LANES, LANES), _pad2(filt_b3[0][None, :], 1, LANES),
        _pad2(filt_w4[0], LANES, 2 * HYENA_WIDTH), _pad2(filt_freq[0][None, :], 1, LANES), deltas, tl)
    del fo
    fwd, inv_r, inv_i = _dft_matrices(l)
    spec = _dft(fwd, hsd, tl)
    row0 = (jnp.arange(l) == 0)[:, None]
    sa = spec[0, :l]
    sd = jnp.where(row0, spec[0, l:l + 1], sa)
    sb = jnp.where(row0, 0.0, spec[1, l:])
    zr, zi = _dft_mul(fwd, gbf, sa, sb, sd, tl)
    yn = _idft(inv_r, inv_i, zr, zi, gbf, x0, hyena_bias[0][None, :], hyena_out_norm_w[0][None, :], tl)

    w_out0 = w_out[0]
    base, h2hi, h2lo, h2rt = _merge(
        an, yn, x, mods, w_out0[:ATTN_WIDTH][perm].astype(BF16), w_out0[ATTN_WIDTH:].astype(BF16),
        norm2_w[0][None, :], sh_w_gate[0].astype(BF16), sh_w_up[0].astype(BF16), sh_w_down[0].astype(BF16), tl)

    tt = 256
    rwt = router_w[0].T
    rw_hi = rwt.astype(BF16)
    rw_lo = (rwt - rw_hi.astype(F32)).astype(BF16)
    bias = jnp.broadcast_to(router_bias[0][:, None], (N_EXPERTS, tt))
    ti = jnp.arange(tt)
    tri = jnp.stack([(ti[:, None] < ti[None, :]), jnp.ones((tt, tt), bool)]).astype(BF16)
    idx, wgt, pos, cnt = _router(h2hi.reshape(t, d), h2lo.reshape(t, d), rw_hi, rw_lo, bias, tri, tt)

    counts = cnt[:, 0].astype(jnp.int32)
    padded = (counts + EXPERT_ROWS - 1) // EXPERT_ROWS * EXPERT_ROWS
    pad_end = jnp.cumsum(padded)
    pad_start = pad_end - padded
    dest = _dest(idx, pos, jnp.broadcast_to(pad_start.astype(F32)[:, None], (N_EXPERTS, tt)), tt)
    n_rows = (t * TOP_K + N_EXPERTS * (EXPERT_ROWS - 1) + EXPERT_ROWS - 1) // EXPERT_ROWS * EXPERT_ROWS
    nblk = n_rows // EXPERT_ROWS
    blk_row = jnp.arange(nblk, dtype=jnp.int32) * EXPERT_ROWS
    blk_e = jnp.minimum(jnp.sum((pad_end[None, :] <= blk_row[:, None]).astype(jnp.int32), axis=1), N_EXPERTS - 1)
    n_used = (pad_end[-1:] // EXPERT_ROWS).astype(jnp.int32)

    last_blk = jnp.maximum(pad_end // EXPERT_ROWS - 1, 0).astype(jnp.int32)
    xs = _dispatch(dest, h2rt, _zero_tails(last_blk, n_rows), min(512, t))
    assign = _row_assignment(dest.reshape(TOP_K * t), n_rows, t).reshape(nblk, 1, EXPERT_ROWS)
    slots = _experts(blk_e, n_used, assign, xs, exp_w_gate[0], exp_w_up[0], exp_w_down[0], t)
    return _combine(slots, wgt.T, base, mods, final_norm_w[None, :], min(256, l))
```

```python
import functools
import math

import jax
import jax.numpy as jnp
from jax import lax
from jax.experimental import pallas as pl
from jax.experimental.pallas import tpu as pltpu

F32 = jnp.float32
BF16 = jnp.bfloat16
HIGHEST = lax.Precision.HIGHEST

GRID_W = 64
N_HEADS = 8
N_KV_HEADS = 2
HEAD_DIM = 64
ATTN_WIDTH = N_HEADS * HEAD_DIM
KV_WIDTH = N_KV_HEADS * HEAD_DIM
HYENA_WIDTH = 512
Q_END = ATTN_WIDTH
K_END = Q_END + KV_WIDTH
V_END = K_END + KV_WIDTH
ROPE_THETA = 10000.0
FILTER_EMB = 33
FILTER_DECAY_FAST = 0.3
FILTER_DECAY_SLOW = 1.5
FILTER_TARGET = 1e-2
N_EXPERTS = 256
TOP_K = 8
N_GROUPS = 8
TOPK_GROUPS = 4
ROUTE_SCALE = 2.5
EPS = 1e-6

LANES = 128
ROW_SUB = 4
EXPERT_ROWS = 512
EXPERT_SUB = 256
NEG_INF = float("-inf")


def _params(semantics, vmem_mb=48):
    return pltpu.CompilerParams(dimension_semantics=semantics, vmem_limit_bytes=vmem_mb * 1024 * 1024)


def _rms(x, w):
    return x * lax.rsqrt(jnp.mean(x * x, axis=-1, keepdims=True) + EPS) * w


def _sigmoid(x):
    return 1.0 / (1.0 + jnp.exp(-x))


def _adaln_kernel(c_ref, w_ref, b_ref, o_ref):
    c = c_ref[...]
    s = c * _sigmoid(c)
    o_ref[...] = jnp.dot(s, w_ref[...], precision=HIGHEST, preferred_element_type=F32) + b_ref[...]


def _adaln(cond, w, b):
    rows, d = cond.shape
    n = w.shape[1]
    tn = 1536
    return pl.pallas_call(
        _adaln_kernel,
        out_shape=jax.ShapeDtypeStruct((rows, n), F32),
        grid=(n // tn,),
        in_specs=[pl.BlockSpec((rows, d), lambda j: (0, 0)),
                  pl.BlockSpec((d, tn), lambda j: (0, j)),
                  pl.BlockSpec((1, tn), lambda j: (0, j))],
        out_specs=pl.BlockSpec((rows, tn), lambda j: (0, j)),
        compiler_params=_params(("arbitrary",)),
        name="adaln",
    )(cond, w, b)


def _head_rms(t, gmat, w):
    ms = jnp.dot((t * t).astype(BF16), gmat, preferred_element_type=F32)
    return t * lax.rsqrt(ms + EPS) * w


def _modulated(x, norm_w, shift, scale):
    return _rms(x, norm_w) * (1.0 + scale) + shift


def _ctx_kv_kernel(ctx_ref, mod_ref, n1_ref, w_ref, g_ref, kn_ref, kc_ref, vc_ref):
    x = ctx_ref[0]
    h = _modulated(x, n1_ref[...], mod_ref[0:1, :], mod_ref[1:2, :])
    kv = jnp.dot(h.astype(BF16), w_ref[...], preferred_element_type=F32)
    k = _head_rms(kv[:, :KV_WIDTH], g_ref[...], kn_ref[...])
    v = kv[:, KV_WIDTH:]
    kc_ref[0] = k.astype(BF16)
    vc_ref[0] = jnp.concatenate([v, jnp.ones_like(v)], axis=1).astype(BF16)


def _ctx_kv(ctx, cmod, n1w, w_kv, gk, knw):
    b, c, d = ctx.shape
    return pl.pallas_call(
        _ctx_kv_kernel,
        out_shape=(jax.ShapeDtypeStruct((b, c, KV_WIDTH), BF16),
                   jax.ShapeDtypeStruct((b, c, 2 * KV_WIDTH), BF16)),
        grid=(b,),
        in_specs=[pl.BlockSpec((1, c, d), lambda i: (i, 0, 0)),
                  pl.BlockSpec(cmod.shape, lambda i: (0, 0)),
                  pl.BlockSpec((1, d), lambda i: (0, 0)),
                  pl.BlockSpec(w_kv.shape, lambda i: (0, 0)),
                  pl.BlockSpec(gk.shape, lambda i: (0, 0)),
                  pl.BlockSpec((1, KV_WIDTH), lambda i: (0, 0))],
        out_specs=(pl.BlockSpec((1, c, KV_WIDTH), lambda i: (i, 0, 0)),
                   pl.BlockSpec((1, c, 2 * KV_WIDTH), lambda i: (i, 0, 0))),
        compiler_params=_params(("arbitrary",)),
        name="ctx_kv",
    )(ctx, cmod, n1w, w_kv, gk, knw)


def _rope(t, cos, sin, even):
    width = t.shape[1]
    partner = jnp.where(even, pltpu.roll(t, width - 1, axis=1), pltpu.roll(t, 1, axis=1))
    return t * cos + partner * sin


def _inproj_kernel(x_ref, mod_ref, n1_ref, w_ref, gq_ref, qn_ref, kn_ref, cos_ref, sin_ref,
                   q_ref, k_ref, v_ref, u_ref):
    x = x_ref[0]
    h = _modulated(x, n1_ref[...], mod_ref[0, 0:1, :], mod_ref[0, 1:2, :])
    p = jnp.dot(h.astype(BF16), w_ref[...], preferred_element_type=F32)
    gq = gq_ref[...]
    q = _head_rms(p[:, :Q_END], gq, qn_ref[...])
    k = _head_rms(p[:, Q_END:K_END], gq[:KV_WIDTH, :KV_WIDTH], kn_ref[...])
    v = p[:, K_END:V_END]
    cos = cos_ref[...]
    sin = sin_ref[...]
    reps = Q_END // LANES
    cos_q = jnp.concatenate([cos] * reps, axis=1)
    sin_q = jnp.concatenate([sin] * reps, axis=1)
    even_q = (lax.broadcasted_iota(jnp.int32, (1, Q_END), 1) & 1) == 0
    even_k = (lax.broadcasted_iota(jnp.int32, (1, KV_WIDTH), 1) & 1) == 0
    q = _rope(q, cos_q, sin_q, even_q) * (HEAD_DIM ** -0.5)
    k = _rope(k, cos, sin, even_k)
    q_ref[0] = q.astype(BF16)
    k_ref[0] = k.astype(BF16)
    v_ref[0] = jnp.concatenate([v, jnp.ones_like(v)], axis=1).astype(BF16)
    u_ref[0] = p[:, V_END:].astype(BF16)


def _inproj(x, mods, n1w, w_in, gq, qnw, knw, cos, sin, tl):
    b, l, d = x.shape
    ncol = w_in.shape[1]
    nu = ncol - V_END
    return pl.pallas_call(
        _inproj_kernel,
        out_shape=(jax.ShapeDtypeStruct((b, l, Q_END), BF16),
                   jax.ShapeDtypeStruct((b, l, KV_WIDTH), BF16),
                   jax.ShapeDtypeStruct((b, l, 2 * KV_WIDTH), BF16),
                   jax.ShapeDtypeStruct((b, l, nu), BF16)),
        grid=(l // tl, b),
        in_specs=[pl.BlockSpec((1, tl, d), lambda i, j: (j, i, 0)),
                  pl.BlockSpec((1, 6, d), lambda i, j: (j, 0, 0)),
                  pl.BlockSpec((1, d), lambda i, j: (0, 0)),
                  pl.BlockSpec((d, ncol), lambda i, j: (0, 0)),
                  pl.BlockSpec(gq.shape, lambda i, j: (0, 0)),
                  pl.BlockSpec((1, Q_END), lambda i, j: (0, 0)),
                  pl.BlockSpec((1, KV_WIDTH), lambda i, j: (0, 0)),
                  pl.BlockSpec((tl, LANES), lambda i, j: (i, 0)),
                  pl.BlockSpec((tl, LANES), lambda i, j: (i, 0))],
        out_specs=(pl.BlockSpec((1, tl, Q_END), lambda i, j: (j, i, 0)),
                   pl.BlockSpec((1, tl, KV_WIDTH), lambda i, j: (j, i, 0)),
                   pl.BlockSpec((1, tl, 2 * KV_WIDTH), lambda i, j: (j, i, 0)),
                   pl.BlockSpec((1, tl, nu), lambda i, j: (j, i, 0))),
        compiler_params=_params(("arbitrary", "arbitrary")),
        name="inproj",
    )(x, mods, n1w, w_in, gq, qnw, knw, cos, sin)


def _attn_kernel(q_ref, k_ref, v_ref, wn_ref, o_ref, *, sub):
    tq = q_ref.shape[1]
    low = lax.broadcasted_iota(jnp.int32, (1, LANES), 1) < HEAD_DIM
    nt = (((1,), (1,)), ((), ()))
    kk = k_ref[0]
    vv = v_ref[0]
    for r in range(0, tq, sub):
        outs = []
        for j in range(Q_END // LANES):
            qv = q_ref[0, r:r + sub, LANES * j:LANES * (j + 1)]
            zero = jnp.zeros_like(qv)
            halves = []
            for g in range(N_KV_HEADS):
                qh = jnp.where(low, qv, zero) if g == 0 else jnp.where(low, zero, qv)
                s = lax.dot_general(qh, kk, nt, preferred_element_type=F32)
                p = jnp.exp(s - jnp.max(s, axis=-1, keepdims=True)).astype(BF16)
                pv = jnp.dot(p, vv, preferred_element_type=F32)
                halves.append(pv[:, :LANES] / pv[:, LANES:])
            outs.append(jnp.where(low, halves[0], halves[1]))
        a = jnp.concatenate(outs, axis=1)
        o_ref[0, r:r + sub, :] = _rms(a, wn_ref[...]).astype(BF16)


def _attention(q, k, v, wn, tq, sub):
    b, l, _ = q.shape
    n = k.shape[1]
    return pl.pallas_call(
        functools.partial(_attn_kernel, sub=sub),
        out_shape=jax.ShapeDtypeStruct((b, l, Q_END), BF16),
        grid=(b, l // tq),
        in_specs=[pl.BlockSpec((1, tq, Q_END), lambda i, j: (i, j, 0)),
                  pl.BlockSpec((1, n, KV_WIDTH), lambda i, j: (i, 0, 0)),
                  pl.BlockSpec((1, n, 2 * KV_WIDTH), lambda i, j: (i, 0, 0)),
                  pl.BlockSpec((1, Q_END), lambda i, j: (0, 0))],
        out_specs=pl.BlockSpec((1, tq, Q_END), lambda i, j: (i, j, 0)),
        compiler_params=_params(("arbitrary", "arbitrary")),
        name="attn",
    )(q, k, v, wn)


def _hyena_pre_kernel(u0_ref, u1_ref, u2_ref, cw_ref, cb_ref, g_ref, x0_ref):
    l = u0_ref.shape[1]
    row = lax.broadcasted_iota(jnp.int32, (l, LANES), 0)

    def conv(u_ref, gi):
        u = u_ref[0].astype(F32)
        prev = jnp.where(row == 0, 0.0, pltpu.roll(u, 1, axis=0))
        nxt = jnp.where(row == l - 1, 0.0, pltpu.roll(u, l - 1, axis=0))
        w = cw_ref[gi]
        return w[0:1] * prev + w[1:2] * u + w[2:3] * nxt + cb_ref[gi:gi + 1, :]

    x0 = conv(u0_ref, 0)
    x1 = conv(u1_ref, 1)
    v = conv(u2_ref, 2)
    g = v * x1
    g_ref[0] = g.astype(BF16)
    x0_ref[0] = x0.astype(BF16)


def _hyena_pre(u, cw, cb):
    b, l, _ = u.shape
    nblk = HYENA_WIDTH // LANES
    ublk = lambda gi: pl.BlockSpec((1, l, LANES), lambda i, j: (i, 0, gi * nblk + j))
    oblk = pl.BlockSpec((1, l, LANES), lambda i, j: (i, 0, j))
    return pl.pallas_call(
        _hyena_pre_kernel,
        out_shape=(jax.ShapeDtypeStruct((b, l, HYENA_WIDTH), BF16),
                   jax.ShapeDtypeStruct((b, l, HYENA_WIDTH), BF16)),
        grid=(b, nblk),
        in_specs=[ublk(0), ublk(1), ublk(2),
                  pl.BlockSpec((3, 3, LANES), lambda i, j: (0, 0, j)),
                  pl.BlockSpec((3, LANES), lambda i, j: (0, j))],
        out_specs=(oblk, oblk),
        compiler_params=_params(("arbitrary", "arbitrary")),
        name="hyena_pre",
    )(u, u, u, cw, cb)


def _filter_kernel(z_ref, w1_ref, b1_ref, w2_ref, b2_ref, w3_ref, b3_ref, w4_ref, fr_ref, dl_ref, o_ref):
    tl = z_ref.shape[0]
    z = z_ref[...]
    fr = fr_ref[...]
    dot = lambda a, w: jnp.dot(a, w, precision=HIGHEST, preferred_element_type=F32)
    h = jnp.sin(fr * (dot(z, w1_ref[...]) + b1_ref[...]))
    h = jnp.sin(fr * (dot(h, w2_ref[...]) + b2_ref[...]))
    h = jnp.sin(fr * (dot(h, w3_ref[...]) + b3_ref[...]))
    h = dot(h, w4_ref[...])
    t = z[:, 0:1]
    decay = jnp.exp(-t * jnp.abs(dl_ref[...]))
    hf = h[:, :HYENA_WIDTH] * decay
    hb = h[:, HYENA_WIDTH:] * decay
    row = lax.broadcasted_iota(jnp.int32, (tl, HYENA_WIDTH), 0) + pl.program_id(0) * tl
    hb = jnp.where(row == 0, 0.0, hb)
    o_ref[0] = hf + hb
    o_ref[1] = hf - hb


def _hyena_filter(z, w1, b1, w2, b2, w3, b3, w4, freq, deltas, tl):
    l = z.shape[0]
    full = lambda a: pl.BlockSpec(a.shape, lambda i: (0,) * a.ndim)
    return pl.pallas_call(
        _filter_kernel,
        out_shape=jax.ShapeDtypeStruct((2, l, HYENA_WIDTH), F32),
        grid=(l // tl,),
        in_specs=[pl.BlockSpec((tl, z.shape[1]), lambda i: (i, 0)),
                  full(w1), full(b1), full(w2), full(b2), full(w3), full(b3), full(w4), full(freq), full(deltas)],
        out_specs=pl.BlockSpec((2, tl, HYENA_WIDTH), lambda i: (0, i, 0)),
        compiler_params=_params(("arbitrary",)),
        name="hyena_filter",
    )(z, w1, b1, w2, b2, w3, b3, w4, freq, deltas)


def _dft_kernel(f_ref, x_ref, o_ref):
    o_ref[0] = jnp.dot(f_ref[...], x_ref[0].astype(BF16), preferred_element_type=F32)


def _dft(fmat, x, tf):
    nb, l, w = x.shape
    n = fmat.shape[0]
    return pl.pallas_call(
        _dft_kernel,
        out_shape=jax.ShapeDtypeStruct((nb, n, w), F32),
        grid=(n // tf, nb),
        in_specs=[pl.BlockSpec((tf, l), lambda i, j: (i, 0)),
                  pl.BlockSpec((1, l, w), lambda i, j: (j, 0, 0))],
        out_specs=pl.BlockSpec((1, tf, w), lambda i, j: (j, i, 0)),
        compiler_params=_params(("arbitrary", "arbitrary")),
        name="dft_filter",
    )(fmat, x)


def _dft_mul_kernel(fr_ref, fi_ref, x_ref, a_ref, b_ref, d_ref, zr_ref, zi_ref):
    x = x_ref[0]
    xr = jnp.dot(fr_ref[...], x, preferred_element_type=F32)
    xi = jnp.dot(fi_ref[...], x, preferred_element_type=F32)
    bb = b_ref[...]
    zr_ref[0] = (xr * a_ref[...] - xi * bb).astype(BF16)
    zi_ref[0] = (xr * bb + xi * d_ref[...]).astype(BF16)


def _dft_mul(fmat, g, sa, sb, sd, tf):
    b, l, w = g.shape
    nf = l // tf
    spec = pl.BlockSpec((tf, w), lambda i, j: (i, 0))
    return pl.pallas_call(
        _dft_mul_kernel,
        out_shape=(jax.ShapeDtypeStruct((b, l, w), BF16), jax.ShapeDtypeStruct((b, l, w), BF16)),
        grid=(nf, b),
        in_specs=[pl.BlockSpec((tf, l), lambda i, j: (i, 0)),
                  pl.BlockSpec((tf, l), lambda i, j: (i + nf, 0)),
                  pl.BlockSpec((1, l, w), lambda i, j: (j, 0, 0)),
                  spec, spec, spec],
        out_specs=(pl.BlockSpec((1, tf, w), lambda i, j: (j, i, 0)),
                   pl.BlockSpec((1, tf, w), lambda i, j: (j, i, 0))),
        compiler_params=_params(("arbitrary", "arbitrary")),
        name="dft_mul",
    )(fmat, fmat, g, sa, sb, sd)


def _idft_kernel(fr_ref, fi_ref, zr_ref, zi_ref, g_ref, x0_ref, hb_ref, wn_ref, o_ref):
    conv = (jnp.dot(fr_ref[...], zr_ref[0], preferred_element_type=F32)
            + jnp.dot(fi_ref[...], zi_ref[0], preferred_element_type=F32))
    y = (conv + g_ref[0].astype(F32) * hb_ref[...]) * x0_ref[0].astype(F32)
    o_ref[0] = _rms(y, wn_ref[...]).astype(BF16)


def _idft(finv_r, finv_i, zr, zi, g, x0, hbias, wn, tt):
    b, l, w = zr.shape
    tile = pl.BlockSpec((1, tt, w), lambda i, j: (j, i, 0))
    return pl.pallas_call(
        _idft_kernel,
        out_shape=jax.ShapeDtypeStruct((b, l, w), BF16),
        grid=(l // tt, b),
        in_specs=[pl.BlockSpec((tt, l), lambda i, j: (i, 0)),
                  pl.BlockSpec((tt, l), lambda i, j: (i, 0)),
                  pl.BlockSpec((1, l, w), lambda i, j: (j, 0, 0)),
                  pl.BlockSpec((1, l, w), lambda i, j: (j, 0, 0)),
                  tile, tile,
                  pl.BlockSpec((1, w), lambda i, j: (0, 0)),
                  pl.BlockSpec((1, w), lambda i, j: (0, 0))],
        out_specs=tile,
        compiler_params=_params(("arbitrary", "arbitrary")),
        name="idft",
    )(finv_r, finv_i, zr, zi, g, x0, hbias, wn)


def _bf16_bits(v):
    return lax.bitcast_convert_type(v.astype(BF16).astype(F32), jnp.uint32)


def _store_row_tiles(ref, val):
    rows, half = val.shape[0], val.shape[1] // 2
    assert half == ROW_SUB * LANES
    for j in range(ROW_SUB):
        lo = _bf16_bits(val[:, LANES * j:LANES * (j + 1)]) >> 16
        hi = _bf16_bits(val[:, half + LANES * j:half + LANES * (j + 1)]) & jnp.uint32(0xFFFF0000)
        ref[pl.ds(j, rows, stride=ROW_SUB), :] = lo | hi


def _load_row_tiles(ref, rows):
    words = [ref[pl.ds(j, rows, stride=ROW_SUB), :] for j in range(ROW_SUB)]
    lo = [lax.bitcast_convert_type(w << 16, F32) for w in words]
    hi = [lax.bitcast_convert_type(w & jnp.uint32(0xFFFF0000), F32) for w in words]
    return jnp.concatenate(lo + hi, axis=1)


def _merge_kernel(a_ref, y_ref, x_ref, mod_ref, wa_ref, wy_ref, n2_ref, sg_ref, su_ref, sd_ref,
                  base_ref, hi_ref, lo_ref, rt_ref):
    m = (jnp.dot(a_ref[0], wa_ref[...], preferred_element_type=F32)
         + jnp.dot(y_ref[0], wy_ref[...], preferred_element_type=F32))
    x1 = x_ref[0] + mod_ref[0, 2:3, :] * m
    h2 = _modulated(x1, n2_ref[...], mod_ref[0, 3:4, :], mod_ref[0, 4:5, :])
    hi = h2.astype(BF16)
    hi_ref[0] = hi
    lo_ref[0] = (h2 - hi.astype(F32)).astype(BF16)
    _store_row_tiles(rt_ref, h2)
    gate = jnp.dot(hi, sg_ref[...], preferred_element_type=F32)
    up = jnp.dot(hi, su_ref[...], preferred_element_type=F32)
    act = (gate * _sigmoid(gate) * up).astype(BF16)
    shared = jnp.dot(act, sd_ref[...], preferred_element_type=F32)
    base_ref[0] = x1 + mod_ref[0, 5:6, :] * shared


def _merge(an, yn, x, mods, wa, wy, n2w, sg, su, sd, tl):
    b, l, d = x.shape
    full = lambda a: pl.BlockSpec(a.shape, lambda i, j: (0,) * a.ndim)
    half = pl.BlockSpec((1, tl, an.shape[2]), lambda i, j: (i, j, 0))
    wide = pl.BlockSpec((1, tl, d), lambda i, j: (i, j, 0))
    per_b = l // tl
    return pl.pallas_call(
        _merge_kernel,
        out_shape=(jax.ShapeDtypeStruct((b, l, d), F32),
                   jax.ShapeDtypeStruct((b, l, d), BF16),
                   jax.ShapeDtypeStruct((b, l, d), BF16),
                   jax.ShapeDtypeStruct((b * l * ROW_SUB, LANES), jnp.uint32)),
        grid=(b, per_b),
        in_specs=[half, half, wide,
                  pl.BlockSpec((1, 6, d), lambda i, j: (i, 0, 0)),
                  full(wa), full(wy), full(n2w), full(sg), full(su), full(sd)],
        out_specs=(wide, wide, wide,
                   pl.BlockSpec((tl * ROW_SUB, LANES), lambda i, j: (i * per_b + j, 0))),
        compiler_params=_params(("arbitrary", "arbitrary")),
        name="merge",
    )(an, yn, x, mods, wa, wy, n2w, sg, su, sd)


def _router_kernel(hi_ref, lo_ref, whi_ref, wlo_ref, bias_ref, tri_ref,
                   idx_ref, wgt_ref, pos_ref, cnt_ref, run_ref):
    tt = hi_ref.shape[0]
    per_group = N_EXPERTS // N_GROUPS

    @pl.when(pl.program_id(0) == 0)
    def _():
        run_ref[...] = jnp.zeros_like(run_ref)

    nt = (((1,), (1,)), ((), ()))
    hi = hi_ref[...]
    whi = whi_ref[...]
    logits = (lax.dot_general(whi, hi, nt, preferred_element_type=F32)
              + lax.dot_general(whi, lo_ref[...], nt, preferred_element_type=F32)
              + lax.dot_general(wlo_ref[...], hi, nt, preferred_element_type=F32))
    scores = _sigmoid(logits)
    biased = scores + bias_ref[...]

    ridx = lax.broadcasted_iota(jnp.int32, (per_group, tt), 0)
    groups = [biased[g * per_group:(g + 1) * per_group, :] for g in range(N_GROUPS)]
    gs = []
    for blk in groups:
        m1 = jnp.max(blk, axis=0, keepdims=True)
        i1 = jnp.min(jnp.where(blk == m1, ridx, per_group), axis=0, keepdims=True)
        m2 = jnp.max(jnp.where(ridx == i1, NEG_INF, blk), axis=0, keepdims=True)
        gs.append(m1 + m2)

    kept = []
    for g in range(N_GROUPS):
        ahead = jnp.zeros((1, tt), F32)
        for o in range(N_GROUPS):
            if o != g:
                wins = (gs[o] >= gs[g]) if o < g else (gs[o] > gs[g])
                ahead = ahead + jnp.where(wins, 1.0, 0.0)
        kept.append(jnp.where(ahead < TOPK_GROUPS, groups[g], NEG_INF))
    cur = jnp.concatenate(kept, axis=0)

    eidx = lax.broadcasted_iota(jnp.int32, cur.shape, 0)
    kept_mask = cur
    picks = []
    wsel = []
    for _ in range(TOP_K):
        mx = jnp.max(cur, axis=0, keepdims=True)
        first = jnp.min(jnp.where(cur == mx, eidx, N_EXPERTS), axis=0, keepdims=True)
        sel = eidx == first
        picks.append(first)
        wsel.append(jnp.sum(jnp.where(sel, scores, 0.0), axis=0, keepdims=True))
        cur = jnp.where(sel, NEG_INF, cur)
    w = jnp.concatenate(wsel, axis=0)
    w = w / jnp.sum(w, axis=0, keepdims=True) * ROUTE_SCALE
    idx = jnp.concatenate(picks, axis=0)

    oh = jnp.where(cur == NEG_INF, jnp.where(kept_mask == NEG_INF, 0.0, 1.0), 0.0).astype(BF16)
    before = jnp.dot(oh, tri_ref[0], preferred_element_type=F32)
    total = jnp.dot(oh, tri_ref[1], preferred_element_type=F32)
    pos_ref[...] = run_ref[...] + before
    run_ref[...] = run_ref[...] + total

    idx_ref[...] = idx
    wgt_ref[...] = w
    cnt_ref[...] = run_ref[...]


def _router(hi, lo, whi, wlo, bias, tri, tt):
    t, d = hi.shape
    tok = pl.BlockSpec((tt, d), lambda i: (i, 0))
    full = lambda a: pl.BlockSpec(a.shape, lambda i: (0,) * a.ndim)
    out = pl.BlockSpec((TOP_K, tt), lambda i: (0, i))
    return pl.pallas_call(
        _router_kernel,
        out_shape=(jax.ShapeDtypeStruct((TOP_K, t), jnp.int32),
                   jax.ShapeDtypeStruct((TOP_K, t), F32),
                   jax.ShapeDtypeStruct((N_EXPERTS, t), F32),
                   jax.ShapeDtypeStruct((N_EXPERTS, tt), F32)),
        grid=(t // tt,),
        in_specs=[tok, tok, full(whi), full(wlo), full(bias), full(tri)],
        out_specs=(out, out, pl.BlockSpec((N_EXPERTS, tt), lambda i: (0, i)),
                   pl.BlockSpec((N_EXPERTS, tt), lambda i: (0, 0))),
        scratch_shapes=[pltpu.VMEM((N_EXPERTS, tt), F32)],
        compiler_params=_params(("arbitrary",)),
        name="router",
    )(hi, lo, whi, wlo, bias, tri)


def _row_tile(ref, r):
    return ref.at[pl.ds(pl.multiple_of(r * ROW_SUB, ROW_SUB), ROW_SUB), :]


def _dest_kernel(idx_ref, pos_ref, start_ref, dest_ref):
    eidx = lax.broadcasted_iota(jnp.int32, start_ref.shape, 0)
    row = start_ref[...] + pos_ref[...]
    rows = [jnp.sum(jnp.where(eidx == idx_ref[k:k + 1, :], row, 0.0), axis=0, keepdims=True)
            for k in range(TOP_K)]
    dest_ref[...] = jnp.concatenate(rows, axis=0).astype(jnp.int32)


def _dest(idx, pos, start, tt):
    t = idx.shape[1]
    blk = pl.BlockSpec((TOP_K, tt), lambda i: (0, i))
    return pl.pallas_call(
        _dest_kernel,
        out_shape=jax.ShapeDtypeStruct((TOP_K, t), jnp.int32),
        grid=(t // tt,),
        in_specs=[blk, pl.BlockSpec((N_EXPERTS, tt), lambda i: (0, i)), pl.BlockSpec(start.shape, lambda i: (0, 0))],
        out_specs=blk,
        compiler_params=_params(("arbitrary",)),
        name="dest",
    )(idx, pos, start)


def _zero_tails_kernel(last_ref, o_ref):
    del last_ref
    o_ref[...] = jnp.zeros(o_ref.shape, o_ref.dtype)


def _zero_tails(last_blk, n_rows):
    return pl.pallas_call(
        _zero_tails_kernel,
        out_shape=jax.ShapeDtypeStruct((n_rows * ROW_SUB, LANES), jnp.uint32),
        grid_spec=pltpu.PrefetchScalarGridSpec(
            num_scalar_prefetch=1,
            grid=(last_blk.shape[0],),
            in_specs=[],
            out_specs=pl.BlockSpec((EXPERT_ROWS * ROW_SUB, LANES), lambda e, last: (last[e], 0))),
        compiler_params=_params(("arbitrary",)),
        name="zero_tails",
    )(last_blk)


def _dispatch_kernel(dest_ref, h_ref, xs_in_ref, xs_ref, sem):
    del xs_in_ref
    td = dest_ref.shape[1]

    def row_copy(t, k):
        return pltpu.make_async_copy(_row_tile(h_ref, t), _row_tile(xs_ref, dest_ref[k, t]), sem)

    def issue(t, carry):
        for k in range(TOP_K):
            row_copy(t, k).start(priority=k % 2)
        return carry

    lax.fori_loop(0, td, issue, 0)

    def drain(t, carry):
        for k in range(TOP_K):
            row_copy(t, k).wait()
        return carry

    lax.fori_loop(0, td, drain, 0)


def _dispatch(dest, h_rt, xs0, td):
    t = dest.shape[1]
    return pl.pallas_call(
        _dispatch_kernel,
        out_shape=jax.ShapeDtypeStruct(xs0.shape, xs0.dtype),
        grid=(t // td,),
        in_specs=[pl.BlockSpec((TOP_K, td), lambda i: (0, i), memory_space=pltpu.SMEM),
                  pl.BlockSpec((td * ROW_SUB, LANES), lambda i: (i, 0)),
                  pl.BlockSpec(memory_space=pl.ANY)],
        out_specs=pl.BlockSpec(memory_space=pl.ANY),
        scratch_shapes=[pltpu.SemaphoreType.DMA],
        input_output_aliases={2: 0},
        compiler_params=_params(("arbitrary",)),
        name="dispatch",
    )(dest, h_rt, xs0)


def _experts_kernel(be_ref, nu_ref, x_ref, wg_ref, wu_ref, wd_ref, y_ref, wgu_bf, wd_bf):
    i = pl.program_id(0)
    used = i < nu_ref[0]
    new_expert = jnp.logical_or(i == 0, be_ref[i] != be_ref[jnp.maximum(i - 1, 0)])

    @pl.when(jnp.logical_and(used, new_expert))
    def _():
        ff = wg_ref.shape[2]
        wgu_bf[:, :ff] = wg_ref[0].astype(BF16)
        wgu_bf[:, ff:] = wu_ref[0].astype(BF16)
        wd_bf[...] = wd_ref[0].astype(BF16)

    @pl.when(used)
    def _():
        ff = wg_ref.shape[2]
        for r in range(0, EXPERT_ROWS, EXPERT_SUB):
            rows = pl.ds(r * ROW_SUB, EXPERT_SUB * ROW_SUB)
            x = _load_row_tiles(x_ref.at[rows, :], EXPERT_SUB).astype(BF16)
            gu = jnp.dot(x, wgu_bf[...], preferred_element_type=F32)
            gate = gu[:, :ff]
            act = (gate * _sigmoid(gate) * gu[:, ff:]).astype(BF16)
            _store_row_tiles(y_ref.at[rows, :], jnp.dot(act, wd_bf[...], preferred_element_type=F32))


def _experts(blk_e, n_used, xs, wg, wu, wd):
    d, ff = wg.shape[1], wg.shape[2]
    nblk = xs.shape[0] // (EXPERT_ROWS * ROW_SUB)
    row = lambda i, be, nu: (jnp.minimum(i, nu[0] - 1), 0)
    wsel = lambda i, be, nu: (be[jnp.minimum(i, nu[0] - 1)], 0, 0)
    return pl.pallas_call(
        _experts_kernel,
        out_shape=jax.ShapeDtypeStruct(xs.shape, xs.dtype),
        grid_spec=pltpu.PrefetchScalarGridSpec(
            num_scalar_prefetch=2,
            grid=(nblk,),
            in_specs=[pl.BlockSpec((EXPERT_ROWS * ROW_SUB, LANES), row),
                      pl.BlockSpec((1, d, ff), wsel),
                      pl.BlockSpec((1, d, ff), wsel),
                      pl.BlockSpec((1, ff, d), wsel)],
            out_specs=pl.BlockSpec((EXPERT_ROWS * ROW_SUB, LANES), row),
            scratch_shapes=[pltpu.VMEM((d, 2 * ff), BF16), pltpu.VMEM((ff, d), BF16)]),
        compiler_params=_params(("arbitrary",)),
        name="experts",
    )(blk_e, n_used, xs, wg, wu, wd)


def _combine_kernel(dest_ref, next_ref, ys_ref, w_ref, base_ref, mod_ref, fw_ref, o_ref, buf, sem):
    tc = dest_ref.shape[1]
    i = pl.program_id(0)
    slot = i % 2

    def row_copy(row, s, t, k):
        return pltpu.make_async_copy(_row_tile(ys_ref, row), _row_tile(buf.at[s, k], t), sem.at[s])

    def issue(d_ref, s):
        def body(t, carry):
            for k in range(TOP_K):
                row_copy(d_ref[k, t], s, t, k).start(priority=k % 2)
            return carry

        lax.fori_loop(0, tc, body, 0)

    @pl.when(i == 0)
    def _():
        issue(dest_ref, 0)

    @pl.when(i + 1 < pl.num_programs(0))
    def _():
        issue(next_ref, 1 - slot)

    def drain(t, carry):
        for k in range(TOP_K):
            row_copy(0, slot, t, k).wait()
        return carry

    lax.fori_loop(0, tc, drain, 0)

    w = w_ref[...]
    routed = w[:, 0:1] * _load_row_tiles(buf.at[slot, 0], tc)
    for k in range(1, TOP_K):
        routed = routed + w[:, k:k + 1] * _load_row_tiles(buf.at[slot, k], tc)
    x = base_ref[0] + mod_ref[0, 5:6, :] * routed
    o_ref[0] = _rms(x, fw_ref[...])


def _combine(dest, ys, wt, base, mods, fw, tc):
    b, l, d = base.shape
    per_b = l // tc
    steps = b * per_b
    return pl.pallas_call(
        _combine_kernel,
        out_shape=jax.ShapeDtypeStruct((b, l, d), F32),
        grid=(steps,),
        in_specs=[pl.BlockSpec((TOP_K, tc), lambda i: (0, i), memory_space=pltpu.SMEM),
                  pl.BlockSpec((TOP_K, tc), lambda i: (0, jnp.minimum(i + 1, steps - 1)), memory_space=pltpu.SMEM),
                  pl.BlockSpec(memory_space=pl.ANY),
                  pl.BlockSpec((tc, TOP_K), lambda i: (i, 0)),
                  pl.BlockSpec((1, tc, d), lambda i: (i // per_b, i % per_b, 0)),
                  pl.BlockSpec((1, 6, d), lambda i: (i // per_b, 0, 0)),
                  pl.BlockSpec((1, d), lambda i: (0, 0))],
        out_specs=pl.BlockSpec((1, tc, d), lambda i: (i // per_b, i % per_b, 0)),
        scratch_shapes=[pltpu.VMEM((2, TOP_K, tc * ROW_SUB, LANES), jnp.uint32), pltpu.SemaphoreType.DMA((2,))],
        compiler_params=_params(("arbitrary",)),
        name="combine",
    )(dest, dest, ys, wt, base, mods, fw)


def _rope_tables(l):
    t = jnp.arange(l, dtype=jnp.int32)
    row = (t // GRID_W).astype(F32)
    col = (t % GRID_W).astype(F32)
    n_freq = HEAD_DIM // 4
    inv = ROPE_THETA ** (-jnp.arange(n_freq, dtype=F32) / n_freq)
    ang = jnp.concatenate([row[:, None] * inv, col[:, None] * inv], axis=-1)
    cos = jnp.repeat(jnp.cos(ang), 2, axis=1)
    sin = jnp.repeat(jnp.sin(ang), 2, axis=1)
    sign = jnp.tile(jnp.array([-1.0, 1.0], F32), HEAD_DIM // 2)
    reps = LANES // HEAD_DIM
    return jnp.tile(cos, (1, reps)), jnp.tile(sin * sign, (1, reps))


def _filter_features(l):
    t = jnp.linspace(0.0, 1.0, l, dtype=F32)[:, None]
    bands = (FILTER_EMB - 1) // 2
    w = 2.0 * math.pi * jnp.arange(l, dtype=F32)[:, None] / l
    f = jnp.linspace(1e-4, bands - 1, bands, dtype=F32)[None, :]
    z = jnp.concatenate([t, jnp.cos(f * w), -jnp.sin(f * w)], axis=-1)
    min_decay = math.log(FILTER_TARGET) / FILTER_DECAY_FAST
    max_decay = math.log(FILTER_TARGET) / FILTER_DECAY_SLOW
    deltas = jnp.linspace(min_decay, max_decay, HYENA_WIDTH, dtype=F32)[None, :]
    return jnp.pad(z, ((0, 0), (0, LANES - FILTER_EMB))), deltas


def _dft_matrices(l):
    n = 2 * l
    idx = jnp.arange(l, dtype=jnp.int32)
    r = math.isqrt(l)
    assert r * r == l
    sub = jnp.arange(r, dtype=jnp.int32)
    hi = ((r * sub[:, None] * idx[None, :]) % n).astype(F32) * (2.0 * math.pi / n)
    lo = ((sub[:, None] * idx[None, :]) % n).astype(F32) * (2.0 * math.pi / n)
    ch, sh, cl, sl = jnp.cos(hi)[:, None, :], jnp.sin(hi)[:, None, :], jnp.cos(lo)[None], jnp.sin(lo)[None]
    c = (ch * cl - sh * sl).reshape(l, l)
    s = (sh * cl + ch * sl).reshape(l, l)
    alt = jnp.where(idx % 2 == 0, 1.0, -1.0).astype(F32)
    first = (idx == 0)[:, None]
    fwd = jnp.concatenate([c, jnp.where(first, alt[None, :], -s)], axis=0).astype(BF16)
    firstc = (idx == 0)[None, :]
    inv_r = (jnp.where(firstc, 1.0, 2.0) * c / n).astype(BF16)
    inv_i = (jnp.where(firstc, alt[:, None], -2.0 * s) / n).astype(BF16)
    return fwd, inv_r, inv_i


def _head_perm():
    order = []
    for j in range(N_HEADS // 2):
        order += list(range(j * HEAD_DIM, (j + 1) * HEAD_DIM))
        order += list(range((j + N_HEADS // 2) * HEAD_DIM, (j + 1 + N_HEADS // 2) * HEAD_DIM))
    return jnp.array(order, jnp.int32)


def _pad2(a, rows, cols):
    return jnp.pad(a, ((0, rows - a.shape[0]), (0, cols - a.shape[1])))


def kernel(x, c, ctx, c_ctx, mod_w, mod_b, norm1_w, w_in, q_norm_w, k_norm_w, conv_w, conv_b, filt_w1, filt_b1, filt_w2, filt_b2, filt_w3, filt_b3, filt_w4, filt_freq, hyena_bias, attn_out_norm_w, hyena_out_norm_w, w_out, norm2_w, router_w, router_bias, exp_w_gate, exp_w_up, exp_w_down, sh_w_gate, sh_w_up, sh_w_down, final_norm_w):
    b, l, d = x.shape
    t = b * l
    assert mod_w.shape[0] == 1, "single-layer stack"
    tl = min(512, l)

    cond = jnp.concatenate([c, c_ctx[None, :], jnp.zeros((-(b + 1) % 8, d), F32)], axis=0)
    mod = _adaln(cond, mod_w[0], mod_b[0][None, :])
    mods = mod[:b].reshape(b, 6, d)
    cmod = mod[b].reshape(6, d)

    perm = _head_perm()
    w_in0 = w_in[0]
    w_in_k = jnp.concatenate([w_in0[:, :Q_END][:, perm], w_in0[:, Q_END:]], axis=1).astype(BF16)
    w_kv = w_in0[:, Q_END:V_END].astype(BF16)
    gq = jnp.kron(jnp.eye(N_HEADS, dtype=F32), jnp.full((HEAD_DIM, HEAD_DIM), 1.0 / HEAD_DIM, F32)).astype(BF16)
    qnw = jnp.tile(q_norm_w[0], N_HEADS)[None, :]
    knw = jnp.tile(k_norm_w[0], N_KV_HEADS)[None, :]
    n1w = norm1_w[0][None, :]
    cos, sin = _rope_tables(l)

    kc, vc = _ctx_kv(ctx, cmod, n1w, w_kv, gq[:KV_WIDTH, :KV_WIDTH], knw)
    q, k, v, u = _inproj(x, mods, n1w, w_in_k, gq, qnw, knw, cos, sin, tl)
    k_all = jnp.concatenate([kc, k], axis=1)
    v_all = jnp.concatenate([vc, v], axis=1)
    an = _attention(q, k_all, v_all, attn_out_norm_w[0][perm][None, :], min(256, l), min(256, l))

    cw = conv_w[0].reshape(3, 3, HYENA_WIDTH).transpose(1, 0, 2)
    cb = conv_b[0].reshape(3, HYENA_WIDTH)
    gbf, x0 = _hyena_pre(u, cw, cb)
    z, deltas = _filter_features(l)
    fo = filt_w2.shape[1]
    hsd = _hyena_filter(
        z, _pad2(filt_w1[0], LANES, LANES), _pad2(filt_b1[0][None, :], 1, LANES),
        _pad2(filt_w2[0], LANES, LANES), _pad2(filt_b2[0][None, :], 1, LANES),
        _pad2(filt_w3[0], LANES, LANES), _pad2(filt_b3[0][None, :], 1, LANES),
        _pad2(filt_w4[0], LANES, 2 * HYENA_WIDTH), _pad2(filt_freq[0][None, :], 1, LANES), deltas, tl)
    del fo
    fwd, inv_r, inv_i = _dft_matrices(l)
    spec = _dft(fwd, hsd, tl)
    row0 = (jnp.arange(l) == 0)[:, None]
    sa = spec[0, :l]
    sd = jnp.where(row0, spec[0, l:l + 1], sa)
    sb = jnp.where(row0, 0.0, spec[1, l:])
    zr, zi = _dft_mul(fwd, gbf, sa, sb, sd, tl)
    yn = _idft(inv_r, inv_i, zr, zi, gbf, x0, hyena_bias[0][None, :], hyena_out_norm_w[0][None, :], tl)

    w_out0 = w_out[0]
    base, h2hi, h2lo, h2rt = _merge(
        an, yn, x, mods, w_out0[:ATTN_WIDTH][perm].astype(BF16), w_out0[ATTN_WIDTH:].astype(BF16),
        norm2_w[0][None, :], sh_w_gate[0].astype(BF16), sh_w_up[0].astype(BF16), sh_w_down[0].astype(BF16), tl)

    tt = 256
    rwt = router_w[0].T
    rw_hi = rwt.astype(BF16)
    rw_lo = (rwt - rw_hi.astype(F32)).astype(BF16)
    bias = jnp.broadcast_to(router_bias[0][:, None], (N_EXPERTS, tt))
    ti = jnp.arange(tt)
    tri = jnp.stack([(ti[:, None] < ti[None, :]), jnp.ones((tt, tt), bool)]).astype(BF16)
    idx, wgt, pos, cnt = _router(h2hi.reshape(t, d), h2lo.reshape(t, d), rw_hi, rw_lo, bias, tri, tt)

    counts = cnt[:, 0].astype(jnp.int32)
    padded = (counts + EXPERT_ROWS - 1) // EXPERT_ROWS * EXPERT_ROWS
    pad_end = jnp.cumsum(padded)
    pad_start = pad_end - padded
    td = min(1024, t)
    dest = _dest(idx, pos, jnp.broadcast_to(pad_start.astype(F32)[:, None], (N_EXPERTS, td)), td)
    n_rows = (t * TOP_K + N_EXPERTS * (EXPERT_ROWS - 1) + EXPERT_ROWS - 1) // EXPERT_ROWS * EXPERT_ROWS
    nblk = n_rows // EXPERT_ROWS
    blk_row = jnp.arange(nblk, dtype=jnp.int32) * EXPERT_ROWS
    blk_e = jnp.minimum(jnp.sum((pad_end[None, :] <= blk_row[:, None]).astype(jnp.int32), axis=1), N_EXPERTS - 1)
    n_used = (pad_end[-1:] // EXPERT_ROWS).astype(jnp.int32)

    last_blk = jnp.maximum(pad_end // EXPERT_ROWS - 1, 0).astype(jnp.int32)
    xs = _dispatch(dest, h2rt, _zero_tails(last_blk, n_rows), min(512, t))
    ys = _experts(blk_e, n_used, xs, exp_w_gate[0], exp_w_up[0], exp_w_down[0])
    return _combine(dest, ys, wgt.T, base, mods, final_norm_w[None, :], min(128, l))
```

```python
import functools
import math

import jax
import jax.numpy as jnp
from jax import lax
from jax.experimental import pallas as pl
from jax.experimental.pallas import tpu as pltpu

F32 = jnp.float32
BF16 = jnp.bfloat16
HIGHEST = lax.Precision.HIGHEST

GRID_W = 64
N_HEADS = 8
N_KV_HEADS = 2
HEAD_DIM = 64
ATTN_WIDTH = N_HEADS * HEAD_DIM
KV_WIDTH = N_KV_HEADS * HEAD_DIM
HYENA_WIDTH = 512
Q_END = ATTN_WIDTH
K_END = Q_END + KV_WIDTH
V_END = K_END + KV_WIDTH
ROPE_THETA = 10000.0
FILTER_EMB = 33
FILTER_DECAY_FAST = 0.3
FILTER_DECAY_SLOW = 1.5
FILTER_TARGET = 1e-2
N_EXPERTS = 256
TOP_K = 8
N_GROUPS = 8
TOPK_GROUPS = 4
ROUTE_SCALE = 2.5
EPS = 1e-6

LANES = 128
ROW_SUB = 4
EXPERT_ROWS = 512
EXPERT_SUB = 256
NEG_INF = float("-inf")


def _params(semantics, vmem_mb=48):
    return pltpu.CompilerParams(dimension_semantics=semantics, vmem_limit_bytes=vmem_mb * 1024 * 1024)


def _rms(x, w):
    return x * lax.rsqrt(jnp.mean(x * x, axis=-1, keepdims=True) + EPS) * w


def _sigmoid(x):
    return 1.0 / (1.0 + jnp.exp(-x))


def _adaln_kernel(c_ref, w_ref, b_ref, o_ref):
    c = c_ref[...]
    s = c * _sigmoid(c)
    o_ref[...] = jnp.dot(s, w_ref[...], precision=HIGHEST, preferred_element_type=F32) + b_ref[...]


def _adaln(cond, w, b):
    rows, d = cond.shape
    n = w.shape[1]
    tn = 1536
    return pl.pallas_call(
        _adaln_kernel,
        out_shape=jax.ShapeDtypeStruct((rows, n), F32),
        grid=(n // tn,),
        in_specs=[pl.BlockSpec((rows, d), lambda j: (0, 0)),
                  pl.BlockSpec((d, tn), lambda j: (0, j)),
                  pl.BlockSpec((1, tn), lambda j: (0, j))],
        out_specs=pl.BlockSpec((rows, tn), lambda j: (0, j)),
        compiler_params=_params(("arbitrary",)),
        name="adaln",
    )(cond, w, b)


def _head_rms(t, gmat, w):
    ms = jnp.dot((t * t).astype(BF16), gmat, preferred_element_type=F32)
    return t * lax.rsqrt(ms + EPS) * w


def _modulated(x, norm_w, shift, scale):
    return _rms(x, norm_w) * (1.0 + scale) + shift


def _ctx_kv_kernel(ctx_ref, mod_ref, n1_ref, w_ref, g_ref, kn_ref, kc_ref, vc_ref):
    x = ctx_ref[0]
    h = _modulated(x, n1_ref[...], mod_ref[0:1, :], mod_ref[1:2, :])
    kv = jnp.dot(h.astype(BF16), w_ref[...], preferred_element_type=F32)
    k = _head_rms(kv[:, :KV_WIDTH], g_ref[...], kn_ref[...])
    v = kv[:, KV_WIDTH:]
    kc_ref[0] = k.astype(BF16)
    vc_ref[0] = jnp.concatenate([v, jnp.ones_like(v)], axis=1).astype(BF16)


def _ctx_kv(ctx, cmod, n1w, w_kv, gk, knw):
    b, c, d = ctx.shape
    return pl.pallas_call(
        _ctx_kv_kernel,
        out_shape=(jax.ShapeDtypeStruct((b, c, KV_WIDTH), BF16),
                   jax.ShapeDtypeStruct((b, c, 2 * KV_WIDTH), BF16)),
        grid=(b,),
        in_specs=[pl.BlockSpec((1, c, d), lambda i: (i, 0, 0)),
                  pl.BlockSpec(cmod.shape, lambda i: (0, 0)),
                  pl.BlockSpec((1, d), lambda i: (0, 0)),
                  pl.BlockSpec(w_kv.shape, lambda i: (0, 0)),
                  pl.BlockSpec(gk.shape, lambda i: (0, 0)),
                  pl.BlockSpec((1, KV_WIDTH), lambda i: (0, 0))],
        out_specs=(pl.BlockSpec((1, c, KV_WIDTH), lambda i: (i, 0, 0)),
                   pl.BlockSpec((1, c, 2 * KV_WIDTH), lambda i: (i, 0, 0))),
        compiler_params=_params(("arbitrary",)),
        name="ctx_kv",
    )(ctx, cmod, n1w, w_kv, gk, knw)


def _rope(t, cos, sin, even):
    width = t.shape[1]
    partner = jnp.where(even, pltpu.roll(t, width - 1, axis=1), pltpu.roll(t, 1, axis=1))
    return t * cos + partner * sin


def _inproj_kernel(x_ref, mod_ref, n1_ref, w_ref, gq_ref, qn_ref, kn_ref, cos_ref, sin_ref,
                   q_ref, k_ref, v_ref, u_ref):
    x = x_ref[0]
    h = _modulated(x, n1_ref[...], mod_ref[0, 0:1, :], mod_ref[0, 1:2, :])
    p = jnp.dot(h.astype(BF16), w_ref[...], preferred_element_type=F32)
    gq = gq_ref[...]
    q = _head_rms(p[:, :Q_END], gq, qn_ref[...])
    k = _head_rms(p[:, Q_END:K_END], gq[:KV_WIDTH, :KV_WIDTH], kn_ref[...])
    v = p[:, K_END:V_END]
    cos = cos_ref[...]
    sin = sin_ref[...]
    reps = Q_END // LANES
    cos_q = jnp.concatenate([cos] * reps, axis=1)
    sin_q = jnp.concatenate([sin] * reps, axis=1)
    even_q = (lax.broadcasted_iota(jnp.int32, (1, Q_END), 1) & 1) == 0
    even_k = (lax.broadcasted_iota(jnp.int32, (1, KV_WIDTH), 1) & 1) == 0
    q = _rope(q, cos_q, sin_q, even_q) * (HEAD_DIM ** -0.5)
    k = _rope(k, cos, sin, even_k)
    q_ref[0] = q.astype(BF16)
    k_ref[0] = k.astype(BF16)
    v_ref[0] = jnp.concatenate([v, jnp.ones_like(v)], axis=1).astype(BF16)
    u_ref[0] = p[:, V_END:].astype(BF16)


def _inproj(x, mods, n1w, w_in, gq, qnw, knw, cos, sin, tl):
    b, l, d = x.shape
    ncol = w_in.shape[1]
    nu = ncol - V_END
    return pl.pallas_call(
        _inproj_kernel,
        out_shape=(jax.ShapeDtypeStruct((b, l, Q_END), BF16),
                   jax.ShapeDtypeStruct((b, l, KV_WIDTH), BF16),
                   jax.ShapeDtypeStruct((b, l, 2 * KV_WIDTH), BF16),
                   jax.ShapeDtypeStruct((b, l, nu), BF16)),
        grid=(l // tl, b),
        in_specs=[pl.BlockSpec((1, tl, d), lambda i, j: (j, i, 0)),
                  pl.BlockSpec((1, 6, d), lambda i, j: (j, 0, 0)),
                  pl.BlockSpec((1, d), lambda i, j: (0, 0)),
                  pl.BlockSpec((d, ncol), lambda i, j: (0, 0)),
                  pl.BlockSpec(gq.shape, lambda i, j: (0, 0)),
                  pl.BlockSpec((1, Q_END), lambda i, j: (0, 0)),
                  pl.BlockSpec((1, KV_WIDTH), lambda i, j: (0, 0)),
                  pl.BlockSpec((tl, LANES), lambda i, j: (i, 0)),
                  pl.BlockSpec((tl, LANES), lambda i, j: (i, 0))],
        out_specs=(pl.BlockSpec((1, tl, Q_END), lambda i, j: (j, i, 0)),
                   pl.BlockSpec((1, tl, KV_WIDTH), lambda i, j: (j, i, 0)),
                   pl.BlockSpec((1, tl, 2 * KV_WIDTH), lambda i, j: (j, i, 0)),
                   pl.BlockSpec((1, tl, nu), lambda i, j: (j, i, 0))),
        compiler_params=_params(("arbitrary", "arbitrary")),
        name="inproj",
    )(x, mods, n1w, w_in, gq, qnw, knw, cos, sin)


def _attn_kernel(q_ref, k_ref, v_ref, wn_ref, o_ref, *, sub):
    tq = q_ref.shape[1]
    low = lax.broadcasted_iota(jnp.int32, (1, LANES), 1) < HEAD_DIM
    nt = (((1,), (1,)), ((), ()))
    kk = k_ref[0]
    vv = v_ref[0]
    for r in range(0, tq, sub):
        outs = []
        for j in range(Q_END // LANES):
            qv = q_ref[0, r:r + sub, LANES * j:LANES * (j + 1)]
            zero = jnp.zeros_like(qv)
            halves = []
            for g in range(N_KV_HEADS):
                qh = jnp.where(low, qv, zero) if g == 0 else jnp.where(low, zero, qv)
                s = lax.dot_general(qh, kk, nt, preferred_element_type=F32)
                p = jnp.exp(s - jnp.max(s, axis=-1, keepdims=True)).astype(BF16)
                pv = jnp.dot(p, vv, preferred_element_type=F32)
                halves.append(pv[:, :LANES] / pv[:, LANES:])
            outs.append(jnp.where(low, halves[0], halves[1]))
        a = jnp.concatenate(outs, axis=1)
        o_ref[0, r:r + sub, :] = _rms(a, wn_ref[...]).astype(BF16)


def _attention(q, k, v, wn, tq, sub):
    b, l, _ = q.shape
    n = k.shape[1]
    return pl.pallas_call(
        functools.partial(_attn_kernel, sub=sub),
        out_shape=jax.ShapeDtypeStruct((b, l, Q_END), BF16),
        grid=(b, l // tq),
        in_specs=[pl.BlockSpec((1, tq, Q_END), lambda i, j: (i, j, 0)),
                  pl.BlockSpec((1, n, KV_WIDTH), lambda i, j: (i, 0, 0)),
                  pl.BlockSpec((1, n, 2 * KV_WIDTH), lambda i, j: (i, 0, 0)),
                  pl.BlockSpec((1, Q_END), lambda i, j: (0, 0))],
        out_specs=pl.BlockSpec((1, tq, Q_END), lambda i, j: (i, j, 0)),
        compiler_params=_params(("arbitrary", "arbitrary")),
        name="attn",
    )(q, k, v, wn)


def _hyena_pre_kernel(u0_ref, u1_ref, u2_ref, cw_ref, cb_ref, g_ref, x0_ref):
    l = u0_ref.shape[1]
    row = lax.broadcasted_iota(jnp.int32, (l, LANES), 0)

    def conv(u_ref, gi):
        u = u_ref[0].astype(F32)
        prev = jnp.where(row == 0, 0.0, pltpu.roll(u, 1, axis=0))
        nxt = jnp.where(row == l - 1, 0.0, pltpu.roll(u, l - 1, axis=0))
        w = cw_ref[gi]
        return w[0:1] * prev + w[1:2] * u + w[2:3] * nxt + cb_ref[gi:gi + 1, :]

    x0 = conv(u0_ref, 0)
    x1 = conv(u1_ref, 1)
    v = conv(u2_ref, 2)
    g = v * x1
    g_ref[0] = g.astype(BF16)
    x0_ref[0] = x0.astype(BF16)


def _hyena_pre(u, cw, cb):
    b, l, _ = u.shape
    nblk = HYENA_WIDTH // LANES
    ublk = lambda gi: pl.BlockSpec((1, l, LANES), lambda i, j: (i, 0, gi * nblk + j))
    oblk = pl.BlockSpec((1, l, LANES), lambda i, j: (i, 0, j))
    return pl.pallas_call(
        _hyena_pre_kernel,
        out_shape=(jax.ShapeDtypeStruct((b, l, HYENA_WIDTH), BF16),
                   jax.ShapeDtypeStruct((b, l, HYENA_WIDTH), BF16)),
        grid=(b, nblk),
        in_specs=[ublk(0), ublk(1), ublk(2),
                  pl.BlockSpec((3, 3, LANES), lambda i, j: (0, 0, j)),
                  pl.BlockSpec((3, LANES), lambda i, j: (0, j))],
        out_specs=(oblk, oblk),
        compiler_params=_params(("arbitrary", "arbitrary")),
        name="hyena_pre",
    )(u, u, u, cw, cb)


def _filter_kernel(z_ref, w1_ref, b1_ref, w2_ref, b2_ref, w3_ref, b3_ref, w4_ref, fr_ref, dl_ref, o_ref):
    tl = z_ref.shape[0]
    z = z_ref[...]
    fr = fr_ref[...]
    dot = lambda a, w: jnp.dot(a, w, precision=HIGHEST, preferred_element_type=F32)
    h = jnp.sin(fr * (dot(z, w1_ref[...]) + b1_ref[...]))
    h = jnp.sin(fr * (dot(h, w2_ref[...]) + b2_ref[...]))
    h = jnp.sin(fr * (dot(h, w3_ref[...]) + b3_ref[...]))
    h = dot(h, w4_ref[...])
    t = z[:, 0:1]
    decay = jnp.exp(-t * jnp.abs(dl_ref[...]))
    hf = h[:, :HYENA_WIDTH] * decay
    hb = h[:, HYENA_WIDTH:] * decay
    row = lax.broadcasted_iota(jnp.int32, (tl, HYENA_WIDTH), 0) + pl.program_id(0) * tl
    hb = jnp.where(row == 0, 0.0, hb)
    o_ref[0] = hf + hb
    o_ref[1] = hf - hb


def _hyena_filter(z, w1, b1, w2, b2, w3, b3, w4, freq, deltas, tl):
    l = z.shape[0]
    full = lambda a: pl.BlockSpec(a.shape, lambda i: (0,) * a.ndim)
    return pl.pallas_call(
        _filter_kernel,
        out_shape=jax.ShapeDtypeStruct((2, l, HYENA_WIDTH), F32),
        grid=(l // tl,),
        in_specs=[pl.BlockSpec((tl, z.shape[1]), lambda i: (i, 0)),
                  full(w1), full(b1), full(w2), full(b2), full(w3), full(b3), full(w4), full(freq), full(deltas)],
        out_specs=pl.BlockSpec((2, tl, HYENA_WIDTH), lambda i: (0, i, 0)),
        compiler_params=_params(("arbitrary",)),
        name="hyena_filter",
    )(z, w1, b1, w2, b2, w3, b3, w4, freq, deltas)


def _dft_kernel(f_ref, x_ref, o_ref):
    o_ref[0] = jnp.dot(f_ref[...], x_ref[0].astype(BF16), preferred_element_type=F32)


def _dft(fmat, x, tf):
    nb, l, w = x.shape
    n = fmat.shape[0]
    return pl.pallas_call(
        _dft_kernel,
        out_shape=jax.ShapeDtypeStruct((nb, n, w), F32),
        grid=(n // tf, nb),
        in_specs=[pl.BlockSpec((tf, l), lambda i, j: (i, 0)),
                  pl.BlockSpec((1, l, w), lambda i, j: (j, 0, 0))],
        out_specs=pl.BlockSpec((1, tf, w), lambda i, j: (j, i, 0)),
        compiler_params=_params(("arbitrary", "arbitrary")),
        name="dft_filter",
    )(fmat, x)


def _dft_mul_kernel(fr_ref, fi_ref, x_ref, a_ref, b_ref, d_ref, zr_ref, zi_ref):
    x = x_ref[0]
    xr = jnp.dot(fr_ref[...], x, preferred_element_type=F32)
    xi = jnp.dot(fi_ref[...], x, preferred_element_type=F32)
    bb = b_ref[...]
    zr_ref[0] = (xr * a_ref[...] - xi * bb).astype(BF16)
    zi_ref[0] = (xr * bb + xi * d_ref[...]).astype(BF16)


def _dft_mul(fmat, g, sa, sb, sd, tf):
    b, l, w = g.shape
    nf = l // tf
    spec = pl.BlockSpec((tf, w), lambda i, j: (i, 0))
    return pl.pallas_call(
        _dft_mul_kernel,
        out_shape=(jax.ShapeDtypeStruct((b, l, w), BF16), jax.ShapeDtypeStruct((b, l, w), BF16)),
        grid=(nf, b),
        in_specs=[pl.BlockSpec((tf, l), lambda i, j: (i, 0)),
                  pl.BlockSpec((tf, l), lambda i, j: (i + nf, 0)),
                  pl.BlockSpec((1, l, w), lambda i, j: (j, 0, 0)),
                  spec, spec, spec],
        out_specs=(pl.BlockSpec((1, tf, w), lambda i, j: (j, i, 0)),
                   pl.BlockSpec((1, tf, w), lambda i, j: (j, i, 0))),
        compiler_params=_params(("arbitrary", "arbitrary")),
        name="dft_mul",
    )(fmat, fmat, g, sa, sb, sd)


def _idft_kernel(fr_ref, fi_ref, zr_ref, zi_ref, g_ref, x0_ref, hb_ref, wn_ref, o_ref):
    conv = (jnp.dot(fr_ref[...], zr_ref[0], preferred_element_type=F32)
            + jnp.dot(fi_ref[...], zi_ref[0], preferred_element_type=F32))
    y = (conv + g_ref[0].astype(F32) * hb_ref[...]) * x0_ref[0].astype(F32)
    o_ref[0] = _rms(y, wn_ref[...]).astype(BF16)


def _idft(finv_r, finv_i, zr, zi, g, x0, hbias, wn, tt):
    b, l, w = zr.shape
    tile = pl.BlockSpec((1, tt, w), lambda i, j: (j, i, 0))
    return pl.pallas_call(
        _idft_kernel,
        out_shape=jax.ShapeDtypeStruct((b, l, w), BF16),
        grid=(l // tt, b),
        in_specs=[pl.BlockSpec((tt, l), lambda i, j: (i, 0)),
                  pl.BlockSpec((tt, l), lambda i, j: (i, 0)),
                  pl.BlockSpec((1, l, w), lambda i, j: (j, 0, 0)),
                  pl.BlockSpec((1, l, w), lambda i, j: (j, 0, 0)),
                  tile, tile,
                  pl.BlockSpec((1, w), lambda i, j: (0, 0)),
                  pl.BlockSpec((1, w), lambda i, j: (0, 0))],
        out_specs=tile,
        compiler_params=_params(("arbitrary", "arbitrary")),
        name="idft",
    )(finv_r, finv_i, zr, zi, g, x0, hbias, wn)


def _bf16_bits(v):
    return lax.bitcast_convert_type(v.astype(BF16).astype(F32), jnp.uint32)


def _store_row_tiles(ref, val):
    rows, half = val.shape[0], val.shape[1] // 2
    assert half == ROW_SUB * LANES
    for j in range(ROW_SUB):
        lo = _bf16_bits(val[:, LANES * j:LANES * (j + 1)]) >> 16
        hi = _bf16_bits(val[:, half + LANES * j:half + LANES * (j + 1)]) & jnp.uint32(0xFFFF0000)
        ref[pl.ds(j, rows, stride=ROW_SUB), :] = lo | hi


def _load_row_tiles(ref, rows):
    words = [ref[pl.ds(j, rows, stride=ROW_SUB), :] for j in range(ROW_SUB)]
    lo = [lax.bitcast_convert_type(w << 16, F32) for w in words]
    hi = [lax.bitcast_convert_type(w & jnp.uint32(0xFFFF0000), F32) for w in words]
    return jnp.concatenate(lo + hi, axis=1)


def _merge_kernel(a_ref, y_ref, x_ref, mod_ref, wa_ref, wy_ref, n2_ref, sg_ref, su_ref, sd_ref,
                  base_ref, hi_ref, lo_ref, rt_ref):
    m = (jnp.dot(a_ref[0], wa_ref[...], preferred_element_type=F32)
         + jnp.dot(y_ref[0], wy_ref[...], preferred_element_type=F32))
    x1 = x_ref[0] + mod_ref[0, 2:3, :] * m
    h2 = _modulated(x1, n2_ref[...], mod_ref[0, 3:4, :], mod_ref[0, 4:5, :])
    hi = h2.astype(BF16)
    hi_ref[0] = hi
    lo_ref[0] = (h2 - hi.astype(F32)).astype(BF16)
    _store_row_tiles(rt_ref, h2)
    gate = jnp.dot(hi, sg_ref[...], preferred_element_type=F32)
    up = jnp.dot(hi, su_ref[...], preferred_element_type=F32)
    act = (gate * _sigmoid(gate) * up).astype(BF16)
    shared = jnp.dot(act, sd_ref[...], preferred_element_type=F32)
    base_ref[0] = x1 + mod_ref[0, 5:6, :] * shared


def _merge(an, yn, x, mods, wa, wy, n2w, sg, su, sd, tl):
    b, l, d = x.shape
    full = lambda a: pl.BlockSpec(a.shape, lambda i, j: (0,) * a.ndim)
    half = pl.BlockSpec((1, tl, an.shape[2]), lambda i, j: (i, j, 0))
    wide = pl.BlockSpec((1, tl, d), lambda i, j: (i, j, 0))
    per_b = l // tl
    return pl.pallas_call(
        _merge_kernel,
        out_shape=(jax.ShapeDtypeStruct((b, l, d), F32),
                   jax.ShapeDtypeStruct((b, l, d), BF16),
                   jax.ShapeDtypeStruct((b, l, d), BF16),
                   jax.ShapeDtypeStruct((b * l * ROW_SUB, LANES), jnp.uint32)),
        grid=(b, per_b),
        in_specs=[half, half, wide,
                  pl.BlockSpec((1, 6, d), lambda i, j: (i, 0, 0)),
                  full(wa), full(wy), full(n2w), full(sg), full(su), full(sd)],
        out_specs=(wide, wide, wide,
                   pl.BlockSpec((tl * ROW_SUB, LANES), lambda i, j: (i * per_b + j, 0))),
        compiler_params=_params(("arbitrary", "arbitrary")),
        name="merge",
    )(an, yn, x, mods, wa, wy, n2w, sg, su, sd)


def _router_kernel(hi_ref, lo_ref, whi_ref, wlo_ref, bias_ref, tri_ref,
                   idx_ref, wgt_ref, pos_ref, cnt_ref, run_ref):
    tt = hi_ref.shape[0]
    per_group = N_EXPERTS // N_GROUPS

    @pl.when(pl.program_id(0) == 0)
    def _():
        run_ref[...] = jnp.zeros_like(run_ref)

    nt = (((1,), (1,)), ((), ()))
    hi = hi_ref[...]
    whi = whi_ref[...]
    logits = (lax.dot_general(whi, hi, nt, preferred_element_type=F32)
              + lax.dot_general(whi, lo_ref[...], nt, preferred_element_type=F32)
              + lax.dot_general(wlo_ref[...], hi, nt, preferred_element_type=F32))
    scores = _sigmoid(logits)
    biased = scores + bias_ref[...]

    ridx = lax.broadcasted_iota(jnp.int32, (per_group, tt), 0)
    groups = [biased[g * per_group:(g + 1) * per_group, :] for g in range(N_GROUPS)]
    gs = []
    for blk in groups:
        m1 = jnp.max(blk, axis=0, keepdims=True)
        i1 = jnp.min(jnp.where(blk == m1, ridx, per_group), axis=0, keepdims=True)
        m2 = jnp.max(jnp.where(ridx == i1, NEG_INF, blk), axis=0, keepdims=True)
        gs.append(m1 + m2)

    kept = []
    for g in range(N_GROUPS):
        ahead = jnp.zeros((1, tt), F32)
        for o in range(N_GROUPS):
            if o != g:
                wins = (gs[o] >= gs[g]) if o < g else (gs[o] > gs[g])
                ahead = ahead + jnp.where(wins, 1.0, 0.0)
        kept.append(jnp.where(ahead < TOPK_GROUPS, groups[g], NEG_INF))
    cur = jnp.concatenate(kept, axis=0)

    eidx = lax.broadcasted_iota(jnp.int32, cur.shape, 0)
    kept_mask = cur
    picks = []
    wsel = []
    for _ in range(TOP_K):
        mx = jnp.max(cur, axis=0, keepdims=True)
        first = jnp.min(jnp.where(cur == mx, eidx, N_EXPERTS), axis=0, keepdims=True)
        sel = eidx == first
        picks.append(first)
        wsel.append(jnp.sum(jnp.where(sel, scores, 0.0), axis=0, keepdims=True))
        cur = jnp.where(sel, NEG_INF, cur)
    w = jnp.concatenate(wsel, axis=0)
    w = w / jnp.sum(w, axis=0, keepdims=True) * ROUTE_SCALE
    idx = jnp.concatenate(picks, axis=0)

    oh = jnp.where(cur == NEG_INF, jnp.where(kept_mask == NEG_INF, 0.0, 1.0), 0.0).astype(BF16)
    before = jnp.dot(oh, tri_ref[0], preferred_element_type=F32)
    total = jnp.dot(oh, tri_ref[1], preferred_element_type=F32)
    pos_ref[...] = run_ref[...] + before
    run_ref[...] = run_ref[...] + total

    idx_ref[...] = idx
    wgt_ref[...] = w
    cnt_ref[...] = run_ref[...]


def _router(hi, lo, whi, wlo, bias, tri, tt):
    t, d = hi.shape
    tok = pl.BlockSpec((tt, d), lambda i: (i, 0))
    full = lambda a: pl.BlockSpec(a.shape, lambda i: (0,) * a.ndim)
    out = pl.BlockSpec((TOP_K, tt), lambda i: (0, i))
    return pl.pallas_call(
        _router_kernel,
        out_shape=(jax.ShapeDtypeStruct((TOP_K, t), jnp.int32),
                   jax.ShapeDtypeStruct((TOP_K, t), F32),
                   jax.ShapeDtypeStruct((N_EXPERTS, t), F32),
                   jax.ShapeDtypeStruct((N_EXPERTS, tt), F32)),
        grid=(t // tt,),
        in_specs=[tok, tok, full(whi), full(wlo), full(bias), full(tri)],
        out_specs=(out, out, pl.BlockSpec((N_EXPERTS, tt), lambda i: (0, i)),
                   pl.BlockSpec((N_EXPERTS, tt), lambda i: (0, 0))),
        scratch_shapes=[pltpu.VMEM((N_EXPERTS, tt), F32)],
        compiler_params=_params(("arbitrary",)),
        name="router",
    )(hi, lo, whi, wlo, bias, tri)


def _row_tile(ref, r):
    return ref.at[pl.ds(pl.multiple_of(r * ROW_SUB, ROW_SUB), ROW_SUB), :]


def _dest_kernel(idx_ref, pos_ref, start_ref, dest_ref):
    eidx = lax.broadcasted_iota(jnp.int32, start_ref.shape, 0)
    row = start_ref[...] + pos_ref[...]
    rows = [jnp.sum(jnp.where(eidx == idx_ref[k:k + 1, :], row, 0.0), axis=0, keepdims=True)
            for k in range(TOP_K)]
    dest_ref[...] = jnp.concatenate(rows, axis=0).astype(jnp.int32)


def _dest(idx, pos, start, tt):
    t = idx.shape[1]
    blk = pl.BlockSpec((TOP_K, tt), lambda i: (0, i))
    return pl.pallas_call(
        _dest_kernel,
        out_shape=jax.ShapeDtypeStruct((TOP_K, t), jnp.int32),
        grid=(t // tt,),
        in_specs=[blk, pl.BlockSpec((N_EXPERTS, tt), lambda i: (0, i)), pl.BlockSpec(start.shape, lambda i: (0, 0))],
        out_specs=blk,
        compiler_params=_params(("arbitrary",)),
        name="dest",
    )(idx, pos, start)


def _zero_tails_kernel(last_ref, o_ref):
    del last_ref
    o_ref[...] = jnp.zeros(o_ref.shape, o_ref.dtype)


def _zero_tails(last_blk, n_rows):
    return pl.pallas_call(
        _zero_tails_kernel,
        out_shape=jax.ShapeDtypeStruct((n_rows * ROW_SUB, LANES), jnp.uint32),
        grid_spec=pltpu.PrefetchScalarGridSpec(
            num_scalar_prefetch=1,
            grid=(last_blk.shape[0],),
            in_specs=[],
            out_specs=pl.BlockSpec((EXPERT_ROWS * ROW_SUB, LANES), lambda e, last: (last[e], 0))),
        compiler_params=_params(("arbitrary",)),
        name="zero_tails",
    )(last_blk)


def _dispatch_kernel(dest_ref, h_ref, xs_in_ref, xs_ref, sem):
    del xs_in_ref
    td = dest_ref.shape[1]

    def row_copy(t, k):
        return pltpu.make_async_copy(_row_tile(h_ref, t), _row_tile(xs_ref, dest_ref[k, t]), sem)

    def issue(t, carry):
        for k in range(TOP_K):
            row_copy(t, k).start(priority=k % 2)
        return carry

    lax.fori_loop(0, td, issue, 0)

    def drain(t, carry):
        for k in range(TOP_K):
            row_copy(t, k).wait()
        return carry

    lax.fori_loop(0, td, drain, 0)


def _dispatch(dest, h_rt, xs0, td):
    t = dest.shape[1]
    return pl.pallas_call(
        _dispatch_kernel,
        out_shape=jax.ShapeDtypeStruct(xs0.shape, xs0.dtype),
        grid=(t // td,),
        in_specs=[pl.BlockSpec((TOP_K, td), lambda i: (0, i), memory_space=pltpu.SMEM),
                  pl.BlockSpec((td * ROW_SUB, LANES), lambda i: (i, 0)),
                  pl.BlockSpec(memory_space=pl.ANY)],
        out_specs=pl.BlockSpec(memory_space=pl.ANY),
        scratch_shapes=[pltpu.SemaphoreType.DMA],
        input_output_aliases={2: 0},
        compiler_params=_params(("arbitrary",)),
        name="dispatch",
    )(dest, h_rt, xs0)


def _experts_kernel(be_ref, nu_ref, x_ref, wg_ref, wu_ref, wd_ref, y_ref, wgu_bf, wd_bf):
    i = pl.program_id(0)
    used = i < nu_ref[0]
    new_expert = jnp.logical_or(i == 0, be_ref[i] != be_ref[jnp.maximum(i - 1, 0)])

    @pl.when(jnp.logical_and(used, new_expert))
    def _():
        ff = wg_ref.shape[2]
        wgu_bf[:, :ff] = wg_ref[0].astype(BF16)
        wgu_bf[:, ff:] = wu_ref[0].astype(BF16)
        wd_bf[...] = wd_ref[0].astype(BF16)

    @pl.when(used)
    def _():
        ff = wg_ref.shape[2]
        for r in range(0, EXPERT_ROWS, EXPERT_SUB):
            rows = pl.ds(r * ROW_SUB, EXPERT_SUB * ROW_SUB)
            x = _load_row_tiles(x_ref.at[rows, :], EXPERT_SUB).astype(BF16)
            gu = jnp.dot(x, wgu_bf[...], preferred_element_type=F32)
            gate = gu[:, :ff]
            act = (gate * _sigmoid(gate) * gu[:, ff:]).astype(BF16)
            _store_row_tiles(y_ref.at[rows, :], jnp.dot(act, wd_bf[...], preferred_element_type=F32))


def _experts(blk_e, n_used, xs, wg, wu, wd):
    d, ff = wg.shape[1], wg.shape[2]
    nblk = xs.shape[0] // (EXPERT_ROWS * ROW_SUB)
    row = lambda i, be, nu: (jnp.minimum(i, nu[0] - 1), 0)
    wsel = lambda i, be, nu: (be[jnp.minimum(i, nu[0] - 1)], 0, 0)
    return pl.pallas_call(
        _experts_kernel,
        out_shape=jax.ShapeDtypeStruct(xs.shape, xs.dtype),
        grid_spec=pltpu.PrefetchScalarGridSpec(
            num_scalar_prefetch=2,
            grid=(nblk,),
            in_specs=[pl.BlockSpec((EXPERT_ROWS * ROW_SUB, LANES), row),
                      pl.BlockSpec((1, d, ff), wsel),
                      pl.BlockSpec((1, d, ff), wsel),
                      pl.BlockSpec((1, ff, d), wsel)],
            out_specs=pl.BlockSpec((EXPERT_ROWS * ROW_SUB, LANES), row),
            scratch_shapes=[pltpu.VMEM((d, 2 * ff), BF16), pltpu.VMEM((ff, d), BF16)]),
        compiler_params=_params(("arbitrary",)),
        name="experts",
    )(blk_e, n_used, xs, wg, wu, wd)


def _combine_kernel(dest_ref, next_ref, ys_ref, w_ref, base_ref, mod_ref, fw_ref, o_ref, buf, sem):
    tc = dest_ref.shape[1]
    i = pl.program_id(0)
    slot = i % 2

    def row_copy(row, s, t, k):
        return pltpu.make_async_copy(_row_tile(ys_ref, row), _row_tile(buf.at[s, k], t), sem.at[s])

    def issue(d_ref, s):
        for t in range(tc):
            for k in range(TOP_K):
                row_copy(d_ref[k, t], s, t, k).start(priority=k % 2)

    def drain(s):
        for t in range(tc):
            for k in range(TOP_K):
                row_copy(0, s, t, k).wait()

    @pl.when(i == 0)
    def _():
        issue(dest_ref, 0)

    drain(slot)
    issue(next_ref, 1 - slot)
    w = w_ref[...]
    routed = w[:, 0:1] * _load_row_tiles(buf.at[slot, 0], tc)
    for k in range(1, TOP_K):
        routed = routed + w[:, k:k + 1] * _load_row_tiles(buf.at[slot, k], tc)
    x = base_ref[0] + mod_ref[0, 5:6, :] * routed
    o_ref[0] = _rms(x, fw_ref[...])

    @pl.when(i == pl.num_programs(0) - 1)
    def _():
        drain(1 - slot)


def _combine(dest, ys, wt, base, mods, fw, tc):
    b, l, d = base.shape
    per_b = l // tc
    steps = b * per_b
    return pl.pallas_call(
        _combine_kernel,
        out_shape=jax.ShapeDtypeStruct((b, l, d), F32),
        grid=(steps,),
        in_specs=[pl.BlockSpec((TOP_K, tc), lambda i: (0, i), memory_space=pltpu.SMEM),
                  pl.BlockSpec((TOP_K, tc), lambda i: (0, jnp.minimum(i + 1, steps - 1)), memory_space=pltpu.SMEM),
                  pl.BlockSpec(memory_space=pl.ANY),
                  pl.BlockSpec((tc, TOP_K), lambda i: (i, 0)),
                  pl.BlockSpec((1, tc, d), lambda i: (i // per_b, i % per_b, 0)),
                  pl.BlockSpec((1, 6, d), lambda i: (i // per_b, 0, 0)),
                  pl.BlockSpec((1, d), lambda i: (0, 0))],
        out_specs=pl.BlockSpec((1, tc, d), lambda i: (i // per_b, i % per_b, 0)),
        scratch_shapes=[pltpu.VMEM((2, TOP_K, tc * ROW_SUB, LANES), jnp.uint32), pltpu.SemaphoreType.DMA((2,))],
        compiler_params=_params(("arbitrary",)),
        name="combine",
    )(dest, dest, ys, wt, base, mods, fw)


def _rope_tables(l):
    t = jnp.arange(l, dtype=jnp.int32)
    row = (t // GRID_W).astype(F32)
    col = (t % GRID_W).astype(F32)
    n_freq = HEAD_DIM // 4
    inv = ROPE_THETA ** (-jnp.arange(n_freq, dtype=F32) / n_freq)
    ang = jnp.concatenate([row[:, None] * inv, col[:, None] * inv], axis=-1)
    cos = jnp.repeat(jnp.cos(ang), 2, axis=1)
    sin = jnp.repeat(jnp.sin(ang), 2, axis=1)
    sign = jnp.tile(jnp.array([-1.0, 1.0], F32), HEAD_DIM // 2)
    reps = LANES // HEAD_DIM
    return jnp.tile(cos, (1, reps)), jnp.tile(sin * sign, (1, reps))


def _filter_features(l):
    t = jnp.linspace(0.0, 1.0, l, dtype=F32)[:, None]
    bands = (FILTER_EMB - 1) // 2
    w = 2.0 * math.pi * jnp.arange(l, dtype=F32)[:, None] / l
    f = jnp.linspace(1e-4, bands - 1, bands, dtype=F32)[None, :]
    z = jnp.concatenate([t, jnp.cos(f * w), -jnp.sin(f * w)], axis=-1)
    min_decay = math.log(FILTER_TARGET) / FILTER_DECAY_FAST
    max_decay = math.log(FILTER_TARGET) / FILTER_DECAY_SLOW
    deltas = jnp.linspace(min_decay, max_decay, HYENA_WIDTH, dtype=F32)[None, :]
    return jnp.pad(z, ((0, 0), (0, LANES - FILTER_EMB))), deltas


def _dft_matrices(l):
    n = 2 * l
    idx = jnp.arange(l, dtype=jnp.int32)
    r = math.isqrt(l)
    assert r * r == l
    sub = jnp.arange(r, dtype=jnp.int32)
    hi = ((r * sub[:, None] * idx[None, :]) % n).astype(F32) * (2.0 * math.pi / n)
    lo = ((sub[:, None] * idx[None, :]) % n).astype(F32) * (2.0 * math.pi / n)
    ch, sh, cl, sl = jnp.cos(hi)[:, None, :], jnp.sin(hi)[:, None, :], jnp.cos(lo)[None], jnp.sin(lo)[None]
    c = (ch * cl - sh * sl).reshape(l, l)
    s = (sh * cl + ch * sl).reshape(l, l)
    alt = jnp.where(idx % 2 == 0, 1.0, -1.0).astype(F32)
    first = (idx == 0)[:, None]
    fwd = jnp.concatenate([c, jnp.where(first, alt[None, :], -s)], axis=0).astype(BF16)
    firstc = (idx == 0)[None, :]
    inv_r = (jnp.where(firstc, 1.0, 2.0) * c / n).astype(BF16)
    inv_i = (jnp.where(firstc, alt[:, None], -2.0 * s) / n).astype(BF16)
    return fwd, inv_r, inv_i


def _head_perm():
    order = []
    for j in range(N_HEADS // 2):
        order += list(range(j * HEAD_DIM, (j + 1) * HEAD_DIM))
        order += list(range((j + N_HEADS // 2) * HEAD_DIM, (j + 1 + N_HEADS // 2) * HEAD_DIM))
    return jnp.array(order, jnp.int32)


def _pad2(a, rows, cols):
    return jnp.pad(a, ((0, rows - a.shape[0]), (0, cols - a.shape[1])))


def kernel(x, c, ctx, c_ctx, mod_w, mod_b, norm1_w, w_in, q_norm_w, k_norm_w, conv_w, conv_b, filt_w1, filt_b1, filt_w2, filt_b2, filt_w3, filt_b3, filt_w4, filt_freq, hyena_bias, attn_out_norm_w, hyena_out_norm_w, w_out, norm2_w, router_w, router_bias, exp_w_gate, exp_w_up, exp_w_down, sh_w_gate, sh_w_up, sh_w_down, final_norm_w):
    b, l, d = x.shape
    t = b * l
    assert mod_w.shape[0] == 1, "single-layer stack"
    tl = min(512, l)

    cond = jnp.concatenate([c, c_ctx[None, :], jnp.zeros((-(b + 1) % 8, d), F32)], axis=0)
    mod = _adaln(cond, mod_w[0], mod_b[0][None, :])
    mods = mod[:b].reshape(b, 6, d)
    cmod = mod[b].reshape(6, d)

    perm = _head_perm()
    w_in0 = w_in[0]
    w_in_k = jnp.concatenate([w_in0[:, :Q_END][:, perm], w_in0[:, Q_END:]], axis=1).astype(BF16)
    w_kv = w_in0[:, Q_END:V_END].astype(BF16)
    gq = jnp.kron(jnp.eye(N_HEADS, dtype=F32), jnp.full((HEAD_DIM, HEAD_DIM), 1.0 / HEAD_DIM, F32)).astype(BF16)
    qnw = jnp.tile(q_norm_w[0], N_HEADS)[None, :]
    knw = jnp.tile(k_norm_w[0], N_KV_HEADS)[None, :]
    n1w = norm1_w[0][None, :]
    cos, sin = _rope_tables(l)

    kc, vc = _ctx_kv(ctx, cmod, n1w, w_kv, gq[:KV_WIDTH, :KV_WIDTH], knw)
    q, k, v, u = _inproj(x, mods, n1w, w_in_k, gq, qnw, knw, cos, sin, tl)
    k_all = jnp.concatenate([kc, k], axis=1)
    v_all = jnp.concatenate([vc, v], axis=1)
    an = _attention(q, k_all, v_all, attn_out_norm_w[0][perm][None, :], min(256, l), min(256, l))

    cw = conv_w[0].reshape(3, 3, HYENA_WIDTH).transpose(1, 0, 2)
    cb = conv_b[0].reshape(3, HYENA_WIDTH)
    gbf, x0 = _hyena_pre(u, cw, cb)
    z, deltas = _filter_features(l)
    fo = filt_w2.shape[1]
    hsd = _hyena_filter(
        z, _pad2(filt_w1[0], LANES, LANES), _pad2(filt_b1[0][None, :], 1, LANES),
        _pad2(filt_w2[0], LANES, LANES), _pad2(filt_b2[0][None, :], 1, LANES),
        _pad2(filt_w3[0], LANES, LANES), _pad2(filt_b3[0][None, :], 1, LANES),
        _pad2(filt_w4[0], LANES, 2 * HYENA_WIDTH), _pad2(filt_freq[0][None, :], 1, LANES), deltas, tl)
    del fo
    fwd, inv_r, inv_i = _dft_matrices(l)
    spec = _dft(fwd, hsd, tl)
    row0 = (jnp.arange(l) == 0)[:, None]
    sa = spec[0, :l]
    sd = jnp.where(row0, spec[0, l:l + 1], sa)
    sb = jnp.where(row0, 0.0, spec[1, l:])
    zr, zi = _dft_mul(fwd, gbf, sa, sb, sd, tl)
    yn = _idft(inv_r, inv_i, zr, zi, gbf, x0, hyena_bias[0][None, :], hyena_out_norm_w[0][None, :], tl)

    w_out0 = w_out[0]
    base, h2hi, h2lo, h2rt = _merge(
        an, yn, x, mods, w_out0[:ATTN_WIDTH][perm].astype(BF16), w_out0[ATTN_WIDTH:].astype(BF16),
        norm2_w[0][None, :], sh_w_gate[0].astype(BF16), sh_w_up[0].astype(BF16), sh_w_down[0].astype(BF16), tl)

    tt = 256
    rwt = router_w[0].T
    rw_hi = rwt.astype(BF16)
    rw_lo = (rwt - rw_hi.astype(F32)).astype(BF16)
    bias = jnp.broadcast_to(router_bias[0][:, None], (N_EXPERTS, tt))
    ti = jnp.arange(tt)
    tri = jnp.stack([(ti[:, None] < ti[None, :]), jnp.ones((tt, tt), bool)]).astype(BF16)
    idx, wgt, pos, cnt = _router(h2hi.reshape(t, d), h2lo.reshape(t, d), rw_hi, rw_lo, bias, tri, tt)

    counts = cnt[:, 0].astype(jnp.int32)
    padded = (counts + EXPERT_ROWS - 1) // EXPERT_ROWS * EXPERT_ROWS
    pad_end = jnp.cumsum(padded)
    pad_start = pad_end - padded
    td = min(1024, t)
    dest = _dest(idx, pos, jnp.broadcast_to(pad_start.astype(F32)[:, None], (N_EXPERTS, td)), td)
    n_rows = (t * TOP_K + N_EXPERTS * (EXPERT_ROWS - 1) + EXPERT_ROWS - 1) // EXPERT_ROWS * EXPERT_ROWS
    nblk = n_rows // EXPERT_ROWS
    blk_row = jnp.arange(nblk, dtype=jnp.int32) * EXPERT_ROWS
    blk_e = jnp.minimum(jnp.sum((pad_end[None, :] <= blk_row[:, None]).astype(jnp.int32), axis=1), N_EXPERTS - 1)
    n_used = (pad_end[-1:] // EXPERT_ROWS).astype(jnp.int32)

    last_blk = jnp.maximum(pad_end // EXPERT_ROWS - 1, 0).astype(jnp.int32)
    xs = _dispatch(dest, h2rt, _zero_tails(last_blk, n_rows), min(512, t))
    ys = _experts(blk_e, n_used, xs, exp_w_gate[0], exp_w_up[0], exp_w_down[0])
    return _combine(dest, ys, wgt.T, base, mods, final_norm_w[None, :], min(128, l))
```

```python
import functools
import math

import jax
import jax.numpy as jnp
from jax import lax
from jax.experimental import pallas as pl
from jax.experimental.pallas import tpu as pltpu
from jax.experimental.pallas import tpu_sc as plsc

F32 = jnp.float32
BF16 = jnp.bfloat16
HIGHEST = lax.Precision.HIGHEST

GRID_W = 64
N_HEADS = 8
N_KV_HEADS = 2
HEAD_DIM = 64
ATTN_WIDTH = N_HEADS * HEAD_DIM
KV_WIDTH = N_KV_HEADS * HEAD_DIM
HYENA_WIDTH = 512
Q_END = ATTN_WIDTH
K_END = Q_END + KV_WIDTH
V_END = K_END + KV_WIDTH
ROPE_THETA = 10000.0
FILTER_EMB = 33
FILTER_DECAY_FAST = 0.3
FILTER_DECAY_SLOW = 1.5
FILTER_TARGET = 1e-2
N_EXPERTS = 256
TOP_K = 8
N_GROUPS = 8
TOPK_GROUPS = 4
ROUTE_SCALE = 2.5
EPS = 1e-6

LANES = 128
ROW_SUB = 4
EXPERT_ROWS = 512
EXPERT_SUB = 256
NEG_INF = float("-inf")


def _params(semantics, vmem_mb=48):
    return pltpu.CompilerParams(dimension_semantics=semantics, vmem_limit_bytes=vmem_mb * 1024 * 1024)


def _rms(x, w):
    return x * lax.rsqrt(jnp.mean(x * x, axis=-1, keepdims=True) + EPS) * w


def _sigmoid(x):
    return 1.0 / (1.0 + jnp.exp(-x))


def _adaln_kernel(c_ref, w_ref, b_ref, o_ref):
    c = c_ref[...]
    s = c * _sigmoid(c)
    o_ref[...] = jnp.dot(s, w_ref[...], precision=HIGHEST, preferred_element_type=F32) + b_ref[...]


def _adaln(cond, w, b):
    rows, d = cond.shape
    n = w.shape[1]
    tn = 1536
    return pl.pallas_call(
        _adaln_kernel,
        out_shape=jax.ShapeDtypeStruct((rows, n), F32),
        grid=(n // tn,),
        in_specs=[pl.BlockSpec((rows, d), lambda j: (0, 0)),
                  pl.BlockSpec((d, tn), lambda j: (0, j)),
                  pl.BlockSpec((1, tn), lambda j: (0, j))],
        out_specs=pl.BlockSpec((rows, tn), lambda j: (0, j)),
        compiler_params=_params(("arbitrary",)),
        name="adaln",
    )(cond, w, b)


def _head_rms(t, gmat, w):
    ms = jnp.dot((t * t).astype(BF16), gmat, preferred_element_type=F32)
    return t * lax.rsqrt(ms + EPS) * w


def _modulated(x, norm_w, shift, scale):
    return _rms(x, norm_w) * (1.0 + scale) + shift


def _ctx_kv_kernel(ctx_ref, mod_ref, n1_ref, w_ref, g_ref, kn_ref, kc_ref, vc_ref):
    x = ctx_ref[0]
    h = _modulated(x, n1_ref[...], mod_ref[0:1, :], mod_ref[1:2, :])
    kv = jnp.dot(h.astype(BF16), w_ref[...], preferred_element_type=F32)
    k = _head_rms(kv[:, :KV_WIDTH], g_ref[...], kn_ref[...])
    v = kv[:, KV_WIDTH:]
    kc_ref[0] = k.astype(BF16)
    vc_ref[0] = jnp.concatenate([v, jnp.ones_like(v)], axis=1).astype(BF16)


def _ctx_kv(ctx, cmod, n1w, w_kv, gk, knw):
    b, c, d = ctx.shape
    return pl.pallas_call(
        _ctx_kv_kernel,
        out_shape=(jax.ShapeDtypeStruct((b, c, KV_WIDTH), BF16),
                   jax.ShapeDtypeStruct((b, c, 2 * KV_WIDTH), BF16)),
        grid=(b,),
        in_specs=[pl.BlockSpec((1, c, d), lambda i: (i, 0, 0)),
                  pl.BlockSpec(cmod.shape, lambda i: (0, 0)),
                  pl.BlockSpec((1, d), lambda i: (0, 0)),
                  pl.BlockSpec(w_kv.shape, lambda i: (0, 0)),
                  pl.BlockSpec(gk.shape, lambda i: (0, 0)),
                  pl.BlockSpec((1, KV_WIDTH), lambda i: (0, 0))],
        out_specs=(pl.BlockSpec((1, c, KV_WIDTH), lambda i: (i, 0, 0)),
                   pl.BlockSpec((1, c, 2 * KV_WIDTH), lambda i: (i, 0, 0))),
        compiler_params=_params(("arbitrary",)),
        name="ctx_kv",
    )(ctx, cmod, n1w, w_kv, gk, knw)


def _rope(t, cos, sin, even):
    width = t.shape[1]
    partner = jnp.where(even, pltpu.roll(t, width - 1, axis=1), pltpu.roll(t, 1, axis=1))
    return t * cos + partner * sin


def _inproj_kernel(x_ref, mod_ref, n1_ref, w_ref, gq_ref, qn_ref, kn_ref, cos_ref, sin_ref,
                   q_ref, k_ref, v_ref, u_ref):
    x = x_ref[0]
    h = _modulated(x, n1_ref[...], mod_ref[0, 0:1, :], mod_ref[0, 1:2, :])
    p = jnp.dot(h.astype(BF16), w_ref[...], preferred_element_type=F32)
    gq = gq_ref[...]
    q = _head_rms(p[:, :Q_END], gq, qn_ref[...])
    k = _head_rms(p[:, Q_END:K_END], gq[:KV_WIDTH, :KV_WIDTH], kn_ref[...])
    v = p[:, K_END:V_END]
    cos = cos_ref[...]
    sin = sin_ref[...]
    reps = Q_END // LANES
    cos_q = jnp.concatenate([cos] * reps, axis=1)
    sin_q = jnp.concatenate([sin] * reps, axis=1)
    even_q = (lax.broadcasted_iota(jnp.int32, (1, Q_END), 1) & 1) == 0
    even_k = (lax.broadcasted_iota(jnp.int32, (1, KV_WIDTH), 1) & 1) == 0
    q = _rope(q, cos_q, sin_q, even_q) * (HEAD_DIM ** -0.5)
    k = _rope(k, cos, sin, even_k)
    q_ref[0] = q.astype(BF16)
    k_ref[0] = k.astype(BF16)
    v_ref[0] = jnp.concatenate([v, jnp.ones_like(v)], axis=1).astype(BF16)
    u_ref[0] = p[:, V_END:].astype(BF16)


def _inproj(x, mods, n1w, w_in, gq, qnw, knw, cos, sin, tl):
    b, l, d = x.shape
    ncol = w_in.shape[1]
    nu = ncol - V_END
    return pl.pallas_call(
        _inproj_kernel,
        out_shape=(jax.ShapeDtypeStruct((b, l, Q_END), BF16),
                   jax.ShapeDtypeStruct((b, l, KV_WIDTH), BF16),
                   jax.ShapeDtypeStruct((b, l, 2 * KV_WIDTH), BF16),
                   jax.ShapeDtypeStruct((b, l, nu), BF16)),
        grid=(l // tl, b),
        in_specs=[pl.BlockSpec((1, tl, d), lambda i, j: (j, i, 0)),
                  pl.BlockSpec((1, 6, d), lambda i, j: (j, 0, 0)),
                  pl.BlockSpec((1, d), lambda i, j: (0, 0)),
                  pl.BlockSpec((d, ncol), lambda i, j: (0, 0)),
                  pl.BlockSpec(gq.shape, lambda i, j: (0, 0)),
                  pl.BlockSpec((1, Q_END), lambda i, j: (0, 0)),
                  pl.BlockSpec((1, KV_WIDTH), lambda i, j: (0, 0)),
                  pl.BlockSpec((tl, LANES), lambda i, j: (i, 0)),
                  pl.BlockSpec((tl, LANES), lambda i, j: (i, 0))],
        out_specs=(pl.BlockSpec((1, tl, Q_END), lambda i, j: (j, i, 0)),
                   pl.BlockSpec((1, tl, KV_WIDTH), lambda i, j: (j, i, 0)),
                   pl.BlockSpec((1, tl, 2 * KV_WIDTH), lambda i, j: (j, i, 0)),
                   pl.BlockSpec((1, tl, nu), lambda i, j: (j, i, 0))),
        compiler_params=_params(("arbitrary", "arbitrary")),
        name="inproj",
    )(x, mods, n1w, w_in, gq, qnw, knw, cos, sin)


def _attn_kernel(q_ref, k_ref, v_ref, wn_ref, o_ref, *, sub):
    tq = q_ref.shape[1]
    low = lax.broadcasted_iota(jnp.int32, (1, LANES), 1) < HEAD_DIM
    nt = (((1,), (1,)), ((), ()))
    kk = k_ref[0]
    vv = v_ref[0]
    for r in range(0, tq, sub):
        outs = []
        for j in range(Q_END // LANES):
            qv = q_ref[0, r:r + sub, LANES * j:LANES * (j + 1)]
            zero = jnp.zeros_like(qv)
            halves = []
            for g in range(N_KV_HEADS):
                qh = jnp.where(low, qv, zero) if g == 0 else jnp.where(low, zero, qv)
                s = lax.dot_general(qh, kk, nt, preferred_element_type=F32)
                p = jnp.exp(s - jnp.max(s, axis=-1, keepdims=True)).astype(BF16)
                pv = jnp.dot(p, vv, preferred_element_type=F32)
                halves.append(pv[:, :LANES] / pv[:, LANES:])
            outs.append(jnp.where(low, halves[0], halves[1]))
        a = jnp.concatenate(outs, axis=1)
        o_ref[0, r:r + sub, :] = _rms(a, wn_ref[...]).astype(BF16)


def _attention(q, k, v, wn, tq, sub):
    b, l, _ = q.shape
    n = k.shape[1]
    return pl.pallas_call(
        functools.partial(_attn_kernel, sub=sub),
        out_shape=jax.ShapeDtypeStruct((b, l, Q_END), BF16),
        grid=(b, l // tq),
        in_specs=[pl.BlockSpec((1, tq, Q_END), lambda i, j: (i, j, 0)),
                  pl.BlockSpec((1, n, KV_WIDTH), lambda i, j: (i, 0, 0)),
                  pl.BlockSpec((1, n, 2 * KV_WIDTH), lambda i, j: (i, 0, 0)),
                  pl.BlockSpec((1, Q_END), lambda i, j: (0, 0))],
        out_specs=pl.BlockSpec((1, tq, Q_END), lambda i, j: (i, j, 0)),
        compiler_params=_params(("arbitrary", "arbitrary")),
        name="attn",
    )(q, k, v, wn)


def _hyena_pre_kernel(u0_ref, u1_ref, u2_ref, cw_ref, cb_ref, g_ref, x0_ref):
    l = u0_ref.shape[1]
    row = lax.broadcasted_iota(jnp.int32, (l, LANES), 0)

    def conv(u_ref, gi):
        u = u_ref[0].astype(F32)
        prev = jnp.where(row == 0, 0.0, pltpu.roll(u, 1, axis=0))
        nxt = jnp.where(row == l - 1, 0.0, pltpu.roll(u, l - 1, axis=0))
        w = cw_ref[gi]
        return w[0:1] * prev + w[1:2] * u + w[2:3] * nxt + cb_ref[gi:gi + 1, :]

    x0 = conv(u0_ref, 0)
    x1 = conv(u1_ref, 1)
    v = conv(u2_ref, 2)
    g = v * x1
    g_ref[0] = g.astype(BF16)
    x0_ref[0] = x0.astype(BF16)


def _hyena_pre(u, cw, cb):
    b, l, _ = u.shape
    nblk = HYENA_WIDTH // LANES
    ublk = lambda gi: pl.BlockSpec((1, l, LANES), lambda i, j: (i, 0, gi * nblk + j))
    oblk = pl.BlockSpec((1, l, LANES), lambda i, j: (i, 0, j))
    return pl.pallas_call(
        _hyena_pre_kernel,
        out_shape=(jax.ShapeDtypeStruct((b, l, HYENA_WIDTH), BF16),
                   jax.ShapeDtypeStruct((b, l, HYENA_WIDTH), BF16)),
        grid=(b, nblk),
        in_specs=[ublk(0), ublk(1), ublk(2),
                  pl.BlockSpec((3, 3, LANES), lambda i, j: (0, 0, j)),
                  pl.BlockSpec((3, LANES), lambda i, j: (0, j))],
        out_specs=(oblk, oblk),
        compiler_params=_params(("arbitrary", "arbitrary")),
        name="hyena_pre",
    )(u, u, u, cw, cb)


def _filter_kernel(z_ref, w1_ref, b1_ref, w2_ref, b2_ref, w3_ref, b3_ref, w4_ref, fr_ref, dl_ref, o_ref):
    tl = z_ref.shape[0]
    z = z_ref[...]
    fr = fr_ref[...]
    dot = lambda a, w: jnp.dot(a, w, precision=HIGHEST, preferred_element_type=F32)
    h = jnp.sin(fr * (dot(z, w1_ref[...]) + b1_ref[...]))
    h = jnp.sin(fr * (dot(h, w2_ref[...]) + b2_ref[...]))
    h = jnp.sin(fr * (dot(h, w3_ref[...]) + b3_ref[...]))
    h = dot(h, w4_ref[...])
    t = z[:, 0:1]
    decay = jnp.exp(-t * jnp.abs(dl_ref[...]))
    hf = h[:, :HYENA_WIDTH] * decay
    hb = h[:, HYENA_WIDTH:] * decay
    row = lax.broadcasted_iota(jnp.int32, (tl, HYENA_WIDTH), 0) + pl.program_id(0) * tl
    hb = jnp.where(row == 0, 0.0, hb)
    o_ref[0] = hf + hb
    o_ref[1] = hf - hb


def _hyena_filter(z, w1, b1, w2, b2, w3, b3, w4, freq, deltas, tl):
    l = z.shape[0]
    full = lambda a: pl.BlockSpec(a.shape, lambda i: (0,) * a.ndim)
    return pl.pallas_call(
        _filter_kernel,
        out_shape=jax.ShapeDtypeStruct((2, l, HYENA_WIDTH), F32),
        grid=(l // tl,),
        in_specs=[pl.BlockSpec((tl, z.shape[1]), lambda i: (i, 0)),
                  full(w1), full(b1), full(w2), full(b2), full(w3), full(b3), full(w4), full(freq), full(deltas)],
        out_specs=pl.BlockSpec((2, tl, HYENA_WIDTH), lambda i: (0, i, 0)),
        compiler_params=_params(("arbitrary",)),
        name="hyena_filter",
    )(z, w1, b1, w2, b2, w3, b3, w4, freq, deltas)


def _dft_kernel(f_ref, x_ref, o_ref):
    o_ref[0] = jnp.dot(f_ref[...], x_ref[0].astype(BF16), preferred_element_type=F32)


def _dft(fmat, x, tf):
    nb, l, w = x.shape
    n = fmat.shape[0]
    return pl.pallas_call(
        _dft_kernel,
        out_shape=jax.ShapeDtypeStruct((nb, n, w), F32),
        grid=(n // tf, nb),
        in_specs=[pl.BlockSpec((tf, l), lambda i, j: (i, 0)),
                  pl.BlockSpec((1, l, w), lambda i, j: (j, 0, 0))],
        out_specs=pl.BlockSpec((1, tf, w), lambda i, j: (j, i, 0)),
        compiler_params=_params(("arbitrary", "arbitrary")),
        name="dft_filter",
    )(fmat, x)


def _dft_mul_kernel(fr_ref, fi_ref, x_ref, a_ref, b_ref, d_ref, zr_ref, zi_ref):
    x = x_ref[0]
    xr = jnp.dot(fr_ref[...], x, preferred_element_type=F32)
    xi = jnp.dot(fi_ref[...], x, preferred_element_type=F32)
    bb = b_ref[...]
    zr_ref[0] = (xr * a_ref[...] - xi * bb).astype(BF16)
    zi_ref[0] = (xr * bb + xi * d_ref[...]).astype(BF16)


def _dft_mul(fmat, g, sa, sb, sd, tf):
    b, l, w = g.shape
    nf = l // tf
    spec = pl.BlockSpec((tf, w), lambda i, j: (i, 0))
    return pl.pallas_call(
        _dft_mul_kernel,
        out_shape=(jax.ShapeDtypeStruct((b, l, w), BF16), jax.ShapeDtypeStruct((b, l, w), BF16)),
        grid=(nf, b),
        in_specs=[pl.BlockSpec((tf, l), lambda i, j: (i, 0)),
                  pl.BlockSpec((tf, l), lambda i, j: (i + nf, 0)),
                  pl.BlockSpec((1, l, w), lambda i, j: (j, 0, 0)),
                  spec, spec, spec],
        out_specs=(pl.BlockSpec((1, tf, w), lambda i, j: (j, i, 0)),
                   pl.BlockSpec((1, tf, w), lambda i, j: (j, i, 0))),
        compiler_params=_params(("arbitrary", "arbitrary")),
        name="dft_mul",
    )(fmat, fmat, g, sa, sb, sd)


def _idft_kernel(fr_ref, fi_ref, zr_ref, zi_ref, g_ref, x0_ref, hb_ref, wn_ref, o_ref):
    conv = (jnp.dot(fr_ref[...], zr_ref[0], preferred_element_type=F32)
            + jnp.dot(fi_ref[...], zi_ref[0], preferred_element_type=F32))
    y = (conv + g_ref[0].astype(F32) * hb_ref[...]) * x0_ref[0].astype(F32)
    o_ref[0] = _rms(y, wn_ref[...]).astype(BF16)


def _idft(finv_r, finv_i, zr, zi, g, x0, hbias, wn, tt):
    b, l, w = zr.shape
    tile = pl.BlockSpec((1, tt, w), lambda i, j: (j, i, 0))
    return pl.pallas_call(
        _idft_kernel,
        out_shape=jax.ShapeDtypeStruct((b, l, w), BF16),
        grid=(l // tt, b),
        in_specs=[pl.BlockSpec((tt, l), lambda i, j: (i, 0)),
                  pl.BlockSpec((tt, l), lambda i, j: (i, 0)),
                  pl.BlockSpec((1, l, w), lambda i, j: (j, 0, 0)),
                  pl.BlockSpec((1, l, w), lambda i, j: (j, 0, 0)),
                  tile, tile,
                  pl.BlockSpec((1, w), lambda i, j: (0, 0)),
                  pl.BlockSpec((1, w), lambda i, j: (0, 0))],
        out_specs=tile,
        compiler_params=_params(("arbitrary", "arbitrary")),
        name="idft",
    )(finv_r, finv_i, zr, zi, g, x0, hbias, wn)


def _bf16_bits(v):
    return lax.bitcast_convert_type(v.astype(BF16).astype(F32), jnp.uint32)


def _store_row_tiles(ref, val):
    rows, half = val.shape[0], val.shape[1] // 2
    assert half == ROW_SUB * LANES
    for j in range(ROW_SUB):
        lo = _bf16_bits(val[:, LANES * j:LANES * (j + 1)]) >> 16
        hi = _bf16_bits(val[:, half + LANES * j:half + LANES * (j + 1)]) & jnp.uint32(0xFFFF0000)
        ref[pl.ds(j, rows, stride=ROW_SUB), :] = lo | hi


def _load_row_tiles(ref, rows):
    words = [ref[pl.ds(j, rows, stride=ROW_SUB), :] for j in range(ROW_SUB)]
    lo = [lax.bitcast_convert_type(w << 16, F32) for w in words]
    hi = [lax.bitcast_convert_type(w & jnp.uint32(0xFFFF0000), F32) for w in words]
    return jnp.concatenate(lo + hi, axis=1)


def _merge_kernel(a_ref, y_ref, x_ref, mod_ref, wa_ref, wy_ref, n2_ref, sg_ref, su_ref, sd_ref,
                  base_ref, hi_ref, lo_ref, rt_ref):
    m = (jnp.dot(a_ref[0], wa_ref[...], preferred_element_type=F32)
         + jnp.dot(y_ref[0], wy_ref[...], preferred_element_type=F32))
    x1 = x_ref[0] + mod_ref[0, 2:3, :] * m
    h2 = _modulated(x1, n2_ref[...], mod_ref[0, 3:4, :], mod_ref[0, 4:5, :])
    hi = h2.astype(BF16)
    hi_ref[0] = hi
    lo_ref[0] = (h2 - hi.astype(F32)).astype(BF16)
    _store_row_tiles(rt_ref, h2)
    gate = jnp.dot(hi, sg_ref[...], preferred_element_type=F32)
    up = jnp.dot(hi, su_ref[...], preferred_element_type=F32)
    act = (gate * _sigmoid(gate) * up).astype(BF16)
    shared = jnp.dot(act, sd_ref[...], preferred_element_type=F32)
    base_ref[0] = x1 + mod_ref[0, 5:6, :] * shared


def _merge(an, yn, x, mods, wa, wy, n2w, sg, su, sd, tl):
    b, l, d = x.shape
    full = lambda a: pl.BlockSpec(a.shape, lambda i, j: (0,) * a.ndim)
    half = pl.BlockSpec((1, tl, an.shape[2]), lambda i, j: (i, j, 0))
    wide = pl.BlockSpec((1, tl, d), lambda i, j: (i, j, 0))
    per_b = l // tl
    return pl.pallas_call(
        _merge_kernel,
        out_shape=(jax.ShapeDtypeStruct((b, l, d), F32),
                   jax.ShapeDtypeStruct((b, l, d), BF16),
                   jax.ShapeDtypeStruct((b, l, d), BF16),
                   jax.ShapeDtypeStruct((b * l * ROW_SUB, LANES), jnp.uint32)),
        grid=(b, per_b),
        in_specs=[half, half, wide,
                  pl.BlockSpec((1, 6, d), lambda i, j: (i, 0, 0)),
                  full(wa), full(wy), full(n2w), full(sg), full(su), full(sd)],
        out_specs=(wide, wide, wide,
                   pl.BlockSpec((tl * ROW_SUB, LANES), lambda i, j: (i * per_b + j, 0))),
        compiler_params=_params(("arbitrary", "arbitrary")),
        name="merge",
    )(an, yn, x, mods, wa, wy, n2w, sg, su, sd)


def _router_kernel(hi_ref, lo_ref, whi_ref, wlo_ref, bias_ref, tri_ref,
                   idx_ref, wgt_ref, pos_ref, cnt_ref, run_ref):
    tt = hi_ref.shape[0]
    per_group = N_EXPERTS // N_GROUPS

    @pl.when(pl.program_id(0) == 0)
    def _():
        run_ref[...] = jnp.zeros_like(run_ref)

    nt = (((1,), (1,)), ((), ()))
    hi = hi_ref[...]
    whi = whi_ref[...]
    logits = (lax.dot_general(whi, hi, nt, preferred_element_type=F32)
              + lax.dot_general(whi, lo_ref[...], nt, preferred_element_type=F32)
              + lax.dot_general(wlo_ref[...], hi, nt, preferred_element_type=F32))
    scores = _sigmoid(logits)
    biased = scores + bias_ref[...]

    ridx = lax.broadcasted_iota(jnp.int32, (per_group, tt), 0)
    groups = [biased[g * per_group:(g + 1) * per_group, :] for g in range(N_GROUPS)]
    gs = []
    for blk in groups:
        m1 = jnp.max(blk, axis=0, keepdims=True)
        i1 = jnp.min(jnp.where(blk == m1, ridx, per_group), axis=0, keepdims=True)
        m2 = jnp.max(jnp.where(ridx == i1, NEG_INF, blk), axis=0, keepdims=True)
        gs.append(m1 + m2)

    kept = []
    for g in range(N_GROUPS):
        ahead = jnp.zeros((1, tt), F32)
        for o in range(N_GROUPS):
            if o != g:
                wins = (gs[o] >= gs[g]) if o < g else (gs[o] > gs[g])
                ahead = ahead + jnp.where(wins, 1.0, 0.0)
        kept.append(jnp.where(ahead < TOPK_GROUPS, groups[g], NEG_INF))
    cur = jnp.concatenate(kept, axis=0)

    eidx = lax.broadcasted_iota(jnp.int32, cur.shape, 0)
    kept_mask = cur
    picks = []
    wsel = []
    for _ in range(TOP_K):
        mx = jnp.max(cur, axis=0, keepdims=True)
        first = jnp.min(jnp.where(cur == mx, eidx, N_EXPERTS), axis=0, keepdims=True)
        sel = eidx == first
        picks.append(first)
        wsel.append(jnp.sum(jnp.where(sel, scores, 0.0), axis=0, keepdims=True))
        cur = jnp.where(sel, NEG_INF, cur)
    w = jnp.concatenate(wsel, axis=0)
    w = w / jnp.sum(w, axis=0, keepdims=True) * ROUTE_SCALE
    idx = jnp.concatenate(picks, axis=0)

    oh = jnp.where(cur == NEG_INF, jnp.where(kept_mask == NEG_INF, 0.0, 1.0), 0.0).astype(BF16)
    before = jnp.dot(oh, tri_ref[0], preferred_element_type=F32)
    total = jnp.dot(oh, tri_ref[1], preferred_element_type=F32)
    pos_ref[...] = run_ref[...] + before
    run_ref[...] = run_ref[...] + total

    idx_ref[...] = idx
    wgt_ref[...] = w
    cnt_ref[...] = run_ref[...]


def _router(hi, lo, whi, wlo, bias, tri, tt):
    t, d = hi.shape
    tok = pl.BlockSpec((tt, d), lambda i: (i, 0))
    full = lambda a: pl.BlockSpec(a.shape, lambda i: (0,) * a.ndim)
    out = pl.BlockSpec((TOP_K, tt), lambda i: (0, i))
    return pl.pallas_call(
        _router_kernel,
        out_shape=(jax.ShapeDtypeStruct((TOP_K, t), jnp.int32),
                   jax.ShapeDtypeStruct((TOP_K, t), F32),
                   jax.ShapeDtypeStruct((N_EXPERTS, t), F32),
                   jax.ShapeDtypeStruct((N_EXPERTS, tt), F32)),
        grid=(t // tt,),
        in_specs=[tok, tok, full(whi), full(wlo), full(bias), full(tri)],
        out_specs=(out, out, pl.BlockSpec((N_EXPERTS, tt), lambda i: (0, i)),
                   pl.BlockSpec((N_EXPERTS, tt), lambda i: (0, 0))),
        scratch_shapes=[pltpu.VMEM((N_EXPERTS, tt), F32)],
        compiler_params=_params(("arbitrary",)),
        name="router",
    )(hi, lo, whi, wlo, bias, tri)


def _row_tile(ref, r):
    return ref.at[pl.ds(pl.multiple_of(r * ROW_SUB, ROW_SUB), ROW_SUB), :]


def _dest_kernel(idx_ref, pos_ref, start_ref, dest_ref):
    eidx = lax.broadcasted_iota(jnp.int32, start_ref.shape, 0)
    row = start_ref[...] + pos_ref[...]
    rows = [jnp.sum(jnp.where(eidx == idx_ref[k:k + 1, :], row, 0.0), axis=0, keepdims=True)
            for k in range(TOP_K)]
    dest_ref[...] = jnp.concatenate(rows, axis=0).astype(jnp.int32)


def _dest(idx, pos, start, tt):
    t = idx.shape[1]
    blk = pl.BlockSpec((TOP_K, tt), lambda i: (0, i))
    return pl.pallas_call(
        _dest_kernel,
        out_shape=jax.ShapeDtypeStruct((TOP_K, t), jnp.int32),
        grid=(t // tt,),
        in_specs=[blk, pl.BlockSpec((N_EXPERTS, tt), lambda i: (0, i)), pl.BlockSpec(start.shape, lambda i: (0, 0))],
        out_specs=blk,
        compiler_params=_params(("arbitrary",)),
        name="dest",
    )(idx, pos, start)


def _zero_tails_kernel(last_ref, o_ref):
    del last_ref
    o_ref[...] = jnp.zeros(o_ref.shape, o_ref.dtype)


def _zero_tails(last_blk, n_rows):
    return pl.pallas_call(
        _zero_tails_kernel,
        out_shape=jax.ShapeDtypeStruct((n_rows * ROW_SUB, LANES), jnp.uint32),
        grid_spec=pltpu.PrefetchScalarGridSpec(
            num_scalar_prefetch=1,
            grid=(last_blk.shape[0],),
            in_specs=[],
            out_specs=pl.BlockSpec((EXPERT_ROWS * ROW_SUB, LANES), lambda e, last: (last[e], 0))),
        compiler_params=_params(("arbitrary",)),
        name="zero_tails",
    )(last_blk)


def _dispatch_kernel(dest_ref, h_ref, xs_in_ref, xs_ref, sem):
    del xs_in_ref
    td = dest_ref.shape[1]

    def row_copy(t, k):
        return pltpu.make_async_copy(_row_tile(h_ref, t), _row_tile(xs_ref, dest_ref[k, t]), sem)

    def issue(t, carry):
        for k in range(TOP_K):
            row_copy(t, k).start(priority=k % 2)
        return carry

    lax.fori_loop(0, td, issue, 0)

    def drain(t, carry):
        for k in range(TOP_K):
            row_copy(t, k).wait()
        return carry

    lax.fori_loop(0, td, drain, 0)


def _dispatch(dest, h_rt, xs0, td):
    t = dest.shape[1]
    return pl.pallas_call(
        _dispatch_kernel,
        out_shape=jax.ShapeDtypeStruct(xs0.shape, xs0.dtype),
        grid=(t // td,),
        in_specs=[pl.BlockSpec((TOP_K, td), lambda i: (0, i), memory_space=pltpu.SMEM),
                  pl.BlockSpec((td * ROW_SUB, LANES), lambda i: (i, 0)),
                  pl.BlockSpec(memory_space=pl.ANY)],
        out_specs=pl.BlockSpec(memory_space=pl.ANY),
        scratch_shapes=[pltpu.SemaphoreType.DMA],
        input_output_aliases={2: 0},
        compiler_params=_params(("arbitrary",)),
        name="dispatch",
    )(dest, h_rt, xs0)


def _row_assignment(dest_flat, n_rows, t):
    info = plsc.get_sparse_core_info()
    lanes = info.num_lanes
    workers = info.num_cores * info.num_subcores
    own = n_rows // workers
    total = TOP_K * t
    chunk = min(8192, total)
    assert n_rows % (workers * lanes) == 0 and total % chunk == 0 and chunk % lanes == 0

    def body(dest_hbm, out_hbm, buf, part):
        wid = lax.axis_index("s") * info.num_cores + lax.axis_index("c")
        base = wid * own
        lane = lax.iota(jnp.int32, lanes)

        @pl.loop(0, own, step=lanes)
        def _(r):
            buf[pl.ds(r, lanes)] = jnp.zeros((lanes,), jnp.int32)

        @pl.loop(0, total, step=chunk)
        def _(c):
            pltpu.sync_copy(dest_hbm.at[pl.ds(c, chunk)], part)

            @pl.loop(0, chunk, step=lanes)
            def _(j):
                rel = part[pl.ds(j, lanes)] - base
                mine = jnp.logical_and(rel >= 0, rel < own)
                plsc.store_scatter(buf, [jnp.where(mine, rel, 0)], c + j + lane, mask=mine)

        pltpu.sync_copy(buf, out_hbm.at[pl.ds(base, own)])

    return pl.kernel(
        body,
        out_type=jax.ShapeDtypeStruct((n_rows,), jnp.int32),
        mesh=plsc.VectorSubcoreMesh(core_axis_name="c", subcore_axis_name="s"),
        scratch_types=[pltpu.VMEM((own,), jnp.int32), pltpu.VMEM((chunk,), jnp.int32)],
        compiler_params=pltpu.CompilerParams(needs_layout_passes=False),
        name="row_assignment",
    )(dest_flat)


def _experts_kernel(be_ref, nu_ref, cur_ref, nxt_ref, h_ref, wg_ref, wu_ref, wd_ref, y_ref,
                    xbuf, sem, wgu_bf, wd_bf):
    i = pl.program_id(0)
    n_used = nu_ref[0]
    used = i < n_used
    slot = i % 2
    new_expert = jnp.logical_or(i == 0, be_ref[i] != be_ref[jnp.maximum(i - 1, 0)])

    def row_copy(token, s, r):
        return pltpu.make_async_copy(_row_tile(h_ref, token), _row_tile(xbuf.at[s], r), sem.at[s])

    def issue(a_ref, s):
        for r in range(EXPERT_ROWS):
            row_copy(a_ref[0, 0, r], s, r).start(priority=r % 2)

    def drain(s):
        for r in range(EXPERT_ROWS):
            row_copy(0, s, r).wait()

    @pl.when(i == 0)
    def _():
        issue(cur_ref, 0)

    @pl.when(jnp.logical_and(used, new_expert))
    def _():
        ff = wg_ref.shape[2]
        wgu_bf[:, :ff] = wg_ref[0].astype(BF16)
        wgu_bf[:, ff:] = wu_ref[0].astype(BF16)
        wd_bf[...] = wd_ref[0].astype(BF16)

    @pl.when(used)
    def _():
        drain(slot)
        issue(nxt_ref, 1 - slot)
        ff = wg_ref.shape[2]
        for r in range(0, EXPERT_ROWS, EXPERT_SUB):
            rows = pl.ds(r * ROW_SUB, EXPERT_SUB * ROW_SUB)
            x = _load_row_tiles(xbuf.at[slot, rows, :], EXPERT_SUB).astype(BF16)
            gu = jnp.dot(x, wgu_bf[...], preferred_element_type=F32)
            gate = gu[:, :ff]
            act = (gate * _sigmoid(gate) * gu[:, ff:]).astype(BF16)
            _store_row_tiles(y_ref.at[rows, :], jnp.dot(act, wd_bf[...], preferred_element_type=F32))

    @pl.when(i == n_used - 1)
    def _():
        drain(1 - slot)


def _experts(blk_e, n_used, row_tok, h_rt, wg, wu, wd):
    d, ff = wg.shape[1], wg.shape[2]
    nblk = row_tok.shape[0]
    last = lambda i, nu: jnp.minimum(i, nu[0] - 1)
    wsel = lambda i, be, nu: (be[last(i, nu)], 0, 0)
    return pl.pallas_call(
        _experts_kernel,
        out_shape=jax.ShapeDtypeStruct((nblk * EXPERT_ROWS * ROW_SUB, LANES), jnp.uint32),
        grid_spec=pltpu.PrefetchScalarGridSpec(
            num_scalar_prefetch=2,
            grid=(nblk,),
            in_specs=[pl.BlockSpec((1, 1, EXPERT_ROWS), lambda i, be, nu: (last(i, nu), 0, 0),
                                   memory_space=pltpu.SMEM),
                      pl.BlockSpec((1, 1, EXPERT_ROWS), lambda i, be, nu: (last(i + 1, nu), 0, 0),
                                   memory_space=pltpu.SMEM),
                      pl.BlockSpec(memory_space=pl.ANY),
                      pl.BlockSpec((1, d, ff), wsel),
                      pl.BlockSpec((1, d, ff), wsel),
                      pl.BlockSpec((1, ff, d), wsel)],
            out_specs=pl.BlockSpec((EXPERT_ROWS * ROW_SUB, LANES), lambda i, be, nu: (last(i, nu), 0)),
            scratch_shapes=[pltpu.VMEM((2, EXPERT_ROWS * ROW_SUB, LANES), jnp.uint32),
                            pltpu.SemaphoreType.DMA((2,)),
                            pltpu.VMEM((d, 2 * ff), BF16), pltpu.VMEM((ff, d), BF16)]),
        compiler_params=_params(("arbitrary",)),
        name="experts",
    )(blk_e, n_used, row_tok, row_tok, h_rt, wg, wu, wd)


def _combine_kernel(dest_ref, next_ref, ys_ref, w_ref, base_ref, mod_ref, fw_ref, o_ref, buf, sem):
    tc = dest_ref.shape[1]
    i = pl.program_id(0)
    slot = i % 2

    def row_copy(row, s, t, k):
        return pltpu.make_async_copy(_row_tile(ys_ref, row), _row_tile(buf.at[s, k], t), sem.at[s])

    def issue(d_ref, s):
        for t in range(tc):
            for k in range(TOP_K):
                row_copy(d_ref[k, t], s, t, k).start(priority=k % 2)

    def drain(s):
        for t in range(tc):
            for k in range(TOP_K):
                row_copy(0, s, t, k).wait()

    @pl.when(i == 0)
    def _():
        issue(dest_ref, 0)

    drain(slot)
    issue(next_ref, 1 - slot)
    w = w_ref[...]
    routed = w[:, 0:1] * _load_row_tiles(buf.at[slot, 0], tc)
    for k in range(1, TOP_K):
        routed = routed + w[:, k:k + 1] * _load_row_tiles(buf.at[slot, k], tc)
    x = base_ref[0] + mod_ref[0, 5:6, :] * routed
    o_ref[0] = _rms(x, fw_ref[...])

    @pl.when(i == pl.num_programs(0) - 1)
    def _():
        drain(1 - slot)


def _combine(dest, ys, wt, base, mods, fw, tc):
    b, l, d = base.shape
    per_b = l // tc
    steps = b * per_b
    return pl.pallas_call(
        _combine_kernel,
        out_shape=jax.ShapeDtypeStruct((b, l, d), F32),
        grid=(steps,),
        in_specs=[pl.BlockSpec((TOP_K, tc), lambda i: (0, i), memory_space=pltpu.SMEM),
                  pl.BlockSpec((TOP_K, tc), lambda i: (0, jnp.minimum(i + 1, steps - 1)), memory_space=pltpu.SMEM),
                  pl.BlockSpec(memory_space=pl.ANY),
                  pl.BlockSpec((tc, TOP_K), lambda i: (i, 0)),
                  pl.BlockSpec((1, tc, d), lambda i: (i // per_b, i % per_b, 0)),
                  pl.BlockSpec((1, 6, d), lambda i: (i // per_b, 0, 0)),
                  pl.BlockSpec((1, d), lambda i: (0, 0))],
        out_specs=pl.BlockSpec((1, tc, d), lambda i: (i // per_b, i % per_b, 0)),
        scratch_shapes=[pltpu.VMEM((2, TOP_K, tc * ROW_SUB, LANES), jnp.uint32), pltpu.SemaphoreType.DMA((2,))],
        compiler_params=_params(("arbitrary",)),
        name="combine",
    )(dest, dest, ys, wt, base, mods, fw)


def _rope_tables(l):
    t = jnp.arange(l, dtype=jnp.int32)
    row = (t // GRID_W).astype(F32)
    col = (t % GRID_W).astype(F32)
    n_freq = HEAD_DIM // 4
    inv = ROPE_THETA ** (-jnp.arange(n_freq, dtype=F32) / n_freq)
    ang = jnp.concatenate([row[:, None] * inv, col[:, None] * inv], axis=-1)
    cos = jnp.repeat(jnp.cos(ang), 2, axis=1)
    sin = jnp.repeat(jnp.sin(ang), 2, axis=1)
    sign = jnp.tile(jnp.array([-1.0, 1.0], F32), HEAD_DIM // 2)
    reps = LANES // HEAD_DIM
    return jnp.tile(cos, (1, reps)), jnp.tile(sin * sign, (1, reps))


def _filter_features(l):
    t = jnp.linspace(0.0, 1.0, l, dtype=F32)[:, None]
    bands = (FILTER_EMB - 1) // 2
    w = 2.0 * math.pi * jnp.arange(l, dtype=F32)[:, None] / l
    f = jnp.linspace(1e-4, bands - 1, bands, dtype=F32)[None, :]
    z = jnp.concatenate([t, jnp.cos(f * w), -jnp.sin(f * w)], axis=-1)
    min_decay = math.log(FILTER_TARGET) / FILTER_DECAY_FAST
    max_decay = math.log(FILTER_TARGET) / FILTER_DECAY_SLOW
    deltas = jnp.linspace(min_decay, max_decay, HYENA_WIDTH, dtype=F32)[None, :]
    return jnp.pad(z, ((0, 0), (0, LANES - FILTER_EMB))), deltas


def _dft_matrices(l):
    n = 2 * l
    idx = jnp.arange(l, dtype=jnp.int32)
    r = math.isqrt(l)
    assert r * r == l
    sub = jnp.arange(r, dtype=jnp.int32)
    hi = ((r * sub[:, None] * idx[None, :]) % n).astype(F32) * (2.0 * math.pi / n)
    lo = ((sub[:, None] * idx[None, :]) % n).astype(F32) * (2.0 * math.pi / n)
    ch, sh, cl, sl = jnp.cos(hi)[:, None, :], jnp.sin(hi)[:, None, :], jnp.cos(lo)[None], jnp.sin(lo)[None]
    c = (ch * cl - sh * sl).reshape(l, l)
    s = (sh * cl + ch * sl).reshape(l, l)
    alt = jnp.where(idx % 2 == 0, 1.0, -1.0).astype(F32)
    first = (idx == 0)[:, None]
    fwd = jnp.concatenate([c, jnp.where(first, alt[None, :], -s)], axis=0).astype(BF16)
    firstc = (idx == 0)[None, :]
    inv_r = (jnp.where(firstc, 1.0, 2.0) * c / n).astype(BF16)
    inv_i = (jnp.where(firstc, alt[:, None], -2.0 * s) / n).astype(BF16)
    return fwd, inv_r, inv_i


def _head_perm():
    order = []
    for j in range(N_HEADS // 2):
        order += list(range(j * HEAD_DIM, (j + 1) * HEAD_DIM))
        order += list(range((j + N_HEADS // 2) * HEAD_DIM, (j + 1 + N_HEADS // 2) * HEAD_DIM))
    return jnp.array(order, jnp.int32)


def _pad2(a, rows, cols):
    return jnp.pad(a, ((0, rows - a.shape[0]), (0, cols - a.shape[1])))


def kernel(x, c, ctx, c_ctx, mod_w, mod_b, norm1_w, w_in, q_norm_w, k_norm_w, conv_w, conv_b, filt_w1, filt_b1, filt_w2, filt_b2, filt_w3, filt_b3, filt_w4, filt_freq, hyena_bias, attn_out_norm_w, hyena_out_norm_w, w_out, norm2_w, router_w, router_bias, exp_w_gate, exp_w_up, exp_w_down, sh_w_gate, sh_w_up, sh_w_down, final_norm_w):
    b, l, d = x.shape
    t = b * l
    assert mod_w.shape[0] == 1, "single-layer stack"
    tl = min(512, l)

    cond = jnp.concatenate([c, c_ctx[None, :], jnp.zeros((-(b + 1) % 8, d), F32)], axis=0)
    mod = _adaln(cond, mod_w[0], mod_b[0][None, :])
    mods = mod[:b].reshape(b, 6, d)
    cmod = mod[b].reshape(6, d)

    perm = _head_perm()
    w_in0 = w_in[0]
    w_in_k = jnp.concatenate([w_in0[:, :Q_END][:, perm], w_in0[:, Q_END:]], axis=1).astype(BF16)
    w_kv = w_in0[:, Q_END:V_END].astype(BF16)
    gq = jnp.kron(jnp.eye(N_HEADS, dtype=F32), jnp.full((HEAD_DIM, HEAD_DIM), 1.0 / HEAD_DIM, F32)).astype(BF16)
    qnw = jnp.tile(q_norm_w[0], N_HEADS)[None, :]
    knw = jnp.tile(k_norm_w[0], N_KV_HEADS)[None, :]
    n1w = norm1_w[0][None, :]
    cos, sin = _rope_tables(l)

    kc, vc = _ctx_kv(ctx, cmod, n1w, w_kv, gq[:KV_WIDTH, :KV_WIDTH], knw)
    q, k, v, u = _inproj(x, mods, n1w, w_in_k, gq, qnw, knw, cos, sin, tl)
    k_all = jnp.concatenate([kc, k], axis=1)
    v_all = jnp.concatenate([vc, v], axis=1)
    an = _attention(q, k_all, v_all, attn_out_norm_w[0][perm][None, :], min(256, l), min(256, l))

    cw = conv_w[0].reshape(3, 3, HYENA_WIDTH).transpose(1, 0, 2)
    cb = conv_b[0].reshape(3, HYENA_WIDTH)
    gbf, x0 = _hyena_pre(u, cw, cb)
    z, deltas = _filter_features(l)
    fo = filt_w2.shape[1]
    hsd = _hyena_filter(
        z, _pad2(filt_w1[0], LANES, LANES), _pad2(filt_b1[0][None, :], 1, LANES),
        _pad2(filt_w2[0], LANES, LANES), _pad2(filt_b2[0][None, :], 1, LANES),
        _pad2(filt_w3[0], LANES, LANES), _pad2(filt_b3[0][None, :], 1, LANES),
        _pad2(filt_w4[0], LANES, 2 * HYENA_WIDTH), _pad2(filt_freq[0][None, :], 1, LANES), deltas, tl)
    del fo
    fwd, inv_r, inv_i = _dft_matrices(l)
    spec = _dft(fwd, hsd, tl)
    row0 = (jnp.arange(l) == 0)[:, None]
    sa = spec[0, :l]
    sd = jnp.where(row0, spec[0, l:l + 1], sa)
    sb = jnp.where(row0, 0.0, spec[1, l:])
    zr, zi = _dft_mul(fwd, gbf, sa, sb, sd, tl)
    yn = _idft(inv_r, inv_i, zr, zi, gbf, x0, hyena_bias[0][None, :], hyena_out_norm_w[0][None, :], tl)

    w_out0 = w_out[0]
    base, h2hi, h2lo, h2rt = _merge(
        an, yn, x, mods, w_out0[:ATTN_WIDTH][perm].astype(BF16), w_out0[ATTN_WIDTH:].astype(BF16),
        norm2_w[0][None, :], sh_w_gate[0].astype(BF16), sh_w_up[0].astype(BF16), sh_w_down[0].astype(BF16), tl)

    tt = 256
    rwt = router_w[0].T
    rw_hi = rwt.astype(BF16)
    rw_lo = (rwt - rw_hi.astype(F32)).astype(BF16)
    bias = jnp.broadcast_to(router_bias[0][:, None], (N_EXPERTS, tt))
    ti = jnp.arange(tt)
    tri = jnp.stack([(ti[:, None] < ti[None, :]), jnp.ones((tt, tt), bool)]).astype(BF16)
    idx, wgt, pos, cnt = _router(h2hi.reshape(t, d), h2lo.reshape(t, d), rw_hi, rw_lo, bias, tri, tt)

    counts = cnt[:, 0].astype(jnp.int32)
    padded = (counts + EXPERT_ROWS - 1) // EXPERT_ROWS * EXPERT_ROWS
    pad_end = jnp.cumsum(padded)
    pad_start = pad_end - padded
    td = min(1024, t)
    dest = _dest(idx, pos, jnp.broadcast_to(pad_start.astype(F32)[:, None], (N_EXPERTS, td)), td)
    n_rows = (t * TOP_K + N_EXPERTS * (EXPERT_ROWS - 1) + EXPERT_ROWS - 1) // EXPERT_ROWS * EXPERT_ROWS
    nblk = n_rows // EXPERT_ROWS
    blk_row = jnp.arange(nblk, dtype=jnp.int32) * EXPERT_ROWS
    blk_e = jnp.minimum(jnp.sum((pad_end[None, :] <= blk_row[:, None]).astype(jnp.int32), axis=1), N_EXPERTS - 1)
    n_used = (pad_end[-1:] // EXPERT_ROWS).astype(jnp.int32)

    row_tok = (_row_assignment(dest.reshape(TOP_K * t), n_rows, t) % t).reshape(nblk, 1, EXPERT_ROWS)
    ys = _experts(blk_e, n_used, row_tok, h2rt, exp_w_gate[0], exp_w_up[0], exp_w_down[0])
    return _combine(dest, ys, wgt.T, base, mods, final_norm_w[None, :], min(128, l))
```

```python
import functools
import math

import jax
import jax.numpy as jnp
from jax import lax
from jax.experimental import pallas as pl
from jax.experimental.pallas import tpu as pltpu

F32 = jnp.float32
BF16 = jnp.bfloat16
HIGHEST = lax.Precision.HIGHEST

GRID_W = 64
N_HEADS = 8
N_KV_HEADS = 2
HEAD_DIM = 64
ATTN_WIDTH = N_HEADS * HEAD_DIM
KV_WIDTH = N_KV_HEADS * HEAD_DIM
HYENA_WIDTH = 512
Q_END = ATTN_WIDTH
K_END = Q_END + KV_WIDTH
V_END = K_END + KV_WIDTH
ROPE_THETA = 10000.0
FILTER_EMB = 33
FILTER_DECAY_FAST = 0.3
FILTER_DECAY_SLOW = 1.5
FILTER_TARGET = 1e-2
N_EXPERTS = 256
TOP_K = 8
N_GROUPS = 8
TOPK_GROUPS = 4
ROUTE_SCALE = 2.5
EPS = 1e-6

LANES = 128
ROW_SUB = 4
EXPERT_ROWS = 512
EXPERT_SUB = 256
NEG_INF = float("-inf")


def _params(semantics, vmem_mb=48):
    return pltpu.CompilerParams(dimension_semantics=semantics, vmem_limit_bytes=vmem_mb * 1024 * 1024)


def _rms(x, w):
    return x * lax.rsqrt(jnp.mean(x * x, axis=-1, keepdims=True) + EPS) * w


def _sigmoid(x):
    return 1.0 / (1.0 + jnp.exp(-x))


def _adaln_kernel(c_ref, w_ref, b_ref, o_ref):
    c = c_ref[...]
    s = c * _sigmoid(c)
    o_ref[...] = jnp.dot(s, w_ref[...], precision=HIGHEST, preferred_element_type=F32) + b_ref[...]


def _adaln(cond, w, b):
    rows, d = cond.shape
    n = w.shape[1]
    tn = 1536
    return pl.pallas_call(
        _adaln_kernel,
        out_shape=jax.ShapeDtypeStruct((rows, n), F32),
        grid=(n // tn,),
        in_specs=[pl.BlockSpec((rows, d), lambda j: (0, 0)),
                  pl.BlockSpec((d, tn), lambda j: (0, j)),
                  pl.BlockSpec((1, tn), lambda j: (0, j))],
        out_specs=pl.BlockSpec((rows, tn), lambda j: (0, j)),
        compiler_params=_params(("arbitrary",)),
        name="adaln",
    )(cond, w, b)


def _head_rms(t, gmat, w):
    ms = jnp.dot((t * t).astype(BF16), gmat, preferred_element_type=F32)
    return t * lax.rsqrt(ms + EPS) * w


def _modulated(x, norm_w, shift, scale):
    return _rms(x, norm_w) * (1.0 + scale) + shift


def _ctx_kv_kernel(ctx_ref, mod_ref, n1_ref, w_ref, g_ref, kn_ref, kc_ref, vc_ref):
    x = ctx_ref[0]
    h = _modulated(x, n1_ref[...], mod_ref[0:1, :], mod_ref[1:2, :])
    kv = jnp.dot(h.astype(BF16), w_ref[...], preferred_element_type=F32)
    k = _head_rms(kv[:, :KV_WIDTH], g_ref[...], kn_ref[...])
    v = kv[:, KV_WIDTH:]
    kc_ref[0] = k.astype(BF16)
    vc_ref[0] = jnp.concatenate([v, jnp.ones_like(v)], axis=1).astype(BF16)


def _ctx_kv(ctx, cmod, n1w, w_kv, gk, knw):
    b, c, d = ctx.shape
    return pl.pallas_call(
        _ctx_kv_kernel,
        out_shape=(jax.ShapeDtypeStruct((b, c, KV_WIDTH), BF16),
                   jax.ShapeDtypeStruct((b, c, 2 * KV_WIDTH), BF16)),
        grid=(b,),
        in_specs=[pl.BlockSpec((1, c, d), lambda i: (i, 0, 0)),
                  pl.BlockSpec(cmod.shape, lambda i: (0, 0)),
                  pl.BlockSpec((1, d), lambda i: (0, 0)),
                  pl.BlockSpec(w_kv.shape, lambda i: (0, 0)),
                  pl.BlockSpec(gk.shape, lambda i: (0, 0)),
                  pl.BlockSpec((1, KV_WIDTH), lambda i: (0, 0))],
        out_specs=(pl.BlockSpec((1, c, KV_WIDTH), lambda i: (i, 0, 0)),
                   pl.BlockSpec((1, c, 2 * KV_WIDTH), lambda i: (i, 0, 0))),
        compiler_params=_params(("arbitrary",)),
        name="ctx_kv",
    )(ctx, cmod, n1w, w_kv, gk, knw)


def _rope(t, cos, sin, even):
    width = t.shape[1]
    partner = jnp.where(even, pltpu.roll(t, width - 1, axis=1), pltpu.roll(t, 1, axis=1))
    return t * cos + partner * sin


def _inproj_kernel(x_ref, mod_ref, n1_ref, w_ref, gq_ref, qn_ref, kn_ref, cos_ref, sin_ref,
                   q_ref, k_ref, v_ref, u_ref):
    x = x_ref[0]
    h = _modulated(x, n1_ref[...], mod_ref[0, 0:1, :], mod_ref[0, 1:2, :])
    p = jnp.dot(h.astype(BF16), w_ref[...], preferred_element_type=F32)
    gq = gq_ref[...]
    q = _head_rms(p[:, :Q_END], gq, qn_ref[...])
    k = _head_rms(p[:, Q_END:K_END], gq[:KV_WIDTH, :KV_WIDTH], kn_ref[...])
    v = p[:, K_END:V_END]
    cos = cos_ref[...]
    sin = sin_ref[...]
    reps = Q_END // LANES
    cos_q = jnp.concatenate([cos] * reps, axis=1)
    sin_q = jnp.concatenate([sin] * reps, axis=1)
    even_q = (lax.broadcasted_iota(jnp.int32, (1, Q_END), 1) & 1) == 0
    even_k = (lax.broadcasted_iota(jnp.int32, (1, KV_WIDTH), 1) & 1) == 0
    q = _rope(q, cos_q, sin_q, even_q) * (HEAD_DIM ** -0.5)
    k = _rope(k, cos, sin, even_k)
    q_ref[0] = q.astype(BF16)
    k_ref[0] = k.astype(BF16)
    v_ref[0] = jnp.concatenate([v, jnp.ones_like(v)], axis=1).astype(BF16)
    u_ref[0] = p[:, V_END:].astype(BF16)


def _inproj(x, mods, n1w, w_in, gq, qnw, knw, cos, sin, tl):
    b, l, d = x.shape
    ncol = w_in.shape[1]
    nu = ncol - V_END
    return pl.pallas_call(
        _inproj_kernel,
        out_shape=(jax.ShapeDtypeStruct((b, l, Q_END), BF16),
                   jax.ShapeDtypeStruct((b, l, KV_WIDTH), BF16),
                   jax.ShapeDtypeStruct((b, l, 2 * KV_WIDTH), BF16),
                   jax.ShapeDtypeStruct((b, l, nu), BF16)),
        grid=(l // tl, b),
        in_specs=[pl.BlockSpec((1, tl, d), lambda i, j: (j, i, 0)),
                  pl.BlockSpec((1, 6, d), lambda i, j: (j, 0, 0)),
                  pl.BlockSpec((1, d), lambda i, j: (0, 0)),
                  pl.BlockSpec((d, ncol), lambda i, j: (0, 0)),
                  pl.BlockSpec(gq.shape, lambda i, j: (0, 0)),
                  pl.BlockSpec((1, Q_END), lambda i, j: (0, 0)),
                  pl.BlockSpec((1, KV_WIDTH), lambda i, j: (0, 0)),
                  pl.BlockSpec((tl, LANES), lambda i, j: (i, 0)),
                  pl.BlockSpec((tl, LANES), lambda i, j: (i, 0))],
        out_specs=(pl.BlockSpec((1, tl, Q_END), lambda i, j: (j, i, 0)),
                   pl.BlockSpec((1, tl, KV_WIDTH), lambda i, j: (j, i, 0)),
                   pl.BlockSpec((1, tl, 2 * KV_WIDTH), lambda i, j: (j, i, 0)),
                   pl.BlockSpec((1, tl, nu), lambda i, j: (j, i, 0))),
        compiler_params=_params(("arbitrary", "arbitrary")),
        name="inproj",
    )(x, mods, n1w, w_in, gq, qnw, knw, cos, sin)


def _attn_kernel(q_ref, k_ref, v_ref, wn_ref, o_ref, *, sub):
    tq = q_ref.shape[1]
    low = lax.broadcasted_iota(jnp.int32, (1, LANES), 1) < HEAD_DIM
    nt = (((1,), (1,)), ((), ()))
    kk = k_ref[0]
    vv = v_ref[0]
    for r in range(0, tq, sub):
        outs = []
        for j in range(Q_END // LANES):
            qv = q_ref[0, r:r + sub, LANES * j:LANES * (j + 1)]
            zero = jnp.zeros_like(qv)
            halves = []
            for g in range(N_KV_HEADS):
                qh = jnp.where(low, qv, zero) if g == 0 else jnp.where(low, zero, qv)
                s = lax.dot_general(qh, kk, nt, preferred_element_type=F32)
                p = jnp.exp(s - jnp.max(s, axis=-1, keepdims=True)).astype(BF16)
                pv = jnp.dot(p, vv, preferred_element_type=F32)
                halves.append(pv[:, :LANES] / pv[:, LANES:])
            outs.append(jnp.where(low, halves[0], halves[1]))
        a = jnp.concatenate(outs, axis=1)
        o_ref[0, r:r + sub, :] = _rms(a, wn_ref[...]).astype(BF16)


def _attention(q, k, v, wn, tq, sub):
    b, l, _ = q.shape
    n = k.shape[1]
    return pl.pallas_call(
        functools.partial(_attn_kernel, sub=sub),
        out_shape=jax.ShapeDtypeStruct((b, l, Q_END), BF16),
        grid=(b, l // tq),
        in_specs=[pl.BlockSpec((1, tq, Q_END), lambda i, j: (i, j, 0)),
                  pl.BlockSpec((1, n, KV_WIDTH), lambda i, j: (i, 0, 0)),
                  pl.BlockSpec((1, n, 2 * KV_WIDTH), lambda i, j: (i, 0, 0)),
                  pl.BlockSpec((1, Q_END), lambda i, j: (0, 0))],
        out_specs=pl.BlockSpec((1, tq, Q_END), lambda i, j: (i, j, 0)),
        compiler_params=_params(("arbitrary", "arbitrary")),
        name="attn",
    )(q, k, v, wn)


def _hyena_pre_kernel(u0_ref, u1_ref, u2_ref, cw_ref, cb_ref, g_ref, x0_ref):
    l = u0_ref.shape[1]
    row = lax.broadcasted_iota(jnp.int32, (l, LANES), 0)

    def conv(u_ref, gi):
        u = u_ref[0].astype(F32)
        prev = jnp.where(row == 0, 0.0, pltpu.roll(u, 1, axis=0))
        nxt = jnp.where(row == l - 1, 0.0, pltpu.roll(u, l - 1, axis=0))
        w = cw_ref[gi]
        return w[0:1] * prev + w[1:2] * u + w[2:3] * nxt + cb_ref[gi:gi + 1, :]

    x0 = conv(u0_ref, 0)
    x1 = conv(u1_ref, 1)
    v = conv(u2_ref, 2)
    g = v * x1
    g_ref[0] = g.astype(BF16)
    x0_ref[0] = x0.astype(BF16)


def _hyena_pre(u, cw, cb):
    b, l, _ = u.shape
    nblk = HYENA_WIDTH // LANES
    ublk = lambda gi: pl.BlockSpec((1, l, LANES), lambda i, j: (i, 0, gi * nblk + j))
    oblk = pl.BlockSpec((1, l, LANES), lambda i, j: (i, 0, j))
    return pl.pallas_call(
        _hyena_pre_kernel,
        out_shape=(jax.ShapeDtypeStruct((b, l, HYENA_WIDTH), BF16),
                   jax.ShapeDtypeStruct((b, l, HYENA_WIDTH), BF16)),
        grid=(b, nblk),
        in_specs=[ublk(0), ublk(1), ublk(2),
                  pl.BlockSpec((3, 3, LANES), lambda i, j: (0, 0, j)),
                  pl.BlockSpec((3, LANES), lambda i, j: (0, j))],
        out_specs=(oblk, oblk),
        compiler_params=_params(("arbitrary", "arbitrary")),
        name="hyena_pre",
    )(u, u, u, cw, cb)


def _filter_kernel(z_ref, w1_ref, b1_ref, w2_ref, b2_ref, w3_ref, b3_ref, w4_ref, fr_ref, dl_ref, o_ref):
    tl = z_ref.shape[0]
    z = z_ref[...]
    fr = fr_ref[...]
    dot = lambda a, w: jnp.dot(a, w, precision=HIGHEST, preferred_element_type=F32)
    h = jnp.sin(fr * (dot(z, w1_ref[...]) + b1_ref[...]))
    h = jnp.sin(fr * (dot(h, w2_ref[...]) + b2_ref[...]))
    h = jnp.sin(fr * (dot(h, w3_ref[...]) + b3_ref[...]))
    h = dot(h, w4_ref[...])
    t = z[:, 0:1]
    decay = jnp.exp(-t * jnp.abs(dl_ref[...]))
    hf = h[:, :HYENA_WIDTH] * decay
    hb = h[:, HYENA_WIDTH:] * decay
    row = lax.broadcasted_iota(jnp.int32, (tl, HYENA_WIDTH), 0) + pl.program_id(0) * tl
    hb = jnp.where(row == 0, 0.0, hb)
    o_ref[0] = hf + hb
    o_ref[1] = hf - hb


def _hyena_filter(z, w1, b1, w2, b2, w3, b3, w4, freq, deltas, tl):
    l = z.shape[0]
    full = lambda a: pl.BlockSpec(a.shape, lambda i: (0,) * a.ndim)
    return pl.pallas_call(
        _filter_kernel,
        out_shape=jax.ShapeDtypeStruct((2, l, HYENA_WIDTH), F32),
        grid=(l // tl,),
        in_specs=[pl.BlockSpec((tl, z.shape[1]), lambda i: (i, 0)),
                  full(w1), full(b1), full(w2), full(b2), full(w3), full(b3), full(w4), full(freq), full(deltas)],
        out_specs=pl.BlockSpec((2, tl, HYENA_WIDTH), lambda i: (0, i, 0)),
        compiler_params=_params(("arbitrary",)),
        name="hyena_filter",
    )(z, w1, b1, w2, b2, w3, b3, w4, freq, deltas)


def _dft_kernel(f_ref, x_ref, o_ref):
    o_ref[0] = jnp.dot(f_ref[...], x_ref[0].astype(BF16), preferred_element_type=F32)


def _dft(fmat, x, tf):
    nb, l, w = x.shape
    n = fmat.shape[0]
    return pl.pallas_call(
        _dft_kernel,
        out_shape=jax.ShapeDtypeStruct((nb, n, w), F32),
        grid=(n // tf, nb),
        in_specs=[pl.BlockSpec((tf, l), lambda i, j: (i, 0)),
                  pl.BlockSpec((1, l, w), lambda i, j: (j, 0, 0))],
        out_specs=pl.BlockSpec((1, tf, w), lambda i, j: (j, i, 0)),
        compiler_params=_params(("arbitrary", "arbitrary")),
        name="dft_filter",
    )(fmat, x)


def _dft_mul_kernel(fr_ref, fi_ref, x_ref, a_ref, b_ref, d_ref, zr_ref, zi_ref):
    x = x_ref[0]
    xr = jnp.dot(fr_ref[...], x, preferred_element_type=F32)
    xi = jnp.dot(fi_ref[...], x, preferred_element_type=F32)
    bb = b_ref[...]
    zr_ref[0] = (xr * a_ref[...] - xi * bb).astype(BF16)
    zi_ref[0] = (xr * bb + xi * d_ref[...]).astype(BF16)


def _dft_mul(fmat, g, sa, sb, sd, tf):
    b, l, w = g.shape
    nf = l // tf
    spec = pl.BlockSpec((tf, w), lambda i, j: (i, 0))
    return pl.pallas_call(
        _dft_mul_kernel,
        out_shape=(jax.ShapeDtypeStruct((b, l, w), BF16), jax.ShapeDtypeStruct((b, l, w), BF16)),
        grid=(nf, b),
        in_specs=[pl.BlockSpec((tf, l), lambda i, j: (i, 0)),
                  pl.BlockSpec((tf, l), lambda i, j: (i + nf, 0)),
                  pl.BlockSpec((1, l, w), lambda i, j: (j, 0, 0)),
                  spec, spec, spec],
        out_specs=(pl.BlockSpec((1, tf, w), lambda i, j: (j, i, 0)),
                   pl.BlockSpec((1, tf, w), lambda i, j: (j, i, 0))),
        compiler_params=_params(("arbitrary", "arbitrary")),
        name="dft_mul",
    )(fmat, fmat, g, sa, sb, sd)


def _idft_kernel(fr_ref, fi_ref, zr_ref, zi_ref, g_ref, x0_ref, hb_ref, wn_ref, o_ref):
    conv = (jnp.dot(fr_ref[...], zr_ref[0], preferred_element_type=F32)
            + jnp.dot(fi_ref[...], zi_ref[0], preferred_element_type=F32))
    y = (conv + g_ref[0].astype(F32) * hb_ref[...]) * x0_ref[0].astype(F32)
    o_ref[0] = _rms(y, wn_ref[...]).astype(BF16)


def _idft(finv_r, finv_i, zr, zi, g, x0, hbias, wn, tt):
    b, l, w = zr.shape
    tile = pl.BlockSpec((1, tt, w), lambda i, j: (j, i, 0))
    return pl.pallas_call(
        _idft_kernel,
        out_shape=jax.ShapeDtypeStruct((b, l, w), BF16),
        grid=(l // tt, b),
        in_specs=[pl.BlockSpec((tt, l), lambda i, j: (i, 0)),
                  pl.BlockSpec((tt, l), lambda i, j: (i, 0)),
                  pl.BlockSpec((1, l, w), lambda i, j: (j, 0, 0)),
                  pl.BlockSpec((1, l, w), lambda i, j: (j, 0, 0)),
                  tile, tile,
                  pl.BlockSpec((1, w), lambda i, j: (0, 0)),
                  pl.BlockSpec((1, w), lambda i, j: (0, 0))],
        out_specs=tile,
        compiler_params=_params(("arbitrary", "arbitrary")),
        name="idft",
    )(finv_r, finv_i, zr, zi, g, x0, hbias, wn)


def _bf16_bits(v):
    return lax.bitcast_convert_type(v.astype(BF16).astype(F32), jnp.uint32)


def _store_row_tiles(ref, val):
    rows, half = val.shape[0], val.shape[1] // 2
    assert half == ROW_SUB * LANES
    for j in range(ROW_SUB):
        lo = _bf16_bits(val[:, LANES * j:LANES * (j + 1)]) >> 16
        hi = _bf16_bits(val[:, half + LANES * j:half + LANES * (j + 1)]) & jnp.uint32(0xFFFF0000)
        ref[pl.ds(j, rows, stride=ROW_SUB), :] = lo | hi


def _load_row_tiles(ref, rows):
    words = [ref[pl.ds(j, rows, stride=ROW_SUB), :] for j in range(ROW_SUB)]
    lo = [lax.bitcast_convert_type(w << 16, F32) for w in words]
    hi = [lax.bitcast_convert_type(w & jnp.uint32(0xFFFF0000), F32) for w in words]
    return jnp.concatenate(lo + hi, axis=1)


def _merge_kernel(a_ref, y_ref, x_ref, mod_ref, wa_ref, wy_ref, n2_ref, sg_ref, su_ref, sd_ref,
                  base_ref, hi_ref, lo_ref, rt_ref):
    m = (jnp.dot(a_ref[0], wa_ref[...], preferred_element_type=F32)
         + jnp.dot(y_ref[0], wy_ref[...], preferred_element_type=F32))
    x1 = x_ref[0] + mod_ref[0, 2:3, :] * m
    h2 = _modulated(x1, n2_ref[...], mod_ref[0, 3:4, :], mod_ref[0, 4:5, :])
    hi = h2.astype(BF16)
    hi_ref[0] = hi
    lo_ref[0] = (h2 - hi.astype(F32)).astype(BF16)
    _store_row_tiles(rt_ref, h2)
    gate = jnp.dot(hi, sg_ref[...], preferred_element_type=F32)
    up = jnp.dot(hi, su_ref[...], preferred_element_type=F32)
    act = (gate * _sigmoid(gate) * up).astype(BF16)
    shared = jnp.dot(act, sd_ref[...], preferred_element_type=F32)
    base_ref[0] = x1 + mod_ref[0, 5:6, :] * shared


def _merge(an, yn, x, mods, wa, wy, n2w, sg, su, sd, tl):
    b, l, d = x.shape
    full = lambda a: pl.BlockSpec(a.shape, lambda i, j: (0,) * a.ndim)
    half = pl.BlockSpec((1, tl, an.shape[2]), lambda i, j: (i, j, 0))
    wide = pl.BlockSpec((1, tl, d), lambda i, j: (i, j, 0))
    per_b = l // tl
    return pl.pallas_call(
        _merge_kernel,
        out_shape=(jax.ShapeDtypeStruct((b, l, d), F32),
                   jax.ShapeDtypeStruct((b, l, d), BF16),
                   jax.ShapeDtypeStruct((b, l, d), BF16),
                   jax.ShapeDtypeStruct((b * l * ROW_SUB, LANES), jnp.uint32)),
        grid=(b, per_b),
        in_specs=[half, half, wide,
                  pl.BlockSpec((1, 6, d), lambda i, j: (i, 0, 0)),
                  full(wa), full(wy), full(n2w), full(sg), full(su), full(sd)],
        out_specs=(wide, wide, wide,
                   pl.BlockSpec((tl * ROW_SUB, LANES), lambda i, j: (i * per_b + j, 0))),
        compiler_params=_params(("arbitrary", "arbitrary")),
        name="merge",
    )(an, yn, x, mods, wa, wy, n2w, sg, su, sd)


def _router_kernel(hi_ref, lo_ref, whi_ref, wlo_ref, bias_ref, tri_ref,
                   idx_ref, wgt_ref, pos_ref, cnt_ref, run_ref):
    tt = hi_ref.shape[0]
    per_group = N_EXPERTS // N_GROUPS

    @pl.when(pl.program_id(0) == 0)
    def _():
        run_ref[...] = jnp.zeros_like(run_ref)

    nt = (((1,), (1,)), ((), ()))
    hi = hi_ref[...]
    whi = whi_ref[...]
    logits = (lax.dot_general(whi, hi, nt, preferred_element_type=F32)
              + lax.dot_general(whi, lo_ref[...], nt, preferred_element_type=F32)
              + lax.dot_general(wlo_ref[...], hi, nt, preferred_element_type=F32))
    scores = _sigmoid(logits)
    biased = scores + bias_ref[...]

    ridx = lax.broadcasted_iota(jnp.int32, (per_group, tt), 0)
    groups = [biased[g * per_group:(g + 1) * per_group, :] for g in range(N_GROUPS)]
    gs = []
    for blk in groups:
        m1 = jnp.max(blk, axis=0, keepdims=True)
        i1 = jnp.min(jnp.where(blk == m1, ridx, per_group), axis=0, keepdims=True)
        m2 = jnp.max(jnp.where(ridx == i1, NEG_INF, blk), axis=0, keepdims=True)
        gs.append(m1 + m2)

    kept = []
    for g in range(N_GROUPS):
        ahead = jnp.zeros((1, tt), F32)
        for o in range(N_GROUPS):
            if o != g:
                wins = (gs[o] >= gs[g]) if o < g else (gs[o] > gs[g])
                ahead = ahead + jnp.where(wins, 1.0, 0.0)
        kept.append(jnp.where(ahead < TOPK_GROUPS, groups[g], NEG_INF))
    cur = jnp.concatenate(kept, axis=0)

    eidx = lax.broadcasted_iota(jnp.int32, cur.shape, 0)
    kept_mask = cur
    picks = []
    wsel = []
    for _ in range(TOP_K):
        mx = jnp.max(cur, axis=0, keepdims=True)
        first = jnp.min(jnp.where(cur == mx, eidx, N_EXPERTS), axis=0, keepdims=True)
        sel = eidx == first
        picks.append(first)
        wsel.append(jnp.sum(jnp.where(sel, scores, 0.0), axis=0, keepdims=True))
        cur = jnp.where(sel, NEG_INF, cur)
    w = jnp.concatenate(wsel, axis=0)
    w = w / jnp.sum(w, axis=0, keepdims=True) * ROUTE_SCALE
    idx = jnp.concatenate(picks, axis=0)

    oh = jnp.where(cur == NEG_INF, jnp.where(kept_mask == NEG_INF, 0.0, 1.0), 0.0).astype(BF16)
    before = jnp.dot(oh, tri_ref[0], preferred_element_type=F32)
    total = jnp.dot(oh, tri_ref[1], preferred_element_type=F32)
    pos_ref[...] = run_ref[...] + before
    run_ref[...] = run_ref[...] + total

    idx_ref[...] = idx
    wgt_ref[...] = w
    cnt_ref[...] = run_ref[...]


def _router(hi, lo, whi, wlo, bias, tri, tt):
    t, d = hi.shape
    tok = pl.BlockSpec((tt, d), lambda i: (i, 0))
    full = lambda a: pl.BlockSpec(a.shape, lambda i: (0,) * a.ndim)
    out = pl.BlockSpec((TOP_K, tt), lambda i: (0, i))
    return pl.pallas_call(
        _router_kernel,
        out_shape=(jax.ShapeDtypeStruct((TOP_K, t), jnp.int32),
                   jax.ShapeDtypeStruct((TOP_K, t), F32),
                   jax.ShapeDtypeStruct((N_EXPERTS, t), F32),
                   jax.ShapeDtypeStruct((N_EXPERTS, tt), F32)),
        grid=(t // tt,),
        in_specs=[tok, tok, full(whi), full(wlo), full(bias), full(tri)],
        out_specs=(out, out, pl.BlockSpec((N_EXPERTS, tt), lambda i: (0, i)),
                   pl.BlockSpec((N_EXPERTS, tt), lambda i: (0, 0))),
        scratch_shapes=[pltpu.VMEM((N_EXPERTS, tt), F32)],
        compiler_params=_params(("arbitrary",)),
        name="router",
    )(hi, lo, whi, wlo, bias, tri)


def _row_tile(ref, r):
    return ref.at[pl.ds(pl.multiple_of(r * ROW_SUB, ROW_SUB), ROW_SUB), :]


def _dest_kernel(idx_ref, pos_ref, start_ref, dest_ref):
    eidx = lax.broadcasted_iota(jnp.int32, start_ref.shape, 0)
    row = start_ref[...] + pos_ref[...]
    rows = [jnp.sum(jnp.where(eidx == idx_ref[k:k + 1, :], row, 0.0), axis=0, keepdims=True)
            for k in range(TOP_K)]
    dest_ref[...] = jnp.concatenate(rows, axis=0).astype(jnp.int32)


def _dest(idx, pos, start, tt):
    t = idx.shape[1]
    blk = pl.BlockSpec((TOP_K, tt), lambda i: (0, i))
    return pl.pallas_call(
        _dest_kernel,
        out_shape=jax.ShapeDtypeStruct((TOP_K, t), jnp.int32),
        grid=(t // tt,),
        in_specs=[blk, pl.BlockSpec((N_EXPERTS, tt), lambda i: (0, i)), pl.BlockSpec(start.shape, lambda i: (0, 0))],
        out_specs=blk,
        compiler_params=_params(("arbitrary",)),
        name="dest",
    )(idx, pos, start)


def _zero_tails_kernel(last_ref, o_ref):
    del last_ref
    o_ref[...] = jnp.zeros(o_ref.shape, o_ref.dtype)


def _zero_tails(last_blk, n_rows):
    return pl.pallas_call(
        _zero_tails_kernel,
        out_shape=jax.ShapeDtypeStruct((n_rows * ROW_SUB, LANES), jnp.uint32),
        grid_spec=pltpu.PrefetchScalarGridSpec(
            num_scalar_prefetch=1,
            grid=(last_blk.shape[0],),
            in_specs=[],
            out_specs=pl.BlockSpec((EXPERT_ROWS * ROW_SUB, LANES), lambda e, last: (last[e], 0))),
        compiler_params=_params(("arbitrary",)),
        name="zero_tails",
    )(last_blk)


def _dispatch_kernel(dest_ref, h_ref, xs_in_ref, xs_ref, sem):
    del xs_in_ref
    td = dest_ref.shape[1]

    def row_copy(t, k):
        return pltpu.make_async_copy(_row_tile(h_ref, t), _row_tile(xs_ref, dest_ref[k, t]), sem)

    def issue(t, carry):
        for k in range(TOP_K):
            row_copy(t, k).start(priority=k % 2)
        return carry

    lax.fori_loop(0, td, issue, 0)

    def drain(t, carry):
        for k in range(TOP_K):
            row_copy(t, k).wait()
        return carry

    lax.fori_loop(0, td, drain, 0)


def _dispatch(dest, h_rt, xs0, td):
    t = dest.shape[1]
    return pl.pallas_call(
        _dispatch_kernel,
        out_shape=jax.ShapeDtypeStruct(xs0.shape, xs0.dtype),
        grid=(t // td,),
        in_specs=[pl.BlockSpec((TOP_K, td), lambda i: (0, i), memory_space=pltpu.SMEM),
                  pl.BlockSpec((td * ROW_SUB, LANES), lambda i: (i, 0)),
                  pl.BlockSpec(memory_space=pl.ANY)],
        out_specs=pl.BlockSpec(memory_space=pl.ANY),
        scratch_shapes=[pltpu.SemaphoreType.DMA],
        input_output_aliases={2: 0},
        compiler_params=_params(("arbitrary",)),
        name="dispatch",
    )(dest, h_rt, xs0)


def _experts_kernel(be_ref, nu_ref, x_ref, wg_ref, wu_ref, wd_ref, y_ref, wgu_bf, wd_bf):
    i = pl.program_id(0)
    used = i < nu_ref[0]
    new_expert = jnp.logical_or(i == 0, be_ref[i] != be_ref[jnp.maximum(i - 1, 0)])

    @pl.when(jnp.logical_and(used, new_expert))
    def _():
        ff = wg_ref.shape[2]
        wgu_bf[:, :ff] = wg_ref[0].astype(BF16)
        wgu_bf[:, ff:] = wu_ref[0].astype(BF16)
        wd_bf[...] = wd_ref[0].astype(BF16)

    @pl.when(used)
    def _():
        ff = wg_ref.shape[2]
        for r in range(0, EXPERT_ROWS, EXPERT_SUB):
            rows = pl.ds(r * ROW_SUB, EXPERT_SUB * ROW_SUB)
            x = _load_row_tiles(x_ref.at[rows, :], EXPERT_SUB).astype(BF16)
            gu = jnp.dot(x, wgu_bf[...], preferred_element_type=F32)
            gate = gu[:, :ff]
            act = (gate * _sigmoid(gate) * gu[:, ff:]).astype(BF16)
            _store_row_tiles(y_ref.at[rows, :], jnp.dot(act, wd_bf[...], preferred_element_type=F32))


def _experts(blk_e, n_used, xs, wg, wu, wd):
    d, ff = wg.shape[1], wg.shape[2]
    nblk = xs.shape[0] // (EXPERT_ROWS * ROW_SUB)
    row = lambda i, be, nu: (jnp.minimum(i, nu[0] - 1), 0)
    wsel = lambda i, be, nu: (be[jnp.minimum(i, nu[0] - 1)], 0, 0)
    return pl.pallas_call(
        _experts_kernel,
        out_shape=jax.ShapeDtypeStruct(xs.shape, xs.dtype),
        grid_spec=pltpu.PrefetchScalarGridSpec(
            num_scalar_prefetch=2,
            grid=(nblk,),
            in_specs=[pl.BlockSpec((EXPERT_ROWS * ROW_SUB, LANES), row),
                      pl.BlockSpec((1, d, ff), wsel),
                      pl.BlockSpec((1, d, ff), wsel),
                      pl.BlockSpec((1, ff, d), wsel)],
            out_specs=pl.BlockSpec((EXPERT_ROWS * ROW_SUB, LANES), row),
            scratch_shapes=[pltpu.VMEM((d, 2 * ff), BF16), pltpu.VMEM((ff, d), BF16)]),
        compiler_params=_params(("arbitrary",)),
        name="experts",
    )(blk_e, n_used, xs, wg, wu, wd)


def _combine_kernel(dest_ref, next_ref, ys_ref, w_ref, base_ref, mod_ref, fw_ref, o_ref, buf, sem):
    tc = dest_ref.shape[1]
    i = pl.program_id(0)
    slot = i % 2

    def row_copy(row, s, t, k):
        return pltpu.make_async_copy(_row_tile(ys_ref, row), _row_tile(buf.at[s, k], t), sem.at[s])

    def issue(d_ref, s):
        for t in range(tc):
            for k in range(TOP_K):
                row_copy(d_ref[k, t], s, t, k).start(priority=k % 2)

    def drain(s):
        for t in range(tc):
            for k in range(TOP_K):
                row_copy(0, s, t, k).wait()

    @pl.when(i == 0)
    def _():
        issue(dest_ref, 0)

    drain(slot)
    issue(next_ref, 1 - slot)
    w = w_ref[...]
    routed = w[:, 0:1] * _load_row_tiles(buf.at[slot, 0], tc)
    for k in range(1, TOP_K):
        routed = routed + w[:, k:k + 1] * _load_row_tiles(buf.at[slot, k], tc)
    x = base_ref[0] + mod_ref[0, 5:6, :] * routed
    o_ref[0] = _rms(x, fw_ref[...])

    @pl.when(i == pl.num_programs(0) - 1)
    def _():
        drain(1 - slot)


def _combine(dest, ys, wt, base, mods, fw, tc):
    b, l, d = base.shape
    per_b = l // tc
    steps = b * per_b
    return pl.pallas_call(
        _combine_kernel,
        out_shape=jax.ShapeDtypeStruct((b, l, d), F32),
        grid=(steps,),
        in_specs=[pl.BlockSpec((TOP_K, tc), lambda i: (0, i), memory_space=pltpu.SMEM),
                  pl.BlockSpec((TOP_K, tc), lambda i: (0, jnp.minimum(i + 1, steps - 1)), memory_space=pltpu.SMEM),
                  pl.BlockSpec(memory_space=pl.ANY),
                  pl.BlockSpec((tc, TOP_K), lambda i: (i, 0)),
                  pl.BlockSpec((1, tc, d), lambda i: (i // per_b, i % per_b, 0)),
                  pl.BlockSpec((1, 6, d), lambda i: (i // per_b, 0, 0)),
                  pl.BlockSpec((1, d), lambda i: (0, 0))],
        out_specs=pl.BlockSpec((1, tc, d), lambda i: (i // per_b, i % per_b, 0)),
        scratch_shapes=[pltpu.VMEM((2, TOP_K, tc * ROW_SUB, LANES), jnp.uint32), pltpu.SemaphoreType.DMA((2,))],
        compiler_params=_params(("arbitrary",)),
        name="combine",
    )(dest, dest, ys, wt, base, mods, fw)


def _rope_tables(l):
    t = jnp.arange(l, dtype=jnp.int32)
    row = (t // GRID_W).astype(F32)
    col = (t % GRID_W).astype(F32)
    n_freq = HEAD_DIM // 4
    inv = ROPE_THETA ** (-jnp.arange(n_freq, dtype=F32) / n_freq)
    ang = jnp.concatenate([row[:, None] * inv, col[:, None] * inv], axis=-1)
    cos = jnp.repeat(jnp.cos(ang), 2, axis=1)
    sin = jnp.repeat(jnp.sin(ang), 2, axis=1)
    sign = jnp.tile(jnp.array([-1.0, 1.0], F32), HEAD_DIM // 2)
    reps = LANES // HEAD_DIM
    return jnp.tile(cos, (1, reps)), jnp.tile(sin * sign, (1, reps))


def _filter_features(l):
    t = jnp.linspace(0.0, 1.0, l, dtype=F32)[:, None]
    bands = (FILTER_EMB - 1) // 2
    w = 2.0 * math.pi * jnp.arange(l, dtype=F32)[:, None] / l
    f = jnp.linspace(1e-4, bands - 1, bands, dtype=F32)[None, :]
    z = jnp.concatenate([t, jnp.cos(f * w), -jnp.sin(f * w)], axis=-1)
    min_decay = math.log(FILTER_TARGET) / FILTER_DECAY_FAST
    max_decay = math.log(FILTER_TARGET) / FILTER_DECAY_SLOW
    deltas = jnp.linspace(min_decay, max_decay, HYENA_WIDTH, dtype=F32)[None, :]
    return jnp.pad(z, ((0, 0), (0, LANES - FILTER_EMB))), deltas


def _dft_matrices(l):
    n = 2 * l
    idx = jnp.arange(l, dtype=jnp.int32)
    r = math.isqrt(l)
    assert r * r == l
    sub = jnp.arange(r, dtype=jnp.int32)
    hi = ((r * sub[:, None] * idx[None, :]) % n).astype(F32) * (2.0 * math.pi / n)
    lo = ((sub[:, None] * idx[None, :]) % n).astype(F32) * (2.0 * math.pi / n)
    ch, sh, cl, sl = jnp.cos(hi)[:, None, :], jnp.sin(hi)[:, None, :], jnp.cos(lo)[None], jnp.sin(lo)[None]
    c = (ch * cl - sh * sl).reshape(l, l)
    s = (sh * cl + ch * sl).reshape(l, l)
    alt = jnp.where(idx % 2 == 0, 1.0, -1.0).astype(F32)
    first = (idx == 0)[:, None]
    fwd = jnp.concatenate([c, jnp.where(first, alt[None, :], -s)], axis=0).astype(BF16)
    firstc = (idx == 0)[None, :]
    inv_r = (jnp.where(firstc, 1.0, 2.0) * c / n).astype(BF16)
    inv_i = (jnp.where(firstc, alt[:, None], -2.0 * s) / n).astype(BF16)
    return fwd, inv_r, inv_i


def _head_perm():
    order = []
    for j in range(N_HEADS // 2):
        order += list(range(j * HEAD_DIM, (j + 1) * HEAD_DIM))
        order += list(range((j + N_HEADS // 2) * HEAD_DIM, (j + 1 + N_HEADS // 2) * HEAD_DIM))
    return jnp.array(order, jnp.int32)


def _pad2(a, rows, cols):
    return jnp.pad(a, ((0, rows - a.shape[0]), (0, cols - a.shape[1])))


def kernel(x, c, ctx, c_ctx, mod_w, mod_b, norm1_w, w_in, q_norm_w, k_norm_w, conv_w, conv_b, filt_w1, filt_b1, filt_w2, filt_b2, filt_w3, filt_b3, filt_w4, filt_freq, hyena_bias, attn_out_norm_w, hyena_out_norm_w, w_out, norm2_w, router_w, router_bias, exp_w_gate, exp_w_up, exp_w_down, sh_w_gate, sh_w_up, sh_w_down, final_norm_w):
    b, l, d = x.shape
    t = b * l
    assert mod_w.shape[0] == 1, "single-layer stack"
    tl = min(512, l)

    cond = jnp.concatenate([c, c_ctx[None, :], jnp.zeros((-(b + 1) % 8, d), F32)], axis=0)
    mod = _adaln(cond, mod_w[0], mod_b[0][None, :])
    mods = mod[:b].reshape(b, 6, d)
    cmod = mod[b].reshape(6, d)

    perm = _head_perm()
    w_in0 = w_in[0]
    w_in_k = jnp.concatenate([w_in0[:, :Q_END][:, perm], w_in0[:, Q_END:]], axis=1).astype(BF16)
    w_kv = w_in0[:, Q_END:V_END].astype(BF16)
    gq = jnp.kron(jnp.eye(N_HEADS, dtype=F32), jnp.full((HEAD_DIM, HEAD_DIM), 1.0 / HEAD_DIM, F32)).astype(BF16)
    qnw = jnp.tile(q_norm_w[0], N_HEADS)[None, :]
    knw = jnp.tile(k_norm_w[0], N_KV_HEADS)[None, :]
    n1w = norm1_w[0][None, :]
    cos, sin = _rope_tables(l)

    kc, vc = _ctx_kv(ctx, cmod, n1w, w_kv, gq[:KV_WIDTH, :KV_WIDTH], knw)
    q, k, v, u = _inproj(x, mods, n1w, w_in_k, gq, qnw, knw, cos, sin, tl)
    k_all = jnp.concatenate([kc, k], axis=1)
    v_all = jnp.concatenate([vc, v], axis=1)
    an = _attention(q, k_all, v_all, attn_out_norm_w[0][perm][None, :], min(256, l), min(128, l))

    cw = conv_w[0].reshape(3, 3, HYENA_WIDTH).transpose(1, 0, 2)
    cb = conv_b[0].reshape(3, HYENA_WIDTH)
    gbf, x0 = _hyena_pre(u, cw, cb)
    z, deltas = _filter_features(l)
    fo = filt_w2.shape[1]
    hsd = _hyena_filter(
        z, _pad2(filt_w1[0], LANES, LANES), _pad2(filt_b1[0][None, :], 1, LANES),
        _pad2(filt_w2[0], LANES, LANES), _pad2(filt_b2[0][None, :], 1, LANES),
        _pad2(filt_w3[0], LANES, LANES), _pad2(filt_b3[0][None, :], 1, LANES),
        _pad2(filt_w4[0], LANES, 2 * HYENA_WIDTH), _pad2(filt_freq[0][None, :], 1, LANES), deltas, tl)
    del fo
    fwd, inv_r, inv_i = _dft_matrices(l)
    spec = _dft(fwd, hsd, tl)
    row0 = (jnp.arange(l) == 0)[:, None]
    sa = spec[0, :l]
    sd = jnp.where(row0, spec[0, l:l + 1], sa)
    sb = jnp.where(row0, 0.0, spec[1, l:])
    zr, zi = _dft_mul(fwd, gbf, sa, sb, sd, tl)
    yn = _idft(inv_r, inv_i, zr, zi, gbf, x0, hyena_bias[0][None, :], hyena_out_norm_w[0][None, :], tl)

    w_out0 = w_out[0]
    base, h2hi, h2lo, h2rt = _merge(
        an, yn, x, mods, w_out0[:ATTN_WIDTH][perm].astype(BF16), w_out0[ATTN_WIDTH:].astype(BF16),
        norm2_w[0][None, :], sh_w_gate[0].astype(BF16), sh_w_up[0].astype(BF16), sh_w_down[0].astype(BF16), tl)

    tt = 256
    rwt = router_w[0].T
    rw_hi = rwt.astype(BF16)
    rw_lo = (rwt - rw_hi.astype(F32)).astype(BF16)
    bias = jnp.broadcast_to(router_bias[0][:, None], (N_EXPERTS, tt))
    ti = jnp.arange(tt)
    tri = jnp.stack([(ti[:, None] < ti[None, :]), jnp.ones((tt, tt), bool)]).astype(BF16)
    idx, wgt, pos, cnt = _router(h2hi.reshape(t, d), h2lo.reshape(t, d), rw_hi, rw_lo, bias, tri, tt)

    counts = cnt[:, 0].astype(jnp.int32)
    padded = (counts + EXPERT_ROWS - 1) // EXPERT_ROWS * EXPERT_ROWS
    pad_end = jnp.cumsum(padded)
    pad_start = pad_end - padded
    td = min(1024, t)
    dest = _dest(idx, pos, jnp.broadcast_to(pad_start.astype(F32)[:, None], (N_EXPERTS, td)), td)
    n_rows = (t * TOP_K + N_EXPERTS * (EXPERT_ROWS - 1) + EXPERT_ROWS - 1) // EXPERT_ROWS * EXPERT_ROWS
    nblk = n_rows // EXPERT_ROWS
    blk_row = jnp.arange(nblk, dtype=jnp.int32) * EXPERT_ROWS
    blk_e = jnp.minimum(jnp.sum((pad_end[None, :] <= blk_row[:, None]).astype(jnp.int32), axis=1), N_EXPERTS - 1)
    n_used = (pad_end[-1:] // EXPERT_ROWS).astype(jnp.int32)

    last_blk = jnp.maximum(pad_end // EXPERT_ROWS - 1, 0).astype(jnp.int32)
    xs = _dispatch(dest, h2rt, _zero_tails(last_blk, n_rows), min(512, t))
    ys = _experts(blk_e, n_used, xs, exp_w_gate[0], exp_w_up[0], exp_w_down[0])
    return _combine(dest, ys, wgt.T, base, mods, final_norm_w[None, :], min(128, l))
```

```python
import functools
import math

import jax
import jax.numpy as jnp
from jax import lax
from jax.experimental import pallas as pl
from jax.experimental.pallas import tpu as pltpu

F32 = jnp.float32
BF16 = jnp.bfloat16
HIGHEST = lax.Precision.HIGHEST

GRID_W = 64
N_HEADS = 8
N_KV_HEADS = 2
HEAD_DIM = 64
ATTN_WIDTH = N_HEADS * HEAD_DIM
KV_WIDTH = N_KV_HEADS * HEAD_DIM
HYENA_WIDTH = 512
Q_END = ATTN_WIDTH
K_END = Q_END + KV_WIDTH
V_END = K_END + KV_WIDTH
ROPE_THETA = 10000.0
FILTER_EMB = 33
FILTER_DECAY_FAST = 0.3
FILTER_DECAY_SLOW = 1.5
FILTER_TARGET = 1e-2
N_EXPERTS = 256
TOP_K = 8
N_GROUPS = 8
TOPK_GROUPS = 4
ROUTE_SCALE = 2.5
EPS = 1e-6

LANES = 128
ROW_SUB = 4
EXPERT_ROWS = 512
EXPERT_SUB = 256
NEG_INF = float("-inf")

ROW_TILE = 512
ATTN_Q_TILE = 256
ATTN_SUB = 128
ROUTER_TILE = 256
DEST_TILE = 1024
DISPATCH_TILE = 512
COMBINE_TILE = 128
ADALN_COLS = 1536
VMEM_LIMIT_MIB = 48


def _params(semantics):
    return pltpu.CompilerParams(dimension_semantics=semantics, vmem_limit_bytes=VMEM_LIMIT_MIB * 1024 * 1024)


def _rms(x, w):
    return x * lax.rsqrt(jnp.mean(x * x, axis=-1, keepdims=True) + EPS) * w


def _sigmoid(x):
    return 1.0 / (1.0 + jnp.exp(-x))


def _adaln_kernel(c_ref, w_ref, b_ref, o_ref):
    c = c_ref[...]
    s = c * _sigmoid(c)
    o_ref[...] = jnp.dot(s, w_ref[...], precision=HIGHEST, preferred_element_type=F32) + b_ref[...]


def _adaln(cond, w, b):
    rows, d = cond.shape
    n = w.shape[1]
    tn = ADALN_COLS
    return pl.pallas_call(
        _adaln_kernel,
        out_shape=jax.ShapeDtypeStruct((rows, n), F32),
        grid=(n // tn,),
        in_specs=[pl.BlockSpec((rows, d), lambda j: (0, 0)),
                  pl.BlockSpec((d, tn), lambda j: (0, j)),
                  pl.BlockSpec((1, tn), lambda j: (0, j))],
        out_specs=pl.BlockSpec((rows, tn), lambda j: (0, j)),
        compiler_params=_params(("arbitrary",)),
        name="adaln",
    )(cond, w, b)


def _head_rms(t, gmat, w):
    ms = jnp.dot((t * t).astype(BF16), gmat, preferred_element_type=F32)
    return t * lax.rsqrt(ms + EPS) * w


def _modulated(x, norm_w, shift, scale):
    return _rms(x, norm_w) * (1.0 + scale) + shift


def _ctx_kv_kernel(ctx_ref, mod_ref, n1_ref, w_ref, g_ref, kn_ref, kc_ref, vc_ref):
    x = ctx_ref[0]
    h = _modulated(x, n1_ref[...], mod_ref[0:1, :], mod_ref[1:2, :])
    kv = jnp.dot(h.astype(BF16), w_ref[...], preferred_element_type=F32)
    k = _head_rms(kv[:, :KV_WIDTH], g_ref[...], kn_ref[...])
    v = kv[:, KV_WIDTH:]
    kc_ref[0] = k.astype(BF16)
    vc_ref[0] = jnp.concatenate([v, jnp.ones_like(v)], axis=1).astype(BF16)


def _ctx_kv(ctx, cmod, n1w, w_kv, gk, knw):
    b, c, d = ctx.shape
    return pl.pallas_call(
        _ctx_kv_kernel,
        out_shape=(jax.ShapeDtypeStruct((b, c, KV_WIDTH), BF16),
                   jax.ShapeDtypeStruct((b, c, 2 * KV_WIDTH), BF16)),
        grid=(b,),
        in_specs=[pl.BlockSpec((1, c, d), lambda i: (i, 0, 0)),
                  pl.BlockSpec(cmod.shape, lambda i: (0, 0)),
                  pl.BlockSpec((1, d), lambda i: (0, 0)),
                  pl.BlockSpec(w_kv.shape, lambda i: (0, 0)),
                  pl.BlockSpec(gk.shape, lambda i: (0, 0)),
                  pl.BlockSpec((1, KV_WIDTH), lambda i: (0, 0))],
        out_specs=(pl.BlockSpec((1, c, KV_WIDTH), lambda i: (i, 0, 0)),
                   pl.BlockSpec((1, c, 2 * KV_WIDTH), lambda i: (i, 0, 0))),
        compiler_params=_params(("arbitrary",)),
        name="ctx_kv",
    )(ctx, cmod, n1w, w_kv, gk, knw)


def _rope(t, cos, sin, even):
    width = t.shape[1]
    partner = jnp.where(even, pltpu.roll(t, width - 1, axis=1), pltpu.roll(t, 1, axis=1))
    return t * cos + partner * sin


def _inproj_kernel(x_ref, mod_ref, n1_ref, w_ref, gq_ref, qn_ref, kn_ref, cos_ref, sin_ref,
                   q_ref, k_ref, v_ref, u_ref):
    x = x_ref[0]
    h = _modulated(x, n1_ref[...], mod_ref[0, 0:1, :], mod_ref[0, 1:2, :])
    p = jnp.dot(h.astype(BF16), w_ref[...], preferred_element_type=F32)
    gq = gq_ref[...]
    q = _head_rms(p[:, :Q_END], gq, qn_ref[...])
    k = _head_rms(p[:, Q_END:K_END], gq[:KV_WIDTH, :KV_WIDTH], kn_ref[...])
    v = p[:, K_END:V_END]
    cos = cos_ref[...]
    sin = sin_ref[...]
    reps = Q_END // LANES
    cos_q = jnp.concatenate([cos] * reps, axis=1)
    sin_q = jnp.concatenate([sin] * reps, axis=1)
    even_q = (lax.broadcasted_iota(jnp.int32, (1, Q_END), 1) & 1) == 0
    even_k = (lax.broadcasted_iota(jnp.int32, (1, KV_WIDTH), 1) & 1) == 0
    q = _rope(q, cos_q, sin_q, even_q) * (HEAD_DIM ** -0.5)
    k = _rope(k, cos, sin, even_k)
    q_ref[0] = q.astype(BF16)
    k_ref[0] = k.astype(BF16)
    v_ref[0] = jnp.concatenate([v, jnp.ones_like(v)], axis=1).astype(BF16)
    u_ref[0] = p[:, V_END:].astype(BF16)


def _inproj(x, mods, n1w, w_in, gq, qnw, knw, cos, sin, tl):
    b, l, d = x.shape
    ncol = w_in.shape[1]
    nu = ncol - V_END
    return pl.pallas_call(
        _inproj_kernel,
        out_shape=(jax.ShapeDtypeStruct((b, l, Q_END), BF16),
                   jax.ShapeDtypeStruct((b, l, KV_WIDTH), BF16),
                   jax.ShapeDtypeStruct((b, l, 2 * KV_WIDTH), BF16),
                   jax.ShapeDtypeStruct((b, l, nu), BF16)),
        grid=(l // tl, b),
        in_specs=[pl.BlockSpec((1, tl, d), lambda i, j: (j, i, 0)),
                  pl.BlockSpec((1, 6, d), lambda i, j: (j, 0, 0)),
                  pl.BlockSpec((1, d), lambda i, j: (0, 0)),
                  pl.BlockSpec((d, ncol), lambda i, j: (0, 0)),
                  pl.BlockSpec(gq.shape, lambda i, j: (0, 0)),
                  pl.BlockSpec((1, Q_END), lambda i, j: (0, 0)),
                  pl.BlockSpec((1, KV_WIDTH), lambda i, j: (0, 0)),
                  pl.BlockSpec((tl, LANES), lambda i, j: (i, 0)),
                  pl.BlockSpec((tl, LANES), lambda i, j: (i, 0))],
        out_specs=(pl.BlockSpec((1, tl, Q_END), lambda i, j: (j, i, 0)),
                   pl.BlockSpec((1, tl, KV_WIDTH), lambda i, j: (j, i, 0)),
                   pl.BlockSpec((1, tl, 2 * KV_WIDTH), lambda i, j: (j, i, 0)),
                   pl.BlockSpec((1, tl, nu), lambda i, j: (j, i, 0))),
        compiler_params=_params(("arbitrary", "arbitrary")),
        name="inproj",
    )(x, mods, n1w, w_in, gq, qnw, knw, cos, sin)


def _attn_kernel(q_ref, k_ref, v_ref, wn_ref, o_ref, *, sub):
    tq = q_ref.shape[1]
    low = lax.broadcasted_iota(jnp.int32, (1, LANES), 1) < HEAD_DIM
    nt = (((1,), (1,)), ((), ()))
    kk = k_ref[0]
    vv = v_ref[0]
    for r in range(0, tq, sub):
        outs = []
        for j in range(Q_END // LANES):
            qv = q_ref[0, r:r + sub, LANES * j:LANES * (j + 1)]
            zero = jnp.zeros_like(qv)
            halves = []
            for g in range(N_KV_HEADS):
                qh = jnp.where(low, qv, zero) if g == 0 else jnp.where(low, zero, qv)
                s = lax.dot_general(qh, kk, nt, preferred_element_type=F32)
                p = jnp.exp(s - jnp.max(s, axis=-1, keepdims=True)).astype(BF16)
                pv = jnp.dot(p, vv, preferred_element_type=F32)
                halves.append(pv[:, :LANES] / pv[:, LANES:])
            outs.append(jnp.where(low, halves[0], halves[1]))
        a = jnp.concatenate(outs, axis=1)
        o_ref[0, r:r + sub, :] = _rms(a, wn_ref[...]).astype(BF16)


def _attention(q, k, v, wn, tq, sub):
    b, l, _ = q.shape
    n = k.shape[1]
    return pl.pallas_call(
        functools.partial(_attn_kernel, sub=sub),
        out_shape=jax.ShapeDtypeStruct((b, l, Q_END), BF16),
        grid=(b, l // tq),
        in_specs=[pl.BlockSpec((1, tq, Q_END), lambda i, j: (i, j, 0)),
                  pl.BlockSpec((1, n, KV_WIDTH), lambda i, j: (i, 0, 0)),
                  pl.BlockSpec((1, n, 2 * KV_WIDTH), lambda i, j: (i, 0, 0)),
                  pl.BlockSpec((1, Q_END), lambda i, j: (0, 0))],
        out_specs=pl.BlockSpec((1, tq, Q_END), lambda i, j: (i, j, 0)),
        compiler_params=_params(("arbitrary", "arbitrary")),
        name="attn",
    )(q, k, v, wn)


def _hyena_pre_kernel(u0_ref, u1_ref, u2_ref, cw_ref, cb_ref, g_ref, x0_ref):
    l = u0_ref.shape[1]
    row = lax.broadcasted_iota(jnp.int32, (l, LANES), 0)

    def conv(u_ref, gi):
        u = u_ref[0].astype(F32)
        prev = jnp.where(row == 0, 0.0, pltpu.roll(u, 1, axis=0))
        nxt = jnp.where(row == l - 1, 0.0, pltpu.roll(u, l - 1, axis=0))
        w = cw_ref[gi]
        return w[0:1] * prev + w[1:2] * u + w[2:3] * nxt + cb_ref[gi:gi + 1, :]

    x0 = conv(u0_ref, 0)
    x1 = conv(u1_ref, 1)
    v = conv(u2_ref, 2)
    g = v * x1
    g_ref[0] = g.astype(BF16)
    x0_ref[0] = x0.astype(BF16)


def _hyena_pre(u, cw, cb):
    b, l, _ = u.shape
    nblk = HYENA_WIDTH // LANES
    ublk = lambda gi: pl.BlockSpec((1, l, LANES), lambda i, j: (i, 0, gi * nblk + j))
    oblk = pl.BlockSpec((1, l, LANES), lambda i, j: (i, 0, j))
    return pl.pallas_call(
        _hyena_pre_kernel,
        out_shape=(jax.ShapeDtypeStruct((b, l, HYENA_WIDTH), BF16),
                   jax.ShapeDtypeStruct((b, l, HYENA_WIDTH), BF16)),
        grid=(b, nblk),
        in_specs=[ublk(0), ublk(1), ublk(2),
                  pl.BlockSpec((3, 3, LANES), lambda i, j: (0, 0, j)),
                  pl.BlockSpec((3, LANES), lambda i, j: (0, j))],
        out_specs=(oblk, oblk),
        compiler_params=_params(("arbitrary", "arbitrary")),
        name="hyena_pre",
    )(u, u, u, cw, cb)


def _filter_kernel(z_ref, w1_ref, b1_ref, w2_ref, b2_ref, w3_ref, b3_ref, w4_ref, fr_ref, dl_ref, o_ref):
    tl = z_ref.shape[0]
    z = z_ref[...]
    fr = fr_ref[...]
    dot = lambda a, w: jnp.dot(a, w, precision=HIGHEST, preferred_element_type=F32)
    h = jnp.sin(fr * (dot(z, w1_ref[...]) + b1_ref[...]))
    h = jnp.sin(fr * (dot(h, w2_ref[...]) + b2_ref[...]))
    h = jnp.sin(fr * (dot(h, w3_ref[...]) + b3_ref[...]))
    h = dot(h, w4_ref[...])
    t = z[:, 0:1]
    decay = jnp.exp(-t * jnp.abs(dl_ref[...]))
    hf = h[:, :HYENA_WIDTH] * decay
    hb = h[:, HYENA_WIDTH:] * decay
    row = lax.broadcasted_iota(jnp.int32, (tl, HYENA_WIDTH), 0) + pl.program_id(0) * tl
    hb = jnp.where(row == 0, 0.0, hb)
    o_ref[0] = hf + hb
    o_ref[1] = hf - hb


def _hyena_filter(z, w1, b1, w2, b2, w3, b3, w4, freq, deltas, tl):
    l = z.shape[0]
    full = lambda a: pl.BlockSpec(a.shape, lambda i: (0,) * a.ndim)
    return pl.pallas_call(
        _filter_kernel,
        out_shape=jax.ShapeDtypeStruct((2, l, HYENA_WIDTH), F32),
        grid=(l // tl,),
        in_specs=[pl.BlockSpec((tl, z.shape[1]), lambda i: (i, 0)),
                  full(w1), full(b1), full(w2), full(b2), full(w3), full(b3), full(w4), full(freq), full(deltas)],
        out_specs=pl.BlockSpec((2, tl, HYENA_WIDTH), lambda i: (0, i, 0)),
        compiler_params=_params(("arbitrary",)),
        name="hyena_filter",
    )(z, w1, b1, w2, b2, w3, b3, w4, freq, deltas)


def _dft_kernel(f_ref, x_ref, o_ref):
    o_ref[0] = jnp.dot(f_ref[...], x_ref[0].astype(BF16), preferred_element_type=F32)


def _dft(fmat, x, tf):
    nb, l, w = x.shape
    n = fmat.shape[0]
    return pl.pallas_call(
        _dft_kernel,
        out_shape=jax.ShapeDtypeStruct((nb, n, w), F32),
        grid=(n // tf, nb),
        in_specs=[pl.BlockSpec((tf, l), lambda i, j: (i, 0)),
                  pl.BlockSpec((1, l, w), lambda i, j: (j, 0, 0))],
        out_specs=pl.BlockSpec((1, tf, w), lambda i, j: (j, i, 0)),
        compiler_params=_params(("arbitrary", "arbitrary")),
        name="dft_filter",
    )(fmat, x)


def _dft_mul_kernel(fr_ref, fi_ref, x_ref, a_ref, b_ref, d_ref, zr_ref, zi_ref):
    x = x_ref[0]
    xr = jnp.dot(fr_ref[...], x, preferred_element_type=F32)
    xi = jnp.dot(fi_ref[...], x, preferred_element_type=F32)
    bb = b_ref[...]
    zr_ref[0] = (xr * a_ref[...] - xi * bb).astype(BF16)
    zi_ref[0] = (xr * bb + xi * d_ref[...]).astype(BF16)


def _dft_mul(fmat, g, sa, sb, sd, tf):
    b, l, w = g.shape
    nf = l // tf
    spec = pl.BlockSpec((tf, w), lambda i, j: (i, 0))
    return pl.pallas_call(
        _dft_mul_kernel,
        out_shape=(jax.ShapeDtypeStruct((b, l, w), BF16), jax.ShapeDtypeStruct((b, l, w), BF16)),
        grid=(nf, b),
        in_specs=[pl.BlockSpec((tf, l), lambda i, j: (i, 0)),
                  pl.BlockSpec((tf, l), lambda i, j: (i + nf, 0)),
                  pl.BlockSpec((1, l, w), lambda i, j: (j, 0, 0)),
                  spec, spec, spec],
        out_specs=(pl.BlockSpec((1, tf, w), lambda i, j: (j, i, 0)),
                   pl.BlockSpec((1, tf, w), lambda i, j: (j, i, 0))),
        compiler_params=_params(("arbitrary", "arbitrary")),
        name="dft_mul",
    )(fmat, fmat, g, sa, sb, sd)


def _idft_kernel(fr_ref, fi_ref, zr_ref, zi_ref, g_ref, x0_ref, hb_ref, wn_ref, o_ref):
    conv = (jnp.dot(fr_ref[...], zr_ref[0], preferred_element_type=F32)
            + jnp.dot(fi_ref[...], zi_ref[0], preferred_element_type=F32))
    y = (conv + g_ref[0].astype(F32) * hb_ref[...]) * x0_ref[0].astype(F32)
    o_ref[0] = _rms(y, wn_ref[...]).astype(BF16)


def _idft(finv_r, finv_i, zr, zi, g, x0, hbias, wn, tt):
    b, l, w = zr.shape
    tile = pl.BlockSpec((1, tt, w), lambda i, j: (j, i, 0))
    return pl.pallas_call(
        _idft_kernel,
        out_shape=jax.ShapeDtypeStruct((b, l, w), BF16),
        grid=(l // tt, b),
        in_specs=[pl.BlockSpec((tt, l), lambda i, j: (i, 0)),
                  pl.BlockSpec((tt, l), lambda i, j: (i, 0)),
                  pl.BlockSpec((1, l, w), lambda i, j: (j, 0, 0)),
                  pl.BlockSpec((1, l, w), lambda i, j: (j, 0, 0)),
                  tile, tile,
                  pl.BlockSpec((1, w), lambda i, j: (0, 0)),
                  pl.BlockSpec((1, w), lambda i, j: (0, 0))],
        out_specs=tile,
        compiler_params=_params(("arbitrary", "arbitrary")),
        name="idft",
    )(finv_r, finv_i, zr, zi, g, x0, hbias, wn)


def _bf16_bits(v):
    return lax.bitcast_convert_type(v.astype(BF16).astype(F32), jnp.uint32)


def _store_row_tiles(ref, val):
    rows, half = val.shape[0], val.shape[1] // 2
    assert half == ROW_SUB * LANES
    for j in range(ROW_SUB):
        lo = _bf16_bits(val[:, LANES * j:LANES * (j + 1)]) >> 16
        hi = _bf16_bits(val[:, half + LANES * j:half + LANES * (j + 1)]) & jnp.uint32(0xFFFF0000)
        ref[pl.ds(j, rows, stride=ROW_SUB), :] = lo | hi


def _load_row_tiles(ref, rows):
    words = [ref[pl.ds(j, rows, stride=ROW_SUB), :] for j in range(ROW_SUB)]
    lo = [lax.bitcast_convert_type(w << 16, F32) for w in words]
    hi = [lax.bitcast_convert_type(w & jnp.uint32(0xFFFF0000), F32) for w in words]
    return jnp.concatenate(lo + hi, axis=1)


def _merge_kernel(a_ref, y_ref, x_ref, mod_ref, wa_ref, wy_ref, n2_ref, sg_ref, su_ref, sd_ref,
                  base_ref, hi_ref, lo_ref, rt_ref):
    m = (jnp.dot(a_ref[0], wa_ref[...], preferred_element_type=F32)
         + jnp.dot(y_ref[0], wy_ref[...], preferred_element_type=F32))
    x1 = x_ref[0] + mod_ref[0, 2:3, :] * m
    h2 = _modulated(x1, n2_ref[...], mod_ref[0, 3:4, :], mod_ref[0, 4:5, :])
    hi = h2.astype(BF16)
    hi_ref[0] = hi
    lo_ref[0] = (h2 - hi.astype(F32)).astype(BF16)
    _store_row_tiles(rt_ref, h2)
    gate = jnp.dot(hi, sg_ref[...], preferred_element_type=F32)
    up = jnp.dot(hi, su_ref[...], preferred_element_type=F32)
    act = (gate * _sigmoid(gate) * up).astype(BF16)
    shared = jnp.dot(act, sd_ref[...], preferred_element_type=F32)
    base_ref[0] = x1 + mod_ref[0, 5:6, :] * shared


def _merge(an, yn, x, mods, wa, wy, n2w, sg, su, sd, tl):
    b, l, d = x.shape
    full = lambda a: pl.BlockSpec(a.shape, lambda i, j: (0,) * a.ndim)
    half = pl.BlockSpec((1, tl, an.shape[2]), lambda i, j: (i, j, 0))
    wide = pl.BlockSpec((1, tl, d), lambda i, j: (i, j, 0))
    per_b = l // tl
    return pl.pallas_call(
        _merge_kernel,
        out_shape=(jax.ShapeDtypeStruct((b, l, d), F32),
                   jax.ShapeDtypeStruct((b, l, d), BF16),
                   jax.ShapeDtypeStruct((b, l, d), BF16),
                   jax.ShapeDtypeStruct((b * l * ROW_SUB, LANES), jnp.uint32)),
        grid=(b, per_b),
        in_specs=[half, half, wide,
                  pl.BlockSpec((1, 6, d), lambda i, j: (i, 0, 0)),
                  full(wa), full(wy), full(n2w), full(sg), full(su), full(sd)],
        out_specs=(wide, wide, wide,
                   pl.BlockSpec((tl * ROW_SUB, LANES), lambda i, j: (i * per_b + j, 0))),
        compiler_params=_params(("arbitrary", "arbitrary")),
        name="merge",
    )(an, yn, x, mods, wa, wy, n2w, sg, su, sd)


def _router_kernel(hi_ref, lo_ref, whi_ref, wlo_ref, bias_ref, tri_ref,
                   idx_ref, wgt_ref, pos_ref, cnt_ref, run_ref):
    tt = hi_ref.shape[0]
    per_group = N_EXPERTS // N_GROUPS

    @pl.when(pl.program_id(0) == 0)
    def _():
        run_ref[...] = jnp.zeros_like(run_ref)

    nt = (((1,), (1,)), ((), ()))
    hi = hi_ref[...]
    whi = whi_ref[...]
    logits = (lax.dot_general(whi, hi, nt, preferred_element_type=F32)
              + lax.dot_general(whi, lo_ref[...], nt, preferred_element_type=F32)
              + lax.dot_general(wlo_ref[...], hi, nt, preferred_element_type=F32))
    scores = _sigmoid(logits)
    biased = scores + bias_ref[...]

    ridx = lax.broadcasted_iota(jnp.int32, (per_group, tt), 0)
    groups = [biased[g * per_group:(g + 1) * per_group, :] for g in range(N_GROUPS)]
    gs = []
    for blk in groups:
        m1 = jnp.max(blk, axis=0, keepdims=True)
        i1 = jnp.min(jnp.where(blk == m1, ridx, per_group), axis=0, keepdims=True)
        m2 = jnp.max(jnp.where(ridx == i1, NEG_INF, blk), axis=0, keepdims=True)
        gs.append(m1 + m2)

    kept = []
    for g in range(N_GROUPS):
        ahead = jnp.zeros((1, tt), F32)
        for o in range(N_GROUPS):
            if o != g:
                wins = (gs[o] >= gs[g]) if o < g else (gs[o] > gs[g])
                ahead = ahead + jnp.where(wins, 1.0, 0.0)
        kept.append(jnp.where(ahead < TOPK_GROUPS, groups[g], NEG_INF))
    cur = jnp.concatenate(kept, axis=0)

    eidx = lax.broadcasted_iota(jnp.int32, cur.shape, 0)
    kept_mask = cur
    picks = []
    wsel = []
    for _ in range(TOP_K):
        mx = jnp.max(cur, axis=0, keepdims=True)
        first = jnp.min(jnp.where(cur == mx, eidx, N_EXPERTS), axis=0, keepdims=True)
        sel = eidx == first
        picks.append(first)
        wsel.append(jnp.sum(jnp.where(sel, scores, 0.0), axis=0, keepdims=True))
        cur = jnp.where(sel, NEG_INF, cur)
    w = jnp.concatenate(wsel, axis=0)
    w = w / jnp.sum(w, axis=0, keepdims=True) * ROUTE_SCALE
    idx = jnp.concatenate(picks, axis=0)

    oh = jnp.where(cur == NEG_INF, jnp.where(kept_mask == NEG_INF, 0.0, 1.0), 0.0).astype(BF16)
    before = jnp.dot(oh, tri_ref[0], preferred_element_type=F32)
    total = jnp.dot(oh, tri_ref[1], preferred_element_type=F32)
    pos_ref[...] = run_ref[...] + before
    run_ref[...] = run_ref[...] + total

    idx_ref[...] = idx
    wgt_ref[...] = w
    cnt_ref[...] = run_ref[...]


def _router(hi, lo, whi, wlo, bias, tri, tt):
    t, d = hi.shape
    tok = pl.BlockSpec((tt, d), lambda i: (i, 0))
    full = lambda a: pl.BlockSpec(a.shape, lambda i: (0,) * a.ndim)
    out = pl.BlockSpec((TOP_K, tt), lambda i: (0, i))
    return pl.pallas_call(
        _router_kernel,
        out_shape=(jax.ShapeDtypeStruct((TOP_K, t), jnp.int32),
                   jax.ShapeDtypeStruct((TOP_K, t), F32),
                   jax.ShapeDtypeStruct((N_EXPERTS, t), F32),
                   jax.ShapeDtypeStruct((N_EXPERTS, tt), F32)),
        grid=(t // tt,),
        in_specs=[tok, tok, full(whi), full(wlo), full(bias), full(tri)],
        out_specs=(out, out, pl.BlockSpec((N_EXPERTS, tt), lambda i: (0, i)),
                   pl.BlockSpec((N_EXPERTS, tt), lambda i: (0, 0))),
        scratch_shapes=[pltpu.VMEM((N_EXPERTS, tt), F32)],
        compiler_params=_params(("arbitrary",)),
        name="router",
    )(hi, lo, whi, wlo, bias, tri)


def _row_tile(ref, r):
    return ref.at[pl.ds(pl.multiple_of(r * ROW_SUB, ROW_SUB), ROW_SUB), :]


def _dest_kernel(idx_ref, pos_ref, start_ref, dest_ref):
    eidx = lax.broadcasted_iota(jnp.int32, start_ref.shape, 0)
    row = start_ref[...] + pos_ref[...]
    rows = [jnp.sum(jnp.where(eidx == idx_ref[k:k + 1, :], row, 0.0), axis=0, keepdims=True)
            for k in range(TOP_K)]
    dest_ref[...] = jnp.concatenate(rows, axis=0).astype(jnp.int32)


def _dest(idx, pos, start, tt):
    t = idx.shape[1]
    blk = pl.BlockSpec((TOP_K, tt), lambda i: (0, i))
    return pl.pallas_call(
        _dest_kernel,
        out_shape=jax.ShapeDtypeStruct((TOP_K, t), jnp.int32),
        grid=(t // tt,),
        in_specs=[blk, pl.BlockSpec((N_EXPERTS, tt), lambda i: (0, i)), pl.BlockSpec(start.shape, lambda i: (0, 0))],
        out_specs=blk,
        compiler_params=_params(("arbitrary",)),
        name="dest",
    )(idx, pos, start)


def _zero_tails_kernel(last_ref, o_ref):
    del last_ref
    o_ref[...] = jnp.zeros(o_ref.shape, o_ref.dtype)


def _zero_tails(last_blk, n_rows):
    return pl.pallas_call(
        _zero_tails_kernel,
        out_shape=jax.ShapeDtypeStruct((n_rows * ROW_SUB, LANES), jnp.uint32),
        grid_spec=pltpu.PrefetchScalarGridSpec(
            num_scalar_prefetch=1,
            grid=(last_blk.shape[0],),
            in_specs=[],
            out_specs=pl.BlockSpec((EXPERT_ROWS * ROW_SUB, LANES), lambda e, last: (last[e], 0))),
        compiler_params=_params(("arbitrary",)),
        name="zero_tails",
    )(last_blk)


def _dispatch_kernel(dest_ref, h_ref, xs_in_ref, xs_ref, sem):
    del xs_in_ref
    td = dest_ref.shape[1]

    def row_copy(t, k):
        return pltpu.make_async_copy(_row_tile(h_ref, t), _row_tile(xs_ref, dest_ref[k, t]), sem)

    def issue(t, carry):
        for k in range(TOP_K):
            row_copy(t, k).start(priority=k % 2)
        return carry

    lax.fori_loop(0, td, issue, 0)

    def drain(t, carry):
        for k in range(TOP_K):
            row_copy(t, k).wait()
        return carry

    lax.fori_loop(0, td, drain, 0)


def _dispatch(dest, h_rt, xs0, td):
    t = dest.shape[1]
    return pl.pallas_call(
        _dispatch_kernel,
        out_shape=jax.ShapeDtypeStruct(xs0.shape, xs0.dtype),
        grid=(t // td,),
        in_specs=[pl.BlockSpec((TOP_K, td), lambda i: (0, i), memory_space=pltpu.SMEM),
                  pl.BlockSpec((td * ROW_SUB, LANES), lambda i: (i, 0)),
                  pl.BlockSpec(memory_space=pl.ANY)],
        out_specs=pl.BlockSpec(memory_space=pl.ANY),
        scratch_shapes=[pltpu.SemaphoreType.DMA],
        input_output_aliases={2: 0},
        compiler_params=_params(("arbitrary",)),
        name="dispatch",
    )(dest, h_rt, xs0)


def _experts_kernel(be_ref, nu_ref, x_ref, wg_ref, wu_ref, wd_ref, y_ref, wgu_bf, wd_bf):
    i = pl.program_id(0)
    used = i < nu_ref[0]
    new_expert = jnp.logical_or(i == 0, be_ref[i] != be_ref[jnp.maximum(i - 1, 0)])

    @pl.when(jnp.logical_and(used, new_expert))
    def _():
        ff = wg_ref.shape[2]
        wgu_bf[:, :ff] = wg_ref[0].astype(BF16)
        wgu_bf[:, ff:] = wu_ref[0].astype(BF16)
        wd_bf[...] = wd_ref[0].astype(BF16)

    @pl.when(used)
    def _():
        ff = wg_ref.shape[2]
        for r in range(0, EXPERT_ROWS, EXPERT_SUB):
            rows = pl.ds(r * ROW_SUB, EXPERT_SUB * ROW_SUB)
            x = _load_row_tiles(x_ref.at[rows, :], EXPERT_SUB).astype(BF16)
            gu = jnp.dot(x, wgu_bf[...], preferred_element_type=F32)
            gate = gu[:, :ff]
            act = (gate * _sigmoid(gate) * gu[:, ff:]).astype(BF16)
            _store_row_tiles(y_ref.at[rows, :], jnp.dot(act, wd_bf[...], preferred_element_type=F32))


def _experts(blk_e, n_used, xs, wg, wu, wd):
    d, ff = wg.shape[1], wg.shape[2]
    nblk = xs.shape[0] // (EXPERT_ROWS * ROW_SUB)
    row = lambda i, be, nu: (jnp.minimum(i, nu[0] - 1), 0)
    wsel = lambda i, be, nu: (be[jnp.minimum(i, nu[0] - 1)], 0, 0)
    return pl.pallas_call(
        _experts_kernel,
        out_shape=jax.ShapeDtypeStruct(xs.shape, xs.dtype),
        grid_spec=pltpu.PrefetchScalarGridSpec(
            num_scalar_prefetch=2,
            grid=(nblk,),
            in_specs=[pl.BlockSpec((EXPERT_ROWS * ROW_SUB, LANES), row),
                      pl.BlockSpec((1, d, ff), wsel),
                      pl.BlockSpec((1, d, ff), wsel),
                      pl.BlockSpec((1, ff, d), wsel)],
            out_specs=pl.BlockSpec((EXPERT_ROWS * ROW_SUB, LANES), row),
            scratch_shapes=[pltpu.VMEM((d, 2 * ff), BF16), pltpu.VMEM((ff, d), BF16)]),
        compiler_params=_params(("arbitrary",)),
        name="experts",
    )(blk_e, n_used, xs, wg, wu, wd)


def _combine_kernel(dest_ref, next_ref, ys_ref, w_ref, base_ref, mod_ref, fw_ref, o_ref, buf, sem):
    tc = dest_ref.shape[1]
    i = pl.program_id(0)
    slot = i % 2

    def row_copy(row, s, t, k):
        return pltpu.make_async_copy(_row_tile(ys_ref, row), _row_tile(buf.at[s, k], t), sem.at[s])

    def issue(d_ref, s):
        for t in range(tc):
            for k in range(TOP_K):
                row_copy(d_ref[k, t], s, t, k).start(priority=k % 2)

    def drain(s):
        for t in range(tc):
            for k in range(TOP_K):
                row_copy(0, s, t, k).wait()

    @pl.when(i == 0)
    def _():
        issue(dest_ref, 0)

    drain(slot)
    issue(next_ref, 1 - slot)
    w = w_ref[...]
    routed = w[:, 0:1] * _load_row_tiles(buf.at[slot, 0], tc)
    for k in range(1, TOP_K):
        routed = routed + w[:, k:k + 1] * _load_row_tiles(buf.at[slot, k], tc)
    x = base_ref[0] + mod_ref[0, 5:6, :] * routed
    o_ref[0] = _rms(x, fw_ref[...])

    @pl.when(i == pl.num_programs(0) - 1)
    def _():
        drain(1 - slot)


def _combine(dest, ys, wt, base, mods, fw, tc):
    b, l, d = base.shape
    per_b = l // tc
    steps = b * per_b
    return pl.pallas_call(
        _combine_kernel,
        out_shape=jax.ShapeDtypeStruct((b, l, d), F32),
        grid=(steps,),
        in_specs=[pl.BlockSpec((TOP_K, tc), lambda i: (0, i), memory_space=pltpu.SMEM),
                  pl.BlockSpec((TOP_K, tc), lambda i: (0, jnp.minimum(i + 1, steps - 1)), memory_space=pltpu.SMEM),
                  pl.BlockSpec(memory_space=pl.ANY),
                  pl.BlockSpec((tc, TOP_K), lambda i: (i, 0)),
                  pl.BlockSpec((1, tc, d), lambda i: (i // per_b, i % per_b, 0)),
                  pl.BlockSpec((1, 6, d), lambda i: (i // per_b, 0, 0)),
                  pl.BlockSpec((1, d), lambda i: (0, 0))],
        out_specs=pl.BlockSpec((1, tc, d), lambda i: (i // per_b, i % per_b, 0)),
        scratch_shapes=[pltpu.VMEM((2, TOP_K, tc * ROW_SUB, LANES), jnp.uint32), pltpu.SemaphoreType.DMA((2,))],
        compiler_params=_params(("arbitrary",)),
        name="combine",
    )(dest, dest, ys, wt, base, mods, fw)


def _rope_tables(l):
    t = jnp.arange(l, dtype=jnp.int32)
    row = (t // GRID_W).astype(F32)
    col = (t % GRID_W).astype(F32)
    n_freq = HEAD_DIM // 4
    inv = ROPE_THETA ** (-jnp.arange(n_freq, dtype=F32) / n_freq)
    ang = jnp.concatenate([row[:, None] * inv, col[:, None] * inv], axis=-1)
    cos = jnp.repeat(jnp.cos(ang), 2, axis=1)
    sin = jnp.repeat(jnp.sin(ang), 2, axis=1)
    sign = jnp.tile(jnp.array([-1.0, 1.0], F32), HEAD_DIM // 2)
    reps = LANES // HEAD_DIM
    return jnp.tile(cos, (1, reps)), jnp.tile(sin * sign, (1, reps))


def _filter_features(l):
    t = jnp.linspace(0.0, 1.0, l, dtype=F32)[:, None]
    bands = (FILTER_EMB - 1) // 2
    w = 2.0 * math.pi * jnp.arange(l, dtype=F32)[:, None] / l
    f = jnp.linspace(1e-4, bands - 1, bands, dtype=F32)[None, :]
    z = jnp.concatenate([t, jnp.cos(f * w), -jnp.sin(f * w)], axis=-1)
    min_decay = math.log(FILTER_TARGET) / FILTER_DECAY_FAST
    max_decay = math.log(FILTER_TARGET) / FILTER_DECAY_SLOW
    deltas = jnp.linspace(min_decay, max_decay, HYENA_WIDTH, dtype=F32)[None, :]
    return jnp.pad(z, ((0, 0), (0, LANES - FILTER_EMB))), deltas


def _dft_matrices(l):
    n = 2 * l
    idx = jnp.arange(l, dtype=jnp.int32)
    r = math.isqrt(l)
    assert r * r == l
    sub = jnp.arange(r, dtype=jnp.int32)
    hi = ((r * sub[:, None] * idx[None, :]) % n).astype(F32) * (2.0 * math.pi / n)
    lo = ((sub[:, None] * idx[None, :]) % n).astype(F32) * (2.0 * math.pi / n)
    ch, sh, cl, sl = jnp.cos(hi)[:, None, :], jnp.sin(hi)[:, None, :], jnp.cos(lo)[None], jnp.sin(lo)[None]
    c = (ch * cl - sh * sl).reshape(l, l)
    s = (sh * cl + ch * sl).reshape(l, l)
    alt = jnp.where(idx % 2 == 0, 1.0, -1.0).astype(F32)
    first = (idx == 0)[:, None]
    fwd = jnp.concatenate([c, jnp.where(first, alt[None, :], -s)], axis=0).astype(BF16)
    firstc = (idx == 0)[None, :]
    inv_r = (jnp.where(firstc, 1.0, 2.0) * c / n).astype(BF16)
    inv_i = (jnp.where(firstc, alt[:, None], -2.0 * s) / n).astype(BF16)
    return fwd, inv_r, inv_i


def _head_perm():
    order = []
    for j in range(N_HEADS // 2):
        order += list(range(j * HEAD_DIM, (j + 1) * HEAD_DIM))
        order += list(range((j + N_HEADS // 2) * HEAD_DIM, (j + 1 + N_HEADS // 2) * HEAD_DIM))
    return jnp.array(order, jnp.int32)


def _pad2(a, rows, cols):
    return jnp.pad(a, ((0, rows - a.shape[0]), (0, cols - a.shape[1])))


def kernel(x, c, ctx, c_ctx, mod_w, mod_b, norm1_w, w_in, q_norm_w, k_norm_w, conv_w, conv_b, filt_w1, filt_b1, filt_w2, filt_b2, filt_w3, filt_b3, filt_w4, filt_freq, hyena_bias, attn_out_norm_w, hyena_out_norm_w, w_out, norm2_w, router_w, router_bias, exp_w_gate, exp_w_up, exp_w_down, sh_w_gate, sh_w_up, sh_w_down, final_norm_w):
    b, l, d = x.shape
    t = b * l
    assert mod_w.shape[0] == 1, "single-layer stack"
    tl = min(ROW_TILE, l)

    cond = jnp.concatenate([c, c_ctx[None, :], jnp.zeros((-(b + 1) % 8, d), F32)], axis=0)
    mod = _adaln(cond, mod_w[0], mod_b[0][None, :])
    mods = mod[:b].reshape(b, 6, d)
    cmod = mod[b].reshape(6, d)

    perm = _head_perm()
    w_in0 = w_in[0]
    w_in_k = jnp.concatenate([w_in0[:, :Q_END][:, perm], w_in0[:, Q_END:]], axis=1).astype(BF16)
    w_kv = w_in0[:, Q_END:V_END].astype(BF16)
    gq = jnp.kron(jnp.eye(N_HEADS, dtype=F32), jnp.full((HEAD_DIM, HEAD_DIM), 1.0 / HEAD_DIM, F32)).astype(BF16)
    qnw = jnp.tile(q_norm_w[0], N_HEADS)[None, :]
    knw = jnp.tile(k_norm_w[0], N_KV_HEADS)[None, :]
    n1w = norm1_w[0][None, :]
    cos, sin = _rope_tables(l)

    kc, vc = _ctx_kv(ctx, cmod, n1w, w_kv, gq[:KV_WIDTH, :KV_WIDTH], knw)
    q, k, v, u = _inproj(x, mods, n1w, w_in_k, gq, qnw, knw, cos, sin, tl)
    k_all = jnp.concatenate([kc, k], axis=1)
    v_all = jnp.concatenate([vc, v], axis=1)
    an = _attention(q, k_all, v_all, attn_out_norm_w[0][perm][None, :], min(ATTN_Q_TILE, l),
                    min(ATTN_SUB, l))

    cw = conv_w[0].reshape(3, 3, HYENA_WIDTH).transpose(1, 0, 2)
    cb = conv_b[0].reshape(3, HYENA_WIDTH)
    gbf, x0 = _hyena_pre(u, cw, cb)
    z, deltas = _filter_features(l)
    hsd = _hyena_filter(
        z, _pad2(filt_w1[0], LANES, LANES), _pad2(filt_b1[0][None, :], 1, LANES),
        _pad2(filt_w2[0], LANES, LANES), _pad2(filt_b2[0][None, :], 1, LANES),
        _pad2(filt_w3[0], LANES, LANES), _pad2(filt_b3[0][None, :], 1, LANES),
        _pad2(filt_w4[0], LANES, 2 * HYENA_WIDTH), _pad2(filt_freq[0][None, :], 1, LANES), deltas, tl)
    fwd, inv_r, inv_i = _dft_matrices(l)
    spec = _dft(fwd, hsd, tl)
    row0 = (jnp.arange(l) == 0)[:, None]
    sa = spec[0, :l]
    sd = jnp.where(row0, spec[0, l:l + 1], sa)
    sb = jnp.where(row0, 0.0, spec[1, l:])
    zr, zi = _dft_mul(fwd, gbf, sa, sb, sd, tl)
    yn = _idft(inv_r, inv_i, zr, zi, gbf, x0, hyena_bias[0][None, :], hyena_out_norm_w[0][None, :], tl)

    w_out0 = w_out[0]
    base, h2hi, h2lo, h2rt = _merge(
        an, yn, x, mods, w_out0[:ATTN_WIDTH][perm].astype(BF16), w_out0[ATTN_WIDTH:].astype(BF16),
        norm2_w[0][None, :], sh_w_gate[0].astype(BF16), sh_w_up[0].astype(BF16), sh_w_down[0].astype(BF16), tl)

    tt = ROUTER_TILE
    rwt = router_w[0].T
    rw_hi = rwt.astype(BF16)
    rw_lo = (rwt - rw_hi.astype(F32)).astype(BF16)
    bias = jnp.broadcast_to(router_bias[0][:, None], (N_EXPERTS, tt))
    ti = jnp.arange(tt)
    tri = jnp.stack([(ti[:, None] < ti[None, :]), jnp.ones((tt, tt), bool)]).astype(BF16)
    idx, wgt, pos, cnt = _router(h2hi.reshape(t, d), h2lo.reshape(t, d), rw_hi, rw_lo, bias, tri, tt)

    counts = cnt[:, 0].astype(jnp.int32)
    padded = (counts + EXPERT_ROWS - 1) // EXPERT_ROWS * EXPERT_ROWS
    pad_end = jnp.cumsum(padded)
    pad_start = pad_end - padded
    td = min(DEST_TILE, t)
    dest = _dest(idx, pos, jnp.broadcast_to(pad_start.astype(F32)[:, None], (N_EXPERTS, td)), td)
    n_rows = (t * TOP_K + N_EXPERTS * (EXPERT_ROWS - 1) + EXPERT_ROWS - 1) // EXPERT_ROWS * EXPERT_ROWS
    nblk = n_rows // EXPERT_ROWS
    blk_row = jnp.arange(nblk, dtype=jnp.int32) * EXPERT_ROWS
    blk_e = jnp.minimum(jnp.sum((pad_end[None, :] <= blk_row[:, None]).astype(jnp.int32), axis=1), N_EXPERTS - 1)
    n_used = (pad_end[-1:] // EXPERT_ROWS).astype(jnp.int32)

    last_blk = jnp.maximum(pad_end // EXPERT_ROWS - 1, 0).astype(jnp.int32)
    xs = _dispatch(dest, h2rt, _zero_tails(last_blk, n_rows), min(DISPATCH_TILE, t))
    ys = _experts(blk_e, n_used, xs, exp_w_gate[0], exp_w_up[0], exp_w_down[0])
    return _combine(dest, ys, wgt.T, base, mods, final_norm_w[None, :], min(COMBINE_TILE, l))
```

```python
import functools
import math

import jax
import jax.numpy as jnp
from jax import lax
from jax.experimental import pallas as pl
from jax.experimental.pallas import tpu as pltpu

F32 = jnp.float32
BF16 = jnp.bfloat16
HIGHEST = lax.Precision.HIGHEST

GRID_W = 64
N_HEADS = 8
N_KV_HEADS = 2
HEAD_DIM = 64
ATTN_WIDTH = N_HEADS * HEAD_DIM
KV_WIDTH = N_KV_HEADS * HEAD_DIM
HYENA_WIDTH = 512
Q_END = ATTN_WIDTH
K_END = Q_END + KV_WIDTH
V_END = K_END + KV_WIDTH
ROPE_THETA = 10000.0
FILTER_EMB = 33
FILTER_DECAY_FAST = 0.3
FILTER_DECAY_SLOW = 1.5
FILTER_TARGET = 1e-2
N_EXPERTS = 256
TOP_K = 8
N_GROUPS = 8
TOPK_GROUPS = 4
ROUTE_SCALE = 2.5
EPS = 1e-6

LANES = 128
ROW_SUB = 4
EXPERT_ROWS = 512
EXPERT_SUB = 256
NEG_INF = float("-inf")

ROW_TILE = 512
ATTN_Q_TILE = 256
ATTN_SUB = 128
ROUTER_TILE = 256
DEST_TILE = 1024
DISPATCH_TILE = 1024
COMBINE_TILE = 256
ADALN_COLS = 1536
VMEM_LIMIT_MIB = 48


def _params(semantics):
    return pltpu.CompilerParams(dimension_semantics=semantics, vmem_limit_bytes=VMEM_LIMIT_MIB * 1024 * 1024)


def _rms(x, w):
    return x * lax.rsqrt(jnp.mean(x * x, axis=-1, keepdims=True) + EPS) * w


def _sigmoid(x):
    return 1.0 / (1.0 + jnp.exp(-x))


def _adaln_kernel(c_ref, w_ref, b_ref, o_ref):
    c = c_ref[...]
    s = c * _sigmoid(c)
    o_ref[...] = jnp.dot(s, w_ref[...], precision=HIGHEST, preferred_element_type=F32) + b_ref[...]


def _adaln(cond, w, b):
    rows, d = cond.shape
    n = w.shape[1]
    tn = ADALN_COLS
    return pl.pallas_call(
        _adaln_kernel,
        out_shape=jax.ShapeDtypeStruct((rows, n), F32),
        grid=(n // tn,),
        in_specs=[pl.BlockSpec((rows, d), lambda j: (0, 0)),
                  pl.BlockSpec((d, tn), lambda j: (0, j)),
                  pl.BlockSpec((1, tn), lambda j: (0, j))],
        out_specs=pl.BlockSpec((rows, tn), lambda j: (0, j)),
        compiler_params=_params(("arbitrary",)),
        name="adaln",
    )(cond, w, b)


def _head_rms(t, gmat, w):
    ms = jnp.dot((t * t).astype(BF16), gmat, preferred_element_type=F32)
    return t * lax.rsqrt(ms + EPS) * w


def _modulated(x, norm_w, shift, scale):
    return _rms(x, norm_w) * (1.0 + scale) + shift


def _ctx_kv_kernel(ctx_ref, mod_ref, n1_ref, w_ref, g_ref, kn_ref, kc_ref, vc_ref):
    x = ctx_ref[0]
    h = _modulated(x, n1_ref[...], mod_ref[0:1, :], mod_ref[1:2, :])
    kv = jnp.dot(h.astype(BF16), w_ref[...], preferred_element_type=F32)
    k = _head_rms(kv[:, :KV_WIDTH], g_ref[...], kn_ref[...])
    v = kv[:, KV_WIDTH:]
    kc_ref[0] = k.astype(BF16)
    vc_ref[0] = jnp.concatenate([v, jnp.ones_like(v)], axis=1).astype(BF16)


def _ctx_kv(ctx, cmod, n1w, w_kv, gk, knw):
    b, c, d = ctx.shape
    return pl.pallas_call(
        _ctx_kv_kernel,
        out_shape=(jax.ShapeDtypeStruct((b, c, KV_WIDTH), BF16),
                   jax.ShapeDtypeStruct((b, c, 2 * KV_WIDTH), BF16)),
        grid=(b,),
        in_specs=[pl.BlockSpec((1, c, d), lambda i: (i, 0, 0)),
                  pl.BlockSpec(cmod.shape, lambda i: (0, 0)),
                  pl.BlockSpec((1, d), lambda i: (0, 0)),
                  pl.BlockSpec(w_kv.shape, lambda i: (0, 0)),
                  pl.BlockSpec(gk.shape, lambda i: (0, 0)),
                  pl.BlockSpec((1, KV_WIDTH), lambda i: (0, 0))],
        out_specs=(pl.BlockSpec((1, c, KV_WIDTH), lambda i: (i, 0, 0)),
                   pl.BlockSpec((1, c, 2 * KV_WIDTH), lambda i: (i, 0, 0))),
        compiler_params=_params(("arbitrary",)),
        name="ctx_kv",
    )(ctx, cmod, n1w, w_kv, gk, knw)


def _rope(t, cos, sin, even):
    width = t.shape[1]
    partner = jnp.where(even, pltpu.roll(t, width - 1, axis=1), pltpu.roll(t, 1, axis=1))
    return t * cos + partner * sin


def _inproj_kernel(x_ref, mod_ref, n1_ref, w_ref, gq_ref, qn_ref, kn_ref, cos_ref, sin_ref,
                   q_ref, k_ref, v_ref, u_ref):
    x = x_ref[0]
    h = _modulated(x, n1_ref[...], mod_ref[0, 0:1, :], mod_ref[0, 1:2, :])
    p = jnp.dot(h.astype(BF16), w_ref[...], preferred_element_type=F32)
    gq = gq_ref[...]
    q = _head_rms(p[:, :Q_END], gq, qn_ref[...])
    k = _head_rms(p[:, Q_END:K_END], gq[:KV_WIDTH, :KV_WIDTH], kn_ref[...])
    v = p[:, K_END:V_END]
    cos = cos_ref[...]
    sin = sin_ref[...]
    reps = Q_END // LANES
    cos_q = jnp.concatenate([cos] * reps, axis=1)
    sin_q = jnp.concatenate([sin] * reps, axis=1)
    even_q = (lax.broadcasted_iota(jnp.int32, (1, Q_END), 1) & 1) == 0
    even_k = (lax.broadcasted_iota(jnp.int32, (1, KV_WIDTH), 1) & 1) == 0
    q = _rope(q, cos_q, sin_q, even_q) * (HEAD_DIM ** -0.5)
    k = _rope(k, cos, sin, even_k)
    q_ref[0] = q.astype(BF16)
    k_ref[0] = k.astype(BF16)
    v_ref[0] = jnp.concatenate([v, jnp.ones_like(v)], axis=1).astype(BF16)
    u_ref[0] = p[:, V_END:].astype(BF16)


def _inproj(x, mods, n1w, w_in, gq, qnw, knw, cos, sin, tl):
    b, l, d = x.shape
    ncol = w_in.shape[1]
    nu = ncol - V_END
    return pl.pallas_call(
        _inproj_kernel,
        out_shape=(jax.ShapeDtypeStruct((b, l, Q_END), BF16),
                   jax.ShapeDtypeStruct((b, l, KV_WIDTH), BF16),
                   jax.ShapeDtypeStruct((b, l, 2 * KV_WIDTH), BF16),
                   jax.ShapeDtypeStruct((b, l, nu), BF16)),
        grid=(l // tl, b),
        in_specs=[pl.BlockSpec((1, tl, d), lambda i, j: (j, i, 0)),
                  pl.BlockSpec((1, 6, d), lambda i, j: (j, 0, 0)),
                  pl.BlockSpec((1, d), lambda i, j: (0, 0)),
                  pl.BlockSpec((d, ncol), lambda i, j: (0, 0)),
                  pl.BlockSpec(gq.shape, lambda i, j: (0, 0)),
                  pl.BlockSpec((1, Q_END), lambda i, j: (0, 0)),
                  pl.BlockSpec((1, KV_WIDTH), lambda i, j: (0, 0)),
                  pl.BlockSpec((tl, LANES), lambda i, j: (i, 0)),
                  pl.BlockSpec((tl, LANES), lambda i, j: (i, 0))],
        out_specs=(pl.BlockSpec((1, tl, Q_END), lambda i, j: (j, i, 0)),
                   pl.BlockSpec((1, tl, KV_WIDTH), lambda i, j: (j, i, 0)),
                   pl.BlockSpec((1, tl, 2 * KV_WIDTH), lambda i, j: (j, i, 0)),
                   pl.BlockSpec((1, tl, nu), lambda i, j: (j, i, 0))),
        compiler_params=_params(("arbitrary", "arbitrary")),
        name="inproj",
    )(x, mods, n1w, w_in, gq, qnw, knw, cos, sin)


def _attn_kernel(q_ref, k_ref, v_ref, wn_ref, o_ref, *, sub):
    tq = q_ref.shape[1]
    low = lax.broadcasted_iota(jnp.int32, (1, LANES), 1) < HEAD_DIM
    nt = (((1,), (1,)), ((), ()))
    kk = k_ref[0]
    vv = v_ref[0]
    for r in range(0, tq, sub):
        outs = []
        for j in range(Q_END // LANES):
            qv = q_ref[0, r:r + sub, LANES * j:LANES * (j + 1)]
            zero = jnp.zeros_like(qv)
            halves = []
            for g in range(N_KV_HEADS):
                qh = jnp.where(low, qv, zero) if g == 0 else jnp.where(low, zero, qv)
                s = lax.dot_general(qh, kk, nt, preferred_element_type=F32)
                p = jnp.exp(s - jnp.max(s, axis=-1, keepdims=True)).astype(BF16)
                pv = jnp.dot(p, vv, preferred_element_type=F32)
                halves.append(pv[:, :LANES] / pv[:, LANES:])
            outs.append(jnp.where(low, halves[0], halves[1]))
        a = jnp.concatenate(outs, axis=1)
        o_ref[0, r:r + sub, :] = _rms(a, wn_ref[...]).astype(BF16)


def _attention(q, k, v, wn, tq, sub):
    b, l, _ = q.shape
    n = k.shape[1]
    return pl.pallas_call(
        functools.partial(_attn_kernel, sub=sub),
        out_shape=jax.ShapeDtypeStruct((b, l, Q_END), BF16),
        grid=(b, l // tq),
        in_specs=[pl.BlockSpec((1, tq, Q_END), lambda i, j: (i, j, 0)),
                  pl.BlockSpec((1, n, KV_WIDTH), lambda i, j: (i, 0, 0)),
                  pl.BlockSpec((1, n, 2 * KV_WIDTH), lambda i, j: (i, 0, 0)),
                  pl.BlockSpec((1, Q_END), lambda i, j: (0, 0))],
        out_specs=pl.BlockSpec((1, tq, Q_END), lambda i, j: (i, j, 0)),
        compiler_params=_params(("arbitrary", "arbitrary")),
        name="attn",
    )(q, k, v, wn)


def _hyena_pre_kernel(u0_ref, u1_ref, u2_ref, cw_ref, cb_ref, g_ref, x0_ref):
    l = u0_ref.shape[1]
    row = lax.broadcasted_iota(jnp.int32, (l, LANES), 0)

    def conv(u_ref, gi):
        u = u_ref[0].astype(F32)
        prev = jnp.where(row == 0, 0.0, pltpu.roll(u, 1, axis=0))
        nxt = jnp.where(row == l - 1, 0.0, pltpu.roll(u, l - 1, axis=0))
        w = cw_ref[gi]
        return w[0:1] * prev + w[1:2] * u + w[2:3] * nxt + cb_ref[gi:gi + 1, :]

    x0 = conv(u0_ref, 0)
    x1 = conv(u1_ref, 1)
    v = conv(u2_ref, 2)
    g = v * x1
    g_ref[0] = g.astype(BF16)
    x0_ref[0] = x0.astype(BF16)


def _hyena_pre(u, cw, cb):
    b, l, _ = u.shape
    nblk = HYENA_WIDTH // LANES
    ublk = lambda gi: pl.BlockSpec((1, l, LANES), lambda i, j: (i, 0, gi * nblk + j))
    oblk = pl.BlockSpec((1, l, LANES), lambda i, j: (i, 0, j))
    return pl.pallas_call(
        _hyena_pre_kernel,
        out_shape=(jax.ShapeDtypeStruct((b, l, HYENA_WIDTH), BF16),
                   jax.ShapeDtypeStruct((b, l, HYENA_WIDTH), BF16)),
        grid=(b, nblk),
        in_specs=[ublk(0), ublk(1), ublk(2),
                  pl.BlockSpec((3, 3, LANES), lambda i, j: (0, 0, j)),
                  pl.BlockSpec((3, LANES), lambda i, j: (0, j))],
        out_specs=(oblk, oblk),
        compiler_params=_params(("arbitrary", "arbitrary")),
        name="hyena_pre",
    )(u, u, u, cw, cb)


def _filter_kernel(z_ref, w1_ref, b1_ref, w2_ref, b2_ref, w3_ref, b3_ref, w4_ref, fr_ref, dl_ref, o_ref):
    tl = z_ref.shape[0]
    z = z_ref[...]
    fr = fr_ref[...]
    dot = lambda a, w: jnp.dot(a, w, precision=HIGHEST, preferred_element_type=F32)
    h = jnp.sin(fr * (dot(z, w1_ref[...]) + b1_ref[...]))
    h = jnp.sin(fr * (dot(h, w2_ref[...]) + b2_ref[...]))
    h = jnp.sin(fr * (dot(h, w3_ref[...]) + b3_ref[...]))
    h = dot(h, w4_ref[...])
    t = z[:, 0:1]
    decay = jnp.exp(-t * jnp.abs(dl_ref[...]))
    hf = h[:, :HYENA_WIDTH] * decay
    hb = h[:, HYENA_WIDTH:] * decay
    row = lax.broadcasted_iota(jnp.int32, (tl, HYENA_WIDTH), 0) + pl.program_id(0) * tl
    hb = jnp.where(row == 0, 0.0, hb)
    o_ref[0] = hf + hb
    o_ref[1] = hf - hb


def _hyena_filter(z, w1, b1, w2, b2, w3, b3, w4, freq, deltas, tl):
    l = z.shape[0]
    full = lambda a: pl.BlockSpec(a.shape, lambda i: (0,) * a.ndim)
    return pl.pallas_call(
        _filter_kernel,
        out_shape=jax.ShapeDtypeStruct((2, l, HYENA_WIDTH), F32),
        grid=(l // tl,),
        in_specs=[pl.BlockSpec((tl, z.shape[1]), lambda i: (i, 0)),
                  full(w1), full(b1), full(w2), full(b2), full(w3), full(b3), full(w4), full(freq), full(deltas)],
        out_specs=pl.BlockSpec((2, tl, HYENA_WIDTH), lambda i: (0, i, 0)),
        compiler_params=_params(("arbitrary",)),
        name="hyena_filter",
    )(z, w1, b1, w2, b2, w3, b3, w4, freq, deltas)


def _dft_kernel(f_ref, x_ref, o_ref):
    o_ref[0] = jnp.dot(f_ref[...], x_ref[0].astype(BF16), preferred_element_type=F32)


def _dft(fmat, x, tf):
    nb, l, w = x.shape
    n = fmat.shape[0]
    return pl.pallas_call(
        _dft_kernel,
        out_shape=jax.ShapeDtypeStruct((nb, n, w), F32),
        grid=(n // tf, nb),
        in_specs=[pl.BlockSpec((tf, l), lambda i, j: (i, 0)),
                  pl.BlockSpec((1, l, w), lambda i, j: (j, 0, 0))],
        out_specs=pl.BlockSpec((1, tf, w), lambda i, j: (j, i, 0)),
        compiler_params=_params(("arbitrary", "arbitrary")),
        name="dft_filter",
    )(fmat, x)


def _dft_mul_kernel(fr_ref, fi_ref, x_ref, a_ref, b_ref, d_ref, zr_ref, zi_ref):
    x = x_ref[0]
    xr = jnp.dot(fr_ref[...], x, preferred_element_type=F32)
    xi = jnp.dot(fi_ref[...], x, preferred_element_type=F32)
    bb = b_ref[...]
    zr_ref[0] = (xr * a_ref[...] - xi * bb).astype(BF16)
    zi_ref[0] = (xr * bb + xi * d_ref[...]).astype(BF16)


def _dft_mul(fmat, g, sa, sb, sd, tf):
    b, l, w = g.shape
    nf = l // tf
    spec = pl.BlockSpec((tf, w), lambda i, j: (i, 0))
    return pl.pallas_call(
        _dft_mul_kernel,
        out_shape=(jax.ShapeDtypeStruct((b, l, w), BF16), jax.ShapeDtypeStruct((b, l, w), BF16)),
        grid=(nf, b),
        in_specs=[pl.BlockSpec((tf, l), lambda i, j: (i, 0)),
                  pl.BlockSpec((tf, l), lambda i, j: (i + nf, 0)),
                  pl.BlockSpec((1, l, w), lambda i, j: (j, 0, 0)),
                  spec, spec, spec],
        out_specs=(pl.BlockSpec((1, tf, w), lambda i, j: (j, i, 0)),
                   pl.BlockSpec((1, tf, w), lambda i, j: (j, i, 0))),
        compiler_params=_params(("arbitrary", "arbitrary")),
        name="dft_mul",
    )(fmat, fmat, g, sa, sb, sd)


def _idft_kernel(fr_ref, fi_ref, zr_ref, zi_ref, g_ref, x0_ref, hb_ref, wn_ref, o_ref):
    conv = (jnp.dot(fr_ref[...], zr_ref[0], preferred_element_type=F32)
            + jnp.dot(fi_ref[...], zi_ref[0], preferred_element_type=F32))
    y = (conv + g_ref[0].astype(F32) * hb_ref[...]) * x0_ref[0].astype(F32)
    o_ref[0] = _rms(y, wn_ref[...]).astype(BF16)


def _idft(finv_r, finv_i, zr, zi, g, x0, hbias, wn, tt):
    b, l, w = zr.shape
    tile = pl.BlockSpec((1, tt, w), lambda i, j: (j, i, 0))
    return pl.pallas_call(
        _idft_kernel,
        out_shape=jax.ShapeDtypeStruct((b, l, w), BF16),
        grid=(l // tt, b),
        in_specs=[pl.BlockSpec((tt, l), lambda i, j: (i, 0)),
                  pl.BlockSpec((tt, l), lambda i, j: (i, 0)),
                  pl.BlockSpec((1, l, w), lambda i, j: (j, 0, 0)),
                  pl.BlockSpec((1, l, w), lambda i, j: (j, 0, 0)),
                  tile, tile,
                  pl.BlockSpec((1, w), lambda i, j: (0, 0)),
                  pl.BlockSpec((1, w), lambda i, j: (0, 0))],
        out_specs=tile,
        compiler_params=_params(("arbitrary", "arbitrary")),
        name="idft",
    )(finv_r, finv_i, zr, zi, g, x0, hbias, wn)


def _bf16_bits(v):
    return lax.bitcast_convert_type(v.astype(BF16).astype(F32), jnp.uint32)


def _store_row_tiles(ref, val):
    rows, half = val.shape[0], val.shape[1] // 2
    assert half == ROW_SUB * LANES
    for j in range(ROW_SUB):
        lo = _bf16_bits(val[:, LANES * j:LANES * (j + 1)]) >> 16
        hi = _bf16_bits(val[:, half + LANES * j:half + LANES * (j + 1)]) & jnp.uint32(0xFFFF0000)
        ref[pl.ds(j, rows, stride=ROW_SUB), :] = lo | hi


def _load_row_tiles(ref, rows):
    words = [ref[pl.ds(j, rows, stride=ROW_SUB), :] for j in range(ROW_SUB)]
    lo = [lax.bitcast_convert_type(w << 16, F32) for w in words]
    hi = [lax.bitcast_convert_type(w & jnp.uint32(0xFFFF0000), F32) for w in words]
    return jnp.concatenate(lo + hi, axis=1)


def _merge_kernel(a_ref, y_ref, x_ref, mod_ref, wa_ref, wy_ref, n2_ref, sg_ref, su_ref, sd_ref,
                  base_ref, hi_ref, lo_ref, rt_ref):
    m = (jnp.dot(a_ref[0], wa_ref[...], preferred_element_type=F32)
         + jnp.dot(y_ref[0], wy_ref[...], preferred_element_type=F32))
    x1 = x_ref[0] + mod_ref[0, 2:3, :] * m
    h2 = _modulated(x1, n2_ref[...], mod_ref[0, 3:4, :], mod_ref[0, 4:5, :])
    hi = h2.astype(BF16)
    hi_ref[0] = hi
    lo_ref[0] = (h2 - hi.astype(F32)).astype(BF16)
    _store_row_tiles(rt_ref, h2)
    gate = jnp.dot(hi, sg_ref[...], preferred_element_type=F32)
    up = jnp.dot(hi, su_ref[...], preferred_element_type=F32)
    act = (gate * _sigmoid(gate) * up).astype(BF16)
    shared = jnp.dot(act, sd_ref[...], preferred_element_type=F32)
    base_ref[0] = x1 + mod_ref[0, 5:6, :] * shared


def _merge(an, yn, x, mods, wa, wy, n2w, sg, su, sd, tl):
    b, l, d = x.shape
    full = lambda a: pl.BlockSpec(a.shape, lambda i, j: (0,) * a.ndim)
    half = pl.BlockSpec((1, tl, an.shape[2]), lambda i, j: (i, j, 0))
    wide = pl.BlockSpec((1, tl, d), lambda i, j: (i, j, 0))
    per_b = l // tl
    return pl.pallas_call(
        _merge_kernel,
        out_shape=(jax.ShapeDtypeStruct((b, l, d), F32),
                   jax.ShapeDtypeStruct((b, l, d), BF16),
                   jax.ShapeDtypeStruct((b, l, d), BF16),
                   jax.ShapeDtypeStruct((b * l * ROW_SUB, LANES), jnp.uint32)),
        grid=(b, per_b),
        in_specs=[half, half, wide,
                  pl.BlockSpec((1, 6, d), lambda i, j: (i, 0, 0)),
                  full(wa), full(wy), full(n2w), full(sg), full(su), full(sd)],
        out_specs=(wide, wide, wide,
                   pl.BlockSpec((tl * ROW_SUB, LANES), lambda i, j: (i * per_b + j, 0))),
        compiler_params=_params(("arbitrary", "arbitrary")),
        name="merge",
    )(an, yn, x, mods, wa, wy, n2w, sg, su, sd)


def _router_kernel(hi_ref, lo_ref, whi_ref, wlo_ref, bias_ref, tri_ref,
                   idx_ref, wgt_ref, pos_ref, cnt_ref, run_ref):
    tt = hi_ref.shape[0]
    per_group = N_EXPERTS // N_GROUPS

    @pl.when(pl.program_id(0) == 0)
    def _():
        run_ref[...] = jnp.zeros_like(run_ref)

    nt = (((1,), (1,)), ((), ()))
    hi = hi_ref[...]
    whi = whi_ref[...]
    logits = (lax.dot_general(whi, hi, nt, preferred_element_type=F32)
              + lax.dot_general(whi, lo_ref[...], nt, preferred_element_type=F32)
              + lax.dot_general(wlo_ref[...], hi, nt, preferred_element_type=F32))
    scores = _sigmoid(logits)
    biased = scores + bias_ref[...]

    ridx = lax.broadcasted_iota(jnp.int32, (per_group, tt), 0)
    groups = [biased[g * per_group:(g + 1) * per_group, :] for g in range(N_GROUPS)]
    gs = []
    for blk in groups:
        m1 = jnp.max(blk, axis=0, keepdims=True)
        i1 = jnp.min(jnp.where(blk == m1, ridx, per_group), axis=0, keepdims=True)
        m2 = jnp.max(jnp.where(ridx == i1, NEG_INF, blk), axis=0, keepdims=True)
        gs.append(m1 + m2)

    kept = []
    for g in range(N_GROUPS):
        ahead = jnp.zeros((1, tt), F32)
        for o in range(N_GROUPS):
            if o != g:
                wins = (gs[o] >= gs[g]) if o < g else (gs[o] > gs[g])
                ahead = ahead + jnp.where(wins, 1.0, 0.0)
        kept.append(jnp.where(ahead < TOPK_GROUPS, groups[g], NEG_INF))
    cur = jnp.concatenate(kept, axis=0)

    eidx = lax.broadcasted_iota(jnp.int32, cur.shape, 0)
    kept_mask = cur
    picks = []
    wsel = []
    for _ in range(TOP_K):
        mx = jnp.max(cur, axis=0, keepdims=True)
        first = jnp.min(jnp.where(cur == mx, eidx, N_EXPERTS), axis=0, keepdims=True)
        sel = eidx == first
        picks.append(first)
        wsel.append(jnp.sum(jnp.where(sel, scores, 0.0), axis=0, keepdims=True))
        cur = jnp.where(sel, NEG_INF, cur)
    w = jnp.concatenate(wsel, axis=0)
    w = w / jnp.sum(w, axis=0, keepdims=True) * ROUTE_SCALE
    idx = jnp.concatenate(picks, axis=0)

    oh = jnp.where(cur == NEG_INF, jnp.where(kept_mask == NEG_INF, 0.0, 1.0), 0.0).astype(BF16)
    before = jnp.dot(oh, tri_ref[0], preferred_element_type=F32)
    total = jnp.dot(oh, tri_ref[1], preferred_element_type=F32)
    pos_ref[...] = run_ref[...] + before
    run_ref[...] = run_ref[...] + total

    idx_ref[...] = idx
    wgt_ref[...] = w
    cnt_ref[...] = run_ref[...]


def _router(hi, lo, whi, wlo, bias, tri, tt):
    t, d = hi.shape
    tok = pl.BlockSpec((tt, d), lambda i: (i, 0))
    full = lambda a: pl.BlockSpec(a.shape, lambda i: (0,) * a.ndim)
    out = pl.BlockSpec((TOP_K, tt), lambda i: (0, i))
    return pl.pallas_call(
        _router_kernel,
        out_shape=(jax.ShapeDtypeStruct((TOP_K, t), jnp.int32),
                   jax.ShapeDtypeStruct((TOP_K, t), F32),
                   jax.ShapeDtypeStruct((N_EXPERTS, t), F32),
                   jax.ShapeDtypeStruct((N_EXPERTS, tt), F32)),
        grid=(t // tt,),
        in_specs=[tok, tok, full(whi), full(wlo), full(bias), full(tri)],
        out_specs=(out, out, pl.BlockSpec((N_EXPERTS, tt), lambda i: (0, i)),
                   pl.BlockSpec((N_EXPERTS, tt), lambda i: (0, 0))),
        scratch_shapes=[pltpu.VMEM((N_EXPERTS, tt), F32)],
        compiler_params=_params(("arbitrary",)),
        name="router",
    )(hi, lo, whi, wlo, bias, tri)


def _row_tile(ref, r):
    return ref.at[pl.ds(pl.multiple_of(r * ROW_SUB, ROW_SUB), ROW_SUB), :]


def _dest_kernel(idx_ref, pos_ref, start_ref, dest_ref):
    eidx = lax.broadcasted_iota(jnp.int32, start_ref.shape, 0)
    row = start_ref[...] + pos_ref[...]
    rows = [jnp.sum(jnp.where(eidx == idx_ref[k:k + 1, :], row, 0.0), axis=0, keepdims=True)
            for k in range(TOP_K)]
    dest_ref[...] = jnp.concatenate(rows, axis=0).astype(jnp.int32)


def _dest(idx, pos, start, tt):
    t = idx.shape[1]
    blk = pl.BlockSpec((TOP_K, tt), lambda i: (0, i))
    return pl.pallas_call(
        _dest_kernel,
        out_shape=jax.ShapeDtypeStruct((TOP_K, t), jnp.int32),
        grid=(t // tt,),
        in_specs=[blk, pl.BlockSpec((N_EXPERTS, tt), lambda i: (0, i)), pl.BlockSpec(start.shape, lambda i: (0, 0))],
        out_specs=blk,
        compiler_params=_params(("arbitrary",)),
        name="dest",
    )(idx, pos, start)


def _zero_tails_kernel(last_ref, o_ref):
    del last_ref
    o_ref[...] = jnp.zeros(o_ref.shape, o_ref.dtype)


def _zero_tails(last_blk, n_rows):
    return pl.pallas_call(
        _zero_tails_kernel,
        out_shape=jax.ShapeDtypeStruct((n_rows * ROW_SUB, LANES), jnp.uint32),
        grid_spec=pltpu.PrefetchScalarGridSpec(
            num_scalar_prefetch=1,
            grid=(last_blk.shape[0],),
            in_specs=[],
            out_specs=pl.BlockSpec((EXPERT_ROWS * ROW_SUB, LANES), lambda e, last: (last[e], 0))),
        compiler_params=_params(("arbitrary",)),
        name="zero_tails",
    )(last_blk)


def _dispatch_kernel(dest_ref, h_ref, xs_in_ref, xs_ref, sem):
    del xs_in_ref
    td = dest_ref.shape[1]

    def row_copy(t, k):
        return pltpu.make_async_copy(_row_tile(h_ref, t), _row_tile(xs_ref, dest_ref[k, t]), sem)

    def issue(t, carry):
        for k in range(TOP_K):
            row_copy(t, k).start(priority=k % 2)
        return carry

    lax.fori_loop(0, td, issue, 0)

    def drain(t, carry):
        for k in range(TOP_K):
            row_copy(t, k).wait()
        return carry

    lax.fori_loop(0, td, drain, 0)


def _dispatch(dest, h_rt, xs0, td):
    t = dest.shape[1]
    return pl.pallas_call(
        _dispatch_kernel,
        out_shape=jax.ShapeDtypeStruct(xs0.shape, xs0.dtype),
        grid=(t // td,),
        in_specs=[pl.BlockSpec((TOP_K, td), lambda i: (0, i), memory_space=pltpu.SMEM),
                  pl.BlockSpec((td * ROW_SUB, LANES), lambda i: (i, 0)),
                  pl.BlockSpec(memory_space=pl.ANY)],
        out_specs=pl.BlockSpec(memory_space=pl.ANY),
        scratch_shapes=[pltpu.SemaphoreType.DMA],
        input_output_aliases={2: 0},
        compiler_params=_params(("arbitrary",)),
        name="dispatch",
    )(dest, h_rt, xs0)


def _experts_kernel(be_ref, nu_ref, x_ref, wg_ref, wu_ref, wd_ref, y_ref, wgu_bf, wd_bf):
    i = pl.program_id(0)
    used = i < nu_ref[0]
    new_expert = jnp.logical_or(i == 0, be_ref[i] != be_ref[jnp.maximum(i - 1, 0)])

    @pl.when(jnp.logical_and(used, new_expert))
    def _():
        ff = wg_ref.shape[2]
        wgu_bf[:, :ff] = wg_ref[0].astype(BF16)
        wgu_bf[:, ff:] = wu_ref[0].astype(BF16)
        wd_bf[...] = wd_ref[0].astype(BF16)

    @pl.when(used)
    def _():
        ff = wg_ref.shape[2]
        for r in range(0, EXPERT_ROWS, EXPERT_SUB):
            rows = pl.ds(r * ROW_SUB, EXPERT_SUB * ROW_SUB)
            x = _load_row_tiles(x_ref.at[rows, :], EXPERT_SUB).astype(BF16)
            gu = jnp.dot(x, wgu_bf[...], preferred_element_type=F32)
            gate = gu[:, :ff]
            act = (gate * _sigmoid(gate) * gu[:, ff:]).astype(BF16)
            _store_row_tiles(y_ref.at[rows, :], jnp.dot(act, wd_bf[...], preferred_element_type=F32))


def _experts(blk_e, n_used, xs, wg, wu, wd):
    d, ff = wg.shape[1], wg.shape[2]
    nblk = xs.shape[0] // (EXPERT_ROWS * ROW_SUB)
    row = lambda i, be, nu: (jnp.minimum(i, nu[0] - 1), 0)
    wsel = lambda i, be, nu: (be[jnp.minimum(i, nu[0] - 1)], 0, 0)
    return pl.pallas_call(
        _experts_kernel,
        out_shape=jax.ShapeDtypeStruct(xs.shape, xs.dtype),
        grid_spec=pltpu.PrefetchScalarGridSpec(
            num_scalar_prefetch=2,
            grid=(nblk,),
            in_specs=[pl.BlockSpec((EXPERT_ROWS * ROW_SUB, LANES), row),
                      pl.BlockSpec((1, d, ff), wsel),
                      pl.BlockSpec((1, d, ff), wsel),
                      pl.BlockSpec((1, ff, d), wsel)],
            out_specs=pl.BlockSpec((EXPERT_ROWS * ROW_SUB, LANES), row),
            scratch_shapes=[pltpu.VMEM((d, 2 * ff), BF16), pltpu.VMEM((ff, d), BF16)]),
        compiler_params=_params(("arbitrary",)),
        name="experts",
    )(blk_e, n_used, xs, wg, wu, wd)


def _combine_kernel(dest_ref, next_ref, ys_ref, w_ref, base_ref, mod_ref, fw_ref, o_ref, buf, sem):
    tc = dest_ref.shape[1]
    i = pl.program_id(0)
    slot = i % 2

    def row_copy(row, s, t, k):
        return pltpu.make_async_copy(_row_tile(ys_ref, row), _row_tile(buf.at[s, k], t), sem.at[s])

    def issue(d_ref, s):
        for t in range(tc):
            for k in range(TOP_K):
                row_copy(d_ref[k, t], s, t, k).start(priority=k % 2)

    def drain(s):
        for t in range(tc):
            for k in range(TOP_K):
                row_copy(0, s, t, k).wait()

    @pl.when(i == 0)
    def _():
        issue(dest_ref, 0)

    drain(slot)
    issue(next_ref, 1 - slot)
    w = w_ref[...]
    routed = w[:, 0:1] * _load_row_tiles(buf.at[slot, 0], tc)
    for k in range(1, TOP_K):
        routed = routed + w[:, k:k + 1] * _load_row_tiles(buf.at[slot, k], tc)
    x = base_ref[0] + mod_ref[0, 5:6, :] * routed
    o_ref[0] = _rms(x, fw_ref[...])

    @pl.when(i == pl.num_programs(0) - 1)
    def _():
        drain(1 - slot)


def _combine(dest, ys, wt, base, mods, fw, tc):
    b, l, d = base.shape
    per_b = l // tc
    steps = b * per_b
    return pl.pallas_call(
        _combine_kernel,
        out_shape=jax.ShapeDtypeStruct((b, l, d), F32),
        grid=(steps,),
        in_specs=[pl.BlockSpec((TOP_K, tc), lambda i: (0, i), memory_space=pltpu.SMEM),
                  pl.BlockSpec((TOP_K, tc), lambda i: (0, jnp.minimum(i + 1, steps - 1)), memory_space=pltpu.SMEM),
                  pl.BlockSpec(memory_space=pl.ANY),
                  pl.BlockSpec((tc, TOP_K), lambda i: (i, 0)),
                  pl.BlockSpec((1, tc, d), lambda i: (i // per_b, i % per_b, 0)),
                  pl.BlockSpec((1, 6, d), lambda i: (i // per_b, 0, 0)),
                  pl.BlockSpec((1, d), lambda i: (0, 0))],
        out_specs=pl.BlockSpec((1, tc, d), lambda i: (i // per_b, i % per_b, 0)),
        scratch_shapes=[pltpu.VMEM((2, TOP_K, tc * ROW_SUB, LANES), jnp.uint32), pltpu.SemaphoreType.DMA((2,))],
        compiler_params=_params(("arbitrary",)),
        name="combine",
    )(dest, dest, ys, wt, base, mods, fw)


def _rope_tables(l):
    t = jnp.arange(l, dtype=jnp.int32)
    row = (t // GRID_W).astype(F32)
    col = (t % GRID_W).astype(F32)
    n_freq = HEAD_DIM // 4
    inv = ROPE_THETA ** (-jnp.arange(n_freq, dtype=F32) / n_freq)
    ang = jnp.concatenate([row[:, None] * inv, col[:, None] * inv], axis=-1)
    cos = jnp.repeat(jnp.cos(ang), 2, axis=1)
    sin = jnp.repeat(jnp.sin(ang), 2, axis=1)
    sign = jnp.tile(jnp.array([-1.0, 1.0], F32), HEAD_DIM // 2)
    reps = LANES // HEAD_DIM
    return jnp.tile(cos, (1, reps)), jnp.tile(sin * sign, (1, reps))


def _filter_features(l):
    t = jnp.linspace(0.0, 1.0, l, dtype=F32)[:, None]
    bands = (FILTER_EMB - 1) // 2
    w = 2.0 * math.pi * jnp.arange(l, dtype=F32)[:, None] / l
    f = jnp.linspace(1e-4, bands - 1, bands, dtype=F32)[None, :]
    z = jnp.concatenate([t, jnp.cos(f * w), -jnp.sin(f * w)], axis=-1)
    min_decay = math.log(FILTER_TARGET) / FILTER_DECAY_FAST
    max_decay = math.log(FILTER_TARGET) / FILTER_DECAY_SLOW
    deltas = jnp.linspace(min_decay, max_decay, HYENA_WIDTH, dtype=F32)[None, :]
    return jnp.pad(z, ((0, 0), (0, LANES - FILTER_EMB))), deltas


def _dft_matrices(l):
    n = 2 * l
    idx = jnp.arange(l, dtype=jnp.int32)
    r = math.isqrt(l)
    assert r * r == l
    sub = jnp.arange(r, dtype=jnp.int32)
    hi = ((r * sub[:, None] * idx[None, :]) % n).astype(F32) * (2.0 * math.pi / n)
    lo = ((sub[:, None] * idx[None, :]) % n).astype(F32) * (2.0 * math.pi / n)
    ch, sh, cl, sl = jnp.cos(hi)[:, None, :], jnp.sin(hi)[:, None, :], jnp.cos(lo)[None], jnp.sin(lo)[None]
    c = (ch * cl - sh * sl).reshape(l, l)
    s = (sh * cl + ch * sl).reshape(l, l)
    alt = jnp.where(idx % 2 == 0, 1.0, -1.0).astype(F32)
    first = (idx == 0)[:, None]
    fwd = jnp.concatenate([c, jnp.where(first, alt[None, :], -s)], axis=0).astype(BF16)
    firstc = (idx == 0)[None, :]
    inv_r = (jnp.where(firstc, 1.0, 2.0) * c / n).astype(BF16)
    inv_i = (jnp.where(firstc, alt[:, None], -2.0 * s) / n).astype(BF16)
    return fwd, inv_r, inv_i


def _head_perm():
    order = []
    for j in range(N_HEADS // 2):
        order += list(range(j * HEAD_DIM, (j + 1) * HEAD_DIM))
        order += list(range((j + N_HEADS // 2) * HEAD_DIM, (j + 1 + N_HEADS // 2) * HEAD_DIM))
    return jnp.array(order, jnp.int32)


def _pad2(a, rows, cols):
    return jnp.pad(a, ((0, rows - a.shape[0]), (0, cols - a.shape[1])))


def kernel(x, c, ctx, c_ctx, mod_w, mod_b, norm1_w, w_in, q_norm_w, k_norm_w, conv_w, conv_b, filt_w1, filt_b1, filt_w2, filt_b2, filt_w3, filt_b3, filt_w4, filt_freq, hyena_bias, attn_out_norm_w, hyena_out_norm_w, w_out, norm2_w, router_w, router_bias, exp_w_gate, exp_w_up, exp_w_down, sh_w_gate, sh_w_up, sh_w_down, final_norm_w):
    b, l, d = x.shape
    t = b * l
    assert mod_w.shape[0] == 1, "single-layer stack"
    tl = min(ROW_TILE, l)

    cond = jnp.concatenate([c, c_ctx[None, :], jnp.zeros((-(b + 1) % 8, d), F32)], axis=0)
    mod = _adaln(cond, mod_w[0], mod_b[0][None, :])
    mods = mod[:b].reshape(b, 6, d)
    cmod = mod[b].reshape(6, d)

    perm = _head_perm()
    w_in0 = w_in[0]
    w_in_k = jnp.concatenate([w_in0[:, :Q_END][:, perm], w_in0[:, Q_END:]], axis=1).astype(BF16)
    w_kv = w_in0[:, Q_END:V_END].astype(BF16)
    gq = jnp.kron(jnp.eye(N_HEADS, dtype=F32), jnp.full((HEAD_DIM, HEAD_DIM), 1.0 / HEAD_DIM, F32)).astype(BF16)
    qnw = jnp.tile(q_norm_w[0], N_HEADS)[None, :]
    knw = jnp.tile(k_norm_w[0], N_KV_HEADS)[None, :]
    n1w = norm1_w[0][None, :]
    cos, sin = _rope_tables(l)

    kc, vc = _ctx_kv(ctx, cmod, n1w, w_kv, gq[:KV_WIDTH, :KV_WIDTH], knw)
    q, k, v, u = _inproj(x, mods, n1w, w_in_k, gq, qnw, knw, cos, sin, tl)
    k_all = jnp.concatenate([kc, k], axis=1)
    v_all = jnp.concatenate([vc, v], axis=1)
    an = _attention(q, k_all, v_all, attn_out_norm_w[0][perm][None, :], min(ATTN_Q_TILE, l),
                    min(ATTN_SUB, l))

    cw = conv_w[0].reshape(3, 3, HYENA_WIDTH).transpose(1, 0, 2)
    cb = conv_b[0].reshape(3, HYENA_WIDTH)
    gbf, x0 = _hyena_pre(u, cw, cb)
    z, deltas = _filter_features(l)
    hsd = _hyena_filter(
        z, _pad2(filt_w1[0], LANES, LANES), _pad2(filt_b1[0][None, :], 1, LANES),
        _pad2(filt_w2[0], LANES, LANES), _pad2(filt_b2[0][None, :], 1, LANES),
        _pad2(filt_w3[0], LANES, LANES), _pad2(filt_b3[0][None, :], 1, LANES),
        _pad2(filt_w4[0], LANES, 2 * HYENA_WIDTH), _pad2(filt_freq[0][None, :], 1, LANES), deltas, tl)
    fwd, inv_r, inv_i = _dft_matrices(l)
    spec = _dft(fwd, hsd, tl)
    row0 = (jnp.arange(l) == 0)[:, None]
    sa = spec[0, :l]
    sd = jnp.where(row0, spec[0, l:l + 1], sa)
    sb = jnp.where(row0, 0.0, spec[1, l:])
    zr, zi = _dft_mul(fwd, gbf, sa, sb, sd, tl)
    yn = _idft(inv_r, inv_i, zr, zi, gbf, x0, hyena_bias[0][None, :], hyena_out_norm_w[0][None, :], tl)

    w_out0 = w_out[0]
    base, h2hi, h2lo, h2rt = _merge(
        an, yn, x, mods, w_out0[:ATTN_WIDTH][perm].astype(BF16), w_out0[ATTN_WIDTH:].astype(BF16),
        norm2_w[0][None, :], sh_w_gate[0].astype(BF16), sh_w_up[0].astype(BF16), sh_w_down[0].astype(BF16), tl)

    tt = ROUTER_TILE
    rwt = router_w[0].T
    rw_hi = rwt.astype(BF16)
    rw_lo = (rwt - rw_hi.astype(F32)).astype(BF16)
    bias = jnp.broadcast_to(router_bias[0][:, None], (N_EXPERTS, tt))
    ti = jnp.arange(tt)
    tri = jnp.stack([(ti[:, None] < ti[None, :]), jnp.ones((tt, tt), bool)]).astype(BF16)
    idx, wgt, pos, cnt = _router(h2hi.reshape(t, d), h2lo.reshape(t, d), rw_hi, rw_lo, bias, tri, tt)

    counts = cnt[:, 0].astype(jnp.int32)
    padded = (counts + EXPERT_ROWS - 1) // EXPERT_ROWS * EXPERT_ROWS
    pad_end = jnp.cumsum(padded)
    pad_start = pad_end - padded
    td = min(DEST_TILE, t)
    dest = _dest(idx, pos, jnp.broadcast_to(pad_start.astype(F32)[:, None], (N_EXPERTS, td)), td)
    n_rows = (t * TOP_K + N_EXPERTS * (EXPERT_ROWS - 1) + EXPERT_ROWS - 1) // EXPERT_ROWS * EXPERT_ROWS
    nblk = n_rows // EXPERT_ROWS
    blk_row = jnp.arange(nblk, dtype=jnp.int32) * EXPERT_ROWS
    blk_e = jnp.minimum(jnp.sum((pad_end[None, :] <= blk_row[:, None]).astype(jnp.int32), axis=1), N_EXPERTS - 1)
    n_used = (pad_end[-1:] // EXPERT_ROWS).astype(jnp.int32)

    last_blk = jnp.maximum(pad_end // EXPERT_ROWS - 1, 0).astype(jnp.int32)
    xs = _dispatch(dest, h2rt, _zero_tails(last_blk, n_rows), min(DISPATCH_TILE, t))
    ys = _experts(blk_e, n_used, xs, exp_w_gate[0], exp_w_up[0], exp_w_down[0])
    return _combine(dest, ys, wgt.T, base, mods, final_norm_w[None, :], min(COMBINE_TILE, l))
```

```python
import functools
import math

import jax
import jax.numpy as jnp
from jax import lax
from jax.experimental import pallas as pl
from jax.experimental.pallas import tpu as pltpu

F32 = jnp.float32
BF16 = jnp.bfloat16
HIGHEST = lax.Precision.HIGHEST

GRID_W = 64
N_HEADS = 8
N_KV_HEADS = 2
HEAD_DIM = 64
ATTN_WIDTH = N_HEADS * HEAD_DIM
KV_WIDTH = N_KV_HEADS * HEAD_DIM
HYENA_WIDTH = 512
Q_END = ATTN_WIDTH
K_END = Q_END + KV_WIDTH
V_END = K_END + KV_WIDTH
ROPE_THETA = 10000.0
FILTER_EMB = 33
FILTER_DECAY_FAST = 0.3
FILTER_DECAY_SLOW = 1.5
FILTER_TARGET = 1e-2
N_EXPERTS = 256
TOP_K = 8
N_GROUPS = 8
TOPK_GROUPS = 4
ROUTE_SCALE = 2.5
EPS = 1e-6

LANES = 128
ROW_SUB = 4
EXPERT_ROWS = 512
EXPERT_SUB = 256
NEG_INF = float("-inf")

ROW_TILE = 512
ATTN_Q_TILE = 512
ATTN_SUB = 128
ROUTER_TILE = 256
DEST_TILE = 1024
DISPATCH_TILE = 1024
COMBINE_TILE = 512
ADALN_COLS = 1536
VMEM_LIMIT_MIB = 48


def _params(semantics):
    return pltpu.CompilerParams(dimension_semantics=semantics, vmem_limit_bytes=VMEM_LIMIT_MIB * 1024 * 1024)


def _rms(x, w):
    return x * lax.rsqrt(jnp.mean(x * x, axis=-1, keepdims=True) + EPS) * w


def _sigmoid(x):
    return 1.0 / (1.0 + jnp.exp(-x))


def _adaln_kernel(c_ref, w_ref, b_ref, o_ref):
    c = c_ref[...]
    s = c * _sigmoid(c)
    o_ref[...] = jnp.dot(s, w_ref[...], precision=HIGHEST, preferred_element_type=F32) + b_ref[...]


def _adaln(cond, w, b):
    rows, d = cond.shape
    n = w.shape[1]
    tn = ADALN_COLS
    return pl.pallas_call(
        _adaln_kernel,
        out_shape=jax.ShapeDtypeStruct((rows, n), F32),
        grid=(n // tn,),
        in_specs=[pl.BlockSpec((rows, d), lambda j: (0, 0)),
                  pl.BlockSpec((d, tn), lambda j: (0, j)),
                  pl.BlockSpec((1, tn), lambda j: (0, j))],
        out_specs=pl.BlockSpec((rows, tn), lambda j: (0, j)),
        compiler_params=_params(("arbitrary",)),
        name="adaln",
    )(cond, w, b)


def _head_rms(t, gmat, w):
    ms = jnp.dot((t * t).astype(BF16), gmat, preferred_element_type=F32)
    return t * lax.rsqrt(ms + EPS) * w


def _modulated(x, norm_w, shift, scale):
    return _rms(x, norm_w) * (1.0 + scale) + shift


def _ctx_kv_kernel(ctx_ref, mod_ref, n1_ref, w_ref, g_ref, kn_ref, kc_ref, vc_ref):
    x = ctx_ref[0]
    h = _modulated(x, n1_ref[...], mod_ref[0:1, :], mod_ref[1:2, :])
    kv = jnp.dot(h.astype(BF16), w_ref[...], preferred_element_type=F32)
    k = _head_rms(kv[:, :KV_WIDTH], g_ref[...], kn_ref[...])
    v = kv[:, KV_WIDTH:]
    kc_ref[0] = k.astype(BF16)
    vc_ref[0] = jnp.concatenate([v, jnp.ones_like(v)], axis=1).astype(BF16)


def _ctx_kv(ctx, cmod, n1w, w_kv, gk, knw):
    b, c, d = ctx.shape
    return pl.pallas_call(
        _ctx_kv_kernel,
        out_shape=(jax.ShapeDtypeStruct((b, c, KV_WIDTH), BF16),
                   jax.ShapeDtypeStruct((b, c, 2 * KV_WIDTH), BF16)),
        grid=(b,),
        in_specs=[pl.BlockSpec((1, c, d), lambda i: (i, 0, 0)),
                  pl.BlockSpec(cmod.shape, lambda i: (0, 0)),
                  pl.BlockSpec((1, d), lambda i: (0, 0)),
                  pl.BlockSpec(w_kv.shape, lambda i: (0, 0)),
                  pl.BlockSpec(gk.shape, lambda i: (0, 0)),
                  pl.BlockSpec((1, KV_WIDTH), lambda i: (0, 0))],
        out_specs=(pl.BlockSpec((1, c, KV_WIDTH), lambda i: (i, 0, 0)),
                   pl.BlockSpec((1, c, 2 * KV_WIDTH), lambda i: (i, 0, 0))),
        compiler_params=_params(("arbitrary",)),
        name="ctx_kv",
    )(ctx, cmod, n1w, w_kv, gk, knw)


def _rope(t, cos, sin, even):
    width = t.shape[1]
    partner = jnp.where(even, pltpu.roll(t, width - 1, axis=1), pltpu.roll(t, 1, axis=1))
    return t * cos + partner * sin


def _inproj_kernel(x_ref, mod_ref, n1_ref, w_ref, gq_ref, qn_ref, kn_ref, cos_ref, sin_ref,
                   q_ref, k_ref, v_ref, u_ref):
    x = x_ref[0]
    h = _modulated(x, n1_ref[...], mod_ref[0, 0:1, :], mod_ref[0, 1:2, :])
    p = jnp.dot(h.astype(BF16), w_ref[...], preferred_element_type=F32)
    gq = gq_ref[...]
    q = _head_rms(p[:, :Q_END], gq, qn_ref[...])
    k = _head_rms(p[:, Q_END:K_END], gq[:KV_WIDTH, :KV_WIDTH], kn_ref[...])
    v = p[:, K_END:V_END]
    cos = cos_ref[...]
    sin = sin_ref[...]
    reps = Q_END // LANES
    cos_q = jnp.concatenate([cos] * reps, axis=1)
    sin_q = jnp.concatenate([sin] * reps, axis=1)
    even_q = (lax.broadcasted_iota(jnp.int32, (1, Q_END), 1) & 1) == 0
    even_k = (lax.broadcasted_iota(jnp.int32, (1, KV_WIDTH), 1) & 1) == 0
    q = _rope(q, cos_q, sin_q, even_q) * (HEAD_DIM ** -0.5)
    k = _rope(k, cos, sin, even_k)
    q_ref[0] = q.astype(BF16)
    k_ref[0] = k.astype(BF16)
    v_ref[0] = jnp.concatenate([v, jnp.ones_like(v)], axis=1).astype(BF16)
    u_ref[0] = p[:, V_END:].astype(BF16)


def _inproj(x, mods, n1w, w_in, gq, qnw, knw, cos, sin, tl):
    b, l, d = x.shape
    ncol = w_in.shape[1]
    nu = ncol - V_END
    return pl.pallas_call(
        _inproj_kernel,
        out_shape=(jax.ShapeDtypeStruct((b, l, Q_END), BF16),
                   jax.ShapeDtypeStruct((b, l, KV_WIDTH), BF16),
                   jax.ShapeDtypeStruct((b, l, 2 * KV_WIDTH), BF16),
                   jax.ShapeDtypeStruct((b, l, nu), BF16)),
        grid=(l // tl, b),
        in_specs=[pl.BlockSpec((1, tl, d), lambda i, j: (j, i, 0)),
                  pl.BlockSpec((1, 6, d), lambda i, j: (j, 0, 0)),
                  pl.BlockSpec((1, d), lambda i, j: (0, 0)),
                  pl.BlockSpec((d, ncol), lambda i, j: (0, 0)),
                  pl.BlockSpec(gq.shape, lambda i, j: (0, 0)),
                  pl.BlockSpec((1, Q_END), lambda i, j: (0, 0)),
                  pl.BlockSpec((1, KV_WIDTH), lambda i, j: (0, 0)),
                  pl.BlockSpec((tl, LANES), lambda i, j: (i, 0)),
                  pl.BlockSpec((tl, LANES), lambda i, j: (i, 0))],
        out_specs=(pl.BlockSpec((1, tl, Q_END), lambda i, j: (j, i, 0)),
                   pl.BlockSpec((1, tl, KV_WIDTH), lambda i, j: (j, i, 0)),
                   pl.BlockSpec((1, tl, 2 * KV_WIDTH), lambda i, j: (j, i, 0)),
                   pl.BlockSpec((1, tl, nu), lambda i, j: (j, i, 0))),
        compiler_params=_params(("arbitrary", "arbitrary")),
        name="inproj",
    )(x, mods, n1w, w_in, gq, qnw, knw, cos, sin)


def _attn_kernel(q_ref, k_ref, v_ref, wn_ref, o_ref, *, sub):
    tq = q_ref.shape[1]
    low = lax.broadcasted_iota(jnp.int32, (1, LANES), 1) < HEAD_DIM
    nt = (((1,), (1,)), ((), ()))
    kk = k_ref[0]
    vv = v_ref[0]
    for r in range(0, tq, sub):
        outs = []
        for j in range(Q_END // LANES):
            qv = q_ref[0, r:r + sub, LANES * j:LANES * (j + 1)]
            zero = jnp.zeros_like(qv)
            halves = []
            for g in range(N_KV_HEADS):
                qh = jnp.where(low, qv, zero) if g == 0 else jnp.where(low, zero, qv)
                s = lax.dot_general(qh, kk, nt, preferred_element_type=F32)
                p = jnp.exp(s - jnp.max(s, axis=-1, keepdims=True)).astype(BF16)
                pv = jnp.dot(p, vv, preferred_element_type=F32)
                halves.append(pv[:, :LANES] / pv[:, LANES:])
            outs.append(jnp.where(low, halves[0], halves[1]))
        a = jnp.concatenate(outs, axis=1)
        o_ref[0, r:r + sub, :] = _rms(a, wn_ref[...]).astype(BF16)


def _attention(q, k, v, wn, tq, sub):
    b, l, _ = q.shape
    n = k.shape[1]
    return pl.pallas_call(
        functools.partial(_attn_kernel, sub=sub),
        out_shape=jax.ShapeDtypeStruct((b, l, Q_END), BF16),
        grid=(b, l // tq),
        in_specs=[pl.BlockSpec((1, tq, Q_END), lambda i, j: (i, j, 0)),
                  pl.BlockSpec((1, n, KV_WIDTH), lambda i, j: (i, 0, 0)),
                  pl.BlockSpec((1, n, 2 * KV_WIDTH), lambda i, j: (i, 0, 0)),
                  pl.BlockSpec((1, Q_END), lambda i, j: (0, 0))],
        out_specs=pl.BlockSpec((1, tq, Q_END), lambda i, j: (i, j, 0)),
        compiler_params=_params(("arbitrary", "arbitrary")),
        name="attn",
    )(q, k, v, wn)


def _hyena_pre_kernel(u0_ref, u1_ref, u2_ref, cw_ref, cb_ref, g_ref, x0_ref):
    l = u0_ref.shape[1]
    row = lax.broadcasted_iota(jnp.int32, (l, LANES), 0)

    def conv(u_ref, gi):
        u = u_ref[0].astype(F32)
        prev = jnp.where(row == 0, 0.0, pltpu.roll(u, 1, axis=0))
        nxt = jnp.where(row == l - 1, 0.0, pltpu.roll(u, l - 1, axis=0))
        w = cw_ref[gi]
        return w[0:1] * prev + w[1:2] * u + w[2:3] * nxt + cb_ref[gi:gi + 1, :]

    x0 = conv(u0_ref, 0)
    x1 = conv(u1_ref, 1)
    v = conv(u2_ref, 2)
    g = v * x1
    g_ref[0] = g.astype(BF16)
    x0_ref[0] = x0.astype(BF16)


def _hyena_pre(u, cw, cb):
    b, l, _ = u.shape
    nblk = HYENA_WIDTH // LANES
    ublk = lambda gi: pl.BlockSpec((1, l, LANES), lambda i, j: (i, 0, gi * nblk + j))
    oblk = pl.BlockSpec((1, l, LANES), lambda i, j: (i, 0, j))
    return pl.pallas_call(
        _hyena_pre_kernel,
        out_shape=(jax.ShapeDtypeStruct((b, l, HYENA_WIDTH), BF16),
                   jax.ShapeDtypeStruct((b, l, HYENA_WIDTH), BF16)),
        grid=(b, nblk),
        in_specs=[ublk(0), ublk(1), ublk(2),
                  pl.BlockSpec((3, 3, LANES), lambda i, j: (0, 0, j)),
                  pl.BlockSpec((3, LANES), lambda i, j: (0, j))],
        out_specs=(oblk, oblk),
        compiler_params=_params(("arbitrary", "arbitrary")),
        name="hyena_pre",
    )(u, u, u, cw, cb)


def _filter_kernel(z_ref, w1_ref, b1_ref, w2_ref, b2_ref, w3_ref, b3_ref, w4_ref, fr_ref, dl_ref, o_ref):
    tl = z_ref.shape[0]
    z = z_ref[...]
    fr = fr_ref[...]
    dot = lambda a, w: jnp.dot(a, w, precision=HIGHEST, preferred_element_type=F32)
    h = jnp.sin(fr * (dot(z, w1_ref[...]) + b1_ref[...]))
    h = jnp.sin(fr * (dot(h, w2_ref[...]) + b2_ref[...]))
    h = jnp.sin(fr * (dot(h, w3_ref[...]) + b3_ref[...]))
    h = dot(h, w4_ref[...])
    t = z[:, 0:1]
    decay = jnp.exp(-t * jnp.abs(dl_ref[...]))
    hf = h[:, :HYENA_WIDTH] * decay
    hb = h[:, HYENA_WIDTH:] * decay
    row = lax.broadcasted_iota(jnp.int32, (tl, HYENA_WIDTH), 0) + pl.program_id(0) * tl
    hb = jnp.where(row == 0, 0.0, hb)
    o_ref[0] = hf + hb
    o_ref[1] = hf - hb


def _hyena_filter(z, w1, b1, w2, b2, w3, b3, w4, freq, deltas, tl):
    l = z.shape[0]
    full = lambda a: pl.BlockSpec(a.shape, lambda i: (0,) * a.ndim)
    return pl.pallas_call(
        _filter_kernel,
        out_shape=jax.ShapeDtypeStruct((2, l, HYENA_WIDTH), F32),
        grid=(l // tl,),
        in_specs=[pl.BlockSpec((tl, z.shape[1]), lambda i: (i, 0)),
                  full(w1), full(b1), full(w2), full(b2), full(w3), full(b3), full(w4), full(freq), full(deltas)],
        out_specs=pl.BlockSpec((2, tl, HYENA_WIDTH), lambda i: (0, i, 0)),
        compiler_params=_params(("arbitrary",)),
        name="hyena_filter",
    )(z, w1, b1, w2, b2, w3, b3, w4, freq, deltas)


def _dft_kernel(f_ref, x_ref, o_ref):
    o_ref[0] = jnp.dot(f_ref[...], x_ref[0].astype(BF16), preferred_element_type=F32)


def _dft(fmat, x, tf):
    nb, l, w = x.shape
    n = fmat.shape[0]
    return pl.pallas_call(
        _dft_kernel,
        out_shape=jax.ShapeDtypeStruct((nb, n, w), F32),
        grid=(n // tf, nb),
        in_specs=[pl.BlockSpec((tf, l), lambda i, j: (i, 0)),
                  pl.BlockSpec((1, l, w), lambda i, j: (j, 0, 0))],
        out_specs=pl.BlockSpec((1, tf, w), lambda i, j: (j, i, 0)),
        compiler_params=_params(("arbitrary", "arbitrary")),
        name="dft_filter",
    )(fmat, x)


def _dft_mul_kernel(fr_ref, fi_ref, x_ref, a_ref, b_ref, d_ref, zr_ref, zi_ref):
    x = x_ref[0]
    xr = jnp.dot(fr_ref[...], x, preferred_element_type=F32)
    xi = jnp.dot(fi_ref[...], x, preferred_element_type=F32)
    bb = b_ref[...]
    zr_ref[0] = (xr * a_ref[...] - xi * bb).astype(BF16)
    zi_ref[0] = (xr * bb + xi * d_ref[...]).astype(BF16)


def _dft_mul(fmat, g, sa, sb, sd, tf):
    b, l, w = g.shape
    nf = l // tf
    spec = pl.BlockSpec((tf, w), lambda i, j: (i, 0))
    return pl.pallas_call(
        _dft_mul_kernel,
        out_shape=(jax.ShapeDtypeStruct((b, l, w), BF16), jax.ShapeDtypeStruct((b, l, w), BF16)),
        grid=(nf, b),
        in_specs=[pl.BlockSpec((tf, l), lambda i, j: (i, 0)),
                  pl.BlockSpec((tf, l), lambda i, j: (i + nf, 0)),
                  pl.BlockSpec((1, l, w), lambda i, j: (j, 0, 0)),
                  spec, spec, spec],
        out_specs=(pl.BlockSpec((1, tf, w), lambda i, j: (j, i, 0)),
                   pl.BlockSpec((1, tf, w), lambda i, j: (j, i, 0))),
        compiler_params=_params(("arbitrary", "arbitrary")),
        name="dft_mul",
    )(fmat, fmat, g, sa, sb, sd)


def _idft_kernel(fr_ref, fi_ref, zr_ref, zi_ref, g_ref, x0_ref, hb_ref, wn_ref, o_ref):
    conv = (jnp.dot(fr_ref[...], zr_ref[0], preferred_element_type=F32)
            + jnp.dot(fi_ref[...], zi_ref[0], preferred_element_type=F32))
    y = (conv + g_ref[0].astype(F32) * hb_ref[...]) * x0_ref[0].astype(F32)
    o_ref[0] = _rms(y, wn_ref[...]).astype(BF16)


def _idft(finv_r, finv_i, zr, zi, g, x0, hbias, wn, tt):
    b, l, w = zr.shape
    tile = pl.BlockSpec((1, tt, w), lambda i, j: (j, i, 0))
    return pl.pallas_call(
        _idft_kernel,
        out_shape=jax.ShapeDtypeStruct((b, l, w), BF16),
        grid=(l // tt, b),
        in_specs=[pl.BlockSpec((tt, l), lambda i, j: (i, 0)),
                  pl.BlockSpec((tt, l), lambda i, j: (i, 0)),
                  pl.BlockSpec((1, l, w), lambda i, j: (j, 0, 0)),
                  pl.BlockSpec((1, l, w), lambda i, j: (j, 0, 0)),
                  tile, tile,
                  pl.BlockSpec((1, w), lambda i, j: (0, 0)),
                  pl.BlockSpec((1, w), lambda i, j: (0, 0))],
        out_specs=tile,
        compiler_params=_params(("arbitrary", "arbitrary")),
        name="idft",
    )(finv_r, finv_i, zr, zi, g, x0, hbias, wn)


def _bf16_bits(v):
    return lax.bitcast_convert_type(v.astype(BF16).astype(F32), jnp.uint32)


def _store_row_tiles(ref, val):
    rows, half = val.shape[0], val.shape[1] // 2
    assert half == ROW_SUB * LANES
    for j in range(ROW_SUB):
        lo = _bf16_bits(val[:, LANES * j:LANES * (j + 1)]) >> 16
        hi = _bf16_bits(val[:, half + LANES * j:half + LANES * (j + 1)]) & jnp.uint32(0xFFFF0000)
        ref[pl.ds(j, rows, stride=ROW_SUB), :] = lo | hi


def _load_row_tiles(ref, rows):
    words = [ref[pl.ds(j, rows, stride=ROW_SUB), :] for j in range(ROW_SUB)]
    lo = [lax.bitcast_convert_type(w << 16, F32) for w in words]
    hi = [lax.bitcast_convert_type(w & jnp.uint32(0xFFFF0000), F32) for w in words]
    return jnp.concatenate(lo + hi, axis=1)


def _merge_kernel(a_ref, y_ref, x_ref, mod_ref, wa_ref, wy_ref, n2_ref, sg_ref, su_ref, sd_ref,
                  base_ref, hi_ref, lo_ref, rt_ref):
    m = (jnp.dot(a_ref[0], wa_ref[...], preferred_element_type=F32)
         + jnp.dot(y_ref[0], wy_ref[...], preferred_element_type=F32))
    x1 = x_ref[0] + mod_ref[0, 2:3, :] * m
    h2 = _modulated(x1, n2_ref[...], mod_ref[0, 3:4, :], mod_ref[0, 4:5, :])
    hi = h2.astype(BF16)
    hi_ref[0] = hi
    lo_ref[0] = (h2 - hi.astype(F32)).astype(BF16)
    _store_row_tiles(rt_ref, h2)
    gate = jnp.dot(hi, sg_ref[...], preferred_element_type=F32)
    up = jnp.dot(hi, su_ref[...], preferred_element_type=F32)
    act = (gate * _sigmoid(gate) * up).astype(BF16)
    shared = jnp.dot(act, sd_ref[...], preferred_element_type=F32)
    base_ref[0] = x1 + mod_ref[0, 5:6, :] * shared


def _merge(an, yn, x, mods, wa, wy, n2w, sg, su, sd, tl):
    b, l, d = x.shape
    full = lambda a: pl.BlockSpec(a.shape, lambda i, j: (0,) * a.ndim)
    half = pl.BlockSpec((1, tl, an.shape[2]), lambda i, j: (i, j, 0))
    wide = pl.BlockSpec((1, tl, d), lambda i, j: (i, j, 0))
    per_b = l // tl
    return pl.pallas_call(
        _merge_kernel,
        out_shape=(jax.ShapeDtypeStruct((b, l, d), F32),
                   jax.ShapeDtypeStruct((b, l, d), BF16),
                   jax.ShapeDtypeStruct((b, l, d), BF16),
                   jax.ShapeDtypeStruct((b * l * ROW_SUB, LANES), jnp.uint32)),
        grid=(b, per_b),
        in_specs=[half, half, wide,
                  pl.BlockSpec((1, 6, d), lambda i, j: (i, 0, 0)),
                  full(wa), full(wy), full(n2w), full(sg), full(su), full(sd)],
        out_specs=(wide, wide, wide,
                   pl.BlockSpec((tl * ROW_SUB, LANES), lambda i, j: (i * per_b + j, 0))),
        compiler_params=_params(("arbitrary", "arbitrary")),
        name="merge",
    )(an, yn, x, mods, wa, wy, n2w, sg, su, sd)


def _router_kernel(hi_ref, lo_ref, whi_ref, wlo_ref, bias_ref, tri_ref,
                   idx_ref, wgt_ref, pos_ref, cnt_ref, run_ref):
    tt = hi_ref.shape[0]
    per_group = N_EXPERTS // N_GROUPS

    @pl.when(pl.program_id(0) == 0)
    def _():
        run_ref[...] = jnp.zeros_like(run_ref)

    nt = (((1,), (1,)), ((), ()))
    hi = hi_ref[...]
    whi = whi_ref[...]
    logits = (lax.dot_general(whi, hi, nt, preferred_element_type=F32)
              + lax.dot_general(whi, lo_ref[...], nt, preferred_element_type=F32)
              + lax.dot_general(wlo_ref[...], hi, nt, preferred_element_type=F32))
    scores = _sigmoid(logits)
    biased = scores + bias_ref[...]

    ridx = lax.broadcasted_iota(jnp.int32, (per_group, tt), 0)
    groups = [biased[g * per_group:(g + 1) * per_group, :] for g in range(N_GROUPS)]
    gs = []
    for blk in groups:
        m1 = jnp.max(blk, axis=0, keepdims=True)
        i1 = jnp.min(jnp.where(blk == m1, ridx, per_group), axis=0, keepdims=True)
        m2 = jnp.max(jnp.where(ridx == i1, NEG_INF, blk), axis=0, keepdims=True)
        gs.append(m1 + m2)

    kept = []
    for g in range(N_GROUPS):
        ahead = jnp.zeros((1, tt), F32)
        for o in range(N_GROUPS):
            if o != g:
                wins = (gs[o] >= gs[g]) if o < g else (gs[o] > gs[g])
                ahead = ahead + jnp.where(wins, 1.0, 0.0)
        kept.append(jnp.where(ahead < TOPK_GROUPS, groups[g], NEG_INF))
    cur = jnp.concatenate(kept, axis=0)

    eidx = lax.broadcasted_iota(jnp.int32, cur.shape, 0)
    kept_mask = cur
    picks = []
    wsel = []
    for _ in range(TOP_K):
        mx = jnp.max(cur, axis=0, keepdims=True)
        first = jnp.min(jnp.where(cur == mx, eidx, N_EXPERTS), axis=0, keepdims=True)
        sel = eidx == first
        picks.append(first)
        wsel.append(jnp.sum(jnp.where(sel, scores, 0.0), axis=0, keepdims=True))
        cur = jnp.where(sel, NEG_INF, cur)
    w = jnp.concatenate(wsel, axis=0)
    w = w / jnp.sum(w, axis=0, keepdims=True) * ROUTE_SCALE
    idx = jnp.concatenate(picks, axis=0)

    oh = jnp.where(cur == NEG_INF, jnp.where(kept_mask == NEG_INF, 0.0, 1.0), 0.0).astype(BF16)
    before = jnp.dot(oh, tri_ref[0], preferred_element_type=F32)
    total = jnp.dot(oh, tri_ref[1], preferred_element_type=F32)
    pos_ref[...] = run_ref[...] + before
    run_ref[...] = run_ref[...] + total

    idx_ref[...] = idx
    wgt_ref[...] = w
    cnt_ref[...] = run_ref[...]


def _router(hi, lo, whi, wlo, bias, tri, tt):
    t, d = hi.shape
    tok = pl.BlockSpec((tt, d), lambda i: (i, 0))
    full = lambda a: pl.BlockSpec(a.shape, lambda i: (0,) * a.ndim)
    out = pl.BlockSpec((TOP_K, tt), lambda i: (0, i))
    return pl.pallas_call(
        _router_kernel,
        out_shape=(jax.ShapeDtypeStruct((TOP_K, t), jnp.int32),
                   jax.ShapeDtypeStruct((TOP_K, t), F32),
                   jax.ShapeDtypeStruct((N_EXPERTS, t), F32),
                   jax.ShapeDtypeStruct((N_EXPERTS, tt), F32)),
        grid=(t // tt,),
        in_specs=[tok, tok, full(whi), full(wlo), full(bias), full(tri)],
        out_specs=(out, out, pl.BlockSpec((N_EXPERTS, tt), lambda i: (0, i)),
                   pl.BlockSpec((N_EXPERTS, tt), lambda i: (0, 0))),
        scratch_shapes=[pltpu.VMEM((N_EXPERTS, tt), F32)],
        compiler_params=_params(("arbitrary",)),
        name="router",
    )(hi, lo, whi, wlo, bias, tri)


def _row_tile(ref, r):
    return ref.at[pl.ds(pl.multiple_of(r * ROW_SUB, ROW_SUB), ROW_SUB), :]


def _dest_kernel(idx_ref, pos_ref, start_ref, dest_ref):
    eidx = lax.broadcasted_iota(jnp.int32, start_ref.shape, 0)
    row = start_ref[...] + pos_ref[...]
    rows = [jnp.sum(jnp.where(eidx == idx_ref[k:k + 1, :], row, 0.0), axis=0, keepdims=True)
            for k in range(TOP_K)]
    dest_ref[...] = jnp.concatenate(rows, axis=0).astype(jnp.int32)


def _dest(idx, pos, start, tt):
    t = idx.shape[1]
    blk = pl.BlockSpec((TOP_K, tt), lambda i: (0, i))
    return pl.pallas_call(
        _dest_kernel,
        out_shape=jax.ShapeDtypeStruct((TOP_K, t), jnp.int32),
        grid=(t // tt,),
        in_specs=[blk, pl.BlockSpec((N_EXPERTS, tt), lambda i: (0, i)), pl.BlockSpec(start.shape, lambda i: (0, 0))],
        out_specs=blk,
        compiler_params=_params(("arbitrary",)),
        name="dest",
    )(idx, pos, start)


def _zero_tails_kernel(last_ref, o_ref):
    del last_ref
    o_ref[...] = jnp.zeros(o_ref.shape, o_ref.dtype)


def _zero_tails(last_blk, n_rows):
    return pl.pallas_call(
        _zero_tails_kernel,
        out_shape=jax.ShapeDtypeStruct((n_rows * ROW_SUB, LANES), jnp.uint32),
        grid_spec=pltpu.PrefetchScalarGridSpec(
            num_scalar_prefetch=1,
            grid=(last_blk.shape[0],),
            in_specs=[],
            out_specs=pl.BlockSpec((EXPERT_ROWS * ROW_SUB, LANES), lambda e, last: (last[e], 0))),
        compiler_params=_params(("arbitrary",)),
        name="zero_tails",
    )(last_blk)


def _dispatch_kernel(dest_ref, h_ref, xs_in_ref, xs_ref, sem):
    del xs_in_ref
    td = dest_ref.shape[1]

    def row_copy(t, k):
        return pltpu.make_async_copy(_row_tile(h_ref, t), _row_tile(xs_ref, dest_ref[k, t]), sem)

    def issue(t, carry):
        for k in range(TOP_K):
            row_copy(t, k).start(priority=k % 2)
        return carry

    lax.fori_loop(0, td, issue, 0)

    def drain(t, carry):
        for k in range(TOP_K):
            row_copy(t, k).wait()
        return carry

    lax.fori_loop(0, td, drain, 0)


def _dispatch(dest, h_rt, xs0, td):
    t = dest.shape[1]
    return pl.pallas_call(
        _dispatch_kernel,
        out_shape=jax.ShapeDtypeStruct(xs0.shape, xs0.dtype),
        grid=(t // td,),
        in_specs=[pl.BlockSpec((TOP_K, td), lambda i: (0, i), memory_space=pltpu.SMEM),
                  pl.BlockSpec((td * ROW_SUB, LANES), lambda i: (i, 0)),
                  pl.BlockSpec(memory_space=pl.ANY)],
        out_specs=pl.BlockSpec(memory_space=pl.ANY),
        scratch_shapes=[pltpu.SemaphoreType.DMA],
        input_output_aliases={2: 0},
        compiler_params=_params(("arbitrary",)),
        name="dispatch",
    )(dest, h_rt, xs0)


def _experts_kernel(be_ref, nu_ref, x_ref, wg_ref, wu_ref, wd_ref, y_ref, wgu_bf, wd_bf):
    i = pl.program_id(0)
    used = i < nu_ref[0]
    new_expert = jnp.logical_or(i == 0, be_ref[i] != be_ref[jnp.maximum(i - 1, 0)])

    @pl.when(jnp.logical_and(used, new_expert))
    def _():
        ff = wg_ref.shape[2]
        wgu_bf[:, :ff] = wg_ref[0].astype(BF16)
        wgu_bf[:, ff:] = wu_ref[0].astype(BF16)
        wd_bf[...] = wd_ref[0].astype(BF16)

    @pl.when(used)
    def _():
        ff = wg_ref.shape[2]
        for r in range(0, EXPERT_ROWS, EXPERT_SUB):
            rows = pl.ds(r * ROW_SUB, EXPERT_SUB * ROW_SUB)
            x = _load_row_tiles(x_ref.at[rows, :], EXPERT_SUB).astype(BF16)
            gu = jnp.dot(x, wgu_bf[...], preferred_element_type=F32)
            gate = gu[:, :ff]
            act = (gate * _sigmoid(gate) * gu[:, ff:]).astype(BF16)
            _store_row_tiles(y_ref.at[rows, :], jnp.dot(act, wd_bf[...], preferred_element_type=F32))


def _experts(blk_e, n_used, xs, wg, wu, wd):
    d, ff = wg.shape[1], wg.shape[2]
    nblk = xs.shape[0] // (EXPERT_ROWS * ROW_SUB)
    row = lambda i, be, nu: (jnp.minimum(i, nu[0] - 1), 0)
    wsel = lambda i, be, nu: (be[jnp.minimum(i, nu[0] - 1)], 0, 0)
    return pl.pallas_call(
        _experts_kernel,
        out_shape=jax.ShapeDtypeStruct(xs.shape, xs.dtype),
        grid_spec=pltpu.PrefetchScalarGridSpec(
            num_scalar_prefetch=2,
            grid=(nblk,),
            in_specs=[pl.BlockSpec((EXPERT_ROWS * ROW_SUB, LANES), row),
                      pl.BlockSpec((1, d, ff), wsel),
                      pl.BlockSpec((1, d, ff), wsel),
                      pl.BlockSpec((1, ff, d), wsel)],
            out_specs=pl.BlockSpec((EXPERT_ROWS * ROW_SUB, LANES), row),
            scratch_shapes=[pltpu.VMEM((d, 2 * ff), BF16), pltpu.VMEM((ff, d), BF16)]),
        compiler_params=_params(("arbitrary",)),
        name="experts",
    )(blk_e, n_used, xs, wg, wu, wd)


def _combine_kernel(dest_ref, next_ref, ys_ref, w_ref, base_ref, mod_ref, fw_ref, o_ref, buf, sem):
    tc = dest_ref.shape[1]
    i = pl.program_id(0)
    slot = i % 2

    def row_copy(row, s, t, k):
        return pltpu.make_async_copy(_row_tile(ys_ref, row), _row_tile(buf.at[s, k], t), sem.at[s])

    def issue(d_ref, s):
        for t in range(tc):
            for k in range(TOP_K):
                row_copy(d_ref[k, t], s, t, k).start(priority=k % 2)

    def drain(s):
        for t in range(tc):
            for k in range(TOP_K):
                row_copy(0, s, t, k).wait()

    @pl.when(i == 0)
    def _():
        issue(dest_ref, 0)

    drain(slot)
    issue(next_ref, 1 - slot)
    w = w_ref[...]
    routed = w[:, 0:1] * _load_row_tiles(buf.at[slot, 0], tc)
    for k in range(1, TOP_K):
        routed = routed + w[:, k:k + 1] * _load_row_tiles(buf.at[slot, k], tc)
    x = base_ref[0] + mod_ref[0, 5:6, :] * routed
    o_ref[0] = _rms(x, fw_ref[...])

    @pl.when(i == pl.num_programs(0) - 1)
    def _():
        drain(1 - slot)


def _combine(dest, ys, wt, base, mods, fw, tc):
    b, l, d = base.shape
    per_b = l // tc
    steps = b * per_b
    return pl.pallas_call(
        _combine_kernel,
        out_shape=jax.ShapeDtypeStruct((b, l, d), F32),
        grid=(steps,),
        in_specs=[pl.BlockSpec((TOP_K, tc), lambda i: (0, i), memory_space=pltpu.SMEM),
                  pl.BlockSpec((TOP_K, tc), lambda i: (0, jnp.minimum(i + 1, steps - 1)), memory_space=pltpu.SMEM),
                  pl.BlockSpec(memory_space=pl.ANY),
                  pl.BlockSpec((tc, TOP_K), lambda i: (i, 0)),
                  pl.BlockSpec((1, tc, d), lambda i: (i // per_b, i % per_b, 0)),
                  pl.BlockSpec((1, 6, d), lambda i: (i // per_b, 0, 0)),
                  pl.BlockSpec((1, d), lambda i: (0, 0))],
        out_specs=pl.BlockSpec((1, tc, d), lambda i: (i // per_b, i % per_b, 0)),
        scratch_shapes=[pltpu.VMEM((2, TOP_K, tc * ROW_SUB, LANES), jnp.uint32), pltpu.SemaphoreType.DMA((2,))],
        compiler_params=_params(("arbitrary",)),
        name="combine",
    )(dest, dest, ys, wt, base, mods, fw)


def _rope_tables(l):
    t = jnp.arange(l, dtype=jnp.int32)
    row = (t // GRID_W).astype(F32)
    col = (t % GRID_W).astype(F32)
    n_freq = HEAD_DIM // 4
    inv = ROPE_THETA ** (-jnp.arange(n_freq, dtype=F32) / n_freq)
    ang = jnp.concatenate([row[:, None] * inv, col[:, None] * inv], axis=-1)
    cos = jnp.repeat(jnp.cos(ang), 2, axis=1)
    sin = jnp.repeat(jnp.sin(ang), 2, axis=1)
    sign = jnp.tile(jnp.array([-1.0, 1.0], F32), HEAD_DIM // 2)
    reps = LANES // HEAD_DIM
    return jnp.tile(cos, (1, reps)), jnp.tile(sin * sign, (1, reps))


def _filter_features(l):
    t = jnp.linspace(0.0, 1.0, l, dtype=F32)[:, None]
    bands = (FILTER_EMB - 1) // 2
    w = 2.0 * math.pi * jnp.arange(l, dtype=F32)[:, None] / l
    f = jnp.linspace(1e-4, bands - 1, bands, dtype=F32)[None, :]
    z = jnp.concatenate([t, jnp.cos(f * w), -jnp.sin(f * w)], axis=-1)
    min_decay = math.log(FILTER_TARGET) / FILTER_DECAY_FAST
    max_decay = math.log(FILTER_TARGET) / FILTER_DECAY_SLOW
    deltas = jnp.linspace(min_decay, max_decay, HYENA_WIDTH, dtype=F32)[None, :]
    return jnp.pad(z, ((0, 0), (0, LANES - FILTER_EMB))), deltas


def _dft_matrices(l):
    n = 2 * l
    idx = jnp.arange(l, dtype=jnp.int32)
    r = math.isqrt(l)
    assert r * r == l
    sub = jnp.arange(r, dtype=jnp.int32)
    hi = ((r * sub[:, None] * idx[None, :]) % n).astype(F32) * (2.0 * math.pi / n)
    lo = ((sub[:, None] * idx[None, :]) % n).astype(F32) * (2.0 * math.pi / n)
    ch, sh, cl, sl = jnp.cos(hi)[:, None, :], jnp.sin(hi)[:, None, :], jnp.cos(lo)[None], jnp.sin(lo)[None]
    c = (ch * cl - sh * sl).reshape(l, l)
    s = (sh * cl + ch * sl).reshape(l, l)
    alt = jnp.where(idx % 2 == 0, 1.0, -1.0).astype(F32)
    first = (idx == 0)[:, None]
    fwd = jnp.concatenate([c, jnp.where(first, alt[None, :], -s)], axis=0).astype(BF16)
    firstc = (idx == 0)[None, :]
    inv_r = (jnp.where(firstc, 1.0, 2.0) * c / n).astype(BF16)
    inv_i = (jnp.where(firstc, alt[:, None], -2.0 * s) / n).astype(BF16)
    return fwd, inv_r, inv_i


def _head_perm():
    order = []
    for j in range(N_HEADS // 2):
        order += list(range(j * HEAD_DIM, (j + 1) * HEAD_DIM))
        order += list(range((j + N_HEADS // 2) * HEAD_DIM, (j + 1 + N_HEADS // 2) * HEAD_DIM))
    return jnp.array(order, jnp.int32)


def _pad2(a, rows, cols):
    return jnp.pad(a, ((0, rows - a.shape[0]), (0, cols - a.shape[1])))


def kernel(x, c, ctx, c_ctx, mod_w, mod_b, norm1_w, w_in, q_norm_w, k_norm_w, conv_w, conv_b, filt_w1, filt_b1, filt_w2, filt_b2, filt_w3, filt_b3, filt_w4, filt_freq, hyena_bias, attn_out_norm_w, hyena_out_norm_w, w_out, norm2_w, router_w, router_bias, exp_w_gate, exp_w_up, exp_w_down, sh_w_gate, sh_w_up, sh_w_down, final_norm_w):
    b, l, d = x.shape
    t = b * l
    assert mod_w.shape[0] == 1, "single-layer stack"
    tl = min(ROW_TILE, l)

    cond = jnp.concatenate([c, c_ctx[None, :], jnp.zeros((-(b + 1) % 8, d), F32)], axis=0)
    mod = _adaln(cond, mod_w[0], mod_b[0][None, :])
    mods = mod[:b].reshape(b, 6, d)
    cmod = mod[b].reshape(6, d)

    perm = _head_perm()
    w_in0 = w_in[0]
    w_in_k = jnp.concatenate([w_in0[:, :Q_END][:, perm], w_in0[:, Q_END:]], axis=1).astype(BF16)
    w_kv = w_in0[:, Q_END:V_END].astype(BF16)
    gq = jnp.kron(jnp.eye(N_HEADS, dtype=F32), jnp.full((HEAD_DIM, HEAD_DIM), 1.0 / HEAD_DIM, F32)).astype(BF16)
    qnw = jnp.tile(q_norm_w[0], N_HEADS)[None, :]
    knw = jnp.tile(k_norm_w[0], N_KV_HEADS)[None, :]
    n1w = norm1_w[0][None, :]
    cos, sin = _rope_tables(l)

    kc, vc = _ctx_kv(ctx, cmod, n1w, w_kv, gq[:KV_WIDTH, :KV_WIDTH], knw)
    q, k, v, u = _inproj(x, mods, n1w, w_in_k, gq, qnw, knw, cos, sin, tl)
    k_all = jnp.concatenate([kc, k], axis=1)
    v_all = jnp.concatenate([vc, v], axis=1)
    an = _attention(q, k_all, v_all, attn_out_norm_w[0][perm][None, :], min(ATTN_Q_TILE, l),
                    min(ATTN_SUB, l))

    cw = conv_w[0].reshape(3, 3, HYENA_WIDTH).transpose(1, 0, 2)
    cb = conv_b[0].reshape(3, HYENA_WIDTH)
    gbf, x0 = _hyena_pre(u, cw, cb)
    z, deltas = _filter_features(l)
    hsd = _hyena_filter(
        z, _pad2(filt_w1[0], LANES, LANES), _pad2(filt_b1[0][None, :], 1, LANES),
        _pad2(filt_w2[0], LANES, LANES), _pad2(filt_b2[0][None, :], 1, LANES),
        _pad2(filt_w3[0], LANES, LANES), _pad2(filt_b3[0][None, :], 1, LANES),
        _pad2(filt_w4[0], LANES, 2 * HYENA_WIDTH), _pad2(filt_freq[0][None, :], 1, LANES), deltas, tl)
    fwd, inv_r, inv_i = _dft_matrices(l)
    spec = _dft(fwd, hsd, tl)
    row0 = (jnp.arange(l) == 0)[:, None]
    sa = spec[0, :l]
    sd = jnp.where(row0, spec[0, l:l + 1], sa)
    sb = jnp.where(row0, 0.0, spec[1, l:])
    zr, zi = _dft_mul(fwd, gbf, sa, sb, sd, tl)
    yn = _idft(inv_r, inv_i, zr, zi, gbf, x0, hyena_bias[0][None, :], hyena_out_norm_w[0][None, :], tl)

    w_out0 = w_out[0]
    base, h2hi, h2lo, h2rt = _merge(
        an, yn, x, mods, w_out0[:ATTN_WIDTH][perm].astype(BF16), w_out0[ATTN_WIDTH:].astype(BF16),
        norm2_w[0][None, :], sh_w_gate[0].astype(BF16), sh_w_up[0].astype(BF16), sh_w_down[0].astype(BF16), tl)

    tt = ROUTER_TILE
    rwt = router_w[0].T
    rw_hi = rwt.astype(BF16)
    rw_lo = (rwt - rw_hi.astype(F32)).astype(BF16)
    bias = jnp.broadcast_to(router_bias[0][:, None], (N_EXPERTS, tt))
    ti = jnp.arange(tt)
    tri = jnp.stack([(ti[:, None] < ti[None, :]), jnp.ones((tt, tt), bool)]).astype(BF16)
    idx, wgt, pos, cnt = _router(h2hi.reshape(t, d), h2lo.reshape(t, d), rw_hi, rw_lo, bias, tri, tt)

    counts = cnt[:, 0].astype(jnp.int32)
    padded = (counts + EXPERT_ROWS - 1) // EXPERT_ROWS * EXPERT_ROWS
    pad_end = jnp.cumsum(padded)
    pad_start = pad_end - padded
    td = min(DEST_TILE, t)
    dest = _dest(idx, pos, jnp.broadcast_to(pad_start.astype(F32)[:, None], (N_EXPERTS, td)), td)
    n_rows = (t * TOP_K + N_EXPERTS * (EXPERT_ROWS - 1) + EXPERT_ROWS - 1) // EXPERT_ROWS * EXPERT_ROWS
    nblk = n_rows // EXPERT_ROWS
    blk_row = jnp.arange(nblk, dtype=jnp.int32) * EXPERT_ROWS
    blk_e = jnp.minimum(jnp.sum((pad_end[None, :] <= blk_row[:, None]).astype(jnp.int32), axis=1), N_EXPERTS - 1)
    n_used = (pad_end[-1:] // EXPERT_ROWS).astype(jnp.int32)

    last_blk = jnp.maximum(pad_end // EXPERT_ROWS - 1, 0).astype(jnp.int32)
    xs = _dispatch(dest, h2rt, _zero_tails(last_blk, n_rows), min(DISPATCH_TILE, t))
    ys = _experts(blk_e, n_used, xs, exp_w_gate[0], exp_w_up[0], exp_w_down[0])
    return _combine(dest, ys, wgt.T, base, mods, final_norm_w[None, :], min(COMBINE_TILE, l))
```

```python
import functools
import math

import jax
import jax.numpy as jnp
from jax import lax
from jax.experimental import pallas as pl
from jax.experimental.pallas import tpu as pltpu

F32 = jnp.float32
BF16 = jnp.bfloat16
HIGHEST = lax.Precision.HIGHEST

GRID_W = 64
N_HEADS = 8
N_KV_HEADS = 2
HEAD_DIM = 64
ATTN_WIDTH = N_HEADS * HEAD_DIM
KV_WIDTH = N_KV_HEADS * HEAD_DIM
HYENA_WIDTH = 512
Q_END = ATTN_WIDTH
K_END = Q_END + KV_WIDTH
V_END = K_END + KV_WIDTH
ROPE_THETA = 10000.0
FILTER_EMB = 33
FILTER_DECAY_FAST = 0.3
FILTER_DECAY_SLOW = 1.5
FILTER_TARGET = 1e-2
N_EXPERTS = 256
TOP_K = 8
N_GROUPS = 8
TOPK_GROUPS = 4
ROUTE_SCALE = 2.5
EPS = 1e-6

LANES = 128
ROW_SUB = 4
EXPERT_ROWS = 512
EXPERT_SUB = 256
NEG_INF = float("-inf")

ROW_TILE = 512
ATTN_Q_TILE = 512
ATTN_SUB = 128
ROUTER_TILE = 256
DEST_TILE = 1024
DISPATCH_TILE = 1024
COMBINE_TILE = 256
ADALN_COLS = 1536
VMEM_LIMIT_MIB = 48


def _params(semantics):
    return pltpu.CompilerParams(dimension_semantics=semantics, vmem_limit_bytes=VMEM_LIMIT_MIB * 1024 * 1024)


def _rms(x, w):
    return x * lax.rsqrt(jnp.mean(x * x, axis=-1, keepdims=True) + EPS) * w


def _sigmoid(x):
    return 1.0 / (1.0 + jnp.exp(-x))


def _adaln_kernel(c_ref, w_ref, b_ref, o_ref):
    c = c_ref[...]
    s = c * _sigmoid(c)
    o_ref[...] = jnp.dot(s, w_ref[...], precision=HIGHEST, preferred_element_type=F32) + b_ref[...]


def _adaln(cond, w, b):
    rows, d = cond.shape
    n = w.shape[1]
    tn = ADALN_COLS
    return pl.pallas_call(
        _adaln_kernel,
        out_shape=jax.ShapeDtypeStruct((rows, n), F32),
        grid=(n // tn,),
        in_specs=[pl.BlockSpec((rows, d), lambda j: (0, 0)),
                  pl.BlockSpec((d, tn), lambda j: (0, j)),
                  pl.BlockSpec((1, tn), lambda j: (0, j))],
        out_specs=pl.BlockSpec((rows, tn), lambda j: (0, j)),
        compiler_params=_params(("arbitrary",)),
        name="adaln",
    )(cond, w, b)


def _head_rms(t, gmat, w):
    ms = jnp.dot((t * t).astype(BF16), gmat, preferred_element_type=F32)
    return t * lax.rsqrt(ms + EPS) * w


def _modulated(x, norm_w, shift, scale):
    return _rms(x, norm_w) * (1.0 + scale) + shift


def _ctx_kv_kernel(ctx_ref, mod_ref, n1_ref, w_ref, g_ref, kn_ref, kc_ref, vc_ref):
    x = ctx_ref[0]
    h = _modulated(x, n1_ref[...], mod_ref[0:1, :], mod_ref[1:2, :])
    kv = jnp.dot(h.astype(BF16), w_ref[...], preferred_element_type=F32)
    k = _head_rms(kv[:, :KV_WIDTH], g_ref[...], kn_ref[...])
    v = kv[:, KV_WIDTH:]
    kc_ref[0] = k.astype(BF16)
    vc_ref[0] = jnp.concatenate([v, jnp.ones_like(v)], axis=1).astype(BF16)


def _ctx_kv(ctx, cmod, n1w, w_kv, gk, knw):
    b, c, d = ctx.shape
    return pl.pallas_call(
        _ctx_kv_kernel,
        out_shape=(jax.ShapeDtypeStruct((b, c, KV_WIDTH), BF16),
                   jax.ShapeDtypeStruct((b, c, 2 * KV_WIDTH), BF16)),
        grid=(b,),
        in_specs=[pl.BlockSpec((1, c, d), lambda i: (i, 0, 0)),
                  pl.BlockSpec(cmod.shape, lambda i: (0, 0)),
                  pl.BlockSpec((1, d), lambda i: (0, 0)),
                  pl.BlockSpec(w_kv.shape, lambda i: (0, 0)),
                  pl.BlockSpec(gk.shape, lambda i: (0, 0)),
                  pl.BlockSpec((1, KV_WIDTH), lambda i: (0, 0))],
        out_specs=(pl.BlockSpec((1, c, KV_WIDTH), lambda i: (i, 0, 0)),
                   pl.BlockSpec((1, c, 2 * KV_WIDTH), lambda i: (i, 0, 0))),
        compiler_params=_params(("arbitrary",)),
        name="ctx_kv",
    )(ctx, cmod, n1w, w_kv, gk, knw)


def _rope(t, cos, sin, even):
    width = t.shape[1]
    partner = jnp.where(even, pltpu.roll(t, width - 1, axis=1), pltpu.roll(t, 1, axis=1))
    return t * cos + partner * sin


def _inproj_kernel(x_ref, mod_ref, n1_ref, w_ref, gq_ref, qn_ref, kn_ref, cos_ref, sin_ref,
                   q_ref, k_ref, v_ref, u_ref):
    x = x_ref[0]
    h = _modulated(x, n1_ref[...], mod_ref[0, 0:1, :], mod_ref[0, 1:2, :])
    p = jnp.dot(h.astype(BF16), w_ref[...], preferred_element_type=F32)
    gq = gq_ref[...]
    q = _head_rms(p[:, :Q_END], gq, qn_ref[...])
    k = _head_rms(p[:, Q_END:K_END], gq[:KV_WIDTH, :KV_WIDTH], kn_ref[...])
    v = p[:, K_END:V_END]
    cos = cos_ref[...]
    sin = sin_ref[...]
    reps = Q_END // LANES
    cos_q = jnp.concatenate([cos] * reps, axis=1)
    sin_q = jnp.concatenate([sin] * reps, axis=1)
    even_q = (lax.broadcasted_iota(jnp.int32, (1, Q_END), 1) & 1) == 0
    even_k = (lax.broadcasted_iota(jnp.int32, (1, KV_WIDTH), 1) & 1) == 0
    q = _rope(q, cos_q, sin_q, even_q) * (HEAD_DIM ** -0.5)
    k = _rope(k, cos, sin, even_k)
    q_ref[0] = q.astype(BF16)
    k_ref[0] = k.astype(BF16)
    v_ref[0] = jnp.concatenate([v, jnp.ones_like(v)], axis=1).astype(BF16)
    u_ref[0] = p[:, V_END:].astype(BF16)


def _inproj(x, mods, n1w, w_in, gq, qnw, knw, cos, sin, tl):
    b, l, d = x.shape
    ncol = w_in.shape[1]
    nu = ncol - V_END
    return pl.pallas_call(
        _inproj_kernel,
        out_shape=(jax.ShapeDtypeStruct((b, l, Q_END), BF16),
                   jax.ShapeDtypeStruct((b, l, KV_WIDTH), BF16),
                   jax.ShapeDtypeStruct((b, l, 2 * KV_WIDTH), BF16),
                   jax.ShapeDtypeStruct((b, l, nu), BF16)),
        grid=(l // tl, b),
        in_specs=[pl.BlockSpec((1, tl, d), lambda i, j: (j, i, 0)),
                  pl.BlockSpec((1, 6, d), lambda i, j: (j, 0, 0)),
                  pl.BlockSpec((1, d), lambda i, j: (0, 0)),
                  pl.BlockSpec((d, ncol), lambda i, j: (0, 0)),
                  pl.BlockSpec(gq.shape, lambda i, j: (0, 0)),
                  pl.BlockSpec((1, Q_END), lambda i, j: (0, 0)),
                  pl.BlockSpec((1, KV_WIDTH), lambda i, j: (0, 0)),
                  pl.BlockSpec((tl, LANES), lambda i, j: (i, 0)),
                  pl.BlockSpec((tl, LANES), lambda i, j: (i, 0))],
        out_specs=(pl.BlockSpec((1, tl, Q_END), lambda i, j: (j, i, 0)),
                   pl.BlockSpec((1, tl, KV_WIDTH), lambda i, j: (j, i, 0)),
                   pl.BlockSpec((1, tl, 2 * KV_WIDTH), lambda i, j: (j, i, 0)),
                   pl.BlockSpec((1, tl, nu), lambda i, j: (j, i, 0))),
        compiler_params=_params(("arbitrary", "arbitrary")),
        name="inproj",
    )(x, mods, n1w, w_in, gq, qnw, knw, cos, sin)


def _attn_kernel(q_ref, k_ref, v_ref, wn_ref, o_ref, *, sub):
    tq = q_ref.shape[1]
    low = lax.broadcasted_iota(jnp.int32, (1, LANES), 1) < HEAD_DIM
    nt = (((1,), (1,)), ((), ()))
    kk = k_ref[0]
    vv = v_ref[0]
    for r in range(0, tq, sub):
        outs = []
        for j in range(Q_END // LANES):
            qv = q_ref[0, r:r + sub, LANES * j:LANES * (j + 1)]
            zero = jnp.zeros_like(qv)
            halves = []
            for g in range(N_KV_HEADS):
                qh = jnp.where(low, qv, zero) if g == 0 else jnp.where(low, zero, qv)
                s = lax.dot_general(qh, kk, nt, preferred_element_type=F32)
                p = jnp.exp(s - jnp.max(s, axis=-1, keepdims=True)).astype(BF16)
                pv = jnp.dot(p, vv, preferred_element_type=F32)
                halves.append(pv[:, :LANES] / pv[:, LANES:])
            outs.append(jnp.where(low, halves[0], halves[1]))
        a = jnp.concatenate(outs, axis=1)
        o_ref[0, r:r + sub, :] = _rms(a, wn_ref[...]).astype(BF16)


def _attention(q, k, v, wn, tq, sub):
    b, l, _ = q.shape
    n = k.shape[1]
    return pl.pallas_call(
        functools.partial(_attn_kernel, sub=sub),
        out_shape=jax.ShapeDtypeStruct((b, l, Q_END), BF16),
        grid=(b, l // tq),
        in_specs=[pl.BlockSpec((1, tq, Q_END), lambda i, j: (i, j, 0)),
                  pl.BlockSpec((1, n, KV_WIDTH), lambda i, j: (i, 0, 0)),
                  pl.BlockSpec((1, n, 2 * KV_WIDTH), lambda i, j: (i, 0, 0)),
                  pl.BlockSpec((1, Q_END), lambda i, j: (0, 0))],
        out_specs=pl.BlockSpec((1, tq, Q_END), lambda i, j: (i, j, 0)),
        compiler_params=_params(("arbitrary", "arbitrary")),
        name="attn",
    )(q, k, v, wn)


def _hyena_pre_kernel(u0_ref, u1_ref, u2_ref, cw_ref, cb_ref, g_ref, x0_ref):
    l = u0_ref.shape[1]
    row = lax.broadcasted_iota(jnp.int32, (l, LANES), 0)

    def conv(u_ref, gi):
        u = u_ref[0].astype(F32)
        prev = jnp.where(row == 0, 0.0, pltpu.roll(u, 1, axis=0))
        nxt = jnp.where(row == l - 1, 0.0, pltpu.roll(u, l - 1, axis=0))
        w = cw_ref[gi]
        return w[0:1] * prev + w[1:2] * u + w[2:3] * nxt + cb_ref[gi:gi + 1, :]

    x0 = conv(u0_ref, 0)
    x1 = conv(u1_ref, 1)
    v = conv(u2_ref, 2)
    g = v * x1
    g_ref[0] = g.astype(BF16)
    x0_ref[0] = x0.astype(BF16)


def _hyena_pre(u, cw, cb):
    b, l, _ = u.shape
    nblk = HYENA_WIDTH // LANES
    ublk = lambda gi: pl.BlockSpec((1, l, LANES), lambda i, j: (i, 0, gi * nblk + j))
    oblk = pl.BlockSpec((1, l, LANES), lambda i, j: (i, 0, j))
    return pl.pallas_call(
        _hyena_pre_kernel,
        out_shape=(jax.ShapeDtypeStruct((b, l, HYENA_WIDTH), BF16),
                   jax.ShapeDtypeStruct((b, l, HYENA_WIDTH), BF16)),
        grid=(b, nblk),
        in_specs=[ublk(0), ublk(1), ublk(2),
                  pl.BlockSpec((3, 3, LANES), lambda i, j: (0, 0, j)),
                  pl.BlockSpec((3, LANES), lambda i, j: (0, j))],
        out_specs=(oblk, oblk),
        compiler_params=_params(("arbitrary", "arbitrary")),
        name="hyena_pre",
    )(u, u, u, cw, cb)


def _filter_kernel(z_ref, w1_ref, b1_ref, w2_ref, b2_ref, w3_ref, b3_ref, w4_ref, fr_ref, dl_ref, o_ref):
    tl = z_ref.shape[0]
    z = z_ref[...]
    fr = fr_ref[...]
    dot = lambda a, w: jnp.dot(a, w, precision=HIGHEST, preferred_element_type=F32)
    h = jnp.sin(fr * (dot(z, w1_ref[...]) + b1_ref[...]))
    h = jnp.sin(fr * (dot(h, w2_ref[...]) + b2_ref[...]))
    h = jnp.sin(fr * (dot(h, w3_ref[...]) + b3_ref[...]))
    h = dot(h, w4_ref[...])
    t = z[:, 0:1]
    decay = jnp.exp(-t * jnp.abs(dl_ref[...]))
    hf = h[:, :HYENA_WIDTH] * decay
    hb = h[:, HYENA_WIDTH:] * decay
    row = lax.broadcasted_iota(jnp.int32, (tl, HYENA_WIDTH), 0) + pl.program_id(0) * tl
    hb = jnp.where(row == 0, 0.0, hb)
    o_ref[0] = hf + hb
    o_ref[1] = hf - hb


def _hyena_filter(z, w1, b1, w2, b2, w3, b3, w4, freq, deltas, tl):
    l = z.shape[0]
    full = lambda a: pl.BlockSpec(a.shape, lambda i: (0,) * a.ndim)
    return pl.pallas_call(
        _filter_kernel,
        out_shape=jax.ShapeDtypeStruct((2, l, HYENA_WIDTH), F32),
        grid=(l // tl,),
        in_specs=[pl.BlockSpec((tl, z.shape[1]), lambda i: (i, 0)),
                  full(w1), full(b1), full(w2), full(b2), full(w3), full(b3), full(w4), full(freq), full(deltas)],
        out_specs=pl.BlockSpec((2, tl, HYENA_WIDTH), lambda i: (0, i, 0)),
        compiler_params=_params(("arbitrary",)),
        name="hyena_filter",
    )(z, w1, b1, w2, b2, w3, b3, w4, freq, deltas)


def _dft_kernel(f_ref, x_ref, o_ref):
    o_ref[0] = jnp.dot(f_ref[...], x_ref[0].astype(BF16), preferred_element_type=F32)


def _dft(fmat, x, tf):
    nb, l, w = x.shape
    n = fmat.shape[0]
    return pl.pallas_call(
        _dft_kernel,
        out_shape=jax.ShapeDtypeStruct((nb, n, w), F32),
        grid=(n // tf, nb),
        in_specs=[pl.BlockSpec((tf, l), lambda i, j: (i, 0)),
                  pl.BlockSpec((1, l, w), lambda i, j: (j, 0, 0))],
        out_specs=pl.BlockSpec((1, tf, w), lambda i, j: (j, i, 0)),
        compiler_params=_params(("arbitrary", "arbitrary")),
        name="dft_filter",
    )(fmat, x)


def _dft_mul_kernel(fr_ref, fi_ref, x_ref, a_ref, b_ref, d_ref, zr_ref, zi_ref):
    x = x_ref[0]
    xr = jnp.dot(fr_ref[...], x, preferred_element_type=F32)
    xi = jnp.dot(fi_ref[...], x, preferred_element_type=F32)
    bb = b_ref[...]
    zr_ref[0] = (xr * a_ref[...] - xi * bb).astype(BF16)
    zi_ref[0] = (xr * bb + xi * d_ref[...]).astype(BF16)


def _dft_mul(fmat, g, sa, sb, sd, tf):
    b, l, w = g.shape
    nf = l // tf
    spec = pl.BlockSpec((tf, w), lambda i, j: (i, 0))
    return pl.pallas_call(
        _dft_mul_kernel,
        out_shape=(jax.ShapeDtypeStruct((b, l, w), BF16), jax.ShapeDtypeStruct((b, l, w), BF16)),
        grid=(nf, b),
        in_specs=[pl.BlockSpec((tf, l), lambda i, j: (i, 0)),
                  pl.BlockSpec((tf, l), lambda i, j: (i + nf, 0)),
                  pl.BlockSpec((1, l, w), lambda i, j: (j, 0, 0)),
                  spec, spec, spec],
        out_specs=(pl.BlockSpec((1, tf, w), lambda i, j: (j, i, 0)),
                   pl.BlockSpec((1, tf, w), lambda i, j: (j, i, 0))),
        compiler_params=_params(("arbitrary", "arbitrary")),
        name="dft_mul",
    )(fmat, fmat, g, sa, sb, sd)


def _idft_kernel(fr_ref, fi_ref, zr_ref, zi_ref, g_ref, x0_ref, hb_ref, wn_ref, o_ref):
    conv = (jnp.dot(fr_ref[...], zr_ref[0], preferred_element_type=F32)
            + jnp.dot(fi_ref[...], zi_ref[0], preferred_element_type=F32))
    y = (conv + g_ref[0].astype(F32) * hb_ref[...]) * x0_ref[0].astype(F32)
    o_ref[0] = _rms(y, wn_ref[...]).astype(BF16)


def _idft(finv_r, finv_i, zr, zi, g, x0, hbias, wn, tt):
    b, l, w = zr.shape
    tile = pl.BlockSpec((1, tt, w), lambda i, j: (j, i, 0))
    return pl.pallas_call(
        _idft_kernel,
        out_shape=jax.ShapeDtypeStruct((b, l, w), BF16),
        grid=(l // tt, b),
        in_specs=[pl.BlockSpec((tt, l), lambda i, j: (i, 0)),
                  pl.BlockSpec((tt, l), lambda i, j: (i, 0)),
                  pl.BlockSpec((1, l, w), lambda i, j: (j, 0, 0)),
                  pl.BlockSpec((1, l, w), lambda i, j: (j, 0, 0)),
                  tile, tile,
                  pl.BlockSpec((1, w), lambda i, j: (0, 0)),
                  pl.BlockSpec((1, w), lambda i, j: (0, 0))],
        out_specs=tile,
        compiler_params=_params(("arbitrary", "arbitrary")),
        name="idft",
    )(finv_r, finv_i, zr, zi, g, x0, hbias, wn)


def _bf16_bits(v):
    return lax.bitcast_convert_type(v.astype(BF16).astype(F32), jnp.uint32)


def _store_row_tiles(ref, val):
    rows, half = val.shape[0], val.shape[1] // 2
    assert half == ROW_SUB * LANES
    for j in range(ROW_SUB):
        lo = _bf16_bits(val[:, LANES * j:LANES * (j + 1)]) >> 16
        hi = _bf16_bits(val[:, half + LANES * j:half + LANES * (j + 1)]) & jnp.uint32(0xFFFF0000)
        ref[pl.ds(j, rows, stride=ROW_SUB), :] = lo | hi


def _load_row_tiles(ref, rows):
    words = [ref[pl.ds(j, rows, stride=ROW_SUB), :] for j in range(ROW_SUB)]
    lo = [lax.bitcast_convert_type(w << 16, F32) for w in words]
    hi = [lax.bitcast_convert_type(w & jnp.uint32(0xFFFF0000), F32) for w in words]
    return jnp.concatenate(lo + hi, axis=1)


def _merge_kernel(a_ref, y_ref, x_ref, mod_ref, wa_ref, wy_ref, n2_ref, sg_ref, su_ref, sd_ref,
                  base_ref, hi_ref, lo_ref, rt_ref):
    m = (jnp.dot(a_ref[0], wa_ref[...], preferred_element_type=F32)
         + jnp.dot(y_ref[0], wy_ref[...], preferred_element_type=F32))
    x1 = x_ref[0] + mod_ref[0, 2:3, :] * m
    h2 = _modulated(x1, n2_ref[...], mod_ref[0, 3:4, :], mod_ref[0, 4:5, :])
    hi = h2.astype(BF16)
    hi_ref[0] = hi
    lo_ref[0] = (h2 - hi.astype(F32)).astype(BF16)
    _store_row_tiles(rt_ref, h2)
    gate = jnp.dot(hi, sg_ref[...], preferred_element_type=F32)
    up = jnp.dot(hi, su_ref[...], preferred_element_type=F32)
    act = (gate * _sigmoid(gate) * up).astype(BF16)
    shared = jnp.dot(act, sd_ref[...], preferred_element_type=F32)
    base_ref[0] = x1 + mod_ref[0, 5:6, :] * shared


def _merge(an, yn, x, mods, wa, wy, n2w, sg, su, sd, tl):
    b, l, d = x.shape
    full = lambda a: pl.BlockSpec(a.shape, lambda i, j: (0,) * a.ndim)
    half = pl.BlockSpec((1, tl, an.shape[2]), lambda i, j: (i, j, 0))
    wide = pl.BlockSpec((1, tl, d), lambda i, j: (i, j, 0))
    per_b = l // tl
    return pl.pallas_call(
        _merge_kernel,
        out_shape=(jax.ShapeDtypeStruct((b, l, d), F32),
                   jax.ShapeDtypeStruct((b, l, d), BF16),
                   jax.ShapeDtypeStruct((b, l, d), BF16),
                   jax.ShapeDtypeStruct((b * l * ROW_SUB, LANES), jnp.uint32)),
        grid=(b, per_b),
        in_specs=[half, half, wide,
                  pl.BlockSpec((1, 6, d), lambda i, j: (i, 0, 0)),
                  full(wa), full(wy), full(n2w), full(sg), full(su), full(sd)],
        out_specs=(wide, wide, wide,
                   pl.BlockSpec((tl * ROW_SUB, LANES), lambda i, j: (i * per_b + j, 0))),
        compiler_params=_params(("arbitrary", "arbitrary")),
        name="merge",
    )(an, yn, x, mods, wa, wy, n2w, sg, su, sd)


def _router_kernel(hi_ref, lo_ref, whi_ref, wlo_ref, bias_ref, tri_ref,
                   idx_ref, wgt_ref, pos_ref, cnt_ref, run_ref):
    tt = hi_ref.shape[0]
    per_group = N_EXPERTS // N_GROUPS

    @pl.when(pl.program_id(0) == 0)
    def _():
        run_ref[...] = jnp.zeros_like(run_ref)

    nt = (((1,), (1,)), ((), ()))
    hi = hi_ref[...]
    whi = whi_ref[...]
    logits = (lax.dot_general(whi, hi, nt, preferred_element_type=F32)
              + lax.dot_general(whi, lo_ref[...], nt, preferred_element_type=F32)
              + lax.dot_general(wlo_ref[...], hi, nt, preferred_element_type=F32))
    scores = _sigmoid(logits)
    biased = scores + bias_ref[...]

    ridx = lax.broadcasted_iota(jnp.int32, (per_group, tt), 0)
    groups = [biased[g * per_group:(g + 1) * per_group, :] for g in range(N_GROUPS)]
    gs = []
    for blk in groups:
        m1 = jnp.max(blk, axis=0, keepdims=True)
        i1 = jnp.min(jnp.where(blk == m1, ridx, per_group), axis=0, keepdims=True)
        m2 = jnp.max(jnp.where(ridx == i1, NEG_INF, blk), axis=0, keepdims=True)
        gs.append(m1 + m2)

    kept = []
    for g in range(N_GROUPS):
        ahead = jnp.zeros((1, tt), F32)
        for o in range(N_GROUPS):
            if o != g:
                wins = (gs[o] >= gs[g]) if o < g else (gs[o] > gs[g])
                ahead = ahead + jnp.where(wins, 1.0, 0.0)
        kept.append(jnp.where(ahead < TOPK_GROUPS, groups[g], NEG_INF))
    cur = jnp.concatenate(kept, axis=0)

    eidx = lax.broadcasted_iota(jnp.int32, cur.shape, 0)
    kept_mask = cur
    picks = []
    wsel = []
    for _ in range(TOP_K):
        mx = jnp.max(cur, axis=0, keepdims=True)
        first = jnp.min(jnp.where(cur == mx, eidx, N_EXPERTS), axis=0, keepdims=True)
        sel = eidx == first
        picks.append(first)
        wsel.append(jnp.sum(jnp.where(sel, scores, 0.0), axis=0, keepdims=True))
        cur = jnp.where(sel, NEG_INF, cur)
    w = jnp.concatenate(wsel, axis=0)
    w = w / jnp.sum(w, axis=0, keepdims=True) * ROUTE_SCALE
    idx = jnp.concatenate(picks, axis=0)

    oh = jnp.where(cur == NEG_INF, jnp.where(kept_mask == NEG_INF, 0.0, 1.0), 0.0).astype(BF16)
    before = jnp.dot(oh, tri_ref[0], preferred_element_type=F32)
    total = jnp.dot(oh, tri_ref[1], preferred_element_type=F32)
    pos_ref[...] = run_ref[...] + before
    run_ref[...] = run_ref[...] + total

    idx_ref[...] = idx
    wgt_ref[...] = w
    cnt_ref[...] = run_ref[...]


def _router(hi, lo, whi, wlo, bias, tri, tt):
    t, d = hi.shape
    tok = pl.BlockSpec((tt, d), lambda i: (i, 0))
    full = lambda a: pl.BlockSpec(a.shape, lambda i: (0,) * a.ndim)
    out = pl.BlockSpec((TOP_K, tt), lambda i: (0, i))
    return pl.pallas_call(
        _router_kernel,
        out_shape=(jax.ShapeDtypeStruct((TOP_K, t), jnp.int32),
                   jax.ShapeDtypeStruct((TOP_K, t), F32),
                   jax.ShapeDtypeStruct((N_EXPERTS, t), F32),
                   jax.ShapeDtypeStruct((N_EXPERTS, tt), F32)),
        grid=(t // tt,),
        in_specs=[tok, tok, full(whi), full(wlo), full(bias), full(tri)],
        out_specs=(out, out, pl.BlockSpec((N_EXPERTS, tt), lambda i: (0, i)),
                   pl.BlockSpec((N_EXPERTS, tt), lambda i: (0, 0))),
        scratch_shapes=[pltpu.VMEM((N_EXPERTS, tt), F32)],
        compiler_params=_params(("arbitrary",)),
        name="router",
    )(hi, lo, whi, wlo, bias, tri)


def _row_tile(ref, r):
    return ref.at[pl.ds(pl.multiple_of(r * ROW_SUB, ROW_SUB), ROW_SUB), :]


def _dest_kernel(idx_ref, pos_ref, start_ref, dest_ref):
    eidx = lax.broadcasted_iota(jnp.int32, start_ref.shape, 0)
    row = start_ref[...] + pos_ref[...]
    rows = [jnp.sum(jnp.where(eidx == idx_ref[k:k + 1, :], row, 0.0), axis=0, keepdims=True)
            for k in range(TOP_K)]
    dest_ref[...] = jnp.concatenate(rows, axis=0).astype(jnp.int32)


def _dest(idx, pos, start, tt):
    t = idx.shape[1]
    blk = pl.BlockSpec((TOP_K, tt), lambda i: (0, i))
    return pl.pallas_call(
        _dest_kernel,
        out_shape=jax.ShapeDtypeStruct((TOP_K, t), jnp.int32),
        grid=(t // tt,),
        in_specs=[blk, pl.BlockSpec((N_EXPERTS, tt), lambda i: (0, i)), pl.BlockSpec(start.shape, lambda i: (0, 0))],
        out_specs=blk,
        compiler_params=_params(("arbitrary",)),
        name="dest",
    )(idx, pos, start)


def _zero_tails_kernel(last_ref, o_ref):
    del last_ref
    o_ref[...] = jnp.zeros(o_ref.shape, o_ref.dtype)


def _zero_tails(last_blk, n_rows):
    return pl.pallas_call(
        _zero_tails_kernel,
        out_shape=jax.ShapeDtypeStruct((n_rows * ROW_SUB, LANES), jnp.uint32),
        grid_spec=pltpu.PrefetchScalarGridSpec(
            num_scalar_prefetch=1,
            grid=(last_blk.shape[0],),
            in_specs=[],
            out_specs=pl.BlockSpec((EXPERT_ROWS * ROW_SUB, LANES), lambda e, last: (last[e], 0))),
        compiler_params=_params(("arbitrary",)),
        name="zero_tails",
    )(last_blk)


def _dispatch_kernel(dest_ref, h_ref, xs_in_ref, xs_ref, sem):
    del xs_in_ref
    td = dest_ref.shape[1]

    def row_copy(t, k):
        return pltpu.make_async_copy(_row_tile(h_ref, t), _row_tile(xs_ref, dest_ref[k, t]), sem)

    def issue(t, carry):
        for k in range(TOP_K):
            row_copy(t, k).start(priority=k % 2)
        return carry

    lax.fori_loop(0, td, issue, 0)

    def drain(t, carry):
        for k in range(TOP_K):
            row_copy(t, k).wait()
        return carry

    lax.fori_loop(0, td, drain, 0)


def _dispatch(dest, h_rt, xs0, td):
    t = dest.shape[1]
    return pl.pallas_call(
        _dispatch_kernel,
        out_shape=jax.ShapeDtypeStruct(xs0.shape, xs0.dtype),
        grid=(t // td,),
        in_specs=[pl.BlockSpec((TOP_K, td), lambda i: (0, i), memory_space=pltpu.SMEM),
                  pl.BlockSpec((td * ROW_SUB, LANES), lambda i: (i, 0)),
                  pl.BlockSpec(memory_space=pl.ANY)],
        out_specs=pl.BlockSpec(memory_space=pl.ANY),
        scratch_shapes=[pltpu.SemaphoreType.DMA],
        input_output_aliases={2: 0},
        compiler_params=_params(("arbitrary",)),
        name="dispatch",
    )(dest, h_rt, xs0)


def _experts_kernel(be_ref, nu_ref, x_ref, wg_ref, wu_ref, wd_ref, y_ref, wgu_bf, wd_bf):
    i = pl.program_id(0)
    used = i < nu_ref[0]
    new_expert = jnp.logical_or(i == 0, be_ref[i] != be_ref[jnp.maximum(i - 1, 0)])

    @pl.when(jnp.logical_and(used, new_expert))
    def _():
        ff = wg_ref.shape[2]
        wgu_bf[:, :ff] = wg_ref[0].astype(BF16)
        wgu_bf[:, ff:] = wu_ref[0].astype(BF16)
        wd_bf[...] = wd_ref[0].astype(BF16)

    @pl.when(used)
    def _():
        ff = wg_ref.shape[2]
        for r in range(0, EXPERT_ROWS, EXPERT_SUB):
            rows = pl.ds(r * ROW_SUB, EXPERT_SUB * ROW_SUB)
            x = _load_row_tiles(x_ref.at[rows, :], EXPERT_SUB).astype(BF16)
            gu = jnp.dot(x, wgu_bf[...], preferred_element_type=F32)
            gate = gu[:, :ff]
            act = (gate * _sigmoid(gate) * gu[:, ff:]).astype(BF16)
            _store_row_tiles(y_ref.at[rows, :], jnp.dot(act, wd_bf[...], preferred_element_type=F32))


def _experts(blk_e, n_used, xs, wg, wu, wd):
    d, ff = wg.shape[1], wg.shape[2]
    nblk = xs.shape[0] // (EXPERT_ROWS * ROW_SUB)
    row = lambda i, be, nu: (jnp.minimum(i, nu[0] - 1), 0)
    wsel = lambda i, be, nu: (be[jnp.minimum(i, nu[0] - 1)], 0, 0)
    return pl.pallas_call(
        _experts_kernel,
        out_shape=jax.ShapeDtypeStruct(xs.shape, xs.dtype),
        grid_spec=pltpu.PrefetchScalarGridSpec(
            num_scalar_prefetch=2,
            grid=(nblk,),
            in_specs=[pl.BlockSpec((EXPERT_ROWS * ROW_SUB, LANES), row),
                      pl.BlockSpec((1, d, ff), wsel),
                      pl.BlockSpec((1, d, ff), wsel),
                      pl.BlockSpec((1, ff, d), wsel)],
            out_specs=pl.BlockSpec((EXPERT_ROWS * ROW_SUB, LANES), row),
            scratch_shapes=[pltpu.VMEM((d, 2 * ff), BF16), pltpu.VMEM((ff, d), BF16)]),
        compiler_params=_params(("arbitrary",)),
        name="experts",
    )(blk_e, n_used, xs, wg, wu, wd)


def _combine_kernel(dest_ref, next_ref, ys_ref, w_ref, base_ref, mod_ref, fw_ref, o_ref, buf, sem):
    tc = dest_ref.shape[1]
    i = pl.program_id(0)
    slot = i % 2

    def row_copy(row, s, t, k):
        return pltpu.make_async_copy(_row_tile(ys_ref, row), _row_tile(buf.at[s, k], t), sem.at[s])

    def issue(d_ref, s):
        for t in range(tc):
            for k in range(TOP_K):
                row_copy(d_ref[k, t], s, t, k).start(priority=k % 2)

    def drain(s):
        for t in range(tc):
            for k in range(TOP_K):
                row_copy(0, s, t, k).wait()

    @pl.when(i == 0)
    def _():
        issue(dest_ref, 0)

    drain(slot)
    issue(next_ref, 1 - slot)
    w = w_ref[...]
    routed = w[:, 0:1] * _load_row_tiles(buf.at[slot, 0], tc)
    for k in range(1, TOP_K):
        routed = routed + w[:, k:k + 1] * _load_row_tiles(buf.at[slot, k], tc)
    x = base_ref[0] + mod_ref[0, 5:6, :] * routed
    o_ref[0] = _rms(x, fw_ref[...])

    @pl.when(i == pl.num_programs(0) - 1)
    def _():
        drain(1 - slot)


def _combine(dest, ys, wt, base, mods, fw, tc):
    b, l, d = base.shape
    per_b = l // tc
    steps = b * per_b
    return pl.pallas_call(
        _combine_kernel,
        out_shape=jax.ShapeDtypeStruct((b, l, d), F32),
        grid=(steps,),
        in_specs=[pl.BlockSpec((TOP_K, tc), lambda i: (0, i), memory_space=pltpu.SMEM),
                  pl.BlockSpec((TOP_K, tc), lambda i: (0, jnp.minimum(i + 1, steps - 1)), memory_space=pltpu.SMEM),
                  pl.BlockSpec(memory_space=pl.ANY),
                  pl.BlockSpec((tc, TOP_K), lambda i: (i, 0)),
                  pl.BlockSpec((1, tc, d), lambda i: (i // per_b, i % per_b, 0)),
                  pl.BlockSpec((1, 6, d), lambda i: (i // per_b, 0, 0)),
                  pl.BlockSpec((1, d), lambda i: (0, 0))],
        out_specs=pl.BlockSpec((1, tc, d), lambda i: (i // per_b, i % per_b, 0)),
        scratch_shapes=[pltpu.VMEM((2, TOP_K, tc * ROW_SUB, LANES), jnp.uint32), pltpu.SemaphoreType.DMA((2,))],
        compiler_params=_params(("arbitrary",)),
        name="combine",
    )(dest, dest, ys, wt, base, mods, fw)


def _rope_tables(l):
    t = jnp.arange(l, dtype=jnp.int32)
    row = (t // GRID_W).astype(F32)
    col = (t % GRID_W).astype(F32)
    n_freq = HEAD_DIM // 4
    inv = ROPE_THETA ** (-jnp.arange(n_freq, dtype=F32) / n_freq)
    ang = jnp.concatenate([row[:, None] * inv, col[:, None] * inv], axis=-1)
    cos = jnp.repeat(jnp.cos(ang), 2, axis=1)
    sin = jnp.repeat(jnp.sin(ang), 2, axis=1)
    sign = jnp.tile(jnp.array([-1.0, 1.0], F32), HEAD_DIM // 2)
    reps = LANES // HEAD_DIM
    return jnp.tile(cos, (1, reps)), jnp.tile(sin * sign, (1, reps))


def _filter_features(l):
    t = jnp.linspace(0.0, 1.0, l, dtype=F32)[:, None]
    bands = (FILTER_EMB - 1) // 2
    w = 2.0 * math.pi * jnp.arange(l, dtype=F32)[:, None] / l
    f = jnp.linspace(1e-4, bands - 1, bands, dtype=F32)[None, :]
    z = jnp.concatenate([t, jnp.cos(f * w), -jnp.sin(f * w)], axis=-1)
    min_decay = math.log(FILTER_TARGET) / FILTER_DECAY_FAST
    max_decay = math.log(FILTER_TARGET) / FILTER_DECAY_SLOW
    deltas = jnp.linspace(min_decay, max_decay, HYENA_WIDTH, dtype=F32)[None, :]
    return jnp.pad(z, ((0, 0), (0, LANES - FILTER_EMB))), deltas


def _dft_matrices(l):
    n = 2 * l
    idx = jnp.arange(l, dtype=jnp.int32)
    r = math.isqrt(l)
    assert r * r == l
    sub = jnp.arange(r, dtype=jnp.int32)
    hi = ((r * sub[:, None] * idx[None, :]) % n).astype(F32) * (2.0 * math.pi / n)
    lo = ((sub[:, None] * idx[None, :]) % n).astype(F32) * (2.0 * math.pi / n)
    ch, sh, cl, sl = jnp.cos(hi)[:, None, :], jnp.sin(hi)[:, None, :], jnp.cos(lo)[None], jnp.sin(lo)[None]
    c = (ch * cl - sh * sl).reshape(l, l)
    s = (sh * cl + ch * sl).reshape(l, l)
    alt = jnp.where(idx % 2 == 0, 1.0, -1.0).astype(F32)
    first = (idx == 0)[:, None]
    fwd = jnp.concatenate([c, jnp.where(first, alt[None, :], -s)], axis=0).astype(BF16)
    firstc = (idx == 0)[None, :]
    inv_r = (jnp.where(firstc, 1.0, 2.0) * c / n).astype(BF16)
    inv_i = (jnp.where(firstc, alt[:, None], -2.0 * s) / n).astype(BF16)
    return fwd, inv_r, inv_i


def _head_perm():
    order = []
    for j in range(N_HEADS // 2):
        order += list(range(j * HEAD_DIM, (j + 1) * HEAD_DIM))
        order += list(range((j + N_HEADS // 2) * HEAD_DIM, (j + 1 + N_HEADS // 2) * HEAD_DIM))
    return jnp.array(order, jnp.int32)


def _pad2(a, rows, cols):
    return jnp.pad(a, ((0, rows - a.shape[0]), (0, cols - a.shape[1])))


def kernel(x, c, ctx, c_ctx, mod_w, mod_b, norm1_w, w_in, q_norm_w, k_norm_w, conv_w, conv_b, filt_w1, filt_b1, filt_w2, filt_b2, filt_w3, filt_b3, filt_w4, filt_freq, hyena_bias, attn_out_norm_w, hyena_out_norm_w, w_out, norm2_w, router_w, router_bias, exp_w_gate, exp_w_up, exp_w_down, sh_w_gate, sh_w_up, sh_w_down, final_norm_w):
    b, l, d = x.shape
    t = b * l
    assert mod_w.shape[0] == 1, "single-layer stack"
    tl = min(ROW_TILE, l)

    cond = jnp.concatenate([c, c_ctx[None, :], jnp.zeros((-(b + 1) % 8, d), F32)], axis=0)
    mod = _adaln(cond, mod_w[0], mod_b[0][None, :])
    mods = mod[:b].reshape(b, 6, d)
    cmod = mod[b].reshape(6, d)

    perm = _head_perm()
    w_in0 = w_in[0]
    w_in_k = jnp.concatenate([w_in0[:, :Q_END][:, perm], w_in0[:, Q_END:]], axis=1).astype(BF16)
    w_kv = w_in0[:, Q_END:V_END].astype(BF16)
    gq = jnp.kron(jnp.eye(N_HEADS, dtype=F32), jnp.full((HEAD_DIM, HEAD_DIM), 1.0 / HEAD_DIM, F32)).astype(BF16)
    qnw = jnp.tile(q_norm_w[0], N_HEADS)[None, :]
    knw = jnp.tile(k_norm_w[0], N_KV_HEADS)[None, :]
    n1w = norm1_w[0][None, :]
    cos, sin = _rope_tables(l)

    kc, vc = _ctx_kv(ctx, cmod, n1w, w_kv, gq[:KV_WIDTH, :KV_WIDTH], knw)
    q, k, v, u = _inproj(x, mods, n1w, w_in_k, gq, qnw, knw, cos, sin, tl)
    k_all = jnp.concatenate([kc, k], axis=1)
    v_all = jnp.concatenate([vc, v], axis=1)
    an = _attention(q, k_all, v_all, attn_out_norm_w[0][perm][None, :], min(ATTN_Q_TILE, l),
                    min(ATTN_SUB, l))

    cw = conv_w[0].reshape(3, 3, HYENA_WIDTH).transpose(1, 0, 2)
    cb = conv_b[0].reshape(3, HYENA_WIDTH)
    gbf, x0 = _hyena_pre(u, cw, cb)
    z, deltas = _filter_features(l)
    hsd = _hyena_filter(
        z, _pad2(filt_w1[0], LANES, LANES), _pad2(filt_b1[0][None, :], 1, LANES),
        _pad2(filt_w2[0], LANES, LANES), _pad2(filt_b2[0][None, :], 1, LANES),
        _pad2(filt_w3[0], LANES, LANES), _pad2(filt_b3[0][None, :], 1, LANES),
        _pad2(filt_w4[0], LANES, 2 * HYENA_WIDTH), _pad2(filt_freq[0][None, :], 1, LANES), deltas, tl)
    fwd, inv_r, inv_i = _dft_matrices(l)
    spec = _dft(fwd, hsd, tl)
    row0 = (jnp.arange(l) == 0)[:, None]
    sa = spec[0, :l]
    sd = jnp.where(row0, spec[0, l:l + 1], sa)
    sb = jnp.where(row0, 0.0, spec[1, l:])
    zr, zi = _dft_mul(fwd, gbf, sa, sb, sd, tl)
    yn = _idft(inv_r, inv_i, zr, zi, gbf, x0, hyena_bias[0][None, :], hyena_out_norm_w[0][None, :], tl)

    w_out0 = w_out[0]
    base, h2hi, h2lo, h2rt = _merge(
        an, yn, x, mods, w_out0[:ATTN_WIDTH][perm].astype(BF16), w_out0[ATTN_WIDTH:].astype(BF16),
        norm2_w[0][None, :], sh_w_gate[0].astype(BF16), sh_w_up[0].astype(BF16), sh_w_down[0].astype(BF16), tl)

    tt = ROUTER_TILE
    rwt = router_w[0].T
    rw_hi = rwt.astype(BF16)
    rw_lo = (rwt - rw_hi.astype(F32)).astype(BF16)
    bias = jnp.broadcast_to(router_bias[0][:, None], (N_EXPERTS, tt))
    ti = jnp.arange(tt)
    tri = jnp.stack([(ti[:, None] < ti[None, :]), jnp.ones((tt, tt), bool)]).astype(BF16)
    idx, wgt, pos, cnt = _router(h2hi.reshape(t, d), h2lo.reshape(t, d), rw_hi, rw_lo, bias, tri, tt)

    counts = cnt[:, 0].astype(jnp.int32)
    padded = (counts + EXPERT_ROWS - 1) // EXPERT_ROWS * EXPERT_ROWS
    pad_end = jnp.cumsum(padded)
    pad_start = pad_end - padded
    td = min(DEST_TILE, t)
    dest = _dest(idx, pos, jnp.broadcast_to(pad_start.astype(F32)[:, None], (N_EXPERTS, td)), td)
    n_rows = (t * TOP_K + N_EXPERTS * (EXPERT_ROWS - 1) + EXPERT_ROWS - 1) // EXPERT_ROWS * EXPERT_ROWS
    nblk = n_rows // EXPERT_ROWS
    blk_row = jnp.arange(nblk, dtype=jnp.int32) * EXPERT_ROWS
    blk_e = jnp.minimum(jnp.sum((pad_end[None, :] <= blk_row[:, None]).astype(jnp.int32), axis=1), N_EXPERTS - 1)
    n_used = (pad_end[-1:] // EXPERT_ROWS).astype(jnp.int32)

    last_blk = jnp.maximum(pad_end // EXPERT_ROWS - 1, 0).astype(jnp.int32)
    xs = _dispatch(dest, h2rt, _zero_tails(last_blk, n_rows), min(DISPATCH_TILE, t))
    ys = _experts(blk_e, n_used, xs, exp_w_gate[0], exp_w_up[0], exp_w_down[0])
    return _combine(dest, ys, wgt.T, base, mods, final_norm_w[None, :], min(COMBINE_TILE, l))
```
